```python
import math
import jax, jax.numpy as jnp
from jax import lax
import numpy as np

D_MODEL = 2048
BATCH = 8
SEQ = 4096
DEPTH = 4

D_FF = 5632
EPS = 1e-6
ROPE_THETA = 10000.0
BLOCK = 128

SSM_GROUP = 16
SSM_WIDTH = D_MODEL // 4
SSM_GROUPS = SSM_WIDTH // SSM_GROUP
SSM_STATE = 64
DT_MIN = 1e-3
DT_MAX = 1e-1

SWA_HEADS = 8
SWA_KV_HEADS = 2
SWA_Q_PER_KV = SWA_HEADS // SWA_KV_HEADS
SWA_HEAD_DIM = 64
SWA_WINDOW = 128
SWA_WIDTH = SWA_HEADS * SWA_HEAD_DIM

MLA_HEADS = 8
MLA_Q_RANK = 512
MLA_KV_RANK = 256
MLA_NOPE_DIM = 128
MLA_ROPE_DIM = 64
MLA_V_DIM = 128
MLA_WIDTH = MLA_HEADS * MLA_V_DIM

D_MIX = SSM_WIDTH + SWA_WIDTH + MLA_WIDTH
ROPE_DIM = SWA_HEAD_DIM

IN_SIZES = (SSM_WIDTH,
            SWA_WIDTH,
            SWA_KV_HEADS * SWA_HEAD_DIM,
            SWA_KV_HEADS * SWA_HEAD_DIM,
            MLA_Q_RANK,
            MLA_KV_RANK,
            MLA_ROPE_DIM)
N_IN = sum(IN_SIZES)
IN_OFFSETS = tuple(int(o) for o in np.cumsum(IN_SIZES)[:-1])

kernel_name = "hymba_s5_swa_mla_macaron"


def _rmsnorm(x, gain):
    xf = x.astype(jnp.float32)
    xf = xf * lax.rsqrt(jnp.mean(xf * xf, axis=-1, keepdims=True) + EPS)
    return xf.astype(x.dtype) * gain


def _rope_tables(positions, dim):
    inv_freq = ROPE_THETA ** (-jnp.arange(0, dim, 2, dtype=jnp.float32) / dim)
    ang = positions.astype(jnp.float32)[..., None] * inv_freq
    return jnp.cos(ang), jnp.sin(ang)


def _apply_rope(x, cos, sin):
    c = cos[:, :, None, :].astype(x.dtype)
    s = sin[:, :, None, :].astype(x.dtype)
    x1, x2 = jnp.split(x, 2, axis=-1)
    return jnp.concatenate([x1 * c - x2 * s, x2 * c + x1 * s], axis=-1)


def _swiglu(h, w_gate, w_up, w_down):
    return (jax.nn.silu(h @ w_gate) * (h @ w_up)) @ w_down


def _complex_affine_combine(left, right):
    a1r, a1i, b1r, b1i = left
    a2r, a2i, b2r, b2i = right
    ar = a2r * a1r - a2i * a1i
    ai = a2r * a1i + a2i * a1r
    br = a2r * b1r - a2i * b1i + b2r
    bi = a2r * b1i + a2i * b1r + b2i
    return (ar, ai, br, bi)


def _s5_mixer(u, log_dt, a_re, a_im, b_re, b_im, c_re, c_im, d_skip, w_glu, b_glu):
    B_, L, _ = u.shape
    f32 = jnp.float32
    uf = u.astype(f32).reshape(B_, L, SSM_GROUPS, SSM_GROUP)
    lr, li = a_re.astype(f32), a_im.astype(f32)
    dt = jnp.exp(log_dt.astype(f32))[:, None]
    mag = jnp.exp(lr * dt)
    abar_r = mag * jnp.cos(li * dt)
    abar_i = mag * jnp.sin(li * dt)
    den = lr * lr + li * li
    nr = abar_r - 1.0
    qr = (nr * lr + abar_i * li) / den
    qi = (abar_i * lr - nr * li) / den
    br, bi = b_re.astype(f32), b_im.astype(f32)
    bbar_r = qr[..., None] * br - qi[..., None] * bi
    bbar_i = qr[..., None] * bi + qi[..., None] * br
    bu_r = jnp.einsum('blgc,gpc->blgp', uf, bbar_r)
    bu_i = jnp.einsum('blgc,gpc->blgp', uf, bbar_i)
    shape = bu_r.shape
    elems = (jnp.broadcast_to(abar_r, shape), jnp.broadcast_to(abar_i, shape), bu_r, bu_i)
    _, _, xr, xi = lax.associative_scan(_complex_affine_combine, elems, axis=1)
    y = (jnp.einsum('blgp,gcp->blgc', xr, c_re.astype(f32))
         - jnp.einsum('blgp,gcp->blgc', xi, c_im.astype(f32)))
    y = y.reshape(B_, L, SSM_WIDTH) + d_skip.astype(f32) * u.astype(f32)
    y = jax.nn.gelu(y).astype(u.dtype)
    return y * jax.nn.sigmoid(y @ w_glu + b_glu)


def _swa_mixer(q, k, v, cos, sin, sinks):
    B_, L = q.shape[:2]
    nb = L // BLOCK
    f32 = jnp.float32
    q = _apply_rope(q.reshape(B_, L, SWA_HEADS, SWA_HEAD_DIM), cos, sin)
    k = _apply_rope(k.reshape(B_, L, SWA_KV_HEADS, SWA_HEAD_DIM), cos, sin)
    v = v.reshape(B_, L, SWA_KV_HEADS, SWA_HEAD_DIM)
    qb = q.reshape(B_, nb, BLOCK, SWA_KV_HEADS, SWA_Q_PER_KV, SWA_HEAD_DIM)

    def with_prev(t):
        tb = t.reshape(B_, nb, BLOCK, SWA_KV_HEADS, SWA_HEAD_DIM)
        prev = jnp.concatenate([jnp.zeros_like(tb[:, :1]), tb[:, :-1]], axis=1)
        return jnp.concatenate([prev, tb], axis=2)

    kb, vb = with_prev(k), with_prev(v)
    s = jnp.einsum('bnqhgd,bnkhd->bnhgqk', qb, kb).astype(f32) * (SWA_HEAD_DIM ** -0.5)
    qi = jnp.arange(BLOCK)[:, None] + BLOCK
    kj = jnp.arange(2 * BLOCK)[None, :]
    band = (qi - kj >= 0) & (qi - kj < SWA_WINDOW)
    has_prev = (jnp.arange(nb) > 0)[:, None, None] | (kj >= BLOCK)[None]
    mask = band[None] & has_prev
    s = jnp.where(mask[None, :, None, None], s, -jnp.inf)
    sink = sinks.astype(f32).reshape(1, 1, SWA_KV_HEADS, SWA_Q_PER_KV, 1, 1)
    m = jnp.maximum(jnp.max(s, axis=-1, keepdims=True), sink)
    e = jnp.exp(s - m)
    p = e / (jnp.sum(e, axis=-1, keepdims=True) + jnp.exp(sink - m))
    o = jnp.einsum('bnhgqk,bnkhd->bnqhgd', p.astype(v.dtype), vb)
    return o.reshape(B_, L, SWA_WIDTH)


def _mla_mixer(c_q, c_kv, k_rope, cos, sin, q_norm, w_uq, kv_norm, w_ukv):
    B_, L = c_q.shape[:2]
    nb = L // BLOCK
    f32 = jnp.float32
    q = (_rmsnorm(c_q, q_norm) @ w_uq).reshape(B_, L, MLA_HEADS, MLA_NOPE_DIM + MLA_ROPE_DIM)
    q_nope = q[..., :MLA_NOPE_DIM]
    q_rope = _apply_rope(q[..., MLA_NOPE_DIM:], cos, sin)
    kv = (_rmsnorm(c_kv, kv_norm) @ w_ukv).reshape(B_, L, MLA_HEADS, MLA_NOPE_DIM + MLA_V_DIM)
    k_nope, v = kv[..., :MLA_NOPE_DIM], kv[..., MLA_NOPE_DIM:]
    k_rope = _apply_rope(k_rope[:, :, None, :], cos, sin)[:, :, 0]
    scale = (MLA_NOPE_DIM + MLA_ROPE_DIM) ** -0.5
    key_pos = jnp.arange(L)

    def to_blocks(t):
        return jnp.moveaxis(t.reshape(B_, nb, BLOCK, *t.shape[2:]), 1, 0)

    def block(args):
        qn, qr, n = args
        s = (jnp.einsum('bqhd,bkhd->bhqk', qn, k_nope)
             + jnp.einsum('bqhd,bkd->bhqk', qr, k_rope)).astype(f32) * scale
        q_pos = n * BLOCK + jnp.arange(BLOCK)
        s = jnp.where(key_pos[None, :] <= q_pos[:, None], s, -jnp.inf)
        p = jax.nn.softmax(s, axis=-1).astype(v.dtype)
        return jnp.einsum('bhqk,bkhd->bqhd', p, v)

    o = lax.map(block, (to_blocks(q_nope), to_blocks(q_rope), jnp.arange(nb)))
    return jnp.moveaxis(o, 0, 1).reshape(B_, L, MLA_WIDTH)


def _fwd_setup_inputs(seed: int = 0) -> dict:
    key = jax.random.key(seed)
    ks = jax.random.split(key, 40)
    f32 = jnp.float32

    def nrm(k, shape, scale):
        return jax.random.normal(k, shape, f32) * scale

    def gain(k, shape):
        return 1.0 + 0.02 * jax.random.normal(k, shape, f32)

    x = jax.random.normal(ks[0], (BATCH, SEQ, D_MODEL), f32)
    offset = jax.random.randint(ks[1], (BATCH, 1), 0, 1024, dtype=jnp.int32)
    positions = (offset + jnp.arange(SEQ, dtype=jnp.int32)[None, :]).astype(jnp.int32)
    n_idx = jnp.arange(SSM_STATE, dtype=f32)
    return {
        "x": x,
        "positions": positions,
        "ffn1_norm": gain(ks[2], (DEPTH, D_MODEL)),
        "ffn1_w_gate": nrm(ks[3], (DEPTH, D_MODEL, D_FF), D_MODEL ** -0.5),
        "ffn1_w_up": nrm(ks[4], (DEPTH, D_MODEL, D_FF), D_MODEL ** -0.5),
        "ffn1_w_down": nrm(ks[5], (DEPTH, D_FF, D_MODEL), D_FF ** -0.5),
        "mix_norm": gain(ks[6], (DEPTH, D_MODEL)),
        "w_in": nrm(ks[7], (DEPTH, D_MODEL, N_IN), D_MODEL ** -0.5),
        "ssm_log_dt": jax.random.uniform(ks[8], (DEPTH, SSM_GROUPS), f32, math.log(DT_MIN), math.log(DT_MAX)),
        "ssm_a_re": -0.5 * jnp.exp(0.05 * jax.random.normal(ks[9], (DEPTH, SSM_GROUPS, SSM_STATE), f32)),
        "ssm_a_im": math.pi * n_idx + 0.01 * jax.random.normal(ks[10], (DEPTH, SSM_GROUPS, SSM_STATE), f32),
        "ssm_b_re": nrm(ks[11], (DEPTH, SSM_GROUPS, SSM_STATE, SSM_GROUP), (2 * SSM_GROUP) ** -0.5),
        "ssm_b_im": nrm(ks[12], (DEPTH, SSM_GROUPS, SSM_STATE, SSM_GROUP), (2 * SSM_GROUP) ** -0.5),
        "ssm_c_re": nrm(ks[13], (DEPTH, SSM_GROUPS, SSM_GROUP, SSM_STATE), (2 * SSM_STATE) ** -0.5),
        "ssm_c_im": nrm(ks[14], (DEPTH, SSM_GROUPS, SSM_GROUP, SSM_STATE), (2 * SSM_STATE) ** -0.5),
        "ssm_d": nrm(ks[15], (DEPTH, SSM_WIDTH), 0.5),
        "ssm_w_glu": nrm(ks[16], (DEPTH, SSM_WIDTH, SSM_WIDTH), SSM_WIDTH ** -0.5),
        "ssm_b_glu": nrm(ks[17], (DEPTH, SSM_WIDTH), 0.01),
        "swa_sinks": nrm(ks[18], (DEPTH, SWA_HEADS), 0.5),
        "mla_q_norm": gain(ks[19], (DEPTH, MLA_Q_RANK)),
        "mla_w_uq": nrm(ks[20], (DEPTH, MLA_Q_RANK, MLA_HEADS * (MLA_NOPE_DIM + MLA_ROPE_DIM)), MLA_Q_RANK ** -0.5),
        "mla_kv_norm": gain(ks[21], (DEPTH, MLA_KV_RANK)),
        "mla_w_ukv": nrm(ks[22], (DEPTH, MLA_KV_RANK, MLA_HEADS * (MLA_NOPE_DIM + MLA_V_DIM)), MLA_KV_RANK ** -0.5),
        "out_norm": gain(ks[23], (DEPTH, D_MIX)),
        "w_out": nrm(ks[24], (DEPTH, D_MIX, D_MODEL), D_MIX ** -0.5),
        "ffn2_norm": gain(ks[25], (DEPTH, D_MODEL)),
        "ffn2_w_gate": nrm(ks[26], (DEPTH, D_MODEL, D_FF), D_MODEL ** -0.5),
        "ffn2_w_up": nrm(ks[27], (DEPTH, D_MODEL, D_FF), D_MODEL ** -0.5),
        "ffn2_w_down": nrm(ks[28], (DEPTH, D_FF, D_MODEL), D_FF ** -0.5),
        "final_norm": gain(ks[29], (D_MODEL,)),
    }


def _fwd_reference(x, positions, ffn1_norm, ffn1_w_gate, ffn1_w_up, ffn1_w_down, mix_norm, w_in,
              ssm_log_dt, ssm_a_re, ssm_a_im, ssm_b_re, ssm_b_im, ssm_c_re, ssm_c_im, ssm_d,
              ssm_w_glu, ssm_b_glu, swa_sinks, mla_q_norm, mla_w_uq, mla_kv_norm, mla_w_ukv,
              out_norm, w_out, ffn2_norm, ffn2_w_gate, ffn2_w_up, ffn2_w_down, final_norm):
    cos, sin = _rope_tables(positions, ROPE_DIM)
    s1 = SSM_WIDTH
    s2 = SSM_WIDTH + SWA_WIDTH
    for l in range(DEPTH):
        h = _rmsnorm(x, ffn1_norm[l])
        x = x + 0.5 * _swiglu(h, ffn1_w_gate[l], ffn1_w_up[l], ffn1_w_down[l])
        h = _rmsnorm(x, mix_norm[l])
        z = h @ w_in[l]
        u, q_s, k_s, v_s, c_q, c_kv, k_r = jnp.split(z, IN_OFFSETS, axis=-1)
        y_ssm = _s5_mixer(u, ssm_log_dt[l], ssm_a_re[l], ssm_a_im[l], ssm_b_re[l], ssm_b_im[l],
                          ssm_c_re[l], ssm_c_im[l], ssm_d[l], ssm_w_glu[l], ssm_b_glu[l])
        y_swa = _swa_mixer(q_s, k_s, v_s, cos, sin, swa_sinks[l])
        y_mla = _mla_mixer(c_q, c_kv, k_r, cos, sin, mla_q_norm[l], mla_w_uq[l],
                           mla_kv_norm[l], mla_w_ukv[l])
        y = jnp.concatenate([_rmsnorm(y_ssm, out_norm[l, :s1]),
                             _rmsnorm(y_swa, out_norm[l, s1:s2]),
                             _rmsnorm(y_mla, out_norm[l, s2:])], axis=-1)
        x = x + y @ w_out[l]
        h = _rmsnorm(x, ffn2_norm[l])
        x = x + 0.5 * _swiglu(h, ffn2_w_gate[l], ffn2_w_up[l], ffn2_w_down[l])
    return _rmsnorm(x, final_norm)


import jax as _jax
import jax.numpy as _jnp

TWIN_FORMAT = 'train_step'
FWD_PARAMS = ['x', 'positions', 'ffn1_norm', 'ffn1_w_gate', 'ffn1_w_up', 'ffn1_w_down', 'mix_norm', 'w_in', 'ssm_log_dt', 'ssm_a_re', 'ssm_a_im', 'ssm_b_re', 'ssm_b_im', 'ssm_c_re', 'ssm_c_im', 'ssm_d', 'ssm_w_glu', 'ssm_b_glu', 'swa_sinks', 'mla_q_norm', 'mla_w_uq', 'mla_kv_norm', 'mla_w_ukv', 'out_norm', 'w_out', 'ffn2_norm', 'ffn2_w_gate', 'ffn2_w_up', 'ffn2_w_down', 'final_norm']
TWIN_WEIGHTS = ['ffn1_norm', 'ffn1_w_gate', 'ffn1_w_up', 'ffn1_w_down', 'mix_norm', 'w_in', 'ssm_log_dt', 'ssm_a_re', 'ssm_a_im', 'ssm_b_re', 'ssm_b_im', 'ssm_c_re', 'ssm_c_im', 'ssm_d', 'ssm_w_glu', 'ssm_b_glu', 'swa_sinks', 'mla_q_norm', 'mla_w_uq', 'mla_kv_norm', 'mla_w_ukv', 'out_norm', 'w_out', 'ffn2_norm', 'ffn2_w_gate', 'ffn2_w_up', 'ffn2_w_down', 'final_norm']
TWIN_DIFF_INPUT = 'x'
TWIN_INPUTS = ['x', 'positions', 'ffn1_norm', 'ffn1_w_gate', 'ffn1_w_up', 'ffn1_w_down', 'mix_norm', 'w_in', 'ssm_log_dt', 'ssm_a_re', 'ssm_a_im', 'ssm_b_re', 'ssm_b_im', 'ssm_c_re', 'ssm_c_im', 'ssm_d', 'ssm_w_glu', 'ssm_b_glu', 'swa_sinks', 'mla_q_norm', 'mla_w_uq', 'mla_kv_norm', 'mla_w_ukv', 'out_norm', 'w_out', 'ffn2_norm', 'ffn2_w_gate', 'ffn2_w_up', 'ffn2_w_down', 'final_norm', 'loss_target', 'm_ffn1_norm', 'm_ffn1_w_gate', 'm_ffn1_w_up', 'm_ffn1_w_down', 'm_mix_norm', 'm_w_in', 'm_ssm_log_dt', 'm_ssm_a_re', 'm_ssm_a_im', 'm_ssm_b_re', 'm_ssm_b_im', 'm_ssm_c_re', 'm_ssm_c_im', 'm_ssm_d', 'm_ssm_w_glu', 'm_ssm_b_glu', 'm_swa_sinks', 'm_mla_q_norm', 'm_mla_w_uq', 'm_mla_kv_norm', 'm_mla_w_ukv', 'm_out_norm', 'm_w_out', 'm_ffn2_norm', 'm_ffn2_w_gate', 'm_ffn2_w_up', 'm_ffn2_w_down', 'm_final_norm', 'v_ffn1_norm', 'v_ffn1_w_gate', 'v_ffn1_w_up', 'v_ffn1_w_down', 'v_mix_norm', 'v_w_in', 'v_ssm_log_dt', 'v_ssm_a_re', 'v_ssm_a_im', 'v_ssm_b_re', 'v_ssm_b_im', 'v_ssm_c_re', 'v_ssm_c_im', 'v_ssm_d', 'v_ssm_w_glu', 'v_ssm_b_glu', 'v_swa_sinks', 'v_mla_q_norm', 'v_mla_w_uq', 'v_mla_kv_norm', 'v_mla_w_ukv', 'v_out_norm', 'v_w_out', 'v_ffn2_norm', 'v_ffn2_w_gate', 'v_ffn2_w_up', 'v_ffn2_w_down', 'v_final_norm']
TWIN_OUTPUTS = ['loss', 'grad_x', 'grad_ffn1_norm', 'grad_ffn1_w_gate', 'grad_ffn1_w_up', 'grad_ffn1_w_down', 'grad_mix_norm', 'grad_w_in', 'grad_ssm_log_dt', 'grad_ssm_a_re', 'grad_ssm_a_im', 'grad_ssm_b_re', 'grad_ssm_b_im', 'grad_ssm_c_re', 'grad_ssm_c_im', 'grad_ssm_d', 'grad_ssm_w_glu', 'grad_ssm_b_glu', 'grad_swa_sinks', 'grad_mla_q_norm', 'grad_mla_w_uq', 'grad_mla_kv_norm', 'grad_mla_w_ukv', 'grad_out_norm', 'grad_w_out', 'grad_ffn2_norm', 'grad_ffn2_w_gate', 'grad_ffn2_w_up', 'grad_ffn2_w_down', 'grad_final_norm', 'delta_ffn1_norm', 'delta_ffn1_w_gate', 'delta_ffn1_w_up', 'delta_ffn1_w_down', 'delta_mix_norm', 'delta_w_in', 'delta_ssm_log_dt', 'delta_ssm_a_re', 'delta_ssm_a_im', 'delta_ssm_b_re', 'delta_ssm_b_im', 'delta_ssm_c_re', 'delta_ssm_c_im', 'delta_ssm_d', 'delta_ssm_w_glu', 'delta_ssm_b_glu', 'delta_swa_sinks', 'delta_mla_q_norm', 'delta_mla_w_uq', 'delta_mla_kv_norm', 'delta_mla_w_ukv', 'delta_out_norm', 'delta_w_out', 'delta_ffn2_norm', 'delta_ffn2_w_gate', 'delta_ffn2_w_up', 'delta_ffn2_w_down', 'delta_final_norm', 'new_m_ffn1_norm', 'new_m_ffn1_w_gate', 'new_m_ffn1_w_up', 'new_m_ffn1_w_down', 'new_m_mix_norm', 'new_m_w_in', 'new_m_ssm_log_dt', 'new_m_ssm_a_re', 'new_m_ssm_a_im', 'new_m_ssm_b_re', 'new_m_ssm_b_im', 'new_m_ssm_c_re', 'new_m_ssm_c_im', 'new_m_ssm_d', 'new_m_ssm_w_glu', 'new_m_ssm_b_glu', 'new_m_swa_sinks', 'new_m_mla_q_norm', 'new_m_mla_w_uq', 'new_m_mla_kv_norm', 'new_m_mla_w_ukv', 'new_m_out_norm', 'new_m_w_out', 'new_m_ffn2_norm', 'new_m_ffn2_w_gate', 'new_m_ffn2_w_up', 'new_m_ffn2_w_down', 'new_m_final_norm', 'new_v_ffn1_norm', 'new_v_ffn1_w_gate', 'new_v_ffn1_w_up', 'new_v_ffn1_w_down', 'new_v_mix_norm', 'new_v_w_in', 'new_v_ssm_log_dt', 'new_v_ssm_a_re', 'new_v_ssm_a_im', 'new_v_ssm_b_re', 'new_v_ssm_b_im', 'new_v_ssm_c_re', 'new_v_ssm_c_im', 'new_v_ssm_d', 'new_v_ssm_w_glu', 'new_v_ssm_b_glu', 'new_v_swa_sinks', 'new_v_mla_q_norm', 'new_v_mla_w_uq', 'new_v_mla_kv_norm', 'new_v_mla_w_ukv', 'new_v_out_norm', 'new_v_w_out', 'new_v_ffn2_norm', 'new_v_ffn2_w_gate', 'new_v_ffn2_w_up', 'new_v_ffn2_w_down', 'new_v_final_norm']
TWIN_LEAF_KINDS = {'loss': 'loss', 'grad_x': 'grad_x', 'grad_ffn1_norm': 'grad_w', 'grad_ffn1_w_gate': 'grad_w', 'grad_ffn1_w_up': 'grad_w', 'grad_ffn1_w_down': 'grad_w', 'grad_mix_norm': 'grad_w', 'grad_w_in': 'grad_w', 'grad_ssm_log_dt': 'grad_w', 'grad_ssm_a_re': 'grad_w', 'grad_ssm_a_im': 'grad_w', 'grad_ssm_b_re': 'grad_w', 'grad_ssm_b_im': 'grad_w', 'grad_ssm_c_re': 'grad_w', 'grad_ssm_c_im': 'grad_w', 'grad_ssm_d': 'grad_w', 'grad_ssm_w_glu': 'grad_w', 'grad_ssm_b_glu': 'grad_w', 'grad_swa_sinks': 'grad_w', 'grad_mla_q_norm': 'grad_w', 'grad_mla_w_uq': 'grad_w', 'grad_mla_kv_norm': 'grad_w', 'grad_mla_w_ukv': 'grad_w', 'grad_out_norm': 'grad_w', 'grad_w_out': 'grad_w', 'grad_ffn2_norm': 'grad_w', 'grad_ffn2_w_gate': 'grad_w', 'grad_ffn2_w_up': 'grad_w', 'grad_ffn2_w_down': 'grad_w', 'grad_final_norm': 'grad_w', 'delta_ffn1_norm': 'delta_w', 'delta_ffn1_w_gate': 'delta_w', 'delta_ffn1_w_up': 'delta_w', 'delta_ffn1_w_down': 'delta_w', 'delta_mix_norm': 'delta_w', 'delta_w_in': 'delta_w', 'delta_ssm_log_dt': 'delta_w', 'delta_ssm_a_re': 'delta_w', 'delta_ssm_a_im': 'delta_w', 'delta_ssm_b_re': 'delta_w', 'delta_ssm_b_im': 'delta_w', 'delta_ssm_c_re': 'delta_w', 'delta_ssm_c_im': 'delta_w', 'delta_ssm_d': 'delta_w', 'delta_ssm_w_glu': 'delta_w', 'delta_ssm_b_glu': 'delta_w', 'delta_swa_sinks': 'delta_w', 'delta_mla_q_norm': 'delta_w', 'delta_mla_w_uq': 'delta_w', 'delta_mla_kv_norm': 'delta_w', 'delta_mla_w_ukv': 'delta_w', 'delta_out_norm': 'delta_w', 'delta_w_out': 'delta_w', 'delta_ffn2_norm': 'delta_w', 'delta_ffn2_w_gate': 'delta_w', 'delta_ffn2_w_up': 'delta_w', 'delta_ffn2_w_down': 'delta_w', 'delta_final_norm': 'delta_w', 'new_m_ffn1_norm': 'new_m', 'new_m_ffn1_w_gate': 'new_m', 'new_m_ffn1_w_up': 'new_m', 'new_m_ffn1_w_down': 'new_m', 'new_m_mix_norm': 'new_m', 'new_m_w_in': 'new_m', 'new_m_ssm_log_dt': 'new_m', 'new_m_ssm_a_re': 'new_m', 'new_m_ssm_a_im': 'new_m', 'new_m_ssm_b_re': 'new_m', 'new_m_ssm_b_im': 'new_m', 'new_m_ssm_c_re': 'new_m', 'new_m_ssm_c_im': 'new_m', 'new_m_ssm_d': 'new_m', 'new_m_ssm_w_glu': 'new_m', 'new_m_ssm_b_glu': 'new_m', 'new_m_swa_sinks': 'new_m', 'new_m_mla_q_norm': 'new_m', 'new_m_mla_w_uq': 'new_m', 'new_m_mla_kv_norm': 'new_m', 'new_m_mla_w_ukv': 'new_m', 'new_m_out_norm': 'new_m', 'new_m_w_out': 'new_m', 'new_m_ffn2_norm': 'new_m', 'new_m_ffn2_w_gate': 'new_m', 'new_m_ffn2_w_up': 'new_m', 'new_m_ffn2_w_down': 'new_m', 'new_m_final_norm': 'new_m', 'new_v_ffn1_norm': 'new_v', 'new_v_ffn1_w_gate': 'new_v', 'new_v_ffn1_w_up': 'new_v', 'new_v_ffn1_w_down': 'new_v', 'new_v_mix_norm': 'new_v', 'new_v_w_in': 'new_v', 'new_v_ssm_log_dt': 'new_v', 'new_v_ssm_a_re': 'new_v', 'new_v_ssm_a_im': 'new_v', 'new_v_ssm_b_re': 'new_v', 'new_v_ssm_b_im': 'new_v', 'new_v_ssm_c_re': 'new_v', 'new_v_ssm_c_im': 'new_v', 'new_v_ssm_d': 'new_v', 'new_v_ssm_w_glu': 'new_v', 'new_v_ssm_b_glu': 'new_v', 'new_v_swa_sinks': 'new_v', 'new_v_mla_q_norm': 'new_v', 'new_v_mla_w_uq': 'new_v', 'new_v_mla_kv_norm': 'new_v', 'new_v_mla_w_ukv': 'new_v', 'new_v_out_norm': 'new_v', 'new_v_w_out': 'new_v', 'new_v_ffn2_norm': 'new_v', 'new_v_ffn2_w_gate': 'new_v', 'new_v_ffn2_w_up': 'new_v', 'new_v_ffn2_w_down': 'new_v', 'new_v_final_norm': 'new_v'}


def _forward(args):
    return _fwd_reference(*[args[k] for k in FWD_PARAMS])


def _output_shape():
    def fwd():
        inp = _fwd_setup_inputs(0)
        return _fwd_reference(*[inp[k] for k in FWD_PARAMS])
    out = _jax.eval_shape(fwd)
    return out.shape, out.dtype

N_MICROBATCH = 1
ADAM_LR = 0.001
ADAM_B1 = 0.9
ADAM_B2 = 0.999
ADAM_EPS = 1e-08
ADAM_WD = 0.01
ADAM_STEP = 10
PER_EXAMPLE_BATCH_AXIS = {'x': 0, 'positions': 0, 'loss_target': 0}
SHARED_INPUTS = []
_WEIGHT_DTYPES = {'ffn1_norm': _jnp.float32, 'ffn1_w_gate': _jnp.float32, 'ffn1_w_up': _jnp.float32, 'ffn1_w_down': _jnp.float32, 'mix_norm': _jnp.float32, 'w_in': _jnp.float32, 'ssm_log_dt': _jnp.float32, 'ssm_a_re': _jnp.float32, 'ssm_a_im': _jnp.float32, 'ssm_b_re': _jnp.float32, 'ssm_b_im': _jnp.float32, 'ssm_c_re': _jnp.float32, 'ssm_c_im': _jnp.float32, 'ssm_d': _jnp.float32, 'ssm_w_glu': _jnp.float32, 'ssm_b_glu': _jnp.float32, 'swa_sinks': _jnp.float32, 'mla_q_norm': _jnp.float32, 'mla_w_uq': _jnp.float32, 'mla_kv_norm': _jnp.float32, 'mla_w_ukv': _jnp.float32, 'out_norm': _jnp.float32, 'w_out': _jnp.float32, 'ffn2_norm': _jnp.float32, 'ffn2_w_gate': _jnp.float32, 'ffn2_w_up': _jnp.float32, 'ffn2_w_down': _jnp.float32, 'final_norm': _jnp.float32}
MOMENT_SCALE = {'ffn1_norm': 3.081393e-02, 'ffn1_w_gate': 1.308135e-02, 'ffn1_w_up': 1.277902e-02, 'ffn1_w_down': 2.119139e-02, 'mix_norm': 8.003912e-02, 'w_in': 7.806270e-02, 'ssm_log_dt': 2.975669e+00, 'ssm_a_re': 9.814860e-03, 'ssm_a_im': 1.193668e-02, 'ssm_b_re': 5.327055e-03, 'ssm_b_im': 4.857417e-03, 'ssm_c_re': 1.100430e-02, 'ssm_c_im': 9.976272e-03, 'ssm_d': 1.850195e-01, 'ssm_w_glu': 8.285011e-03, 'ssm_b_glu': 4.416579e-02, 'swa_sinks': 3.008966e-02, 'mla_q_norm': 4.693387e-02, 'mla_w_uq': 2.391936e-02, 'mla_kv_norm': 1.740186e-01, 'mla_w_ukv': 6.406836e-02, 'out_norm': 8.793997e-02, 'w_out': 8.326192e-02, 'ffn2_norm': 2.010862e-02, 'ffn2_w_gate': 8.631295e-03, 'ffn2_w_up': 8.655362e-03, 'ffn2_w_down': 1.435587e-02, 'final_norm': 1.643173e+01}


def _to_microbatches(a, axis):
    t = _jnp.moveaxis(a, axis, 0)
    t = t.reshape((N_MICROBATCH, t.shape[0] // N_MICROBATCH) + t.shape[1:])
    return _jnp.moveaxis(t, 1, axis + 1)


def setup_inputs(seed: int = 0) -> dict:
    inp = _fwd_setup_inputs(seed)
    key = _jax.random.fold_in(_jax.random.key(seed), 7919)
    shape, _ = _output_shape()
    out = dict(inp)
    out["loss_target"] = _jax.random.normal(_jax.random.fold_in(key, 0), shape, _jnp.float32)
    for i, name in enumerate(TWIN_WEIGHTS):
        w = inp[name].astype(_jnp.float32)
        if MOMENT_SCALE is None:
            s = _jnp.sqrt(_jnp.mean(_jnp.square(w)) + 1e-30)
        else:
            s = MOMENT_SCALE[name]
        km, kv = _jax.random.split(_jax.random.fold_in(key, i + 1))
        out[name] = w
        out["m_" + name] = s * _jax.random.normal(km, w.shape, _jnp.float32)
        out["v_" + name] = (s * s) * _jax.random.uniform(kv, w.shape, _jnp.float32, 0.5, 1.5)
    if N_MICROBATCH > 1:
        for name, axis in PER_EXAMPLE_BATCH_AXIS.items():
            out[name] = _to_microbatches(out[name], axis)
    return {'x': out['x'], 'positions': out['positions'], 'ffn1_norm': out['ffn1_norm'], 'ffn1_w_gate': out['ffn1_w_gate'], 'ffn1_w_up': out['ffn1_w_up'], 'ffn1_w_down': out['ffn1_w_down'], 'mix_norm': out['mix_norm'], 'w_in': out['w_in'], 'ssm_log_dt': out['ssm_log_dt'], 'ssm_a_re': out['ssm_a_re'], 'ssm_a_im': out['ssm_a_im'], 'ssm_b_re': out['ssm_b_re'], 'ssm_b_im': out['ssm_b_im'], 'ssm_c_re': out['ssm_c_re'], 'ssm_c_im': out['ssm_c_im'], 'ssm_d': out['ssm_d'], 'ssm_w_glu': out['ssm_w_glu'], 'ssm_b_glu': out['ssm_b_glu'], 'swa_sinks': out['swa_sinks'], 'mla_q_norm': out['mla_q_norm'], 'mla_w_uq': out['mla_w_uq'], 'mla_kv_norm': out['mla_kv_norm'], 'mla_w_ukv': out['mla_w_ukv'], 'out_norm': out['out_norm'], 'w_out': out['w_out'], 'ffn2_norm': out['ffn2_norm'], 'ffn2_w_gate': out['ffn2_w_gate'], 'ffn2_w_up': out['ffn2_w_up'], 'ffn2_w_down': out['ffn2_w_down'], 'final_norm': out['final_norm'], 'loss_target': out['loss_target'], 'm_ffn1_norm': out['m_ffn1_norm'], 'm_ffn1_w_gate': out['m_ffn1_w_gate'], 'm_ffn1_w_up': out['m_ffn1_w_up'], 'm_ffn1_w_down': out['m_ffn1_w_down'], 'm_mix_norm': out['m_mix_norm'], 'm_w_in': out['m_w_in'], 'm_ssm_log_dt': out['m_ssm_log_dt'], 'm_ssm_a_re': out['m_ssm_a_re'], 'm_ssm_a_im': out['m_ssm_a_im'], 'm_ssm_b_re': out['m_ssm_b_re'], 'm_ssm_b_im': out['m_ssm_b_im'], 'm_ssm_c_re': out['m_ssm_c_re'], 'm_ssm_c_im': out['m_ssm_c_im'], 'm_ssm_d': out['m_ssm_d'], 'm_ssm_w_glu': out['m_ssm_w_glu'], 'm_ssm_b_glu': out['m_ssm_b_glu'], 'm_swa_sinks': out['m_swa_sinks'], 'm_mla_q_norm': out['m_mla_q_norm'], 'm_mla_w_uq': out['m_mla_w_uq'], 'm_mla_kv_norm': out['m_mla_kv_norm'], 'm_mla_w_ukv': out['m_mla_w_ukv'], 'm_out_norm': out['m_out_norm'], 'm_w_out': out['m_w_out'], 'm_ffn2_norm': out['m_ffn2_norm'], 'm_ffn2_w_gate': out['m_ffn2_w_gate'], 'm_ffn2_w_up': out['m_ffn2_w_up'], 'm_ffn2_w_down': out['m_ffn2_w_down'], 'm_final_norm': out['m_final_norm'], 'v_ffn1_norm': out['v_ffn1_norm'], 'v_ffn1_w_gate': out['v_ffn1_w_gate'], 'v_ffn1_w_up': out['v_ffn1_w_up'], 'v_ffn1_w_down': out['v_ffn1_w_down'], 'v_mix_norm': out['v_mix_norm'], 'v_w_in': out['v_w_in'], 'v_ssm_log_dt': out['v_ssm_log_dt'], 'v_ssm_a_re': out['v_ssm_a_re'], 'v_ssm_a_im': out['v_ssm_a_im'], 'v_ssm_b_re': out['v_ssm_b_re'], 'v_ssm_b_im': out['v_ssm_b_im'], 'v_ssm_c_re': out['v_ssm_c_re'], 'v_ssm_c_im': out['v_ssm_c_im'], 'v_ssm_d': out['v_ssm_d'], 'v_ssm_w_glu': out['v_ssm_w_glu'], 'v_ssm_b_glu': out['v_ssm_b_glu'], 'v_swa_sinks': out['v_swa_sinks'], 'v_mla_q_norm': out['v_mla_q_norm'], 'v_mla_w_uq': out['v_mla_w_uq'], 'v_mla_kv_norm': out['v_mla_kv_norm'], 'v_mla_w_ukv': out['v_mla_w_ukv'], 'v_out_norm': out['v_out_norm'], 'v_w_out': out['v_w_out'], 'v_ffn2_norm': out['v_ffn2_norm'], 'v_ffn2_w_gate': out['v_ffn2_w_gate'], 'v_ffn2_w_up': out['v_ffn2_w_up'], 'v_ffn2_w_down': out['v_ffn2_w_down'], 'v_final_norm': out['v_final_norm']}


def _loss(weights, diff, rest, loss_target):
    with _jax.named_scope("forward"):
        args = {**rest, TWIN_DIFF_INPUT: diff, **{k: w.astype(_WEIGHT_DTYPES[k]) for k, w in weights.items()}}
        y = _forward(args)
    with _jax.named_scope("loss_head"):
        err = _jnp.square(y.astype(_jnp.float32) - loss_target)
        return 0.5 * _jnp.sum(_jnp.mean(err, axis=-1)) if err.ndim else 0.5 * err


def _adamw(w, g, m, v):
    m = ADAM_B1 * m + (1.0 - ADAM_B1) * g
    v = ADAM_B2 * v + (1.0 - ADAM_B2) * _jnp.square(g)
    m_hat = m / (1.0 - ADAM_B1 ** ADAM_STEP)
    v_hat = v / (1.0 - ADAM_B2 ** ADAM_STEP)
    delta = -ADAM_LR * (m_hat / (_jnp.sqrt(v_hat) + ADAM_EPS) + ADAM_WD * w)
    return delta, m, v


def reference(x, positions, ffn1_norm, ffn1_w_gate, ffn1_w_up, ffn1_w_down, mix_norm, w_in, ssm_log_dt, ssm_a_re, ssm_a_im, ssm_b_re, ssm_b_im, ssm_c_re, ssm_c_im, ssm_d, ssm_w_glu, ssm_b_glu, swa_sinks, mla_q_norm, mla_w_uq, mla_kv_norm, mla_w_ukv, out_norm, w_out, ffn2_norm, ffn2_w_gate, ffn2_w_up, ffn2_w_down, final_norm, loss_target, m_ffn1_norm, m_ffn1_w_gate, m_ffn1_w_up, m_ffn1_w_down, m_mix_norm, m_w_in, m_ssm_log_dt, m_ssm_a_re, m_ssm_a_im, m_ssm_b_re, m_ssm_b_im, m_ssm_c_re, m_ssm_c_im, m_ssm_d, m_ssm_w_glu, m_ssm_b_glu, m_swa_sinks, m_mla_q_norm, m_mla_w_uq, m_mla_kv_norm, m_mla_w_ukv, m_out_norm, m_w_out, m_ffn2_norm, m_ffn2_w_gate, m_ffn2_w_up, m_ffn2_w_down, m_final_norm, v_ffn1_norm, v_ffn1_w_gate, v_ffn1_w_up, v_ffn1_w_down, v_mix_norm, v_w_in, v_ssm_log_dt, v_ssm_a_re, v_ssm_a_im, v_ssm_b_re, v_ssm_b_im, v_ssm_c_re, v_ssm_c_im, v_ssm_d, v_ssm_w_glu, v_ssm_b_glu, v_swa_sinks, v_mla_q_norm, v_mla_w_uq, v_mla_kv_norm, v_mla_w_ukv, v_out_norm, v_w_out, v_ffn2_norm, v_ffn2_w_gate, v_ffn2_w_up, v_ffn2_w_down, v_final_norm):
    given = dict(x=x, positions=positions, ffn1_norm=ffn1_norm, ffn1_w_gate=ffn1_w_gate, ffn1_w_up=ffn1_w_up, ffn1_w_down=ffn1_w_down, mix_norm=mix_norm, w_in=w_in, ssm_log_dt=ssm_log_dt, ssm_a_re=ssm_a_re, ssm_a_im=ssm_a_im, ssm_b_re=ssm_b_re, ssm_b_im=ssm_b_im, ssm_c_re=ssm_c_re, ssm_c_im=ssm_c_im, ssm_d=ssm_d, ssm_w_glu=ssm_w_glu, ssm_b_glu=ssm_b_glu, swa_sinks=swa_sinks, mla_q_norm=mla_q_norm, mla_w_uq=mla_w_uq, mla_kv_norm=mla_kv_norm, mla_w_ukv=mla_w_ukv, out_norm=out_norm, w_out=w_out, ffn2_norm=ffn2_norm, ffn2_w_gate=ffn2_w_gate, ffn2_w_up=ffn2_w_up, ffn2_w_down=ffn2_w_down, final_norm=final_norm, loss_target=loss_target, m_ffn1_norm=m_ffn1_norm, m_ffn1_w_gate=m_ffn1_w_gate, m_ffn1_w_up=m_ffn1_w_up, m_ffn1_w_down=m_ffn1_w_down, m_mix_norm=m_mix_norm, m_w_in=m_w_in, m_ssm_log_dt=m_ssm_log_dt, m_ssm_a_re=m_ssm_a_re, m_ssm_a_im=m_ssm_a_im, m_ssm_b_re=m_ssm_b_re, m_ssm_b_im=m_ssm_b_im, m_ssm_c_re=m_ssm_c_re, m_ssm_c_im=m_ssm_c_im, m_ssm_d=m_ssm_d, m_ssm_w_glu=m_ssm_w_glu, m_ssm_b_glu=m_ssm_b_glu, m_swa_sinks=m_swa_sinks, m_mla_q_norm=m_mla_q_norm, m_mla_w_uq=m_mla_w_uq, m_mla_kv_norm=m_mla_kv_norm, m_mla_w_ukv=m_mla_w_ukv, m_out_norm=m_out_norm, m_w_out=m_w_out, m_ffn2_norm=m_ffn2_norm, m_ffn2_w_gate=m_ffn2_w_gate, m_ffn2_w_up=m_ffn2_w_up, m_ffn2_w_down=m_ffn2_w_down, m_final_norm=m_final_norm, v_ffn1_norm=v_ffn1_norm, v_ffn1_w_gate=v_ffn1_w_gate, v_ffn1_w_up=v_ffn1_w_up, v_ffn1_w_down=v_ffn1_w_down, v_mix_norm=v_mix_norm, v_w_in=v_w_in, v_ssm_log_dt=v_ssm_log_dt, v_ssm_a_re=v_ssm_a_re, v_ssm_a_im=v_ssm_a_im, v_ssm_b_re=v_ssm_b_re, v_ssm_b_im=v_ssm_b_im, v_ssm_c_re=v_ssm_c_re, v_ssm_c_im=v_ssm_c_im, v_ssm_d=v_ssm_d, v_ssm_w_glu=v_ssm_w_glu, v_ssm_b_glu=v_ssm_b_glu, v_swa_sinks=v_swa_sinks, v_mla_q_norm=v_mla_q_norm, v_mla_w_uq=v_mla_w_uq, v_mla_kv_norm=v_mla_kv_norm, v_mla_w_ukv=v_mla_w_ukv, v_out_norm=v_out_norm, v_w_out=v_w_out, v_ffn2_norm=v_ffn2_norm, v_ffn2_w_gate=v_ffn2_w_gate, v_ffn2_w_up=v_ffn2_w_up, v_ffn2_w_down=v_ffn2_w_down, v_final_norm=v_final_norm)
    weights = {n: given[n] for n in TWIN_WEIGHTS}
    shared = {n: given[n] for n in SHARED_INPUTS}
    per_example = {n: given[n] for n in ['x', 'positions']}
    grad_fn = _jax.value_and_grad(_loss, argnums=(0, 1))

    def one_microbatch(ex, loss_target):
        ex = dict(ex)
        diff = ex.pop(TWIN_DIFF_INPUT)
        return grad_fn(weights, diff, {**shared, **ex}, loss_target)

    if N_MICROBATCH == 1:
        loss, (grad_w, grad_x) = one_microbatch(per_example, given["loss_target"])
    else:
        def body(carry, xs):
            loss_sum, grad_sum = carry
            l_k, (gw_k, gx_k) = one_microbatch(xs[0], xs[1])
            with _jax.named_scope("update"):
                return (loss_sum + l_k, _jax.tree.map(_jnp.add, grad_sum, gw_k)), gx_k

        init = (_jnp.zeros((), _jnp.float32), _jax.tree.map(_jnp.zeros_like, weights))
        (loss, grad_w), grad_x = _jax.lax.scan(body, init, (per_example, given["loss_target"]))
    with _jax.named_scope("update"):
        delta_w, new_m, new_v = {}, {}, {}
        for n in TWIN_WEIGHTS:
            delta_w[n], new_m[n], new_v[n] = _adamw(weights[n], grad_w[n], given["m_" + n], given["v_" + n])
    return (loss, grad_x, *[grad_w[n] for n in TWIN_WEIGHTS], *[delta_w[n] for n in TWIN_WEIGHTS],
            *[new_m[n] for n in TWIN_WEIGHTS], *[new_v[n] for n in TWIN_WEIGHTS])
```

```python
import functools
import math

import jax
import jax.numpy as jnp
import numpy as np
from jax import lax
from jax.experimental import pallas as pl
from jax.experimental.pallas import tpu as pltpu

F32 = jnp.float32
BF16 = jnp.bfloat16

EPS = 1e-6
ROPE_THETA = 10000.0
SSM_GROUP = 16
SSM_STATE = 64
SWA_HEADS = 8
SWA_KV_HEADS = 2
SWA_HEAD_DIM = 64
SWA_BLOCK = 128
SWA_WIDTH = SWA_HEADS * SWA_HEAD_DIM
SWA_KV_WIDTH = SWA_KV_HEADS * SWA_HEAD_DIM
MLA_HEADS = 8
MLA_Q_RANK = 512
MLA_KV_RANK = 256
MLA_NOPE = 128
MLA_ROPE = 64
MLA_V = 128
MLA_SLOT = 256
MLA_WIDTH = MLA_HEADS * MLA_V
ROPE_HALF = 32

ADAM_LR = 0.001
ADAM_B1 = 0.9
ADAM_B2 = 0.999
ADAM_EPS = 1e-08
ADAM_WD = 0.01
ADAM_STEP = 10

N_SHARD = 4
MESH = pl.DeviceIdType.MESH

NN = ((1,), (0,))
NT = ((1,), (1,))
TN = ((0,), (0,))


def _pcall(body, **kw):
    return pl.pallas_call(body, **kw)


def _tile(n, want, align=16):
    if n <= want:
        return n
    t = want - want % align
    while t >= align:
        if n % t == 0:
            return t
        t -= align
    return n


def _mm(name, grid, pairs, acc_shapes, outs, epilogue, extras=()):
    nk = grid[2]
    n_p, n_e, n_o, n_a = len(pairs), len(extras), len(outs), len(acc_shapes)
    dims_idx = [(p[4], p[5]) for p in pairs]

    def body(*refs):
        ab = refs[:2 * n_p]
        ex = refs[2 * n_p:2 * n_p + n_e]
        o = refs[2 * n_p + n_e:2 * n_p + n_e + n_o]
        accs = refs[2 * n_p + n_e + n_o:]

        def partial_sums():
            sums = [None] * n_a
            for p, (dims, ai) in enumerate(dims_idx):
                a = ab[2 * p][...].astype(BF16)
                b = ab[2 * p + 1][...].astype(BF16)
                d = lax.dot_general(a, b, (dims, ((), ())), preferred_element_type=F32)
                sums[ai] = d if sums[ai] is None else sums[ai] + d
            return sums

        def finish(vals):
            res = epilogue(vals, [e[...] for e in ex])
            for r, oref in zip(res, o):
                oref[...] = r.astype(oref.dtype)

        if nk == 1:
            finish(partial_sums())
        else:
            k = pl.program_id(2)

            @pl.when(k == 0)
            def _():
                for acc in accs:
                    acc[...] = jnp.zeros_like(acc)

            for acc, s in zip(accs, partial_sums()):
                acc[...] += s

            @pl.when(k == nk - 1)
            def _():
                finish([acc[...] for acc in accs])

    in_arrays, in_specs = [], []
    for a, a_spec, b, b_spec, _, _ in pairs:
        in_arrays += [a, b]
        in_specs += [a_spec, b_spec]
    for e, e_spec in extras:
        in_arrays.append(e)
        in_specs.append(e_spec)
    res = _pcall(
        body, name=name, grid=grid, in_specs=in_specs,
        out_specs=[o[2] for o in outs],
        out_shape=[jax.ShapeDtypeStruct(o[0], o[1]) for o in outs],
        scratch_shapes=[] if nk == 1 else [pltpu.VMEM(s, F32) for s in acc_shapes],
        compiler_params=pltpu.CompilerParams(dimension_semantics=("parallel", "parallel", "arbitrary")),
    )(*in_arrays)
    return res


def _bs(shape, fn):
    return pl.BlockSpec(shape, fn)


def _mm_simple(name, a, b, dims, *, tm=512, tn=512, tk=512, out_dtype=F32, epilogue=None, extras=(), scale=None):
    if dims == NN:
        (M, K), N = a.shape, b.shape[1]
    elif dims == NT:
        (M, K), N = a.shape, b.shape[0]
    else:
        (K, M), N = a.shape, b.shape[1]
    tm, tn, tk = _tile(M, tm, 128), _tile(N, tn, 128), _tile(K, tk, 128)
    if dims == NN:
        a_spec, b_spec = _bs((tm, tk), lambda i, j, k: (i, k)), _bs((tk, tn), lambda i, j, k: (k, j))
    elif dims == NT:
        a_spec, b_spec = _bs((tm, tk), lambda i, j, k: (i, k)), _bs((tn, tk), lambda i, j, k: (j, k))
    else:
        a_spec, b_spec = _bs((tk, tm), lambda i, j, k: (k, i)), _bs((tk, tn), lambda i, j, k: (k, j))
    ex = []
    for e in extras:
        if e.shape[0] == 1:
            ex.append((e, _bs((1, tn), lambda i, j, k: (0, j))))
        else:
            ex.append((e, _bs((tm, tn), lambda i, j, k: (i, j))))
    if epilogue is None:
        if scale is None:
            epilogue = lambda accs, ex_: (accs[0],)
        else:
            epilogue = lambda accs, ex_: (accs[0] * scale,)
        out_dtypes = (out_dtype,)
    else:
        out_dtypes = out_dtype if isinstance(out_dtype, tuple) else (out_dtype,)
    outs = [((M, N), dt, _bs((tm, tn), lambda i, j, k: (i, j))) for dt in out_dtypes]
    res = _mm(name, (M // tm, N // tn, K // tk), [(a, a_spec, b, b_spec, dims, 0)], [(tm, tn)], outs, epilogue, ex)
    return res[0] if len(res) == 1 else res


def _rowwise(name, fn, tiles, bcasts, outs, accs=(), *, tm=256):
    rows = tiles[0].shape[0]
    tm = _tile(rows, tm)
    n_t, n_b, n_o, n_a = len(tiles), len(bcasts), len(outs), len(accs)

    def body(*refs):
        t = [r[...] for r in refs[:n_t]]
        b = [r[...] for r in refs[n_t:n_t + n_b]]
        o = refs[n_t + n_b:n_t + n_b + n_o]
        a = refs[n_t + n_b + n_o:]
        ov, av = fn(t, b)
        for r, val in zip(o, ov):
            r[...] = val.astype(r.dtype)
        if n_a:
            @pl.when(pl.program_id(0) == 0)
            def _():
                for r in a:
                    r[...] = jnp.zeros_like(r)
            for r, val in zip(a, av):
                r[...] += val

    in_specs = [_bs((tm, x.shape[1]), lambda i: (i, 0)) for x in tiles]
    in_specs += [_bs(x.shape, lambda i, nd=x.ndim: (0,) * nd) for x in bcasts]
    out_specs = [_bs((tm, w), lambda i: (i, 0)) for w, _ in outs]
    out_specs += [_bs((1, w), lambda i: (0, 0)) for w in accs]
    out_shape = [jax.ShapeDtypeStruct((rows, w), dt) for w, dt in outs]
    out_shape += [jax.ShapeDtypeStruct((1, w), F32) for w in accs]
    return _pcall(
        body, name=name, grid=(rows // tm,), in_specs=in_specs, out_specs=out_specs, out_shape=out_shape,
        compiler_params=pltpu.CompilerParams(dimension_semantics=("arbitrary",)),
    )(*tiles, *bcasts)


def _colsum(v):
    return jnp.sum(v, axis=0, keepdims=True)


def _rms_fwd(name, x, gain, out_dtype=BF16):
    def fn(t, b):
        xv = t[0]
        r = lax.rsqrt(jnp.mean(xv * xv, axis=-1, keepdims=True) + EPS)
        return (xv * r * b[0], r), ()
    w = x.shape[1]
    return _rowwise(name, fn, [x], [gain.reshape(1, w)], [(w, out_dtype), (1, F32)])


def _rms_bwd(name, dh, x, rstd, gain, dres=None):
    def fn(t, b):
        dhv, xv, r = t[0], t[1], t[2]
        xh = xv * r
        dxh = dhv * b[0]
        dx = r * (dxh - xh * jnp.mean(dxh * xh, axis=-1, keepdims=True))
        if dres is not None:
            dx = dx + t[3]
        return (dx,), (_colsum(dhv * xh),)
    w = x.shape[1]
    tiles = [dh, x, rstd] + ([dres] if dres is not None else [])
    return _rowwise(name, fn, tiles, [gain.reshape(1, w)], [(w, F32)], [w])


def _sigmoid(v):
    return 1.0 / (1.0 + jnp.exp(-v))


def _ffn_fwd(tag, x, gain, wg2, wu2, wd):
    T, D = x.shape
    fs = wg2.shape[1]
    h, rstd = _rms_fwd(tag + "_rms", x, gain)
    tm = _tile(T, 256)
    a_spec = _bs((tm, D), lambda i, j, k: (j, 0))
    w_spec = _bs((D, fs), lambda i, j, k: (i, 0))
    o_spec = _bs((tm, fs), lambda i, j, k: (j, i))

    def epi(accs, ex):
        a, b = accs
        return a, b, a * _sigmoid(a) * b

    a, b, t = _mm(
        tag + "_up", (N_SHARD, T // tm, 1),
        [(h, a_spec, wg2, w_spec, NN, 0), (h, a_spec, wu2, w_spec, NN, 1)],
        [(tm, fs), (tm, fs)],
        [((T, N_SHARD * fs), F32, o_spec), ((T, N_SHARD * fs), F32, o_spec), ((T, N_SHARD * fs), BF16, o_spec)],
        epi)
    y = _mm_simple(tag + "_down", t, wd, NN, tm=512, tn=D, tk=fs // 2 if fs % 256 == 0 else fs,
                   epilogue=lambda accs, ex: (ex[0] + 0.5 * accs[0],), extras=[x])
    return y, (x, h, rstd, a, b, t)


def _ffn_bwd(tag, dy, saved, gain, wg2, wu2, wd):
    x, h, rstd, a, b, t = saved
    T, D = x.shape
    fs = wg2.shape[1]
    F = N_SHARD * fs
    dwd = _mm_simple(tag + "_dwd", t, dy, TN, tm=fs // 2 if fs % 16 == 0 else fs, tn=D, tk=512, scale=0.5)

    def epi(accs, ex):
        dt = 0.5 * accs[0]
        av, bv = ex
        sig = _sigmoid(av)
        da = dt * bv * (sig * (1.0 + av * (1.0 - sig)))
        db = dt * (av * sig)
        return da, db

    tn = 512 if F % 512 == 0 else (256 if F % 256 == 0 else 128)
    da, db = _mm_simple(tag + "_dt", dy, wd, NT, tm=512, tn=tn, tk=D, out_dtype=(BF16, BF16), epilogue=epi,
                        extras=[a, b])
    tm = _tile(D, 1024)
    per = D // tm
    tk = _tile(T, 512)
    h_spec = _bs((tk, tm), lambda i, j, k: (k, i % per))
    d_spec = _bs((tk, fs), lambda i, j, k: (k, i // per))
    o_spec = _bs((tm, fs), lambda i, j, k: (i, 0))
    dwg2, dwu2 = _mm(
        tag + "_dwgu", (N_SHARD * per, 1, T // tk),
        [(h, h_spec, da, d_spec, TN, 0), (h, h_spec, db, d_spec, TN, 1)],
        [(tm, fs), (tm, fs)],
        [((N_SHARD * D, fs), F32, o_spec), ((N_SHARD * D, fs), F32, o_spec)],
        lambda accs, ex: tuple(accs))
    tm2 = _tile(T, 512)
    tn2 = _tile(D, 1024)
    per2 = D // tn2
    g_spec = _bs((tm2, fs), lambda i, j, k: (i, k))
    w_spec = _bs((tn2, fs), lambda i, j, k: (k * per2 + j, 0))
    o2 = _bs((tm2, tn2), lambda i, j, k: (i, j))
    dh, = _mm(
        tag + "_dh", (T // tm2, per2, N_SHARD),
        [(da, g_spec, wg2, w_spec, NT, 0), (db, g_spec, wu2, w_spec, NT, 0)],
        [(tm2, tn2)], [((T, D), F32, o2)], lambda accs, ex: (accs[0],))
    dx, dgain = _rms_bwd(tag + "_rmsb", dh, x, rstd, gain, dres=dy)
    return dx, dgain, dwg2, dwu2, dwd


def _rope_tables(positions):
    inv_freq = ROPE_THETA ** (-jnp.arange(0, 2 * ROPE_HALF, 2, dtype=F32) / (2 * ROPE_HALF))
    ang = positions.astype(F32)[:, None] * inv_freq
    c, s = jnp.cos(ang), jnp.sin(ang)
    c64 = jnp.concatenate([c, c], axis=1)
    s64 = jnp.concatenate([-s, s], axis=1)
    T = positions.shape[0]
    one, zero = jnp.ones((T, MLA_NOPE), F32), jnp.zeros((T, MLA_NOPE), F32)
    pad1, pad0 = jnp.ones((T, MLA_SLOT - MLA_NOPE - MLA_ROPE), F32), jnp.zeros((T, MLA_SLOT - MLA_NOPE - MLA_ROPE), F32)
    c256 = jnp.concatenate([one, c64, pad1], axis=1)
    s256 = jnp.concatenate([zero, s64, pad0], axis=1)
    return c64, s64, c256, s256


def _half_swap(v):
    w = v.shape[1]
    lane = lax.broadcasted_iota(jnp.int32, v.shape, 1)
    first = (lane % (2 * ROPE_HALF)) < ROPE_HALF
    return jnp.where(first, pltpu.roll(v, w - ROPE_HALF, 1), pltpu.roll(v, ROPE_HALF, 1))


def _rope(name, x, ctab, stab, backward=False):
    reps = x.shape[1] // ctab.shape[1]

    def fn(t, b):
        xv = t[0]
        c = jnp.tile(t[1], (1, reps)) if reps > 1 else t[1]
        s = jnp.tile(t[2], (1, reps)) if reps > 1 else t[2]
        if backward:
            return (xv * c + _half_swap(xv * s),), ()
        return (xv * c + _half_swap(xv) * s,), ()
    return _rowwise(name, fn, [x, ctab, stab], [], [(x.shape[1], F32)])[0]


def _s5_prep(log_dt, a_re, a_im, b_re, b_im, c_re, c_im):
    G, P = a_re.shape
    C = b_re.shape[-1]
    dt = jnp.exp(log_dt)[:, None]
    mag = jnp.exp(a_re * dt)
    abar_r = mag * jnp.cos(a_im * dt)
    abar_i = mag * jnp.sin(a_im * dt)
    den = a_re * a_re + a_im * a_im
    nr = abar_r - 1.0
    qr = (nr * a_re + abar_i * a_im) / den
    qi = (abar_i * a_re - nr * a_im) / den
    bbar_r = qr[..., None] * b_re - qi[..., None] * b_im
    bbar_i = qr[..., None] * b_im + qi[..., None] * b_re
    eye = jnp.eye(G, dtype=F32)
    b_r = jnp.einsum('gpc,gh->gchp', bbar_r, eye).reshape(G * C, G * P)
    b_i = jnp.einsum('gpc,gh->gchp', bbar_i, eye).reshape(G * C, G * P)
    c_r = jnp.einsum('gcp,gh->gphc', c_re, eye).reshape(G * P, G * C)
    c_i = jnp.einsum('gcp,gh->gphc', c_im, eye).reshape(G * P, G * C)
    return abar_r.reshape(1, G * P), abar_i.reshape(1, G * P), b_r, b_i, c_r, -c_i


def _scan_lanes(gp):
    return 512 if gp % 512 == 0 else (256 if gp % 256 == 0 else 128)


def _scan_fwd(name, bu_r, bu_i, a_r, a_i):
    T, gp = bu_r.shape
    tl = _scan_lanes(gp)
    tc = _tile(T, 256)

    def body(br_ref, bi_ref, ar_ref, ai_ref, xr_ref, xi_ref, pr_ref, pi_ref, sr, si):
        @pl.when(pl.program_id(1) == 0)
        def _():
            sr[...] = jnp.zeros_like(sr)
            si[...] = jnp.zeros_like(si)
        ar, ai = ar_ref[...], ai_ref[...]

        def step(g, carry):
            xr, xi = carry
            base = pl.multiple_of(g * 8, 8)
            b_r = br_ref[pl.ds(base, 8), :]
            b_i = bi_ref[pl.ds(base, 8), :]
            rows_r, rows_i, prev_r, prev_i = [], [], [], []
            for j in range(8):
                prev_r.append(xr)
                prev_i.append(xi)
                nr = ar * xr - ai * xi + b_r[j:j + 1, :]
                ni = ar * xi + ai * xr + b_i[j:j + 1, :]
                xr, xi = nr, ni
                rows_r.append(xr)
                rows_i.append(xi)
            xr_ref[pl.ds(base, 8), :] = jnp.concatenate(rows_r, axis=0)
            xi_ref[pl.ds(base, 8), :] = jnp.concatenate(rows_i, axis=0)
            pr_ref[pl.ds(base, 8), :] = jnp.concatenate(prev_r, axis=0)
            pi_ref[pl.ds(base, 8), :] = jnp.concatenate(prev_i, axis=0)
            return xr, xi

        xr, xi = lax.fori_loop(0, tc // 8, step, (sr[...], si[...]))
        sr[...] = xr
        si[...] = xi

    spec = _bs((tc, tl), lambda l, t: (t, l))
    a_spec = _bs((1, tl), lambda l, t: (0, l))
    return _pcall(
        body, name=name, grid=(gp // tl, T // tc),
        in_specs=[spec, spec, a_spec, a_spec], out_specs=[spec] * 4,
        out_shape=[jax.ShapeDtypeStruct((T, gp), F32)] * 4,
        scratch_shapes=[pltpu.VMEM((1, tl), F32), pltpu.VMEM((1, tl), F32)],
        compiler_params=pltpu.CompilerParams(dimension_semantics=("parallel", "arbitrary")),
    )(bu_r, bu_i, a_r, a_i)


def _scan_bwd(name, g_r, g_i, p_r, p_i, a_r, a_i):
    T, gp = g_r.shape
    tl = _scan_lanes(gp)
    tc = _tile(T, 256)
    nt = T // tc
    ng = tc // 8

    def body(gr_ref, gi_ref, pr_ref, pi_ref, ar_ref, ai_ref, lr_ref, li_ref, dar_ref, dai_ref, sr, si, accr, acci):
        t = pl.program_id(1)

        @pl.when(t == 0)
        def _():
            sr[...] = jnp.zeros_like(sr)
            si[...] = jnp.zeros_like(si)
            accr[...] = jnp.zeros_like(accr)
            acci[...] = jnp.zeros_like(acci)
        ar, ai = ar_ref[...], ai_ref[...]

        def step(kk, carry):
            lr, li = carry
            base = pl.multiple_of((ng - 1 - kk) * 8, 8)
            gr8 = gr_ref[pl.ds(base, 8), :]
            gi8 = gi_ref[pl.ds(base, 8), :]
            rows_r, rows_i = [None] * 8, [None] * 8
            for j in range(7, -1, -1):
                nr = gr8[j:j + 1, :] + ar * lr + ai * li
                ni = gi8[j:j + 1, :] - ai * lr + ar * li
                lr, li = nr, ni
                rows_r[j] = lr
                rows_i[j] = li
            lam_r = jnp.concatenate(rows_r, axis=0)
            lam_i = jnp.concatenate(rows_i, axis=0)
            lr_ref[pl.ds(base, 8), :] = lam_r
            li_ref[pl.ds(base, 8), :] = lam_i
            pr8 = pr_ref[pl.ds(base, 8), :]
            pi8 = pi_ref[pl.ds(base, 8), :]
            accr[...] += lam_r * pr8 + lam_i * pi8
            acci[...] += lam_i * pr8 - lam_r * pi8
            return lr, li

        lr, li = lax.fori_loop(0, ng, step, (sr[...], si[...]))
        sr[...] = lr
        si[...] = li

        @pl.when(t == nt - 1)
        def _():
            dar_ref[...] = jnp.sum(accr[...], axis=0, keepdims=True)
            dai_ref[...] = jnp.sum(acci[...], axis=0, keepdims=True)

    spec = _bs((tc, tl), lambda l, t: (nt - 1 - t, l))
    a_spec = _bs((1, tl), lambda l, t: (0, l))
    return _pcall(
        body, name=name, grid=(gp // tl, nt),
        in_specs=[spec] * 4 + [a_spec, a_spec], out_specs=[spec, spec, a_spec, a_spec],
        out_shape=[jax.ShapeDtypeStruct((T, gp), F32)] * 2 + [jax.ShapeDtypeStruct((1, gp), F32)] * 2,
        scratch_shapes=[pltpu.VMEM((1, tl), F32), pltpu.VMEM((1, tl), F32),
                        pltpu.VMEM((8, tl), F32), pltpu.VMEM((8, tl), F32)],
        compiler_params=pltpu.CompilerParams(dimension_semantics=("parallel", "arbitrary")),
    )(g_r, g_i, p_r, p_i, a_r, a_i)


_GELU_C = math.sqrt(2.0 / math.pi)


def _gelu(v):
    return 0.5 * v * (1.0 + jnp.tanh(_GELU_C * (v + 0.044715 * v * v * v)))


def _gelu_grad(v):
    th = jnp.tanh(_GELU_C * (v + 0.044715 * v * v * v))
    return 0.5 * (1.0 + th) + 0.5 * v * (1.0 - th * th) * _GELU_C * (1.0 + 3.0 * 0.044715 * v * v)


def _mm_shared_lhs(name, a, bs_, dims, *, tm, tn, tk):
    if dims == NN:
        (M, K), N = a.shape, bs_[0].shape[1]
    elif dims == NT:
        (M, K), N = a.shape, bs_[0].shape[0]
    else:
        (K, M), N = a.shape, bs_[0].shape[1]
    tm, tn, tk = _tile(M, tm, 128), _tile(N, tn, 128), _tile(K, tk, 128)
    if dims == NN:
        a_spec, b_spec = _bs((tm, tk), lambda i, j, k: (i, k)), _bs((tk, tn), lambda i, j, k: (k, j))
    elif dims == NT:
        a_spec, b_spec = _bs((tm, tk), lambda i, j, k: (i, k)), _bs((tn, tk), lambda i, j, k: (j, k))
    else:
        a_spec, b_spec = _bs((tk, tm), lambda i, j, k: (k, i)), _bs((tk, tn), lambda i, j, k: (k, j))
    o_spec = _bs((tm, tn), lambda i, j, k: (i, j))
    n = len(bs_)
    return _mm(name, (M // tm, N // tn, K // tk), [(a, a_spec, b, b_spec, dims, i) for i, b in enumerate(bs_)],
               [(tm, tn)] * n, [((M, N), F32, o_spec)] * n, lambda accs, ex: tuple(accs))


def _s5_fwd(tag, u, prep, d_skip, w_glu, b_glu):
    a_r, a_i, b_r, b_i, c_r, c_in = prep
    T, W = u.shape
    gp = a_r.shape[1]
    bu_r, bu_i = _mm_shared_lhs(tag + "_bu", u, [b_r, b_i], NN, tm=512, tn=1024, tk=W)
    x_r, x_i, p_r, p_i = _scan_fwd(tag + "_scan", bu_r, bu_i, a_r, a_i)
    tm, tk = _tile(T, 512), _tile(gp, 512)
    x_spec = _bs((tm, tk), lambda i, j, k: (i, k))
    c_spec = _bs((tk, W), lambda i, j, k: (k, 0))
    full = _bs((tm, W), lambda i, j, k: (i, 0))
    row = _bs((1, W), lambda i, j, k: (0, 0))

    def epi(accs, ex):
        y = accs[0] + ex[1] * ex[0]
        return y, _gelu(y)

    ypre, yg = _mm(tag + "_y", (T // tm, 1, gp // tk),
                   [(x_r, x_spec, c_r, c_spec, NN, 0), (x_i, x_spec, c_in, c_spec, NN, 0)],
                   [(tm, W)], [((T, W), F32, full)] * 2, epi, [(u, full), (d_skip.reshape(1, W), row)])

    def epi2(accs, ex):
        pre = accs[0] + ex[1]
        return ex[0] * _sigmoid(pre), pre

    out, pre = _mm_simple(tag + "_glu", yg, w_glu, NN, tm=512, tn=W, tk=W, out_dtype=(F32, F32), epilogue=epi2,
                          extras=[yg, b_glu.reshape(1, W)])
    return out, (u, x_r, x_i, p_r, p_i, ypre, yg, pre)


def _s5_bwd(tag, d_out, saved, prep, d_skip, w_glu):
    u, x_r, x_i, p_r, p_i, ypre, yg, pre = saved
    a_r, a_i, b_r, b_i, c_r, c_in = prep
    T, W = u.shape

    def gate_fn(t, b):
        gate = _sigmoid(t[2])
        dpre = t[0] * t[1] * gate * (1.0 - gate)
        return (dpre, t[0] * gate), (_colsum(dpre),)

    dpre, tmp, db_glu = _rowwise(tag + "_gateb", gate_fn, [d_out, yg, pre], [], [(W, F32), (W, F32)], [W])
    dw_glu = _mm_simple(tag + "_dwglu", yg, dpre, TN, tm=W, tn=W, tk=512)
    dy = _mm_simple(tag + "_dyg", dpre, w_glu, NT, tm=512, tn=W, tk=W,
                    epilogue=lambda accs, ex: ((accs[0] + ex[0]) * _gelu_grad(ex[1]),), extras=[tmp, ypre])
    dd, = _rowwise(tag + "_dd", lambda t, b: ((), (_colsum(t[0] * t[1]),)), [dy, u], [], [], [W])
    dx_r, dx_i = _mm_shared_lhs(tag + "_dx", dy, [c_r, c_in], NT, tm=512, tn=1024, tk=W)
    dc_r = _mm_simple(tag + "_dcr", x_r, dy, TN, tm=512, tn=W, tk=512)
    dc_in = _mm_simple(tag + "_dci", x_i, dy, TN, tm=512, tn=W, tk=512)
    lam_r, lam_i, da_r, da_i = _scan_bwd(tag + "_scanb", dx_r, dx_i, p_r, p_i, a_r, a_i)
    gp = a_r.shape[1]
    tm, tk = _tile(T, 512), _tile(gp, 512)
    l_spec = _bs((tm, tk), lambda i, j, k: (i, k))
    b_spec = _bs((W, tk), lambda i, j, k: (0, k))
    full = _bs((tm, W), lambda i, j, k: (i, 0))
    row = _bs((1, W), lambda i, j, k: (0, 0))
    du, = _mm(tag + "_du", (T // tm, 1, gp // tk),
              [(lam_r, l_spec, b_r, b_spec, NT, 0), (lam_i, l_spec, b_i, b_spec, NT, 0)],
              [(tm, W)], [((T, W), F32, full)], lambda accs, ex: (accs[0] + ex[1] * ex[0],),
              [(dy, full), (d_skip.reshape(1, W), row)])
    db_r, db_i = _mm_shared_lhs(tag + "_db", u, [lam_r, lam_i], TN, tm=W, tn=1024, tk=512)
    return du, (da_r, da_i, db_r, db_i, dc_r, dc_in), dd, dw_glu, db_glu


def _swa_mask(n):
    B = SWA_BLOCK
    r = lax.broadcasted_iota(jnp.int32, (B, 2 * B), 0)
    c = lax.broadcasted_iota(jnp.int32, (B, 2 * B), 1)
    d = r + B - c
    return (d >= 0) & (d < B) & ((n > 0) | (c >= B))


def _swa_specs(T):
    B = SWA_BLOCK
    cur = lambda w: _bs((B, w), lambda n: (n, 0))
    prev = lambda w: _bs((B, w), lambda n: (jnp.maximum(n - 1, 0), 0))
    return cur, prev


def _swa_fwd(name, q, k, v, sinks_b):
    T = q.shape[0]
    B = SWA_BLOCK
    scale = SWA_HEAD_DIM ** -0.5
    per_kv = SWA_HEADS // SWA_KV_HEADS

    def body(q_ref, kc_ref, kp_ref, vc_ref, vp_ref, s_ref, o_ref, l_ref):
        n = pl.program_id(0)
        kcat = jnp.concatenate([kp_ref[...], kc_ref[...]], axis=0).astype(BF16)
        vcat = jnp.concatenate([vp_ref[...], vc_ref[...]], axis=0).astype(BF16)
        mask = _swa_mask(n)
        lane = lax.broadcasted_iota(jnp.int32, (B, 128), 1)
        lo = lane < SWA_HEAD_DIM
        lse_out = jnp.zeros((B, 128), F32)
        for jb in range(SWA_HEADS // 2):
            qblk = q_ref[:, jb * 128:(jb + 1) * 128]
            h = (2 * jb) // per_kv
            half_h = lo if h == 0 else jnp.logical_not(lo)
            outs = []
            for e in range(2):
                j = 2 * jb + e
                qa = qblk if e == h else pltpu.roll(qblk, SWA_HEAD_DIM, 1)
                qm = jnp.where(half_h, qa, 0.0).astype(BF16)
                s = lax.dot_general(qm, kcat, (NT, ((), ())), preferred_element_type=F32) * scale
                s = jnp.where(mask, s, -jnp.inf)
                sk = s_ref[j:j + 1, 0:1]
                m = jnp.maximum(jnp.max(s, axis=1, keepdims=True), sk)
                ex = jnp.exp(s - m)
                den = jnp.sum(ex, axis=1, keepdims=True) + jnp.exp(sk - m)
                p = ex / den
                r = lax.dot_general(p.astype(BF16), vcat, (NN, ((), ())), preferred_element_type=F32)
                outs.append(r if e == h else pltpu.roll(r, SWA_HEAD_DIM, 1))
                lse_out = jnp.where(lane == j, m + jnp.log(den), lse_out)
            o_ref[:, jb * 128:(jb + 1) * 128] = jnp.where(lo, outs[0], outs[1])
        l_ref[...] = lse_out

    cur, prev = _swa_specs(T)
    return _pcall(
        body, name=name, grid=(T // B,),
        in_specs=[cur(SWA_WIDTH), cur(SWA_KV_WIDTH), prev(SWA_KV_WIDTH), cur(SWA_KV_WIDTH), prev(SWA_KV_WIDTH),
                  _bs(sinks_b.shape, lambda n: (0, 0))],
        out_specs=[cur(SWA_WIDTH), cur(128)],
        out_shape=[jax.ShapeDtypeStruct((T, SWA_WIDTH), F32), jax.ShapeDtypeStruct((T, 128), F32)],
        compiler_params=pltpu.CompilerParams(dimension_semantics=("parallel",)),
    )(q, k, k, v, v, sinks_b)


def _swa_bwd(name, q, k, v, sinks_b, o, lse, do):
    T = q.shape[0]
    B = SWA_BLOCK
    scale = SWA_HEAD_DIM ** -0.5
    per_kv = SWA_HEADS // SWA_KV_HEADS

    def body(q_ref, kc_ref, kp_ref, vc_ref, vp_ref, s_ref, o_ref, l_ref, do_ref,
             dq_ref, dkc_ref, dkp_ref, dvc_ref, dvp_ref, ds_ref):
        n = pl.program_id(0)

        @pl.when(n == 0)
        def _():
            ds_ref[...] = jnp.zeros_like(ds_ref)
        kcat = jnp.concatenate([kp_ref[...], kc_ref[...]], axis=0).astype(BF16)
        vcat = jnp.concatenate([vp_ref[...], vc_ref[...]], axis=0).astype(BF16)
        mask = _swa_mask(n)
        lane = lax.broadcasted_iota(jnp.int32, (B, 128), 1)
        lane1 = lax.broadcasted_iota(jnp.int32, (1, 128), 1)
        lo = lane < SWA_HEAD_DIM
        lblk = l_ref[...]
        dk = jnp.zeros((2 * B, 128), F32)
        dv = jnp.zeros((2 * B, 128), F32)
        dsink = jnp.zeros((1, 128), F32)
        for jb in range(SWA_HEADS // 2):
            sl = slice(jb * 128, (jb + 1) * 128)
            qblk, doblk = q_ref[:, sl], do_ref[:, sl]
            prod = doblk * o_ref[:, sl]
            h = (2 * jb) // per_kv
            half_h = lo if h == 0 else jnp.logical_not(lo)
            parts = []
            for e in range(2):
                j = 2 * jb + e
                half_e = lo if e == 0 else jnp.logical_not(lo)
                dsum = jnp.sum(jnp.where(half_e, prod, 0.0), axis=1, keepdims=True)
                lj = jnp.sum(jnp.where(lane == j, lblk, 0.0), axis=1, keepdims=True)
                qa = qblk if e == h else pltpu.roll(qblk, SWA_HEAD_DIM, 1)
                da = doblk if e == h else pltpu.roll(doblk, SWA_HEAD_DIM, 1)
                qm = jnp.where(half_h, qa, 0.0).astype(BF16)
                dm = jnp.where(half_h, da, 0.0).astype(BF16)
                s = lax.dot_general(qm, kcat, (NT, ((), ())), preferred_element_type=F32) * scale
                p = jnp.where(mask, jnp.exp(s - lj), 0.0)
                dp = lax.dot_general(dm, vcat, (NT, ((), ())), preferred_element_type=F32)
                dsb = (p * (dp - dsum) * scale).astype(BF16)
                dqa = lax.dot_general(dsb, kcat, (NN, ((), ())), preferred_element_type=F32)
                parts.append(dqa if e == h else pltpu.roll(dqa, SWA_HEAD_DIM, 1))
                dk = dk + lax.dot_general(dsb, qm, (TN, ((), ())), preferred_element_type=F32)
                dv = dv + lax.dot_general(p.astype(BF16), dm, (TN, ((), ())), preferred_element_type=F32)
                sk = s_ref[j:j + 1, 0:1]
                contrib = jnp.sum(jnp.exp(sk - lj) * dsum, axis=0, keepdims=True)
                dsink = jnp.where(lane1 == j, dsink - contrib, dsink)
            dq_ref[:, sl] = jnp.where(lo, parts[0], parts[1])
        dkp_ref[...] = dk[:B]
        dkc_ref[...] = dk[B:]
        dvp_ref[...] = dv[:B]
        dvc_ref[...] = dv[B:]
        ds_ref[...] += dsink

    cur, prev = _swa_specs(T)
    kv = jax.ShapeDtypeStruct((T, SWA_KV_WIDTH), F32)
    return _pcall(
        body, name=name, grid=(T // B,),
        in_specs=[cur(SWA_WIDTH), cur(SWA_KV_WIDTH), prev(SWA_KV_WIDTH), cur(SWA_KV_WIDTH), prev(SWA_KV_WIDTH),
                  _bs(sinks_b.shape, lambda n: (0, 0)), cur(SWA_WIDTH), cur(128), cur(SWA_WIDTH)],
        out_specs=[cur(SWA_WIDTH)] + [cur(SWA_KV_WIDTH)] * 4 + [_bs((1, 128), lambda n: (0, 0))],
        out_shape=[jax.ShapeDtypeStruct((T, SWA_WIDTH), F32), kv, kv, kv, kv, jax.ShapeDtypeStruct((1, 128), F32)],
        compiler_params=pltpu.CompilerParams(dimension_semantics=("arbitrary",)),
    )(q, k, k, v, v, sinks_b, o, lse, do)


def _shift_add(name, cur, prv):
    T, W = cur.shape
    B = SWA_BLOCK
    nb = T // B

    def body(c_ref, p_ref, o_ref):
        n = pl.program_id(0)
        o_ref[...] = c_ref[...] + jnp.where(n < nb - 1, p_ref[...], 0.0)

    return _pcall(
        body, name=name, grid=(nb,),
        in_specs=[_bs((B, W), lambda n: (n, 0)), _bs((B, W), lambda n: (jnp.minimum(n + 1, nb - 1), 0))],
        out_specs=_bs((B, W), lambda n: (n, 0)), out_shape=jax.ShapeDtypeStruct((T, W), F32),
        compiler_params=pltpu.CompilerParams(dimension_semantics=("parallel",)),
    )(cur, prv)


MLA_BLOCK = 512
_MLA_SCALE = (MLA_NOPE + MLA_ROPE) ** -0.5


def _causal(qi, ki, tb):
    r = lax.broadcasted_iota(jnp.int32, (tb, tb), 0) + qi * tb
    c = lax.broadcasted_iota(jnp.int32, (tb, tb), 1) + ki * tb
    return c <= r


def _mla_fwd(name, qcat, kcat, v):
    T = qcat.shape[0]
    tb = _tile(T, MLA_BLOCK)
    nb = T // tb

    def body(q_ref, k_ref, v_ref, o_ref, l_ref, m_sc, l_sc, acc):
        qi, ki = pl.program_id(1), pl.program_id(2)

        @pl.when(ki == 0)
        def _():
            m_sc[...] = jnp.full_like(m_sc, -jnp.inf)
            l_sc[...] = jnp.zeros_like(l_sc)
            acc[...] = jnp.zeros_like(acc)

        @pl.when(ki <= qi)
        def _():
            s = lax.dot_general(q_ref[...].astype(BF16), k_ref[...].astype(BF16), (NT, ((), ())),
                                preferred_element_type=F32) * _MLA_SCALE
            s = jnp.where(_causal(qi, ki, tb), s, -jnp.inf)
            m_prev = m_sc[...]
            m_new = jnp.maximum(m_prev, jnp.max(s, axis=1, keepdims=True))
            alpha = jnp.exp(m_prev - m_new)
            p = jnp.exp(s - m_new)
            l_sc[...] = alpha * l_sc[...] + jnp.sum(p, axis=1, keepdims=True)
            acc[...] = alpha * acc[...] + lax.dot_general(p.astype(BF16), v_ref[...].astype(BF16), (NN, ((), ())),
                                                          preferred_element_type=F32)
            m_sc[...] = m_new

        @pl.when(ki == nb - 1)
        def _():
            o_ref[...] = acc[...] / l_sc[...]
            l_ref[...] = m_sc[...] + jnp.log(l_sc[...])

    return _pcall(
        body, name=name, grid=(MLA_HEADS, nb, nb),
        in_specs=[_bs((tb, MLA_SLOT), lambda h, qi, ki: (qi, h)),
                  _bs((tb, MLA_SLOT), lambda h, qi, ki: (jnp.minimum(ki, qi), h)),
                  _bs((tb, MLA_V), lambda h, qi, ki: (jnp.minimum(ki, qi), h))],
        out_specs=[_bs((tb, MLA_V), lambda h, qi, ki: (qi, h)), _bs((None, tb, 1), lambda h, qi, ki: (h, qi, 0))],
        out_shape=[jax.ShapeDtypeStruct((T, MLA_WIDTH), F32), jax.ShapeDtypeStruct((MLA_HEADS, T, 1), F32)],
        scratch_shapes=[pltpu.VMEM((tb, 1), F32), pltpu.VMEM((tb, 1), F32), pltpu.VMEM((tb, MLA_V), F32)],
        compiler_params=pltpu.CompilerParams(dimension_semantics=("parallel", "parallel", "arbitrary")),
    )(qcat, kcat, v)


def _mla_probs(q_ref, k_ref, v_ref, o_ref, do_ref, l_ref, qi, ki, tb):
    s = lax.dot_general(q_ref[...].astype(BF16), k_ref[...].astype(BF16), (NT, ((), ())),
                        preferred_element_type=F32) * _MLA_SCALE
    p = jnp.where(_causal(qi, ki, tb), jnp.exp(s - l_ref[...]), 0.0)
    do = do_ref[...]
    dp = lax.dot_general(do.astype(BF16), v_ref[...].astype(BF16), (NT, ((), ())), preferred_element_type=F32)
    dsum = jnp.sum(do * o_ref[...], axis=1, keepdims=True)
    return p, p * (dp - dsum) * _MLA_SCALE


def _mla_bwd_q(name, qcat, kcat, v, o, lse, do):
    T = qcat.shape[0]
    tb = _tile(T, MLA_BLOCK)
    nb = T // tb

    def body(q_ref, k_ref, v_ref, o_ref, do_ref, l_ref, dq_ref, acc):
        qi, ki = pl.program_id(1), pl.program_id(2)

        @pl.when(ki == 0)
        def _():
            acc[...] = jnp.zeros_like(acc)

        @pl.when(ki <= qi)
        def _():
            _, ds = _mla_probs(q_ref, k_ref, v_ref, o_ref, do_ref, l_ref, qi, ki, tb)
            acc[...] += lax.dot_general(ds.astype(BF16), k_ref[...].astype(BF16), (NN, ((), ())),
                                        preferred_element_type=F32)

        @pl.when(ki == nb - 1)
        def _():
            dq_ref[...] = acc[...]

    qs = lambda w: _bs((tb, w), lambda h, qi, ki: (qi, h))
    ks = lambda w: _bs((tb, w), lambda h, qi, ki: (jnp.minimum(ki, qi), h))
    return _pcall(
        body, name=name, grid=(MLA_HEADS, nb, nb),
        in_specs=[qs(MLA_SLOT), ks(MLA_SLOT), ks(MLA_V), qs(MLA_V), qs(MLA_V),
                  _bs((None, tb, 1), lambda h, qi, ki: (h, qi, 0))],
        out_specs=qs(MLA_SLOT), out_shape=jax.ShapeDtypeStruct((T, MLA_HEADS * MLA_SLOT), F32),
        scratch_shapes=[pltpu.VMEM((tb, MLA_SLOT), F32)],
        compiler_params=pltpu.CompilerParams(dimension_semantics=("parallel", "parallel", "arbitrary")),
    )(qcat, kcat, v, o, do, lse)


def _mla_bwd_kv(name, qcat, kcat, v, o, lse, do):
    T = qcat.shape[0]
    tb = _tile(T, MLA_BLOCK)
    nb = T // tb

    def body(q_ref, k_ref, v_ref, o_ref, do_ref, l_ref, dk_ref, dv_ref, dk_acc, dv_acc):
        ki, qi = pl.program_id(1), pl.program_id(2)

        @pl.when(qi == 0)
        def _():
            dk_acc[...] = jnp.zeros_like(dk_acc)
            dv_acc[...] = jnp.zeros_like(dv_acc)

        @pl.when(qi >= ki)
        def _():
            p, ds = _mla_probs(q_ref, k_ref, v_ref, o_ref, do_ref, l_ref, qi, ki, tb)
            dv_acc[...] += lax.dot_general(p.astype(BF16), do_ref[...].astype(BF16), (TN, ((), ())),
                                           preferred_element_type=F32)
            dk_acc[...] += lax.dot_general(ds.astype(BF16), q_ref[...].astype(BF16), (TN, ((), ())),
                                           preferred_element_type=F32)

        @pl.when(qi == nb - 1)
        def _():
            dk_ref[...] = dk_acc[...]
            dv_ref[...] = dv_acc[...]

    qs = lambda w: _bs((tb, w), lambda h, ki, qi: (jnp.maximum(qi, ki), h))
    ks = lambda w: _bs((tb, w), lambda h, ki, qi: (ki, h))
    return _pcall(
        body, name=name, grid=(MLA_HEADS, nb, nb),
        in_specs=[qs(MLA_SLOT), ks(MLA_SLOT), ks(MLA_V), qs(MLA_V), qs(MLA_V),
                  _bs((None, tb, 1), lambda h, ki, qi: (h, jnp.maximum(qi, ki), 0))],
        out_specs=[ks(MLA_SLOT), ks(MLA_V)],
        out_shape=[jax.ShapeDtypeStruct((T, MLA_HEADS * MLA_SLOT), F32), jax.ShapeDtypeStruct((T, MLA_WIDTH), F32)],
        scratch_shapes=[pltpu.VMEM((tb, MLA_SLOT), F32), pltpu.VMEM((tb, MLA_V), F32)],
        compiler_params=pltpu.CompilerParams(dimension_semantics=("parallel", "parallel", "arbitrary")),
    )(qcat, kcat, v, o, do, lse)


def _kcat_fwd(name, kpre, krs, c256, s256):
    def fn(t, b):
        kr = t[1] * t[2] + _half_swap(t[1]) * t[3]
        return (t[0] + jnp.tile(kr, (1, MLA_HEADS)),), ()
    return _rowwise(name, fn, [kpre, krs, c256, s256], [], [(kpre.shape[1], F32)])[0]


def _kcat_bwd(name, dkcat, c256, s256):
    def fn(t, b):
        d = t[0][:, 0:MLA_SLOT]
        for h in range(1, MLA_HEADS):
            d = d + t[0][:, h * MLA_SLOT:(h + 1) * MLA_SLOT]
        return (d * t[1] + _half_swap(d * t[2]),), ()
    return _rowwise(name, fn, [dkcat, c256, s256], [], [(MLA_SLOT, F32)])[0]


def _in_widths(ws):
    return [ws, SWA_WIDTH, SWA_KV_WIDTH, SWA_KV_WIDTH, MLA_Q_RANK, MLA_KV_RANK, MLA_SLOT]


def _kr_offset(ws):
    return ws + SWA_WIDTH + 2 * SWA_KV_WIDTH + MLA_Q_RANK + MLA_KV_RANK


def _pad_w_in(w, ws):
    z = lambda n: jnp.zeros((w.shape[0], n), w.dtype)
    o = _kr_offset(ws)
    return jnp.concatenate([w[:, :o], z(MLA_NOPE), w[:, o:], z(MLA_SLOT - MLA_NOPE - MLA_ROPE)], axis=1)


def _unpad_w_in(dw, ws):
    o = _kr_offset(ws)
    return jnp.concatenate([dw[:, :o], dw[:, o + MLA_NOPE:o + MLA_NOPE + MLA_ROPE]], axis=1)


def _pad_w_uq(w):
    r = w.shape[0]
    w3 = w.reshape(r, MLA_HEADS, MLA_NOPE + MLA_ROPE)
    return jnp.pad(w3, ((0, 0), (0, 0), (0, MLA_SLOT - MLA_NOPE - MLA_ROPE))).reshape(r, MLA_HEADS * MLA_SLOT)


def _unpad_w_uq(dw):
    r = dw.shape[0]
    return dw.reshape(r, MLA_HEADS, MLA_SLOT)[..., :MLA_NOPE + MLA_ROPE].reshape(r, -1)


def _pad_w_ukv(w):
    r = w.shape[0]
    w3 = w.reshape(r, MLA_HEADS, MLA_NOPE + MLA_V)
    wk = jnp.pad(w3[..., :MLA_NOPE], ((0, 0), (0, 0), (0, MLA_SLOT - MLA_NOPE))).reshape(r, MLA_HEADS * MLA_SLOT)
    wv = w3[..., MLA_NOPE:].reshape(r, MLA_WIDTH)
    return wk, wv


def _unpad_w_ukv(dwk, dwv):
    r = dwk.shape[0]
    return jnp.concatenate([dwk.reshape(r, MLA_HEADS, MLA_SLOT)[..., :MLA_NOPE], dwv.reshape(r, MLA_HEADS, MLA_V)],
                           axis=-1).reshape(r, -1)


def _layer_prep(lw):
    ws = lw['ssm_d'].shape[0]
    p = dict(lw)
    w_in_pad = _pad_w_in(lw['w_in'], ws)
    p['w_in_pad'] = w_in_pad
    offs = np.cumsum([0] + _in_widths(ws))
    p['w_in_parts'] = [w_in_pad[:, offs[i]:offs[i + 1]] for i in range(7)]
    p['s5_prep'] = _s5_prep(lw['ssm_log_dt'], lw['ssm_a_re'], lw['ssm_a_im'], lw['ssm_b_re'], lw['ssm_b_im'],
                            lw['ssm_c_re'], lw['ssm_c_im'])
    p['sinks_b'] = jnp.broadcast_to(lw['swa_sinks'][:, None], (SWA_HEADS, 128))
    p['w_uq_pad'] = _pad_w_uq(lw['mla_w_uq'])
    p['w_k_pad'], p['w_v'] = _pad_w_ukv(lw['mla_w_ukv'])
    p['w_ukv_pad'] = jnp.concatenate([p['w_k_pad'], p['w_v']], axis=1)
    b = [0, ws, ws + SWA_WIDTH, ws + SWA_WIDTH + MLA_WIDTH]
    p['w_out_g'] = [lw['w_out'][b[g]:b[g + 1]] for g in range(3)]
    p['out_norm_g'] = [lw['out_norm'][b[g]:b[g + 1]] for g in range(3)]
    return p


def _mixer_fwd(tag, x, p, tabs):
    T, D = x.shape
    c64, s64, c256, s256 = tabs
    ws = p['ssm_d'].shape[0]
    widths = _in_widths(ws)
    n_in = sum(widths)
    offs = [int(o) for o in np.cumsum([0] + widths[:-1])]
    h, rstd = _rms_fwd(tag + "_rms", x, p['mix_norm'])
    tm = _tile(T, 256)
    row_i = lambda w: _bs((tm, w), lambda i, j, k: (i, 0))
    parts = _mm(tag + "_in", (T // tm, 1, 1),
                [(h, row_i(D), p['w_in_pad'], _bs((D, n_in), lambda i, j, k: (0, 0)), NN, 0)],
                [(tm, n_in)], [((T, w), F32, row_i(w)) for w in widths],
                lambda accs, ex: tuple(accs[0][:, o:o + w] for o, w in zip(offs, widths)))
    u, q, k, v, cq, ckv, krs = parts
    y_ssm, s5_saved = _s5_fwd(tag + "_s5", u, p['s5_prep'], p['ssm_d'], p['ssm_w_glu'], p['ssm_b_glu'])
    q_r = _rope(tag + "_ropeq", q, c64, s64)
    k_r = _rope(tag + "_ropek", k, c64, s64)
    y_swa, lse_swa = _swa_fwd(tag + "_swa", q_r, k_r, v, p['sinks_b'])
    cqn, r_q = _rms_fwd(tag + "_rmsq", cq, p['mla_q_norm'])
    ckvn, r_kv = _rms_fwd(tag + "_rmskv", ckv, p['mla_kv_norm'])
    qpre = _mm_simple(tag + "_uq", cqn, p['w_uq_pad'], NN, tm=512, tn=1024, tk=MLA_Q_RANK)
    nk_, nv_ = MLA_HEADS * MLA_SLOT, MLA_WIDTH
    kpre, vm = _mm(tag + "_ukv", (T // tm, 1, 1),
                   [(ckvn, row_i(MLA_KV_RANK), p['w_ukv_pad'], _bs((MLA_KV_RANK, nk_ + nv_), lambda i, j, k: (0, 0)),
                     NN, 0)],
                   [(tm, nk_ + nv_)], [((T, nk_), F32, row_i(nk_)), ((T, nv_), F32, row_i(nv_))],
                   lambda accs, ex: (accs[0][:, :nk_], accs[0][:, nk_:]))
    qcat = _rope(tag + "_ropemq", qpre, c256, s256)
    kcat = _kcat_fwd(tag + "_kcat", kpre, krs, c256, s256)
    y_mla, lse_mla = _mla_fwd(tag + "_mla", qcat, kcat, vm)
    ys = [y_ssm, y_swa, y_mla]
    yn, rs = [], []
    for g in range(3):
        n_, r_ = _rms_fwd(f"{tag}_rmso{g}", ys[g], p['out_norm_g'][g])
        yn.append(n_)
        rs.append(r_)
    tm3, tn3 = _tile(T, 512), _tile(D, 1024)
    pairs = []
    for g in range(3):
        wg = ys[g].shape[1]
        pairs.append((yn[g], _bs((tm3, wg), lambda i, j, k: (i, 0)), p['w_out_g'][g],
                      _bs((wg, tn3), lambda i, j, k: (0, j)), NN, 0))
    o_spec = _bs((tm3, tn3), lambda i, j, k: (i, j))
    x2, = _mm(tag + "_out", (T // tm3, D // tn3, 1), pairs, [(tm3, tn3)], [((T, D), F32, o_spec)],
              lambda accs, ex: (ex[0] + accs[0],), [(x, o_spec)])
    saved = (x, h, rstd, q_r, k_r, v, cq, ckv, s5_saved, y_swa, lse_swa, cqn, r_q, ckvn, r_kv, qcat, kcat, vm,
             y_mla, lse_mla, ys, yn, rs)
    return x2, saved


def _mixer_bwd(tag, dx2, saved, p, tabs):
    (x, h, rstd, q_r, k_r, v, cq, ckv, s5_saved, y_swa, lse_swa, cqn, r_q, ckvn, r_kv, qcat, kcat, vm,
     y_mla, lse_mla, ys, yn, rs) = saved
    T, D = x.shape
    c64, s64, c256, s256 = tabs
    ws = p['ssm_d'].shape[0]
    g_ = {}
    dys, dwo, don = [], [], []
    for g in range(3):
        wg = ys[g].shape[1]
        dyn = _mm_simple(f"{tag}_dyn{g}", dx2, p['w_out_g'][g], NT, tm=512, tn=wg, tk=1024)
        dwo.append(_mm_simple(f"{tag}_dwo{g}", yn[g], dx2, TN, tm=512, tn=1024, tk=512))
        dy_g, don_g = _rms_bwd(f"{tag}_rmsob{g}", dyn, ys[g], rs[g], p['out_norm_g'][g])
        dys.append(dy_g)
        don.append(don_g)
    g_['w_out'] = jnp.concatenate(dwo, axis=0)
    g_['out_norm'] = jnp.concatenate(don, axis=1)[0]
    dqcat = _mla_bwd_q(tag + "_mlabq", qcat, kcat, vm, y_mla, lse_mla, dys[2])
    dkcat, dvm = _mla_bwd_kv(tag + "_mlabkv", qcat, kcat, vm, y_mla, lse_mla, dys[2])
    dqpre = _rope(tag + "_ropemqb", dqcat, c256, s256, backward=True)
    dkrs = _kcat_bwd(tag + "_kcatb", dkcat, c256, s256)
    g_['mla_w_uq'] = _unpad_w_uq(_mm_simple(tag + "_dwuq", cqn, dqpre, TN, tm=MLA_Q_RANK, tn=1024, tk=512))
    dcqn = _mm_simple(tag + "_dcqn", dqpre, p['w_uq_pad'], NT, tm=512, tn=MLA_Q_RANK, tk=1024)
    dcq, dqn = _rms_bwd(tag + "_rmsqb", dcqn, cq, r_q, p['mla_q_norm'])
    g_['mla_q_norm'] = dqn[0]
    dwk = _mm_simple(tag + "_dwk", ckvn, dkcat, TN, tm=MLA_KV_RANK, tn=1024, tk=512)
    dwv = _mm_simple(tag + "_dwv", ckvn, dvm, TN, tm=MLA_KV_RANK, tn=1024, tk=512)
    g_['mla_w_ukv'] = _unpad_w_ukv(dwk, dwv)
    tm = _tile(T, 512)
    nk_, nv_ = MLA_HEADS * MLA_SLOT, MLA_WIDTH
    tkk = _tile(nk_, 1024)
    dckvn_k = _mm_simple(tag + "_dckvk", dkcat, p['w_k_pad'], NT, tm=512, tn=MLA_KV_RANK, tk=tkk)
    dckvn = _mm_simple(tag + "_dckvv", dvm, p['w_v'], NT, tm=512, tn=MLA_KV_RANK, tk=nv_,
                       epilogue=lambda accs, ex: (accs[0] + ex[0],), extras=[dckvn_k])
    dckv, dkvn = _rms_bwd(tag + "_rmskvb", dckvn, ckv, r_kv, p['mla_kv_norm'])
    g_['mla_kv_norm'] = dkvn[0]
    dq_r, dkc, dkp, dvc, dvp, dsinks = _swa_bwd(tag + "_swab", q_r, k_r, v, p['sinks_b'], y_swa, lse_swa, dys[1])
    g_['swa_sinks'] = dsinks[0, :SWA_HEADS]
    dk_r = _shift_add(tag + "_dksum", dkc, dkp)
    dv = _shift_add(tag + "_dvsum", dvc, dvp)
    dq = _rope(tag + "_ropeqb", dq_r, c64, s64, backward=True)
    dk = _rope(tag + "_ropekb", dk_r, c64, s64, backward=True)
    du, s5g, dd, dw_glu, db_glu = _s5_bwd(tag + "_s5b", dys[0], s5_saved, p['s5_prep'], p['ssm_d'], p['ssm_w_glu'])
    g_['ssm_d'], g_['ssm_w_glu'], g_['ssm_b_glu'] = dd[0], dw_glu, db_glu[0]
    _, pull = jax.vjp(_s5_prep, p['ssm_log_dt'], p['ssm_a_re'], p['ssm_a_im'], p['ssm_b_re'], p['ssm_b_im'],
                      p['ssm_c_re'], p['ssm_c_im'])
    for name, val in zip(['ssm_log_dt', 'ssm_a_re', 'ssm_a_im', 'ssm_b_re', 'ssm_b_im', 'ssm_c_re', 'ssm_c_im'],
                         pull(tuple(s5g))):
        g_[name] = val
    dparts = [du, dq, dk, dv, dcq, dckv, dkrs]
    widths = _in_widths(ws)
    tn = _tile(D, 1024)
    pairs = []
    for dp_, wp_, w in zip(dparts, p['w_in_parts'], widths):
        pairs.append((dp_, _bs((tm, w), lambda i, j, k: (i, 0)), wp_, _bs((tn, w), lambda i, j, k: (j, 0)), NT, 0))
    o_spec = _bs((tm, tn), lambda i, j, k: (i, j))
    dh, = _mm(tag + "_dh", (T // tm, D // tn, 1), pairs, [(tm, tn)], [((T, D), F32, o_spec)],
              lambda accs, ex: (accs[0],))
    tmw, tk = _tile(D, 512), _tile(T, 512)
    pairs = []
    for i_, (dp_, w) in enumerate(zip(dparts, widths)):
        pairs.append((h, _bs((tk, tmw), lambda i, j, k: (k, i)), dp_, _bs((tk, w), lambda i, j, k: (k, 0)), TN, i_))
    dws = _mm(tag + "_dwin", (D // tmw, 1, T // tk), pairs, [(tmw, w) for w in widths],
              [((D, w), F32, _bs((tmw, w), lambda i, j, k: (i, 0))) for w in widths], lambda accs, ex: tuple(accs))
    g_['w_in'] = _unpad_w_in(jnp.concatenate(dws, axis=1), ws)
    dx, dmix = _rms_bwd(tag + "_rmsb", dh, x, rstd, p['mix_norm'], dres=dx2)
    g_['mix_norm'] = dmix[0]
    return dx, g_


def _final_loss(name, x, target, gain):
    D = x.shape[1]

    def fn(t, b):
        xv = t[0]
        r = lax.rsqrt(jnp.mean(xv * xv, axis=-1, keepdims=True) + EPS)
        xh = xv * r
        err = xh * b[0] - t[1]
        part = 0.5 * jnp.sum(jnp.sum(err * err, axis=-1, keepdims=True), axis=0, keepdims=True) / D
        dy = err / D
        dxh = dy * b[0]
        dx = r * (dxh - xh * jnp.mean(dxh * xh, axis=-1, keepdims=True))
        return (dx,), (jnp.broadcast_to(part, (1, 128)), _colsum(dy * xh))
    dx, part, dg = _rowwise(name, fn, [x, target], [gain.reshape(1, D)], [(D, F32)], [128, D])
    return part[0, 0], dx, dg[0]


def _device_step(x, positions, target, layers, final_norm):
    tabs = _rope_tables(positions)
    preps = [_layer_prep(lw) for lw in layers]
    saved = []
    for l, p in enumerate(preps):
        x, s1 = _ffn_fwd("ffn1", x, p['ffn1_norm'], p['ffn1_wg2'], p['ffn1_wu2'], p['ffn1_wd'])
        x, s2 = _mixer_fwd("mix", x, p, tabs)
        x, s3 = _ffn_fwd("ffn2", x, p['ffn2_norm'], p['ffn2_wg2'], p['ffn2_wu2'], p['ffn2_wd'])
        saved.append((s1, s2, s3))
    loss_part, dx, dfinal = _final_loss("loss", x, target, final_norm)
    grads = [None] * len(preps)
    for l in range(len(preps) - 1, -1, -1):
        p = preps[l]
        s1, s2, s3 = saved[l]
        g_ = {}
        dx, g_['ffn2_norm'], g_['ffn2_wg2'], g_['ffn2_wu2'], g_['ffn2_wd'] = _ffn_bwd(
            "ffn2b", dx, s3, p['ffn2_norm'], p['ffn2_wg2'], p['ffn2_wu2'], p['ffn2_wd'])
        dx, gm = _mixer_bwd("mixb", dx, s2, p, tabs)
        g_.update(gm)
        dx, g_['ffn1_norm'], g_['ffn1_wg2'], g_['ffn1_wu2'], g_['ffn1_wd'] = _ffn_bwd(
            "ffn1b", dx, s1, p['ffn1_norm'], p['ffn1_wg2'], p['ffn1_wu2'], p['ffn1_wd'])
        g_['ffn1_norm'], g_['ffn2_norm'] = g_['ffn1_norm'][0], g_['ffn2_norm'][0]
        grads[l] = g_
    return loss_part, dx, grads, dfinal


_ANY = pl.BlockSpec(memory_space=pl.ANY)
_CHIP_MASKS = (2, 1, 3)


def _place():
    x, y, c = lax.axis_index("x"), lax.axis_index("y"), lax.axis_index("c")
    chips = [(1 - x, y), (x, 1 - y), (1 - x, 1 - y)]
    return x, y, c, 2 * x + y, chips


def _gather_call(name, shards):
    n = len(shards)

    def body(*refs):
        ins, outs = refs[:n], refs[n:2 * n]
        ici_send, ici_recv, d2d_send, d2d_recv, loc = refs[2 * n:]
        x, y, c, s, chips = _place()
        sib = (x, y, 1 - c)
        local = [pltpu.make_async_copy(ins[a], outs[a].at[s], loc.at[a]) for a in range(n)]
        for cp in local:
            cp.start()

        def rows(a, cc):
            half = ins[a].shape[0] // 2
            return pl.ds(cc * half, half)

        sends, fwds = [], []
        for a in range(n):
            for j in range(3):
                cp = pltpu.make_async_remote_copy(
                    src_ref=ins[a].at[rows(a, c)], dst_ref=outs[a].at[s, rows(a, c)],
                    send_sem=ici_send.at[a * 3 + j], recv_sem=ici_recv.at[a * 3 + j],
                    device_id=(*chips[j], c), device_id_type=MESH)
                cp.start()
                sends.append(cp)
        for a in range(n):
            for j in range(3):
                src_s = s ^ _CHIP_MASKS[j]
                got = outs[a].at[src_s, rows(a, c)]
                pltpu.make_async_remote_copy(
                    src_ref=got, dst_ref=got, send_sem=ici_send.at[a * 3 + j], recv_sem=ici_recv.at[a * 3 + j],
                    device_id=(*chips[j], c), device_id_type=MESH).wait_recv()
                fw = pltpu.make_async_remote_copy(
                    src_ref=got, dst_ref=got, send_sem=d2d_send.at[a * 3 + j], recv_sem=d2d_recv.at[a * 3 + j],
                    device_id=sib, device_id_type=MESH)
                fw.start()
                fwds.append(fw)
        for a in range(n):
            for j in range(3):
                theirs = outs[a].at[s ^ _CHIP_MASKS[j], rows(a, 1 - c)]
                pltpu.make_async_remote_copy(
                    src_ref=theirs, dst_ref=theirs, send_sem=d2d_send.at[a * 3 + j], recv_sem=d2d_recv.at[a * 3 + j],
                    device_id=sib, device_id_type=MESH).wait_recv()
        for cp in sends + fwds:
            cp.wait_send()
        for cp in local:
            cp.wait()

    return _pcall(
        body, name=name, in_specs=[_ANY] * n, out_specs=[_ANY] * n,
        out_shape=[jax.ShapeDtypeStruct((N_SHARD,) + a.shape, a.dtype) for a in shards],
        scratch_shapes=[pltpu.SemaphoreType.DMA((3 * n,))] * 4 + [pltpu.SemaphoreType.DMA((n,))],
    )(*shards)


def _pair_exchange_call(name, grads):
    n = len(grads)

    def body(*refs):
        ins, outs = refs[:n], refs[n:2 * n]
        send, recv = refs[2 * n:]
        x, y, c, s, chips = _place()
        cps = []
        for a in range(n):
            half = ins[a].shape[1] // 2
            cp = pltpu.make_async_remote_copy(
                src_ref=ins[a].at[:, pl.ds((1 - c) * half, half), :], dst_ref=outs[a],
                send_sem=send.at[a], recv_sem=recv.at[a], device_id=(x, y, 1 - c), device_id_type=MESH)
            cp.start()
            cps.append(cp)
        for cp in cps:
            cp.wait()

    return _pcall(
        body, name=name, in_specs=[_ANY] * n, out_specs=[_ANY] * n,
        out_shape=[jax.ShapeDtypeStruct((g.shape[0], g.shape[1] // 2, g.shape[2]), g.dtype) for g in grads],
        scratch_shapes=[pltpu.SemaphoreType.DMA((n,))] * 2,
    )(*grads)


def _half_rows(nb):
    return lambda i: lax.axis_index("c") * nb + i


def _pair_add(name, g, p1):
    ns, half, w = p1.shape
    tm = _tile(half, 256)
    nb = half // tm
    mine = _half_rows(nb)

    def body(g_ref, p_ref, o_ref):
        o_ref[...] = (g_ref[...] + p_ref[...]).astype(o_ref.dtype)

    return _pcall(
        body, name=name, grid=(ns, nb),
        in_specs=[_bs((None, tm, w), lambda s, i: (s, mine(i), 0)), _bs((None, tm, w), lambda s, i: (s, i, 0))],
        out_specs=_bs((None, tm, w), lambda s, i: (s, i, 0)), out_shape=jax.ShapeDtypeStruct(p1.shape, BF16),
        compiler_params=pltpu.CompilerParams(dimension_semantics=("parallel", "parallel")),
    )(g, p1)


def _chip_exchange_call(name, sums):
    n = len(sums)

    def body(*refs):
        ins, outs = refs[:n], refs[n:2 * n]
        send, recv = refs[2 * n:]
        x, y, c, s, chips = _place()
        cps = []
        for a in range(n):
            for j in range(3):
                cp = pltpu.make_async_remote_copy(
                    src_ref=ins[a].at[s ^ _CHIP_MASKS[j]], dst_ref=outs[a].at[j],
                    send_sem=send.at[a * 3 + j], recv_sem=recv.at[a * 3 + j],
                    device_id=(*chips[j], c), device_id_type=MESH)
                cp.start()
                cps.append(cp)
        for cp in cps:
            cp.wait()

    return _pcall(
        body, name=name, in_specs=[_ANY] * n, out_specs=[_ANY] * n,
        out_shape=[jax.ShapeDtypeStruct((3,) + a.shape[1:], a.dtype) for a in sums],
        scratch_shapes=[pltpu.SemaphoreType.DMA((3 * n,))] * 2,
    )(*sums)


def _chip_sum(name, g, p1, p2):
    ns, half, w = p1.shape
    tm = _tile(half, 256)
    nb = half // tm
    mine = _half_rows(nb)
    shard = lambda: 2 * lax.axis_index("x") + lax.axis_index("y")

    def body(g_ref, p_ref, a_ref, b_ref, c_ref, o_ref):
        o_ref[...] = (((g_ref[...] + p_ref[...]) + a_ref[...].astype(F32)) + b_ref[...].astype(F32)) \
            + c_ref[...].astype(F32)

    blk = lambda j: _bs((None, tm, w), lambda i: (j, i, 0))
    return _pcall(
        body, name=name, grid=(nb,),
        in_specs=[_bs((None, tm, w), lambda i: (shard(), mine(i), 0)), _bs((None, tm, w), lambda i: (shard(), i, 0)),
                  blk(0), blk(1), blk(2)],
        out_specs=_bs((tm, w), lambda i: (i, 0)), out_shape=jax.ShapeDtypeStruct((half, w), F32),
        compiler_params=pltpu.CompilerParams(dimension_semantics=("parallel",)),
    )(g, p1, p2, p2, p2)


def _pair_join_call(name, halves):
    n = len(halves)

    def body(*refs):
        ins, outs = refs[:n], refs[n:2 * n]
        send, recv, loc = refs[2 * n:]
        x, y, c, s, chips = _place()
        cps, local = [], []
        for a in range(n):
            half = ins[a].shape[0]
            dst = outs[a].at[pl.ds(c * half, half)]
            lc = pltpu.make_async_copy(ins[a], dst, loc.at[a])
            lc.start()
            local.append(lc)
            cp = pltpu.make_async_remote_copy(
                src_ref=ins[a], dst_ref=dst, send_sem=send.at[a], recv_sem=recv.at[a],
                device_id=(x, y, 1 - c), device_id_type=MESH)
            cp.start()
            cps.append(cp)
        for a in range(n):
            half = ins[a].shape[0]
            theirs = outs[a].at[pl.ds((1 - c) * half, half)]
            pltpu.make_async_remote_copy(
                src_ref=theirs, dst_ref=theirs, send_sem=send.at[a], recv_sem=recv.at[a],
                device_id=(x, y, 1 - c), device_id_type=MESH).wait_recv()
        for cp in cps:
            cp.wait_send()
        for lc in local:
            lc.wait()

    return _pcall(
        body, name=name, in_specs=[_ANY] * n, out_specs=[_ANY] * n,
        out_shape=[jax.ShapeDtypeStruct((2 * h.shape[0], h.shape[1]), h.dtype) for h in halves],
        scratch_shapes=[pltpu.SemaphoreType.DMA((n,))] * 3,
    )(*halves)


def _reduce_to_shards(tag, grads):
    p1 = _pair_exchange_call(tag + "_pairx", grads)
    sums = [_pair_add(tag + "_pairadd", g, p) for g, p in zip(grads, p1)]
    p2 = _chip_exchange_call(tag + "_chipx", sums)
    halves = [_chip_sum(tag + "_chipsum", g, p, q) for g, p, q in zip(grads, p1, p2)]
    return _pair_join_call(tag + "_join", halves)


def _all_sum_small(tag, buf):
    n_dev = 8

    def body(in_ref, out_ref, send, recv, loc):
        x, y, c, s, chips = _place()
        me = 4 * x + 2 * y + c
        lc = pltpu.make_async_copy(in_ref, out_ref.at[me], loc)
        lc.start()
        cps = []
        for k in range(1, n_dev):
            to = (x ^ (k >> 2), y ^ ((k >> 1) & 1), c ^ (k & 1))
            cp = pltpu.make_async_remote_copy(
                src_ref=in_ref, dst_ref=out_ref.at[me], send_sem=send.at[k - 1], recv_sem=recv.at[k - 1],
                device_id=to, device_id_type=MESH)
            cp.start()
            cps.append(cp)
        for k in range(1, n_dev):
            theirs = out_ref.at[me ^ k]
            pltpu.make_async_remote_copy(
                src_ref=theirs, dst_ref=theirs, send_sem=send.at[k - 1], recv_sem=recv.at[k - 1],
                device_id=(x, y, c), device_id_type=MESH).wait_recv()
        for cp in cps:
            cp.wait_send()
        lc.wait()

    allb = _pcall(
        body, name=tag + "_gather", in_specs=[_ANY], out_specs=_ANY,
        out_shape=jax.ShapeDtypeStruct((n_dev,) + buf.shape, buf.dtype),
        scratch_shapes=[pltpu.SemaphoreType.DMA((n_dev - 1,))] * 2 + [pltpu.SemaphoreType.DMA(())],
    )(buf)
    rows = buf.shape[0]
    tm = _tile(rows, 512)

    def sum_body(a_ref, o_ref):
        acc = a_ref[0]
        for k in range(1, n_dev):
            acc = acc + a_ref[k]
        o_ref[...] = acc

    return _pcall(
        sum_body, name=tag + "_sum", grid=(rows // tm,),
        in_specs=[_bs((n_dev, tm, 128), lambda i: (0, i, 0))], out_specs=_bs((tm, 128), lambda i: (i, 0)),
        out_shape=jax.ShapeDtypeStruct(buf.shape, F32),
        compiler_params=pltpu.CompilerParams(dimension_semantics=("parallel",)),
    )(allb)


def _adamw(name, w, g, m, v):
    def fn(t, b):
        wv, gv, mv, vv = t
        m2 = ADAM_B1 * mv + (1.0 - ADAM_B1) * gv
        v2 = ADAM_B2 * vv + (1.0 - ADAM_B2) * (gv * gv)
        m_hat = m2 / (1.0 - ADAM_B1 ** ADAM_STEP)
        v_hat = v2 / (1.0 - ADAM_B2 ** ADAM_STEP)
        delta = -ADAM_LR * (m_hat / (jnp.sqrt(v_hat) + ADAM_EPS) + ADAM_WD * wv)
        return (delta, m2, v2), ()
    wd = w.shape[1]
    return _rowwise(name, fn, [w, g, m, v], [], [(wd, F32)] * 3, tm=256 if wd > 1024 else 512)


_WEIGHTS = ['ffn1_norm', 'ffn1_w_gate', 'ffn1_w_up', 'ffn1_w_down', 'mix_norm', 'w_in', 'ssm_log_dt', 'ssm_a_re',
            'ssm_a_im', 'ssm_b_re', 'ssm_b_im', 'ssm_c_re', 'ssm_c_im', 'ssm_d', 'ssm_w_glu', 'ssm_b_glu',
            'swa_sinks', 'mla_q_norm', 'mla_w_uq', 'mla_kv_norm', 'mla_w_ukv', 'out_norm', 'w_out', 'ffn2_norm',
            'ffn2_w_gate', 'ffn2_w_up', 'ffn2_w_down', 'final_norm']
_COL_SHARDED = ['ffn1_w_gate', 'ffn1_w_up', 'w_in', 'mla_w_uq', 'mla_w_ukv', 'ffn2_w_gate', 'ffn2_w_up']
_ROW_SHARDED = ['ffn1_w_down', 'ssm_w_glu', 'w_out', 'ffn2_w_down']
_STACKED = {'ffn1_w_gate': 'ffn1_wg2', 'ffn1_w_up': 'ffn1_wu2', 'ffn2_w_gate': 'ffn2_wg2', 'ffn2_w_up': 'ffn2_wu2'}
_RENAMED = {'ffn1_w_down': 'ffn1_wd', 'ffn2_w_down': 'ffn2_wd'}
_BIG = _COL_SHARDED + _ROW_SHARDED
_SMALL = [n for n in _WEIGHTS if n not in _BIG]


def _pack(vals):
    flat = jnp.concatenate([v.reshape(-1) for v in vals])
    pad = (-flat.shape[0]) % 1024
    return jnp.pad(flat, (0, pad)).reshape(-1, 128)


def _unpack(buf, like):
    flat = buf.reshape(-1)
    out, o = [], 0
    for v in like:
        out.append(flat[o:o + v.size].reshape(v.shape))
        o += v.size
    return out


def kernel(x, positions, ffn1_norm, ffn1_w_gate, ffn1_w_up, ffn1_w_down, mix_norm, w_in, ssm_log_dt, ssm_a_re, ssm_a_im, ssm_b_re, ssm_b_im, ssm_c_re, ssm_c_im, ssm_d, ssm_w_glu, ssm_b_glu, swa_sinks, mla_q_norm, mla_w_uq, mla_kv_norm, mla_w_ukv, out_norm, w_out, ffn2_norm, ffn2_w_gate, ffn2_w_up, ffn2_w_down, final_norm, loss_target, m_ffn1_norm, m_ffn1_w_gate, m_ffn1_w_up, m_ffn1_w_down, m_mix_norm, m_w_in, m_ssm_log_dt, m_ssm_a_re, m_ssm_a_im, m_ssm_b_re, m_ssm_b_im, m_ssm_c_re, m_ssm_c_im, m_ssm_d, m_ssm_w_glu, m_ssm_b_glu, m_swa_sinks, m_mla_q_norm, m_mla_w_uq, m_mla_kv_norm, m_mla_w_ukv, m_out_norm, m_w_out, m_ffn2_norm, m_ffn2_w_gate, m_ffn2_w_up, m_ffn2_w_down, m_final_norm, v_ffn1_norm, v_ffn1_w_gate, v_ffn1_w_up, v_ffn1_w_down, v_mix_norm, v_w_in, v_ssm_log_dt, v_ssm_a_re, v_ssm_a_im, v_ssm_b_re, v_ssm_b_im, v_ssm_c_re, v_ssm_c_im, v_ssm_d, v_ssm_w_glu, v_ssm_b_glu, v_swa_sinks, v_mla_q_norm, v_mla_w_uq, v_mla_kv_norm, v_mla_w_ukv, v_out_norm, v_w_out, v_ffn2_norm, v_ffn2_w_gate, v_ffn2_w_up, v_ffn2_w_down, v_final_norm):
    w = dict(zip(_WEIGHTS, (ffn1_norm, ffn1_w_gate, ffn1_w_up, ffn1_w_down, mix_norm, w_in, ssm_log_dt, ssm_a_re, ssm_a_im, ssm_b_re, ssm_b_im, ssm_c_re, ssm_c_im, ssm_d, ssm_w_glu, ssm_b_glu, swa_sinks, mla_q_norm, mla_w_uq, mla_kv_norm, mla_w_ukv, out_norm, w_out, ffn2_norm, ffn2_w_gate, ffn2_w_up, ffn2_w_down, final_norm)))
    m = dict(zip(_WEIGHTS, (m_ffn1_norm, m_ffn1_w_gate, m_ffn1_w_up, m_ffn1_w_down, m_mix_norm, m_w_in, m_ssm_log_dt, m_ssm_a_re, m_ssm_a_im, m_ssm_b_re, m_ssm_b_im, m_ssm_c_re, m_ssm_c_im, m_ssm_d, m_ssm_w_glu, m_ssm_b_glu, m_swa_sinks, m_mla_q_norm, m_mla_w_uq, m_mla_kv_norm, m_mla_w_ukv, m_out_norm, m_w_out, m_ffn2_norm, m_ffn2_w_gate, m_ffn2_w_up, m_ffn2_w_down, m_final_norm)))
    v = dict(zip(_WEIGHTS, (v_ffn1_norm, v_ffn1_w_gate, v_ffn1_w_up, v_ffn1_w_down, v_mix_norm, v_w_in, v_ssm_log_dt, v_ssm_a_re, v_ssm_a_im, v_ssm_b_re, v_ssm_b_im, v_ssm_c_re, v_ssm_c_im, v_ssm_d, v_ssm_w_glu, v_ssm_b_glu, v_swa_sinks, v_mla_q_norm, v_mla_w_uq, v_mla_kv_norm, v_mla_w_ukv, v_out_norm, v_w_out, v_ffn2_norm, v_ffn2_w_gate, v_ffn2_w_up, v_ffn2_w_down, v_final_norm)))
    n_layers = ffn1_norm.shape[0]

    layers = []
    for l in range(n_layers):
        full = _gather_call("gather", [w[n][l].astype(BF16) for n in _BIG])
        lw = {n: w[n][l] for n in _SMALL if n != 'final_norm'}
        for n, g in zip(_BIG, full):
            ns, r, c = g.shape
            if n in _STACKED:
                lw[_STACKED[n]] = g.reshape(ns * r, c)
            elif n in _ROW_SHARDED:
                lw[_RENAMED.get(n, n)] = g.reshape(ns * r, c)
            else:
                lw[n] = jnp.moveaxis(g, 0, 1).reshape(r, ns * c)
        layers.append(lw)

    loss_part, dx, grads, dfinal = _device_step(x[0], positions[0], loss_target[0], layers, final_norm)
    loss = lax.psum(loss_part, ("x", "y", "c"))

    reduced = {n: [] for n in _BIG}
    for l in range(n_layers):
        contrib = []
        for n in _BIG:
            r, c = w[n].shape[1:]
            if n in _STACKED:
                g = grads[l][_STACKED[n]].reshape(N_SHARD, r, c)
            elif n in _ROW_SHARDED:
                g = grads[l][_RENAMED.get(n, n)].reshape(N_SHARD, r, c)
            else:
                g = jnp.moveaxis(grads[l][n].reshape(r, N_SHARD, c), 1, 0)
            contrib.append(g)
        for n, g in zip(_BIG, _reduce_to_shards("rs", contrib)):
            reduced[n].append(g)
    grad = {n: jnp.stack(reduced[n]) for n in _BIG}

    small_like = [w[n] for n in _SMALL]
    small_g = [jnp.stack([grads[l][n] for l in range(n_layers)]) for n in _SMALL if n != 'final_norm'] + [dfinal]
    g_small = _all_sum_small("small", _pack(small_g))
    d_small, m_small, v_small = _adamw("adam_small", _pack(small_like), g_small, _pack([m[n] for n in _SMALL]),
                                       _pack([v[n] for n in _SMALL]))
    delta, new_m, new_v = {}, {}, {}
    for n, gv, dv_, mv, vv in zip(_SMALL, _unpack(g_small, small_like), _unpack(d_small, small_like),
                                  _unpack(m_small, small_like), _unpack(v_small, small_like)):
        grad[n], delta[n], new_m[n], new_v[n] = gv, dv_, mv, vv
    for n in _BIG:
        shp = w[n].shape
        two = lambda t: t.reshape(shp[0] * shp[1], shp[2])
        d2, m2, v2 = _adamw("adam_" + n, two(w[n]), two(grad[n]), two(m[n]), two(v[n]))
        delta[n], new_m[n], new_v[n] = d2.reshape(shp), m2.reshape(shp), v2.reshape(shp)
    return (loss, dx[None], *[grad[n] for n in _WEIGHTS], *[delta[n] for n in _WEIGHTS],
            *[new_m[n] for n in _WEIGHTS], *[new_v[n] for n in _WEIGHTS])
```

```python
import functools
import math

import jax
import jax.numpy as jnp
import numpy as np
from jax import lax
from jax.experimental import pallas as pl
from jax.experimental.pallas import tpu as pltpu

F32 = jnp.float32
BF16 = jnp.bfloat16

EPS = 1e-6
ROPE_THETA = 10000.0
SSM_GROUP = 16
SSM_STATE = 64
SWA_HEADS = 8
SWA_KV_HEADS = 2
SWA_HEAD_DIM = 64
SWA_BLOCK = 128
SWA_WIDTH = SWA_HEADS * SWA_HEAD_DIM
SWA_KV_WIDTH = SWA_KV_HEADS * SWA_HEAD_DIM
MLA_HEADS = 8
MLA_Q_RANK = 512
MLA_KV_RANK = 256
MLA_NOPE = 128
MLA_ROPE = 64
MLA_V = 128
MLA_SLOT = 256
MLA_WIDTH = MLA_HEADS * MLA_V
ROPE_HALF = 32

ADAM_LR = 0.001
ADAM_B1 = 0.9
ADAM_B2 = 0.999
ADAM_EPS = 1e-08
ADAM_WD = 0.01
ADAM_STEP = 10

N_SHARD = 4
MESH = pl.DeviceIdType.MESH

NN = ((1,), (0,))
NT = ((1,), (1,))
TN = ((0,), (0,))


def _pcall(body, **kw):
    return pl.pallas_call(body, **kw)


def _tile(n, want, align=16):
    if n <= want:
        return n
    t = want - want % align
    while t >= align:
        if n % t == 0:
            return t
        t -= align
    return n


def _mm(name, grid, pairs, acc_shapes, outs, epilogue, extras=()):
    nk = grid[2]
    n_p, n_e, n_o, n_a = len(pairs), len(extras), len(outs), len(acc_shapes)
    dims_idx = [(p[4], p[5]) for p in pairs]

    def body(*refs):
        ab = refs[:2 * n_p]
        ex = refs[2 * n_p:2 * n_p + n_e]
        o = refs[2 * n_p + n_e:2 * n_p + n_e + n_o]
        accs = refs[2 * n_p + n_e + n_o:]

        def partial_sums():
            sums = [None] * n_a
            for p, (dims, ai) in enumerate(dims_idx):
                a = ab[2 * p][...].astype(BF16)
                b = ab[2 * p + 1][...].astype(BF16)
                d = lax.dot_general(a, b, (dims, ((), ())), preferred_element_type=F32)
                sums[ai] = d if sums[ai] is None else sums[ai] + d
            return sums

        def finish(vals):
            res = epilogue(vals, [e[...] for e in ex])
            for r, oref in zip(res, o):
                oref[...] = r.astype(oref.dtype)

        if nk == 1:
            finish(partial_sums())
        else:
            k = pl.program_id(2)

            @pl.when(k == 0)
            def _():
                for acc in accs:
                    acc[...] = jnp.zeros_like(acc)

            for acc, s in zip(accs, partial_sums()):
                acc[...] += s

            @pl.when(k == nk - 1)
            def _():
                finish([acc[...] for acc in accs])

    in_arrays, in_specs = [], []
    for a, a_spec, b, b_spec, _, _ in pairs:
        in_arrays += [a, b]
        in_specs += [a_spec, b_spec]
    for e, e_spec in extras:
        in_arrays.append(e)
        in_specs.append(e_spec)
    res = _pcall(
        body, name=name, grid=grid, in_specs=in_specs,
        out_specs=[o[2] for o in outs],
        out_shape=[jax.ShapeDtypeStruct(o[0], o[1]) for o in outs],
        scratch_shapes=[] if nk == 1 else [pltpu.VMEM(s, F32) for s in acc_shapes],
        compiler_params=pltpu.CompilerParams(dimension_semantics=("parallel", "parallel", "arbitrary")),
    )(*in_arrays)
    return res


def _bs(shape, fn):
    return pl.BlockSpec(shape, fn)


def _mm_simple(name, a, b, dims, *, tm=512, tn=512, tk=512, out_dtype=F32, epilogue=None, extras=(), scale=None):
    if dims == NN:
        (M, K), N = a.shape, b.shape[1]
    elif dims == NT:
        (M, K), N = a.shape, b.shape[0]
    else:
        (K, M), N = a.shape, b.shape[1]
    tm, tn, tk = _tile(M, tm, 128), _tile(N, tn, 128), _tile(K, tk, 128)
    if dims == NN:
        a_spec, b_spec = _bs((tm, tk), lambda i, j, k: (i, k)), _bs((tk, tn), lambda i, j, k: (k, j))
    elif dims == NT:
        a_spec, b_spec = _bs((tm, tk), lambda i, j, k: (i, k)), _bs((tn, tk), lambda i, j, k: (j, k))
    else:
        a_spec, b_spec = _bs((tk, tm), lambda i, j, k: (k, i)), _bs((tk, tn), lambda i, j, k: (k, j))
    ex = []
    for e in extras:
        if e.shape[0] == 1:
            ex.append((e, _bs((1, tn), lambda i, j, k: (0, j))))
        else:
            ex.append((e, _bs((tm, tn), lambda i, j, k: (i, j))))
    if epilogue is None:
        if scale is None:
            epilogue = lambda accs, ex_: (accs[0],)
        else:
            epilogue = lambda accs, ex_: (accs[0] * scale,)
        out_dtypes = (out_dtype,)
    else:
        out_dtypes = out_dtype if isinstance(out_dtype, tuple) else (out_dtype,)
    outs = [((M, N), dt, _bs((tm, tn), lambda i, j, k: (i, j))) for dt in out_dtypes]
    res = _mm(name, (M // tm, N // tn, K // tk), [(a, a_spec, b, b_spec, dims, 0)], [(tm, tn)], outs, epilogue, ex)
    return res[0] if len(res) == 1 else res


def _rowwise(name, fn, tiles, bcasts, outs, accs=(), *, tm=256):
    rows = tiles[0].shape[0]
    tm = _tile(rows, tm)
    n_t, n_b, n_o, n_a = len(tiles), len(bcasts), len(outs), len(accs)

    def body(*refs):
        t = [r[...] for r in refs[:n_t]]
        b = [r[...] for r in refs[n_t:n_t + n_b]]
        o = refs[n_t + n_b:n_t + n_b + n_o]
        a = refs[n_t + n_b + n_o:]
        ov, av = fn(t, b)
        for r, val in zip(o, ov):
            r[...] = val.astype(r.dtype)
        if n_a:
            @pl.when(pl.program_id(0) == 0)
            def _():
                for r in a:
                    r[...] = jnp.zeros_like(r)
            for r, val in zip(a, av):
                r[...] += val

    in_specs = [_bs((tm, x.shape[1]), lambda i: (i, 0)) for x in tiles]
    in_specs += [_bs(x.shape, lambda i, nd=x.ndim: (0,) * nd) for x in bcasts]
    out_specs = [_bs((tm, w), lambda i: (i, 0)) for w, _ in outs]
    out_specs += [_bs((1, w), lambda i: (0, 0)) for w in accs]
    out_shape = [jax.ShapeDtypeStruct((rows, w), dt) for w, dt in outs]
    out_shape += [jax.ShapeDtypeStruct((1, w), F32) for w in accs]
    return _pcall(
        body, name=name, grid=(rows // tm,), in_specs=in_specs, out_specs=out_specs, out_shape=out_shape,
        compiler_params=pltpu.CompilerParams(dimension_semantics=("arbitrary",)),
    )(*tiles, *bcasts)


def _colsum(v):
    return jnp.sum(v, axis=0, keepdims=True)


def _rms_fwd(name, x, gain, out_dtype=BF16):
    def fn(t, b):
        xv = t[0]
        r = lax.rsqrt(jnp.mean(xv * xv, axis=-1, keepdims=True) + EPS)
        return (xv * r * b[0], r), ()
    w = x.shape[1]
    return _rowwise(name, fn, [x], [gain.reshape(1, w)], [(w, out_dtype), (1, F32)])


def _rms_bwd(name, dh, x, rstd, gain, dres=None):
    def fn(t, b):
        dhv, xv, r = t[0], t[1], t[2]
        xh = xv * r
        dxh = dhv * b[0]
        dx = r * (dxh - xh * jnp.mean(dxh * xh, axis=-1, keepdims=True))
        if dres is not None:
            dx = dx + t[3]
        return (dx,), (_colsum(dhv * xh),)
    w = x.shape[1]
    tiles = [dh, x, rstd] + ([dres] if dres is not None else [])
    return _rowwise(name, fn, tiles, [gain.reshape(1, w)], [(w, F32)], [w])


def _sigmoid(v):
    return 1.0 / (1.0 + jnp.exp(-v))


def _ffn_fwd(tag, x, gain, wg2, wu2, wd):
    T, D = x.shape
    fs = wg2.shape[1]
    h, rstd = _rms_fwd(tag + "_rms", x, gain)
    tm = _tile(T, 256)
    a_spec = _bs((tm, D), lambda i, j, k: (j, 0))
    w_spec = _bs((D, fs), lambda i, j, k: (i, 0))
    o_spec = _bs((tm, fs), lambda i, j, k: (j, i))

    def epi(accs, ex):
        a, b = accs
        return a, b, a * _sigmoid(a) * b

    a, b, t = _mm(
        tag + "_up", (N_SHARD, T // tm, 1),
        [(h, a_spec, wg2, w_spec, NN, 0), (h, a_spec, wu2, w_spec, NN, 1)],
        [(tm, fs), (tm, fs)],
        [((T, N_SHARD * fs), F32, o_spec), ((T, N_SHARD * fs), F32, o_spec), ((T, N_SHARD * fs), BF16, o_spec)],
        epi)
    y = _mm_simple(tag + "_down", t, wd, NN, tm=512, tn=D, tk=fs // 2 if fs % 256 == 0 else fs,
                   epilogue=lambda accs, ex: (ex[0] + 0.5 * accs[0],), extras=[x])
    return y, (x, h, rstd, a, b, t)


def _ffn_bwd(tag, dy, saved, gain, wg2, wu2, wd):
    x, h, rstd, a, b, t = saved
    T, D = x.shape
    fs = wg2.shape[1]
    F = N_SHARD * fs
    dwd = _mm_simple(tag + "_dwd", t, dy, TN, tm=fs // 2 if fs % 16 == 0 else fs, tn=D, tk=512, scale=0.5)

    def epi(accs, ex):
        dt = 0.5 * accs[0]
        av, bv = ex
        sig = _sigmoid(av)
        da = dt * bv * (sig * (1.0 + av * (1.0 - sig)))
        db = dt * (av * sig)
        return da, db

    tn = 512 if F % 512 == 0 else (256 if F % 256 == 0 else 128)
    da, db = _mm_simple(tag + "_dt", dy, wd, NT, tm=512, tn=tn, tk=D, out_dtype=(BF16, BF16), epilogue=epi,
                        extras=[a, b])
    tm = _tile(D, 1024)
    per = D // tm
    tk = _tile(T, 512)
    h_spec = _bs((tk, tm), lambda i, j, k: (k, i % per))
    d_spec = _bs((tk, fs), lambda i, j, k: (k, i // per))
    o_spec = _bs((tm, fs), lambda i, j, k: (i, 0))
    dwg2, dwu2 = _mm(
        tag + "_dwgu", (N_SHARD * per, 1, T // tk),
        [(h, h_spec, da, d_spec, TN, 0), (h, h_spec, db, d_spec, TN, 1)],
        [(tm, fs), (tm, fs)],
        [((N_SHARD * D, fs), F32, o_spec), ((N_SHARD * D, fs), F32, o_spec)],
        lambda accs, ex: tuple(accs))
    tm2 = _tile(T, 512)
    tn2 = _tile(D, 1024)
    per2 = D // tn2
    g_spec = _bs((tm2, fs), lambda i, j, k: (i, k))
    w_spec = _bs((tn2, fs), lambda i, j, k: (k * per2 + j, 0))
    o2 = _bs((tm2, tn2), lambda i, j, k: (i, j))
    dh, = _mm(
        tag + "_dh", (T // tm2, per2, N_SHARD),
        [(da, g_spec, wg2, w_spec, NT, 0), (db, g_spec, wu2, w_spec, NT, 0)],
        [(tm2, tn2)], [((T, D), F32, o2)], lambda accs, ex: (accs[0],))
    dx, dgain = _rms_bwd(tag + "_rmsb", dh, x, rstd, gain, dres=dy)
    return dx, dgain, dwg2, dwu2, dwd


def _rope_tables(positions):
    inv_freq = ROPE_THETA ** (-jnp.arange(0, 2 * ROPE_HALF, 2, dtype=F32) / (2 * ROPE_HALF))
    ang = positions.astype(F32)[:, None] * inv_freq
    c, s = jnp.cos(ang), jnp.sin(ang)
    c64 = jnp.concatenate([c, c], axis=1)
    s64 = jnp.concatenate([-s, s], axis=1)
    T = positions.shape[0]
    one, zero = jnp.ones((T, MLA_NOPE), F32), jnp.zeros((T, MLA_NOPE), F32)
    pad1, pad0 = jnp.ones((T, MLA_SLOT - MLA_NOPE - MLA_ROPE), F32), jnp.zeros((T, MLA_SLOT - MLA_NOPE - MLA_ROPE), F32)
    c256 = jnp.concatenate([one, c64, pad1], axis=1)
    s256 = jnp.concatenate([zero, s64, pad0], axis=1)
    return c64, s64, c256, s256


def _half_swap(v):
    w = v.shape[1]
    lane = lax.broadcasted_iota(jnp.int32, v.shape, 1)
    first = (lane % (2 * ROPE_HALF)) < ROPE_HALF
    return jnp.where(first, pltpu.roll(v, w - ROPE_HALF, 1), pltpu.roll(v, ROPE_HALF, 1))


def _rope(name, x, ctab, stab, backward=False, out_dtype=F32):
    reps = x.shape[1] // ctab.shape[1]

    def fn(t, b):
        xv = t[0]
        c = jnp.tile(t[1], (1, reps)) if reps > 1 else t[1]
        s = jnp.tile(t[2], (1, reps)) if reps > 1 else t[2]
        if backward:
            return (xv * c + _half_swap(xv * s),), ()
        return (xv * c + _half_swap(xv) * s,), ()
    return _rowwise(name, fn, [x, ctab, stab], [], [(x.shape[1], out_dtype)])[0]


def _s5_prep(log_dt, a_re, a_im, b_re, b_im, c_re, c_im):
    G, P = a_re.shape
    C = b_re.shape[-1]
    dt = jnp.exp(log_dt)[:, None]
    mag = jnp.exp(a_re * dt)
    abar_r = mag * jnp.cos(a_im * dt)
    abar_i = mag * jnp.sin(a_im * dt)
    den = a_re * a_re + a_im * a_im
    nr = abar_r - 1.0
    qr = (nr * a_re + abar_i * a_im) / den
    qi = (abar_i * a_re - nr * a_im) / den
    bbar_r = qr[..., None] * b_re - qi[..., None] * b_im
    bbar_i = qr[..., None] * b_im + qi[..., None] * b_re
    eye = jnp.eye(G, dtype=F32)
    b_r = jnp.einsum('gpc,gh->gchp', bbar_r, eye).reshape(G * C, G * P)
    b_i = jnp.einsum('gpc,gh->gchp', bbar_i, eye).reshape(G * C, G * P)
    c_r = jnp.einsum('gcp,gh->gphc', c_re, eye).reshape(G * P, G * C)
    c_i = jnp.einsum('gcp,gh->gphc', c_im, eye).reshape(G * P, G * C)
    return abar_r.reshape(1, G * P), abar_i.reshape(1, G * P), b_r, b_i, c_r, -c_i


def _scan_lanes(gp):
    return 512 if gp % 512 == 0 else (256 if gp % 256 == 0 else 128)


def _scan_fwd(name, bu_r, bu_i, a_r, a_i):
    T, gp = bu_r.shape
    tl = _scan_lanes(gp)
    tc = _tile(T, 256)

    def body(br_ref, bi_ref, ar_ref, ai_ref, xr_ref, xi_ref, pr_ref, pi_ref, sr, si):
        @pl.when(pl.program_id(1) == 0)
        def _():
            sr[...] = jnp.zeros_like(sr)
            si[...] = jnp.zeros_like(si)
        ar, ai = ar_ref[...], ai_ref[...]

        def step(g, carry):
            xr, xi = carry
            base = pl.multiple_of(g * 8, 8)
            b_r = br_ref[pl.ds(base, 8), :]
            b_i = bi_ref[pl.ds(base, 8), :]
            rows_r, rows_i, prev_r, prev_i = [], [], [], []
            for j in range(8):
                prev_r.append(xr)
                prev_i.append(xi)
                nr = ar * xr - ai * xi + b_r[j:j + 1, :]
                ni = ar * xi + ai * xr + b_i[j:j + 1, :]
                xr, xi = nr, ni
                rows_r.append(xr)
                rows_i.append(xi)
            xr_ref[pl.ds(base, 8), :] = jnp.concatenate(rows_r, axis=0)
            xi_ref[pl.ds(base, 8), :] = jnp.concatenate(rows_i, axis=0)
            pr_ref[pl.ds(base, 8), :] = jnp.concatenate(prev_r, axis=0)
            pi_ref[pl.ds(base, 8), :] = jnp.concatenate(prev_i, axis=0)
            return xr, xi

        xr, xi = lax.fori_loop(0, tc // 8, step, (sr[...], si[...]))
        sr[...] = xr
        si[...] = xi

    spec = _bs((tc, tl), lambda l, t: (t, l))
    a_spec = _bs((1, tl), lambda l, t: (0, l))
    return _pcall(
        body, name=name, grid=(gp // tl, T // tc),
        in_specs=[spec, spec, a_spec, a_spec], out_specs=[spec] * 4,
        out_shape=[jax.ShapeDtypeStruct((T, gp), F32)] * 4,
        scratch_shapes=[pltpu.VMEM((1, tl), F32), pltpu.VMEM((1, tl), F32)],
        compiler_params=pltpu.CompilerParams(dimension_semantics=("parallel", "arbitrary")),
    )(bu_r, bu_i, a_r, a_i)


def _scan_bwd(name, g_r, g_i, p_r, p_i, a_r, a_i):
    T, gp = g_r.shape
    tl = _scan_lanes(gp)
    tc = _tile(T, 256)
    nt = T // tc
    ng = tc // 8

    def body(gr_ref, gi_ref, pr_ref, pi_ref, ar_ref, ai_ref, lr_ref, li_ref, dar_ref, dai_ref, sr, si, accr, acci):
        t = pl.program_id(1)

        @pl.when(t == 0)
        def _():
            sr[...] = jnp.zeros_like(sr)
            si[...] = jnp.zeros_like(si)
            accr[...] = jnp.zeros_like(accr)
            acci[...] = jnp.zeros_like(acci)
        ar, ai = ar_ref[...], ai_ref[...]

        def step(kk, carry):
            lr, li = carry
            base = pl.multiple_of((ng - 1 - kk) * 8, 8)
            gr8 = gr_ref[pl.ds(base, 8), :]
            gi8 = gi_ref[pl.ds(base, 8), :]
            rows_r, rows_i = [None] * 8, [None] * 8
            for j in range(7, -1, -1):
                nr = gr8[j:j + 1, :] + ar * lr + ai * li
                ni = gi8[j:j + 1, :] - ai * lr + ar * li
                lr, li = nr, ni
                rows_r[j] = lr
                rows_i[j] = li
            lam_r = jnp.concatenate(rows_r, axis=0)
            lam_i = jnp.concatenate(rows_i, axis=0)
            lr_ref[pl.ds(base, 8), :] = lam_r
            li_ref[pl.ds(base, 8), :] = lam_i
            pr8 = pr_ref[pl.ds(base, 8), :]
            pi8 = pi_ref[pl.ds(base, 8), :]
            accr[...] += lam_r * pr8 + lam_i * pi8
            acci[...] += lam_i * pr8 - lam_r * pi8
            return lr, li

        lr, li = lax.fori_loop(0, ng, step, (sr[...], si[...]))
        sr[...] = lr
        si[...] = li

        @pl.when(t == nt - 1)
        def _():
            dar_ref[...] = jnp.sum(accr[...], axis=0, keepdims=True)
            dai_ref[...] = jnp.sum(acci[...], axis=0, keepdims=True)

    spec = _bs((tc, tl), lambda l, t: (nt - 1 - t, l))
    a_spec = _bs((1, tl), lambda l, t: (0, l))
    return _pcall(
        body, name=name, grid=(gp // tl, nt),
        in_specs=[spec] * 4 + [a_spec, a_spec], out_specs=[spec, spec, a_spec, a_spec],
        out_shape=[jax.ShapeDtypeStruct((T, gp), F32)] * 2 + [jax.ShapeDtypeStruct((1, gp), F32)] * 2,
        scratch_shapes=[pltpu.VMEM((1, tl), F32), pltpu.VMEM((1, tl), F32),
                        pltpu.VMEM((8, tl), F32), pltpu.VMEM((8, tl), F32)],
        compiler_params=pltpu.CompilerParams(dimension_semantics=("parallel", "arbitrary")),
    )(g_r, g_i, p_r, p_i, a_r, a_i)


_GELU_C = math.sqrt(2.0 / math.pi)


def _gelu(v):
    return 0.5 * v * (1.0 + jnp.tanh(_GELU_C * (v + 0.044715 * v * v * v)))


def _gelu_grad(v):
    th = jnp.tanh(_GELU_C * (v + 0.044715 * v * v * v))
    return 0.5 * (1.0 + th) + 0.5 * v * (1.0 - th * th) * _GELU_C * (1.0 + 3.0 * 0.044715 * v * v)


def _mm_shared_lhs(name, a, bs_, dims, *, tm, tn, tk):
    if dims == NN:
        (M, K), N = a.shape, bs_[0].shape[1]
    elif dims == NT:
        (M, K), N = a.shape, bs_[0].shape[0]
    else:
        (K, M), N = a.shape, bs_[0].shape[1]
    tm, tn, tk = _tile(M, tm, 128), _tile(N, tn, 128), _tile(K, tk, 128)
    if dims == NN:
        a_spec, b_spec = _bs((tm, tk), lambda i, j, k: (i, k)), _bs((tk, tn), lambda i, j, k: (k, j))
    elif dims == NT:
        a_spec, b_spec = _bs((tm, tk), lambda i, j, k: (i, k)), _bs((tn, tk), lambda i, j, k: (j, k))
    else:
        a_spec, b_spec = _bs((tk, tm), lambda i, j, k: (k, i)), _bs((tk, tn), lambda i, j, k: (k, j))
    o_spec = _bs((tm, tn), lambda i, j, k: (i, j))
    n = len(bs_)
    return _mm(name, (M // tm, N // tn, K // tk), [(a, a_spec, b, b_spec, dims, i) for i, b in enumerate(bs_)],
               [(tm, tn)] * n, [((M, N), F32, o_spec)] * n, lambda accs, ex: tuple(accs))


def _s5_fwd(tag, u, prep, d_skip, w_glu, b_glu):
    a_r, a_i, b_r, b_i, c_r, c_in = prep
    T, W = u.shape
    gp = a_r.shape[1]
    bu_r, bu_i = _mm_shared_lhs(tag + "_bu", u, [b_r, b_i], NN, tm=512, tn=1024, tk=W)
    x_r, x_i, p_r, p_i = _scan_fwd(tag + "_scan", bu_r, bu_i, a_r, a_i)
    tm, tk = _tile(T, 512), _tile(gp, 512)
    x_spec = _bs((tm, tk), lambda i, j, k: (i, k))
    c_spec = _bs((tk, W), lambda i, j, k: (k, 0))
    full = _bs((tm, W), lambda i, j, k: (i, 0))
    row = _bs((1, W), lambda i, j, k: (0, 0))

    def epi(accs, ex):
        y = accs[0] + ex[1] * ex[0]
        return y, _gelu(y)

    ypre, yg = _mm(tag + "_y", (T // tm, 1, gp // tk),
                   [(x_r, x_spec, c_r, c_spec, NN, 0), (x_i, x_spec, c_in, c_spec, NN, 0)],
                   [(tm, W)], [((T, W), F32, full)] * 2, epi, [(u, full), (d_skip.reshape(1, W), row)])

    def epi2(accs, ex):
        pre = accs[0] + ex[1]
        return ex[0] * _sigmoid(pre), pre

    out, pre = _mm_simple(tag + "_glu", yg, w_glu, NN, tm=512, tn=W, tk=W, out_dtype=(F32, F32), epilogue=epi2,
                          extras=[yg, b_glu.reshape(1, W)])
    return out, (u, x_r, x_i, p_r, p_i, ypre, yg, pre)


def _s5_bwd(tag, d_out, saved, prep, d_skip, w_glu):
    u, x_r, x_i, p_r, p_i, ypre, yg, pre = saved
    a_r, a_i, b_r, b_i, c_r, c_in = prep
    T, W = u.shape

    def gate_fn(t, b):
        gate = _sigmoid(t[2])
        dpre = t[0] * t[1] * gate * (1.0 - gate)
        return (dpre, t[0] * gate), (_colsum(dpre),)

    dpre, tmp, db_glu = _rowwise(tag + "_gateb", gate_fn, [d_out, yg, pre], [], [(W, F32), (W, F32)], [W])
    dw_glu = _mm_simple(tag + "_dwglu", yg, dpre, TN, tm=W, tn=W, tk=512)
    dy = _mm_simple(tag + "_dyg", dpre, w_glu, NT, tm=512, tn=W, tk=W,
                    epilogue=lambda accs, ex: ((accs[0] + ex[0]) * _gelu_grad(ex[1]),), extras=[tmp, ypre])
    dd, = _rowwise(tag + "_dd", lambda t, b: ((), (_colsum(t[0] * t[1]),)), [dy, u], [], [], [W])
    dx_r, dx_i = _mm_shared_lhs(tag + "_dx", dy, [c_r, c_in], NT, tm=512, tn=1024, tk=W)
    dc_r = _mm_simple(tag + "_dcr", x_r, dy, TN, tm=512, tn=W, tk=512)
    dc_in = _mm_simple(tag + "_dci", x_i, dy, TN, tm=512, tn=W, tk=512)
    lam_r, lam_i, da_r, da_i = _scan_bwd(tag + "_scanb", dx_r, dx_i, p_r, p_i, a_r, a_i)
    gp = a_r.shape[1]
    tm, tk = _tile(T, 512), _tile(gp, 512)
    l_spec = _bs((tm, tk), lambda i, j, k: (i, k))
    b_spec = _bs((W, tk), lambda i, j, k: (0, k))
    full = _bs((tm, W), lambda i, j, k: (i, 0))
    row = _bs((1, W), lambda i, j, k: (0, 0))
    du, = _mm(tag + "_du", (T // tm, 1, gp // tk),
              [(lam_r, l_spec, b_r, b_spec, NT, 0), (lam_i, l_spec, b_i, b_spec, NT, 0)],
              [(tm, W)], [((T, W), F32, full)], lambda accs, ex: (accs[0] + ex[1] * ex[0],),
              [(dy, full), (d_skip.reshape(1, W), row)])
    db_r, db_i = _mm_shared_lhs(tag + "_db", u, [lam_r, lam_i], TN, tm=W, tn=1024, tk=512)
    return du, (da_r, da_i, db_r, db_i, dc_r, dc_in), dd, dw_glu, db_glu


def _swa_mask(n):
    B = SWA_BLOCK
    r = lax.broadcasted_iota(jnp.int32, (B, 2 * B), 0)
    c = lax.broadcasted_iota(jnp.int32, (B, 2 * B), 1)
    d = r + B - c
    return (d >= 0) & (d < B) & ((n > 0) | (c >= B))


def _swa_specs(T):
    B = SWA_BLOCK
    cur = lambda w: _bs((B, w), lambda n: (n, 0))
    prev = lambda w: _bs((B, w), lambda n: (jnp.maximum(n - 1, 0), 0))
    return cur, prev


def _swa_fwd(name, q, k, v, sinks_b):
    T = q.shape[0]
    B = SWA_BLOCK
    scale = SWA_HEAD_DIM ** -0.5
    per_kv = SWA_HEADS // SWA_KV_HEADS

    def body(q_ref, kc_ref, kp_ref, vc_ref, vp_ref, s_ref, o_ref, l_ref):
        n = pl.program_id(0)
        kcat = jnp.concatenate([kp_ref[...], kc_ref[...]], axis=0).astype(BF16)
        vcat = jnp.concatenate([vp_ref[...], vc_ref[...]], axis=0).astype(BF16)
        mask = _swa_mask(n)
        lane = lax.broadcasted_iota(jnp.int32, (B, 128), 1)
        lo = lane < SWA_HEAD_DIM
        lse_out = jnp.zeros((B, 128), F32)
        for jb in range(SWA_HEADS // 2):
            qblk = q_ref[:, jb * 128:(jb + 1) * 128]
            h = (2 * jb) // per_kv
            half_h = lo if h == 0 else jnp.logical_not(lo)
            outs = []
            for e in range(2):
                j = 2 * jb + e
                qa = qblk if e == h else pltpu.roll(qblk, SWA_HEAD_DIM, 1)
                qm = jnp.where(half_h, qa, 0.0).astype(BF16)
                s = lax.dot_general(qm, kcat, (NT, ((), ())), preferred_element_type=F32) * scale
                s = jnp.where(mask, s, -jnp.inf)
                sk = s_ref[j:j + 1, 0:1]
                m = jnp.maximum(jnp.max(s, axis=1, keepdims=True), sk)
                ex = jnp.exp(s - m)
                den = jnp.sum(ex, axis=1, keepdims=True) + jnp.exp(sk - m)
                p = ex / den
                r = lax.dot_general(p.astype(BF16), vcat, (NN, ((), ())), preferred_element_type=F32)
                outs.append(r if e == h else pltpu.roll(r, SWA_HEAD_DIM, 1))
                lse_out = jnp.where(lane == j, m + jnp.log(den), lse_out)
            o_ref[:, jb * 128:(jb + 1) * 128] = jnp.where(lo, outs[0], outs[1])
        l_ref[...] = lse_out

    cur, prev = _swa_specs(T)
    return _pcall(
        body, name=name, grid=(T // B,),
        in_specs=[cur(SWA_WIDTH), cur(SWA_KV_WIDTH), prev(SWA_KV_WIDTH), cur(SWA_KV_WIDTH), prev(SWA_KV_WIDTH),
                  _bs(sinks_b.shape, lambda n: (0, 0))],
        out_specs=[cur(SWA_WIDTH), cur(128)],
        out_shape=[jax.ShapeDtypeStruct((T, SWA_WIDTH), F32), jax.ShapeDtypeStruct((T, 128), F32)],
        compiler_params=pltpu.CompilerParams(dimension_semantics=("parallel",)),
    )(q, k, k, v, v, sinks_b)


def _swa_bwd(name, q, k, v, sinks_b, o, lse, do):
    T = q.shape[0]
    B = SWA_BLOCK
    scale = SWA_HEAD_DIM ** -0.5
    per_kv = SWA_HEADS // SWA_KV_HEADS

    def body(q_ref, kc_ref, kp_ref, vc_ref, vp_ref, s_ref, o_ref, l_ref, do_ref,
             dq_ref, dkc_ref, dkp_ref, dvc_ref, dvp_ref, ds_ref):
        n = pl.program_id(0)

        @pl.when(n == 0)
        def _():
            ds_ref[...] = jnp.zeros_like(ds_ref)
        kcat = jnp.concatenate([kp_ref[...], kc_ref[...]], axis=0).astype(BF16)
        vcat = jnp.concatenate([vp_ref[...], vc_ref[...]], axis=0).astype(BF16)
        mask = _swa_mask(n)
        lane = lax.broadcasted_iota(jnp.int32, (B, 128), 1)
        lane1 = lax.broadcasted_iota(jnp.int32, (1, 128), 1)
        lo = lane < SWA_HEAD_DIM
        lblk = l_ref[...]
        dk = jnp.zeros((2 * B, 128), F32)
        dv = jnp.zeros((2 * B, 128), F32)
        dsink = jnp.zeros((1, 128), F32)
        for jb in range(SWA_HEADS // 2):
            sl = slice(jb * 128, (jb + 1) * 128)
            qblk, doblk = q_ref[:, sl], do_ref[:, sl]
            prod = doblk * o_ref[:, sl]
            h = (2 * jb) // per_kv
            half_h = lo if h == 0 else jnp.logical_not(lo)
            parts = []
            for e in range(2):
                j = 2 * jb + e
                half_e = lo if e == 0 else jnp.logical_not(lo)
                dsum = jnp.sum(jnp.where(half_e, prod, 0.0), axis=1, keepdims=True)
                lj = jnp.sum(jnp.where(lane == j, lblk, 0.0), axis=1, keepdims=True)
                qa = qblk if e == h else pltpu.roll(qblk, SWA_HEAD_DIM, 1)
                da = doblk if e == h else pltpu.roll(doblk, SWA_HEAD_DIM, 1)
                qm = jnp.where(half_h, qa, 0.0).astype(BF16)
                dm = jnp.where(half_h, da, 0.0).astype(BF16)
                s = lax.dot_general(qm, kcat, (NT, ((), ())), preferred_element_type=F32) * scale
                p = jnp.where(mask, jnp.exp(s - lj), 0.0)
                dp = lax.dot_general(dm, vcat, (NT, ((), ())), preferred_element_type=F32)
                dsb = (p * (dp - dsum) * scale).astype(BF16)
                dqa = lax.dot_general(dsb, kcat, (NN, ((), ())), preferred_element_type=F32)
                parts.append(dqa if e == h else pltpu.roll(dqa, SWA_HEAD_DIM, 1))
                dk = dk + lax.dot_general(dsb, qm, (TN, ((), ())), preferred_element_type=F32)
                dv = dv + lax.dot_general(p.astype(BF16), dm, (TN, ((), ())), preferred_element_type=F32)
                sk = s_ref[j:j + 1, 0:1]
                contrib = jnp.sum(jnp.exp(sk - lj) * dsum, axis=0, keepdims=True)
                dsink = jnp.where(lane1 == j, dsink - contrib, dsink)
            dq_ref[:, sl] = jnp.where(lo, parts[0], parts[1])
        dkp_ref[...] = dk[:B]
        dkc_ref[...] = dk[B:]
        dvp_ref[...] = dv[:B]
        dvc_ref[...] = dv[B:]
        ds_ref[...] += dsink

    cur, prev = _swa_specs(T)
    kv = jax.ShapeDtypeStruct((T, SWA_KV_WIDTH), F32)
    return _pcall(
        body, name=name, grid=(T // B,),
        in_specs=[cur(SWA_WIDTH), cur(SWA_KV_WIDTH), prev(SWA_KV_WIDTH), cur(SWA_KV_WIDTH), prev(SWA_KV_WIDTH),
                  _bs(sinks_b.shape, lambda n: (0, 0)), cur(SWA_WIDTH), cur(128), cur(SWA_WIDTH)],
        out_specs=[cur(SWA_WIDTH)] + [cur(SWA_KV_WIDTH)] * 4 + [_bs((1, 128), lambda n: (0, 0))],
        out_shape=[jax.ShapeDtypeStruct((T, SWA_WIDTH), F32), kv, kv, kv, kv, jax.ShapeDtypeStruct((1, 128), F32)],
        compiler_params=pltpu.CompilerParams(dimension_semantics=("arbitrary",)),
    )(q, k, k, v, v, sinks_b, o, lse, do)


def _shift_add(name, cur, prv):
    T, W = cur.shape
    B = SWA_BLOCK
    nb = T // B

    def body(c_ref, p_ref, o_ref):
        n = pl.program_id(0)
        o_ref[...] = c_ref[...] + jnp.where(n < nb - 1, p_ref[...], 0.0)

    return _pcall(
        body, name=name, grid=(nb,),
        in_specs=[_bs((B, W), lambda n: (n, 0)), _bs((B, W), lambda n: (jnp.minimum(n + 1, nb - 1), 0))],
        out_specs=_bs((B, W), lambda n: (n, 0)), out_shape=jax.ShapeDtypeStruct((T, W), F32),
        compiler_params=pltpu.CompilerParams(dimension_semantics=("parallel",)),
    )(cur, prv)


MLA_BLOCK = 512
_MLA_SCALE = (MLA_NOPE + MLA_ROPE) ** -0.5


def _lower_tri(tb):
    r = lax.broadcasted_iota(jnp.int32, (tb, tb), 0)
    c = lax.broadcasted_iota(jnp.int32, (tb, tb), 1)
    return c <= r


def _mla_fwd(name, qcat, kcat, v):
    T = qcat.shape[0]
    tb = _tile(T, MLA_BLOCK)
    nb = T // tb

    def body(q_ref, k_ref, v_ref, o_ref, l_ref):
        qi = pl.program_id(1)
        q = q_ref[...]

        def block(kb, carry, diagonal):
            m_prev, l_prev, acc = carry
            rows = pl.ds(pl.multiple_of(kb * tb, tb), tb)
            s = lax.dot_general(q, k_ref[rows, :], (NT, ((), ())), preferred_element_type=F32) * _MLA_SCALE
            if diagonal:
                s = jnp.where(_lower_tri(tb), s, -jnp.inf)
            m_new = jnp.maximum(m_prev, jnp.max(s, axis=1, keepdims=True))
            alpha = jnp.exp(m_prev - m_new)
            p = jnp.exp(s - m_new)
            l_new = alpha * l_prev + jnp.sum(p, axis=1, keepdims=True)
            acc = alpha * acc + lax.dot_general(p.astype(BF16), v_ref[rows, :], (NN, ((), ())),
                                                preferred_element_type=F32)
            return m_new, l_new, acc

        init = (jnp.full((tb, 1), -jnp.inf, F32), jnp.zeros((tb, 1), F32), jnp.zeros((tb, MLA_V), F32))
        carry = lax.fori_loop(0, qi, lambda kb, c: block(kb, c, False), init)
        m_fin, l_fin, acc = block(qi, carry, True)
        o_ref[...] = acc / l_fin
        l_ref[...] = m_fin + jnp.log(l_fin)

    return _pcall(
        body, name=name, grid=(MLA_HEADS, nb),
        in_specs=[_bs((tb, MLA_SLOT), lambda h, qi: (qi, h)), _bs((T, MLA_SLOT), lambda h, qi: (0, h)),
                  _bs((T, MLA_V), lambda h, qi: (0, h))],
        out_specs=[_bs((tb, MLA_V), lambda h, qi: (qi, h)), _bs((None, tb, 1), lambda h, qi: (h, qi, 0))],
        out_shape=[jax.ShapeDtypeStruct((T, MLA_WIDTH), F32), jax.ShapeDtypeStruct((MLA_HEADS, T, 1), F32)],
        compiler_params=pltpu.CompilerParams(dimension_semantics=("parallel", "arbitrary")),
    )(qcat, kcat, v)


def _mla_bwd_q(name, qcat, kcat, v, o, lse, do):
    T = qcat.shape[0]
    tb = _tile(T, MLA_BLOCK)
    nb = T // tb

    def body(q_ref, k_ref, v_ref, o_ref, do_ref, l_ref, dq_ref, ds_ref):
        qi = pl.program_id(1)
        q = q_ref[...]
        do = do_ref[...]
        dsum = jnp.sum(do * o_ref[...], axis=1, keepdims=True)
        ds_ref[...] = dsum
        dob = do.astype(BF16)
        lse_q = l_ref[...]

        def block(kb, acc, diagonal):
            rows = pl.ds(pl.multiple_of(kb * tb, tb), tb)
            k = k_ref[rows, :]
            s = lax.dot_general(q, k, (NT, ((), ())), preferred_element_type=F32) * _MLA_SCALE
            p = jnp.exp(s - lse_q)
            if diagonal:
                p = jnp.where(_lower_tri(tb), p, 0.0)
            dp = lax.dot_general(dob, v_ref[rows, :], (NT, ((), ())), preferred_element_type=F32)
            ds = (p * (dp - dsum) * _MLA_SCALE).astype(BF16)
            return acc + lax.dot_general(ds, k, (NN, ((), ())), preferred_element_type=F32)

        acc = lax.fori_loop(0, qi, lambda kb, a: block(kb, a, False), jnp.zeros((tb, MLA_SLOT), F32))
        dq_ref[...] = block(qi, acc, True)

    qs = lambda w: _bs((tb, w), lambda h, qi: (qi, h))
    col = _bs((None, tb, 1), lambda h, qi: (h, qi, 0))
    return _pcall(
        body, name=name, grid=(MLA_HEADS, nb),
        in_specs=[qs(MLA_SLOT), _bs((T, MLA_SLOT), lambda h, qi: (0, h)), _bs((T, MLA_V), lambda h, qi: (0, h)),
                  qs(MLA_V), qs(MLA_V), col],
        out_specs=[qs(MLA_SLOT), col],
        out_shape=[jax.ShapeDtypeStruct((T, MLA_HEADS * MLA_SLOT), F32), jax.ShapeDtypeStruct((MLA_HEADS, T, 1), F32)],
        compiler_params=pltpu.CompilerParams(dimension_semantics=("parallel", "arbitrary")),
    )(qcat, kcat, v, o, do, lse)


def _mla_bwd_kv(name, qcat, kcat, v, lse, dsum, do):
    T = qcat.shape[0]
    tb = _tile(T, MLA_BLOCK)
    nb = T // tb

    def body(q_ref, k_ref, v_ref, do_ref, l_ref, ds_ref, dk_ref, dv_ref):
        ki = pl.program_id(1)
        k = k_ref[...]
        vv = v_ref[...]

        def block(qb, carry, diagonal):
            dk, dv = carry
            rows = pl.ds(pl.multiple_of(qb * tb, tb), tb)
            q = q_ref[rows, :]
            dob = do_ref[rows, :].astype(BF16)
            s = lax.dot_general(q, k, (NT, ((), ())), preferred_element_type=F32) * _MLA_SCALE
            p = jnp.exp(s - l_ref[rows, :])
            if diagonal:
                p = jnp.where(_lower_tri(tb), p, 0.0)
            dp = lax.dot_general(dob, vv, (NT, ((), ())), preferred_element_type=F32)
            ds = (p * (dp - ds_ref[rows, :]) * _MLA_SCALE).astype(BF16)
            dv = dv + lax.dot_general(p.astype(BF16), dob, (TN, ((), ())), preferred_element_type=F32)
            dk = dk + lax.dot_general(ds, q, (TN, ((), ())), preferred_element_type=F32)
            return dk, dv

        carry = block(ki, (jnp.zeros((tb, MLA_SLOT), F32), jnp.zeros((tb, MLA_V), F32)), True)
        dk, dv = lax.fori_loop(ki + 1, nb, lambda qb, c: block(qb, c, False), carry)
        dk_ref[...] = dk
        dv_ref[...] = dv

    ks = lambda w: _bs((tb, w), lambda h, ki: (ki, h))
    col = _bs((None, T, 1), lambda h, ki: (h, 0, 0))
    return _pcall(
        body, name=name, grid=(MLA_HEADS, nb),
        in_specs=[_bs((T, MLA_SLOT), lambda h, ki: (0, h)), ks(MLA_SLOT), ks(MLA_V),
                  _bs((T, MLA_V), lambda h, ki: (0, h)), col, col],
        out_specs=[ks(MLA_SLOT), ks(MLA_V)],
        out_shape=[jax.ShapeDtypeStruct((T, MLA_HEADS * MLA_SLOT), F32), jax.ShapeDtypeStruct((T, MLA_WIDTH), F32)],
        compiler_params=pltpu.CompilerParams(dimension_semantics=("parallel", "arbitrary")),
    )(qcat, kcat, v, do, lse, dsum)


def _kcat_fwd(name, kpre, krs, c256, s256):
    def fn(t, b):
        kr = t[1] * t[2] + _half_swap(t[1]) * t[3]
        return (t[0] + jnp.tile(kr, (1, MLA_HEADS)),), ()
    return _rowwise(name, fn, [kpre, krs, c256, s256], [], [(kpre.shape[1], BF16)])[0]


def _kcat_bwd(name, dkcat, c256, s256):
    def fn(t, b):
        d = t[0][:, 0:MLA_SLOT]
        for h in range(1, MLA_HEADS):
            d = d + t[0][:, h * MLA_SLOT:(h + 1) * MLA_SLOT]
        return (d * t[1] + _half_swap(d * t[2]),), ()
    return _rowwise(name, fn, [dkcat, c256, s256], [], [(MLA_SLOT, F32)])[0]


def _in_widths(ws):
    return [ws, SWA_WIDTH, SWA_KV_WIDTH, SWA_KV_WIDTH, MLA_Q_RANK, MLA_KV_RANK, MLA_SLOT]


def _kr_offset(ws):
    return ws + SWA_WIDTH + 2 * SWA_KV_WIDTH + MLA_Q_RANK + MLA_KV_RANK


def _pad_w_in(w, ws):
    z = lambda n: jnp.zeros((w.shape[0], n), w.dtype)
    o = _kr_offset(ws)
    return jnp.concatenate([w[:, :o], z(MLA_NOPE), w[:, o:], z(MLA_SLOT - MLA_NOPE - MLA_ROPE)], axis=1)


def _unpad_w_in(dw, ws):
    o = _kr_offset(ws)
    return jnp.concatenate([dw[:, :o], dw[:, o + MLA_NOPE:o + MLA_NOPE + MLA_ROPE]], axis=1)


def _pad_w_uq(w):
    r = w.shape[0]
    w3 = w.reshape(r, MLA_HEADS, MLA_NOPE + MLA_ROPE)
    return jnp.pad(w3, ((0, 0), (0, 0), (0, MLA_SLOT - MLA_NOPE - MLA_ROPE))).reshape(r, MLA_HEADS * MLA_SLOT)


def _unpad_w_uq(dw):
    r = dw.shape[0]
    return dw.reshape(r, MLA_HEADS, MLA_SLOT)[..., :MLA_NOPE + MLA_ROPE].reshape(r, -1)


def _pad_w_ukv(w):
    r = w.shape[0]
    w3 = w.reshape(r, MLA_HEADS, MLA_NOPE + MLA_V)
    wk = jnp.pad(w3[..., :MLA_NOPE], ((0, 0), (0, 0), (0, MLA_SLOT - MLA_NOPE))).reshape(r, MLA_HEADS * MLA_SLOT)
    wv = w3[..., MLA_NOPE:].reshape(r, MLA_WIDTH)
    return wk, wv


def _unpad_w_ukv(dwk, dwv):
    r = dwk.shape[0]
    return jnp.concatenate([dwk.reshape(r, MLA_HEADS, MLA_SLOT)[..., :MLA_NOPE], dwv.reshape(r, MLA_HEADS, MLA_V)],
                           axis=-1).reshape(r, -1)


def _layer_prep(lw):
    ws = lw['ssm_d'].shape[0]
    p = dict(lw)
    w_in_pad = _pad_w_in(lw['w_in'], ws)
    p['w_in_pad'] = w_in_pad
    offs = np.cumsum([0] + _in_widths(ws))
    p['w_in_parts'] = [w_in_pad[:, offs[i]:offs[i + 1]] for i in range(7)]
    p['s5_prep'] = _s5_prep(lw['ssm_log_dt'], lw['ssm_a_re'], lw['ssm_a_im'], lw['ssm_b_re'], lw['ssm_b_im'],
                            lw['ssm_c_re'], lw['ssm_c_im'])
    p['sinks_b'] = jnp.broadcast_to(lw['swa_sinks'][:, None], (SWA_HEADS, 128))
    p['w_uq_pad'] = _pad_w_uq(lw['mla_w_uq'])
    p['w_k_pad'], p['w_v'] = _pad_w_ukv(lw['mla_w_ukv'])
    p['w_ukv_pad'] = jnp.concatenate([p['w_k_pad'], p['w_v']], axis=1)
    b = [0, ws, ws + SWA_WIDTH, ws + SWA_WIDTH + MLA_WIDTH]
    p['w_out_g'] = [lw['w_out'][b[g]:b[g + 1]] for g in range(3)]
    p['out_norm_g'] = [lw['out_norm'][b[g]:b[g + 1]] for g in range(3)]
    return p


def _mixer_fwd(tag, x, p, tabs):
    T, D = x.shape
    c64, s64, c256, s256 = tabs
    ws = p['ssm_d'].shape[0]
    widths = _in_widths(ws)
    n_in = sum(widths)
    offs = [int(o) for o in np.cumsum([0] + widths[:-1])]
    h, rstd = _rms_fwd(tag + "_rms", x, p['mix_norm'])
    tm = _tile(T, 256)
    row_i = lambda w: _bs((tm, w), lambda i, j, k: (i, 0))
    parts = _mm(tag + "_in", (T // tm, 1, 1),
                [(h, row_i(D), p['w_in_pad'], _bs((D, n_in), lambda i, j, k: (0, 0)), NN, 0)],
                [(tm, n_in)], [((T, w), F32, row_i(w)) for w in widths],
                lambda accs, ex: tuple(accs[0][:, o:o + w] for o, w in zip(offs, widths)))
    u, q, k, v, cq, ckv, krs = parts
    y_ssm, s5_saved = _s5_fwd(tag + "_s5", u, p['s5_prep'], p['ssm_d'], p['ssm_w_glu'], p['ssm_b_glu'])
    q_r = _rope(tag + "_ropeq", q, c64, s64)
    k_r = _rope(tag + "_ropek", k, c64, s64)
    y_swa, lse_swa = _swa_fwd(tag + "_swa", q_r, k_r, v, p['sinks_b'])
    cqn, r_q = _rms_fwd(tag + "_rmsq", cq, p['mla_q_norm'])
    ckvn, r_kv = _rms_fwd(tag + "_rmskv", ckv, p['mla_kv_norm'])
    qpre = _mm_simple(tag + "_uq", cqn, p['w_uq_pad'], NN, tm=512, tn=1024, tk=MLA_Q_RANK)
    nk_, nv_ = MLA_HEADS * MLA_SLOT, MLA_WIDTH
    kpre, vm = _mm(tag + "_ukv", (T // tm, 1, 1),
                   [(ckvn, row_i(MLA_KV_RANK), p['w_ukv_pad'], _bs((MLA_KV_RANK, nk_ + nv_), lambda i, j, k: (0, 0)),
                     NN, 0)],
                   [(tm, nk_ + nv_)], [((T, nk_), F32, row_i(nk_)), ((T, nv_), BF16, row_i(nv_))],
                   lambda accs, ex: (accs[0][:, :nk_], accs[0][:, nk_:]))
    qcat = _rope(tag + "_ropemq", qpre, c256, s256, out_dtype=BF16)
    kcat = _kcat_fwd(tag + "_kcat", kpre, krs, c256, s256)
    y_mla, lse_mla = _mla_fwd(tag + "_mla", qcat, kcat, vm)
    ys = [y_ssm, y_swa, y_mla]
    yn, rs = [], []
    for g in range(3):
        n_, r_ = _rms_fwd(f"{tag}_rmso{g}", ys[g], p['out_norm_g'][g])
        yn.append(n_)
        rs.append(r_)
    tm3, tn3 = _tile(T, 512), _tile(D, 1024)
    pairs = []
    for g in range(3):
        wg = ys[g].shape[1]
        pairs.append((yn[g], _bs((tm3, wg), lambda i, j, k: (i, 0)), p['w_out_g'][g],
                      _bs((wg, tn3), lambda i, j, k: (0, j)), NN, 0))
    o_spec = _bs((tm3, tn3), lambda i, j, k: (i, j))
    x2, = _mm(tag + "_out", (T // tm3, D // tn3, 1), pairs, [(tm3, tn3)], [((T, D), F32, o_spec)],
              lambda accs, ex: (ex[0] + accs[0],), [(x, o_spec)])
    saved = (x, h, rstd, q_r, k_r, v, cq, ckv, s5_saved, y_swa, lse_swa, cqn, r_q, ckvn, r_kv, qcat, kcat, vm,
             y_mla, lse_mla, ys, yn, rs)
    return x2, saved


def _mixer_bwd(tag, dx2, saved, p, tabs):
    (x, h, rstd, q_r, k_r, v, cq, ckv, s5_saved, y_swa, lse_swa, cqn, r_q, ckvn, r_kv, qcat, kcat, vm,
     y_mla, lse_mla, ys, yn, rs) = saved
    T, D = x.shape
    c64, s64, c256, s256 = tabs
    ws = p['ssm_d'].shape[0]
    g_ = {}
    dys, dwo, don = [], [], []
    for g in range(3):
        wg = ys[g].shape[1]
        dyn = _mm_simple(f"{tag}_dyn{g}", dx2, p['w_out_g'][g], NT, tm=512, tn=wg, tk=1024)
        dwo.append(_mm_simple(f"{tag}_dwo{g}", yn[g], dx2, TN, tm=512, tn=1024, tk=512))
        dy_g, don_g = _rms_bwd(f"{tag}_rmsob{g}", dyn, ys[g], rs[g], p['out_norm_g'][g])
        dys.append(dy_g)
        don.append(don_g)
    g_['w_out'] = jnp.concatenate(dwo, axis=0)
    g_['out_norm'] = jnp.concatenate(don, axis=1)[0]
    dqcat, dsum = _mla_bwd_q(tag + "_mlabq", qcat, kcat, vm, y_mla, lse_mla, dys[2])
    dkcat, dvm = _mla_bwd_kv(tag + "_mlabkv", qcat, kcat, vm, lse_mla, dsum, dys[2])
    dqpre = _rope(tag + "_ropemqb", dqcat, c256, s256, backward=True)
    dkrs = _kcat_bwd(tag + "_kcatb", dkcat, c256, s256)
    g_['mla_w_uq'] = _unpad_w_uq(_mm_simple(tag + "_dwuq", cqn, dqpre, TN, tm=MLA_Q_RANK, tn=1024, tk=512))
    dcqn = _mm_simple(tag + "_dcqn", dqpre, p['w_uq_pad'], NT, tm=512, tn=MLA_Q_RANK, tk=1024)
    dcq, dqn = _rms_bwd(tag + "_rmsqb", dcqn, cq, r_q, p['mla_q_norm'])
    g_['mla_q_norm'] = dqn[0]
    dwk = _mm_simple(tag + "_dwk", ckvn, dkcat, TN, tm=MLA_KV_RANK, tn=1024, tk=512)
    dwv = _mm_simple(tag + "_dwv", ckvn, dvm, TN, tm=MLA_KV_RANK, tn=1024, tk=512)
    g_['mla_w_ukv'] = _unpad_w_ukv(dwk, dwv)
    tm = _tile(T, 512)
    nk_, nv_ = MLA_HEADS * MLA_SLOT, MLA_WIDTH
    tkk = _tile(nk_, 1024)
    dckvn_k = _mm_simple(tag + "_dckvk", dkcat, p['w_k_pad'], NT, tm=512, tn=MLA_KV_RANK, tk=tkk)
    dckvn = _mm_simple(tag + "_dckvv", dvm, p['w_v'], NT, tm=512, tn=MLA_KV_RANK, tk=nv_,
                       epilogue=lambda accs, ex: (accs[0] + ex[0],), extras=[dckvn_k])
    dckv, dkvn = _rms_bwd(tag + "_rmskvb", dckvn, ckv, r_kv, p['mla_kv_norm'])
    g_['mla_kv_norm'] = dkvn[0]
    dq_r, dkc, dkp, dvc, dvp, dsinks = _swa_bwd(tag + "_swab", q_r, k_r, v, p['sinks_b'], y_swa, lse_swa, dys[1])
    g_['swa_sinks'] = dsinks[0, :SWA_HEADS]
    dk_r = _shift_add(tag + "_dksum", dkc, dkp)
    dv = _shift_add(tag + "_dvsum", dvc, dvp)
    dq = _rope(tag + "_ropeqb", dq_r, c64, s64, backward=True)
    dk = _rope(tag + "_ropekb", dk_r, c64, s64, backward=True)
    du, s5g, dd, dw_glu, db_glu = _s5_bwd(tag + "_s5b", dys[0], s5_saved, p['s5_prep'], p['ssm_d'], p['ssm_w_glu'])
    g_['ssm_d'], g_['ssm_w_glu'], g_['ssm_b_glu'] = dd[0], dw_glu, db_glu[0]
    _, pull = jax.vjp(_s5_prep, p['ssm_log_dt'], p['ssm_a_re'], p['ssm_a_im'], p['ssm_b_re'], p['ssm_b_im'],
                      p['ssm_c_re'], p['ssm_c_im'])
    for name, val in zip(['ssm_log_dt', 'ssm_a_re', 'ssm_a_im', 'ssm_b_re', 'ssm_b_im', 'ssm_c_re', 'ssm_c_im'],
                         pull(tuple(s5g))):
        g_[name] = val
    dparts = [du, dq, dk, dv, dcq, dckv, dkrs]
    widths = _in_widths(ws)
    tn = _tile(D, 1024)
    pairs = []
    for dp_, wp_, w in zip(dparts, p['w_in_parts'], widths):
        pairs.append((dp_, _bs((tm, w), lambda i, j, k: (i, 0)), wp_, _bs((tn, w), lambda i, j, k: (j, 0)), NT, 0))
    o_spec = _bs((tm, tn), lambda i, j, k: (i, j))
    dh, = _mm(tag + "_dh", (T // tm, D // tn, 1), pairs, [(tm, tn)], [((T, D), F32, o_spec)],
              lambda accs, ex: (accs[0],))
    tmw, tk = _tile(D, 512), _tile(T, 512)
    pairs = []
    for i_, (dp_, w) in enumerate(zip(dparts, widths)):
        pairs.append((h, _bs((tk, tmw), lambda i, j, k: (k, i)), dp_, _bs((tk, w), lambda i, j, k: (k, 0)), TN, i_))
    dws = _mm(tag + "_dwin", (D // tmw, 1, T // tk), pairs, [(tmw, w) for w in widths],
              [((D, w), F32, _bs((tmw, w), lambda i, j, k: (i, 0))) for w in widths], lambda accs, ex: tuple(accs))
    g_['w_in'] = _unpad_w_in(jnp.concatenate(dws, axis=1), ws)
    dx, dmix = _rms_bwd(tag + "_rmsb", dh, x, rstd, p['mix_norm'], dres=dx2)
    g_['mix_norm'] = dmix[0]
    return dx, g_


def _final_loss(name, x, target, gain):
    D = x.shape[1]

    def fn(t, b):
        xv = t[0]
        r = lax.rsqrt(jnp.mean(xv * xv, axis=-1, keepdims=True) + EPS)
        xh = xv * r
        err = xh * b[0] - t[1]
        part = 0.5 * jnp.sum(jnp.sum(err * err, axis=-1, keepdims=True), axis=0, keepdims=True) / D
        dy = err / D
        dxh = dy * b[0]
        dx = r * (dxh - xh * jnp.mean(dxh * xh, axis=-1, keepdims=True))
        return (dx,), (jnp.broadcast_to(part, (1, 128)), _colsum(dy * xh))
    dx, part, dg = _rowwise(name, fn, [x, target], [gain.reshape(1, D)], [(D, F32)], [128, D])
    return part[0, 0], dx, dg[0]


def _device_step(x, positions, target, layers, final_norm):
    tabs = _rope_tables(positions)
    preps = [_layer_prep(lw) for lw in layers]
    saved = []
    for l, p in enumerate(preps):
        x, s1 = _ffn_fwd("ffn1", x, p['ffn1_norm'], p['ffn1_wg2'], p['ffn1_wu2'], p['ffn1_wd'])
        x, s2 = _mixer_fwd("mix", x, p, tabs)
        x, s3 = _ffn_fwd("ffn2", x, p['ffn2_norm'], p['ffn2_wg2'], p['ffn2_wu2'], p['ffn2_wd'])
        saved.append((s1, s2, s3))
    loss_part, dx, dfinal = _final_loss("loss", x, target, final_norm)
    grads = [None] * len(preps)
    for l in range(len(preps) - 1, -1, -1):
        p = preps[l]
        s1, s2, s3 = saved[l]
        g_ = {}
        dx, g_['ffn2_norm'], g_['ffn2_wg2'], g_['ffn2_wu2'], g_['ffn2_wd'] = _ffn_bwd(
            "ffn2b", dx, s3, p['ffn2_norm'], p['ffn2_wg2'], p['ffn2_wu2'], p['ffn2_wd'])
        dx, gm = _mixer_bwd("mixb", dx, s2, p, tabs)
        g_.update(gm)
        dx, g_['ffn1_norm'], g_['ffn1_wg2'], g_['ffn1_wu2'], g_['ffn1_wd'] = _ffn_bwd(
            "ffn1b", dx, s1, p['ffn1_norm'], p['ffn1_wg2'], p['ffn1_wu2'], p['ffn1_wd'])
        g_['ffn1_norm'], g_['ffn2_norm'] = g_['ffn1_norm'][0], g_['ffn2_norm'][0]
        grads[l] = g_
    return loss_part, dx, grads, dfinal


_ANY = pl.BlockSpec(memory_space=pl.ANY)
_CHIP_MASKS = (2, 1, 3)


def _place():
    x, y, c = lax.axis_index("x"), lax.axis_index("y"), lax.axis_index("c")
    chips = [(1 - x, y), (x, 1 - y), (1 - x, 1 - y)]
    return x, y, c, 2 * x + y, chips


def _stage_shard(name, wl):
    r, c = wl.shape
    tm = _tile(r, 256)
    shard = lambda: 2 * lax.axis_index("x") + lax.axis_index("y")

    def body(w_ref, o_ref):
        o_ref[...] = w_ref[...].astype(o_ref.dtype)

    return _pcall(
        body, name=name, grid=(r // tm,), in_specs=[_bs((tm, c), lambda i: (i, 0))],
        out_specs=_bs((None, tm, c), lambda i: (shard(), i, 0)),
        out_shape=jax.ShapeDtypeStruct((N_SHARD, r, c), BF16),
        compiler_params=pltpu.CompilerParams(dimension_semantics=("parallel",)),
    )(wl)


def _gather_call(name, staged):
    n = len(staged)

    def body(*refs):
        outs = refs[n:2 * n]
        ici_send, ici_recv, d2d_send, d2d_recv = refs[2 * n:]
        x, y, c, s, chips = _place()
        sib = (x, y, 1 - c)

        def rows(a, cc):
            half = outs[a].shape[1] // 2
            return pl.ds(cc * half, half)

        sends, fwds = [], []
        for a in range(n):
            for j in range(3):
                mine = outs[a].at[s, rows(a, c)]
                cp = pltpu.make_async_remote_copy(
                    src_ref=mine, dst_ref=mine, send_sem=ici_send.at[a * 3 + j], recv_sem=ici_recv.at[a * 3 + j],
                    device_id=(*chips[j], c), device_id_type=MESH)
                cp.start()
                sends.append(cp)
        for a in range(n):
            for j in range(3):
                src_s = s ^ _CHIP_MASKS[j]
                got = outs[a].at[src_s, rows(a, c)]
                pltpu.make_async_remote_copy(
                    src_ref=got, dst_ref=got, send_sem=ici_send.at[a * 3 + j], recv_sem=ici_recv.at[a * 3 + j],
                    device_id=(*chips[j], c), device_id_type=MESH).wait_recv()
                fw = pltpu.make_async_remote_copy(
                    src_ref=got, dst_ref=got, send_sem=d2d_send.at[a * 3 + j], recv_sem=d2d_recv.at[a * 3 + j],
                    device_id=sib, device_id_type=MESH)
                fw.start()
                fwds.append(fw)
        for a in range(n):
            for j in range(3):
                theirs = outs[a].at[s ^ _CHIP_MASKS[j], rows(a, 1 - c)]
                pltpu.make_async_remote_copy(
                    src_ref=theirs, dst_ref=theirs, send_sem=d2d_send.at[a * 3 + j], recv_sem=d2d_recv.at[a * 3 + j],
                    device_id=sib, device_id_type=MESH).wait_recv()
        for cp in sends + fwds:
            cp.wait_send()

    return _pcall(
        body, name=name, in_specs=[_ANY] * n, out_specs=[_ANY] * n,
        out_shape=[jax.ShapeDtypeStruct(a.shape, a.dtype) for a in staged],
        input_output_aliases={a: a for a in range(n)},
        scratch_shapes=[pltpu.SemaphoreType.DMA((3 * n,))] * 4,
    )(*staged)


def _pair_exchange_call(name, grads):
    n = len(grads)

    def body(*refs):
        ins, outs = refs[:n], refs[n:2 * n]
        send, recv = refs[2 * n:]
        x, y, c, s, chips = _place()
        cps = []
        for a in range(n):
            half = ins[a].shape[1] // 2
            cp = pltpu.make_async_remote_copy(
                src_ref=ins[a].at[:, pl.ds((1 - c) * half, half), :], dst_ref=outs[a],
                send_sem=send.at[a], recv_sem=recv.at[a], device_id=(x, y, 1 - c), device_id_type=MESH)
            cp.start()
            cps.append(cp)
        for cp in cps:
            cp.wait()

    return _pcall(
        body, name=name, in_specs=[_ANY] * n, out_specs=[_ANY] * n,
        out_shape=[jax.ShapeDtypeStruct((g.shape[0], g.shape[1] // 2, g.shape[2]), g.dtype) for g in grads],
        scratch_shapes=[pltpu.SemaphoreType.DMA((n,))] * 2,
    )(*grads)


def _half_rows(nb):
    return lambda i: lax.axis_index("c") * nb + i


def _pair_add(name, g, p1):
    ns, half, w = p1.shape
    tm = _tile(half, 256)
    nb = half // tm
    mine = _half_rows(nb)

    def body(g_ref, p_ref, o_ref):
        o_ref[...] = (g_ref[...] + p_ref[...]).astype(o_ref.dtype)

    return _pcall(
        body, name=name, grid=(ns, nb),
        in_specs=[_bs((None, tm, w), lambda s, i: (s, mine(i), 0)), _bs((None, tm, w), lambda s, i: (s, i, 0))],
        out_specs=_bs((None, tm, w), lambda s, i: (s, i, 0)), out_shape=jax.ShapeDtypeStruct(p1.shape, BF16),
        compiler_params=pltpu.CompilerParams(dimension_semantics=("parallel", "parallel")),
    )(g, p1)


def _chip_exchange_call(name, sums):
    n = len(sums)

    def body(*refs):
        ins, outs = refs[:n], refs[n:2 * n]
        send, recv = refs[2 * n:]
        x, y, c, s, chips = _place()
        cps = []
        for a in range(n):
            for j in range(3):
                cp = pltpu.make_async_remote_copy(
                    src_ref=ins[a].at[s ^ _CHIP_MASKS[j]], dst_ref=outs[a].at[j],
                    send_sem=send.at[a * 3 + j], recv_sem=recv.at[a * 3 + j],
                    device_id=(*chips[j], c), device_id_type=MESH)
                cp.start()
                cps.append(cp)
        for cp in cps:
            cp.wait()

    return _pcall(
        body, name=name, in_specs=[_ANY] * n, out_specs=[_ANY] * n,
        out_shape=[jax.ShapeDtypeStruct((3,) + a.shape[1:], a.dtype) for a in sums],
        scratch_shapes=[pltpu.SemaphoreType.DMA((3 * n,))] * 2,
    )(*sums)


def _chip_sum(name, g, p1, p2):
    ns, half, w = p1.shape
    tm = _tile(half, 256)
    nb = half // tm
    mine = _half_rows(nb)
    shard = lambda: 2 * lax.axis_index("x") + lax.axis_index("y")

    def body(g_ref, p_ref, a_ref, b_ref, c_ref, o_ref):
        o_ref[...] = (((g_ref[...] + p_ref[...]) + a_ref[...].astype(F32)) + b_ref[...].astype(F32)) \
            + c_ref[...].astype(F32)

    blk = lambda j: _bs((None, tm, w), lambda i: (j, i, 0))
    return _pcall(
        body, name=name, grid=(nb,),
        in_specs=[_bs((None, tm, w), lambda i: (shard(), mine(i), 0)), _bs((None, tm, w), lambda i: (shard(), i, 0)),
                  blk(0), blk(1), blk(2)],
        out_specs=_bs((tm, w), lambda i: (mine(i), 0)), out_shape=jax.ShapeDtypeStruct((2 * half, w), F32),
        compiler_params=pltpu.CompilerParams(dimension_semantics=("parallel",)),
    )(g, p1, p2, p2, p2)


def _pair_join_call(name, halves):
    n = len(halves)

    def body(*refs):
        outs = refs[n:2 * n]
        send, recv = refs[2 * n:]
        x, y, c, s, chips = _place()
        cps = []
        for a in range(n):
            half = outs[a].shape[0] // 2
            mine = outs[a].at[pl.ds(c * half, half)]
            cp = pltpu.make_async_remote_copy(
                src_ref=mine, dst_ref=mine, send_sem=send.at[a], recv_sem=recv.at[a],
                device_id=(x, y, 1 - c), device_id_type=MESH)
            cp.start()
            cps.append(cp)
        for a in range(n):
            half = outs[a].shape[0] // 2
            theirs = outs[a].at[pl.ds((1 - c) * half, half)]
            pltpu.make_async_remote_copy(
                src_ref=theirs, dst_ref=theirs, send_sem=send.at[a], recv_sem=recv.at[a],
                device_id=(x, y, 1 - c), device_id_type=MESH).wait_recv()
        for cp in cps:
            cp.wait_send()

    return _pcall(
        body, name=name, in_specs=[_ANY] * n, out_specs=[_ANY] * n,
        out_shape=[jax.ShapeDtypeStruct(h.shape, h.dtype) for h in halves],
        input_output_aliases={a: a for a in range(n)},
        scratch_shapes=[pltpu.SemaphoreType.DMA((n,))] * 2,
    )(*halves)


def _reduce_to_shards(tag, grads):
    p1 = _pair_exchange_call(tag + "_pairx", grads)
    sums = [_pair_add(tag + "_pairadd", g, p) for g, p in zip(grads, p1)]
    p2 = _chip_exchange_call(tag + "_chipx", sums)
    halves = [_chip_sum(tag + "_chipsum", g, p, q) for g, p, q in zip(grads, p1, p2)]
    return _pair_join_call(tag + "_join", halves)


def _all_sum_small(tag, buf):
    n_dev = 8

    def body(in_ref, out_ref, send, recv, loc):
        x, y, c, s, chips = _place()
        me = 4 * x + 2 * y + c
        lc = pltpu.make_async_copy(in_ref, out_ref.at[me], loc)
        lc.start()
        cps = []
        for k in range(1, n_dev):
            to = (x ^ (k >> 2), y ^ ((k >> 1) & 1), c ^ (k & 1))
            cp = pltpu.make_async_remote_copy(
                src_ref=in_ref, dst_ref=out_ref.at[me], send_sem=send.at[k - 1], recv_sem=recv.at[k - 1],
                device_id=to, device_id_type=MESH)
            cp.start()
            cps.append(cp)
        for k in range(1, n_dev):
            theirs = out_ref.at[me ^ k]
            pltpu.make_async_remote_copy(
                src_ref=theirs, dst_ref=theirs, send_sem=send.at[k - 1], recv_sem=recv.at[k - 1],
                device_id=(x, y, c), device_id_type=MESH).wait_recv()
        for cp in cps:
            cp.wait_send()
        lc.wait()

    allb = _pcall(
        body, name=tag + "_gather", in_specs=[_ANY], out_specs=_ANY,
        out_shape=jax.ShapeDtypeStruct((n_dev,) + buf.shape, buf.dtype),
        scratch_shapes=[pltpu.SemaphoreType.DMA((n_dev - 1,))] * 2 + [pltpu.SemaphoreType.DMA(())],
    )(buf)
    rows = buf.shape[0]
    tm = _tile(rows, 512)

    def sum_body(a_ref, o_ref):
        acc = a_ref[0]
        for k in range(1, n_dev):
            acc = acc + a_ref[k]
        o_ref[...] = acc

    return _pcall(
        sum_body, name=tag + "_sum", grid=(rows // tm,),
        in_specs=[_bs((n_dev, tm, 128), lambda i: (0, i, 0))], out_specs=_bs((tm, 128), lambda i: (i, 0)),
        out_shape=jax.ShapeDtypeStruct(buf.shape, F32),
        compiler_params=pltpu.CompilerParams(dimension_semantics=("parallel",)),
    )(allb)


def _adamw(name, w, g, m, v):
    def fn(t, b):
        wv, gv, mv, vv = t
        m2 = ADAM_B1 * mv + (1.0 - ADAM_B1) * gv
        v2 = ADAM_B2 * vv + (1.0 - ADAM_B2) * (gv * gv)
        m_hat = m2 / (1.0 - ADAM_B1 ** ADAM_STEP)
        v_hat = v2 / (1.0 - ADAM_B2 ** ADAM_STEP)
        delta = -ADAM_LR * (m_hat / (jnp.sqrt(v_hat) + ADAM_EPS) + ADAM_WD * wv)
        return (delta, m2, v2), ()
    wd = w.shape[1]
    return _rowwise(name, fn, [w, g, m, v], [], [(wd, F32)] * 3, tm=256 if wd > 1024 else 512)


_WEIGHTS = ['ffn1_norm', 'ffn1_w_gate', 'ffn1_w_up', 'ffn1_w_down', 'mix_norm', 'w_in', 'ssm_log_dt', 'ssm_a_re',
            'ssm_a_im', 'ssm_b_re', 'ssm_b_im', 'ssm_c_re', 'ssm_c_im', 'ssm_d', 'ssm_w_glu', 'ssm_b_glu',
            'swa_sinks', 'mla_q_norm', 'mla_w_uq', 'mla_kv_norm', 'mla_w_ukv', 'out_norm', 'w_out', 'ffn2_norm',
            'ffn2_w_gate', 'ffn2_w_up', 'ffn2_w_down', 'final_norm']
_COL_SHARDED = ['ffn1_w_gate', 'ffn1_w_up', 'w_in', 'mla_w_uq', 'mla_w_ukv', 'ffn2_w_gate', 'ffn2_w_up']
_ROW_SHARDED = ['ffn1_w_down', 'ssm_w_glu', 'w_out', 'ffn2_w_down']
_STACKED = {'ffn1_w_gate': 'ffn1_wg2', 'ffn1_w_up': 'ffn1_wu2', 'ffn2_w_gate': 'ffn2_wg2', 'ffn2_w_up': 'ffn2_wu2'}
_RENAMED = {'ffn1_w_down': 'ffn1_wd', 'ffn2_w_down': 'ffn2_wd'}
_BIG = _COL_SHARDED + _ROW_SHARDED
_SMALL = [n for n in _WEIGHTS if n not in _BIG]


def _pack(vals):
    flat = jnp.concatenate([v.reshape(-1) for v in vals])
    pad = (-flat.shape[0]) % 1024
    return jnp.pad(flat, (0, pad)).reshape(-1, 128)


def _unpack(buf, like):
    flat = buf.reshape(-1)
    out, o = [], 0
    for v in like:
        out.append(flat[o:o + v.size].reshape(v.shape))
        o += v.size
    return out


def kernel(x, positions, ffn1_norm, ffn1_w_gate, ffn1_w_up, ffn1_w_down, mix_norm, w_in, ssm_log_dt, ssm_a_re, ssm_a_im, ssm_b_re, ssm_b_im, ssm_c_re, ssm_c_im, ssm_d, ssm_w_glu, ssm_b_glu, swa_sinks, mla_q_norm, mla_w_uq, mla_kv_norm, mla_w_ukv, out_norm, w_out, ffn2_norm, ffn2_w_gate, ffn2_w_up, ffn2_w_down, final_norm, loss_target, m_ffn1_norm, m_ffn1_w_gate, m_ffn1_w_up, m_ffn1_w_down, m_mix_norm, m_w_in, m_ssm_log_dt, m_ssm_a_re, m_ssm_a_im, m_ssm_b_re, m_ssm_b_im, m_ssm_c_re, m_ssm_c_im, m_ssm_d, m_ssm_w_glu, m_ssm_b_glu, m_swa_sinks, m_mla_q_norm, m_mla_w_uq, m_mla_kv_norm, m_mla_w_ukv, m_out_norm, m_w_out, m_ffn2_norm, m_ffn2_w_gate, m_ffn2_w_up, m_ffn2_w_down, m_final_norm, v_ffn1_norm, v_ffn1_w_gate, v_ffn1_w_up, v_ffn1_w_down, v_mix_norm, v_w_in, v_ssm_log_dt, v_ssm_a_re, v_ssm_a_im, v_ssm_b_re, v_ssm_b_im, v_ssm_c_re, v_ssm_c_im, v_ssm_d, v_ssm_w_glu, v_ssm_b_glu, v_swa_sinks, v_mla_q_norm, v_mla_w_uq, v_mla_kv_norm, v_mla_w_ukv, v_out_norm, v_w_out, v_ffn2_norm, v_ffn2_w_gate, v_ffn2_w_up, v_ffn2_w_down, v_final_norm):
    w = dict(zip(_WEIGHTS, (ffn1_norm, ffn1_w_gate, ffn1_w_up, ffn1_w_down, mix_norm, w_in, ssm_log_dt, ssm_a_re, ssm_a_im, ssm_b_re, ssm_b_im, ssm_c_re, ssm_c_im, ssm_d, ssm_w_glu, ssm_b_glu, swa_sinks, mla_q_norm, mla_w_uq, mla_kv_norm, mla_w_ukv, out_norm, w_out, ffn2_norm, ffn2_w_gate, ffn2_w_up, ffn2_w_down, final_norm)))
    m = dict(zip(_WEIGHTS, (m_ffn1_norm, m_ffn1_w_gate, m_ffn1_w_up, m_ffn1_w_down, m_mix_norm, m_w_in, m_ssm_log_dt, m_ssm_a_re, m_ssm_a_im, m_ssm_b_re, m_ssm_b_im, m_ssm_c_re, m_ssm_c_im, m_ssm_d, m_ssm_w_glu, m_ssm_b_glu, m_swa_sinks, m_mla_q_norm, m_mla_w_uq, m_mla_kv_norm, m_mla_w_ukv, m_out_norm, m_w_out, m_ffn2_norm, m_ffn2_w_gate, m_ffn2_w_up, m_ffn2_w_down, m_final_norm)))
    v = dict(zip(_WEIGHTS, (v_ffn1_norm, v_ffn1_w_gate, v_ffn1_w_up, v_ffn1_w_down, v_mix_norm, v_w_in, v_ssm_log_dt, v_ssm_a_re, v_ssm_a_im, v_ssm_b_re, v_ssm_b_im, v_ssm_c_re, v_ssm_c_im, v_ssm_d, v_ssm_w_glu, v_ssm_b_glu, v_swa_sinks, v_mla_q_norm, v_mla_w_uq, v_mla_kv_norm, v_mla_w_ukv, v_out_norm, v_w_out, v_ffn2_norm, v_ffn2_w_gate, v_ffn2_w_up, v_ffn2_w_down, v_final_norm)))
    n_layers = ffn1_norm.shape[0]

    layers = []
    for l in range(n_layers):
        full = _gather_call("gather", [_stage_shard("stage_" + n, w[n][l]) for n in _BIG])
        lw = {n: w[n][l] for n in _SMALL if n != 'final_norm'}
        for n, g in zip(_BIG, full):
            ns, r, c = g.shape
            if n in _STACKED:
                lw[_STACKED[n]] = g.reshape(ns * r, c)
            elif n in _ROW_SHARDED:
                lw[_RENAMED.get(n, n)] = g.reshape(ns * r, c)
            else:
                lw[n] = jnp.moveaxis(g, 0, 1).reshape(r, ns * c)
        layers.append(lw)

    loss_part, dx, grads, dfinal = _device_step(x[0], positions[0], loss_target[0], layers, final_norm)
    loss = lax.psum(loss_part, ("x", "y", "c"))

    reduced = {n: [] for n in _BIG}
    for l in range(n_layers):
        contrib = []
        for n in _BIG:
            r, c = w[n].shape[1:]
            if n in _STACKED:
                g = grads[l][_STACKED[n]].reshape(N_SHARD, r, c)
            elif n in _ROW_SHARDED:
                g = grads[l][_RENAMED.get(n, n)].reshape(N_SHARD, r, c)
            else:
                g = jnp.moveaxis(grads[l][n].reshape(r, N_SHARD, c), 1, 0)
            contrib.append(g)
        for n, g in zip(_BIG, _reduce_to_shards("rs", contrib)):
            reduced[n].append(g)
    grad = {n: jnp.stack(reduced[n]) for n in _BIG}

    small_like = [w[n] for n in _SMALL]
    small_g = [jnp.stack([grads[l][n] for l in range(n_layers)]) for n in _SMALL if n != 'final_norm'] + [dfinal]
    g_small = _all_sum_small("small", _pack(small_g))
    d_small, m_small, v_small = _adamw("adam_small", _pack(small_like), g_small, _pack([m[n] for n in _SMALL]),
                                       _pack([v[n] for n in _SMALL]))
    delta, new_m, new_v = {}, {}, {}
    for n, gv, dv_, mv, vv in zip(_SMALL, _unpack(g_small, small_like), _unpack(d_small, small_like),
                                  _unpack(m_small, small_like), _unpack(v_small, small_like)):
        grad[n], delta[n], new_m[n], new_v[n] = gv, dv_, mv, vv
    for n in _BIG:
        shp = w[n].shape
        two = lambda t: t.reshape(shp[0] * shp[1], shp[2])
        d2, m2, v2 = _adamw("adam_" + n, two(w[n]), two(grad[n]), two(m[n]), two(v[n]))
        delta[n], new_m[n], new_v[n] = d2.reshape(shp), m2.reshape(shp), v2.reshape(shp)
    return (loss, dx[None], *[grad[n] for n in _WEIGHTS], *[delta[n] for n in _WEIGHTS],
            *[new_m[n] for n in _WEIGHTS], *[new_v[n] for n in _WEIGHTS])
```

```python
import functools
import math

import jax
import jax.numpy as jnp
import numpy as np
from jax import lax
from jax.experimental import pallas as pl
from jax.experimental.pallas import tpu as pltpu

F32 = jnp.float32
BF16 = jnp.bfloat16

EPS = 1e-6
ROPE_THETA = 10000.0
SSM_GROUP = 16
SSM_STATE = 64
SWA_HEADS = 8
SWA_KV_HEADS = 2
SWA_HEAD_DIM = 64
SWA_BLOCK = 128
SWA_WIDTH = SWA_HEADS * SWA_HEAD_DIM
SWA_KV_WIDTH = SWA_KV_HEADS * SWA_HEAD_DIM
MLA_HEADS = 8
MLA_Q_RANK = 512
MLA_KV_RANK = 256
MLA_NOPE = 128
MLA_ROPE = 64
MLA_V = 128
MLA_SLOT = 256
MLA_WIDTH = MLA_HEADS * MLA_V
ROPE_HALF = 32

ADAM_LR = 0.001
ADAM_B1 = 0.9
ADAM_B2 = 0.999
ADAM_EPS = 1e-08
ADAM_WD = 0.01
ADAM_STEP = 10

N_SHARD = 4
MESH = pl.DeviceIdType.MESH

NN = ((1,), (0,))
NT = ((1,), (1,))
TN = ((0,), (0,))


def _pcall(body, **kw):
    return pl.pallas_call(body, **kw)


def _tile(n, want, align=16):
    if n <= want:
        return n
    t = want - want % align
    while t >= align:
        if n % t == 0:
            return t
        t -= align
    return n


def _mm(name, grid, pairs, acc_shapes, outs, epilogue, extras=()):
    nk = grid[2]
    n_p, n_e, n_o, n_a = len(pairs), len(extras), len(outs), len(acc_shapes)
    dims_idx = [(p[4], p[5]) for p in pairs]

    def body(*refs):
        ab = refs[:2 * n_p]
        ex = refs[2 * n_p:2 * n_p + n_e]
        o = refs[2 * n_p + n_e:2 * n_p + n_e + n_o]
        accs = refs[2 * n_p + n_e + n_o:]

        def partial_sums():
            sums = [None] * n_a
            for p, (dims, ai) in enumerate(dims_idx):
                a = ab[2 * p][...].astype(BF16)
                b = ab[2 * p + 1][...].astype(BF16)
                d = lax.dot_general(a, b, (dims, ((), ())), preferred_element_type=F32)
                sums[ai] = d if sums[ai] is None else sums[ai] + d
            return sums

        def finish(vals):
            res = epilogue(vals, [e[...] for e in ex])
            for r, oref in zip(res, o):
                oref[...] = r.astype(oref.dtype)

        if nk == 1:
            finish(partial_sums())
        else:
            k = pl.program_id(2)

            @pl.when(k == 0)
            def _():
                for acc in accs:
                    acc[...] = jnp.zeros_like(acc)

            for acc, s in zip(accs, partial_sums()):
                acc[...] += s

            @pl.when(k == nk - 1)
            def _():
                finish([acc[...] for acc in accs])

    in_arrays, in_specs = [], []
    for a, a_spec, b, b_spec, _, _ in pairs:
        in_arrays += [a, b]
        in_specs += [a_spec, b_spec]
    for e, e_spec in extras:
        in_arrays.append(e)
        in_specs.append(e_spec)
    res = _pcall(
        body, name=name, grid=grid, in_specs=in_specs,
        out_specs=[o[2] for o in outs],
        out_shape=[jax.ShapeDtypeStruct(o[0], o[1]) for o in outs],
        scratch_shapes=[] if nk == 1 else [pltpu.VMEM(s, F32) for s in acc_shapes],
        compiler_params=pltpu.CompilerParams(dimension_semantics=("parallel", "parallel", "arbitrary")),
    )(*in_arrays)
    return res


def _bs(shape, fn):
    return pl.BlockSpec(shape, fn)


def _mm_simple(name, a, b, dims, *, tm=512, tn=512, tk=512, out_dtype=F32, epilogue=None, extras=(), scale=None):
    if dims == NN:
        (M, K), N = a.shape, b.shape[1]
    elif dims == NT:
        (M, K), N = a.shape, b.shape[0]
    else:
        (K, M), N = a.shape, b.shape[1]
    tm, tn, tk = _tile(M, tm, 128), _tile(N, tn, 128), _tile(K, tk, 128)
    if dims == NN:
        a_spec, b_spec = _bs((tm, tk), lambda i, j, k: (i, k)), _bs((tk, tn), lambda i, j, k: (k, j))
    elif dims == NT:
        a_spec, b_spec = _bs((tm, tk), lambda i, j, k: (i, k)), _bs((tn, tk), lambda i, j, k: (j, k))
    else:
        a_spec, b_spec = _bs((tk, tm), lambda i, j, k: (k, i)), _bs((tk, tn), lambda i, j, k: (k, j))
    ex = []
    for e in extras:
        if e.shape[0] == 1:
            ex.append((e, _bs((1, tn), lambda i, j, k: (0, j))))
        else:
            ex.append((e, _bs((tm, tn), lambda i, j, k: (i, j))))
    if epilogue is None:
        if scale is None:
            epilogue = lambda accs, ex_: (accs[0],)
        else:
            epilogue = lambda accs, ex_: (accs[0] * scale,)
        out_dtypes = (out_dtype,)
    else:
        out_dtypes = out_dtype if isinstance(out_dtype, tuple) else (out_dtype,)
    outs = [((M, N), dt, _bs((tm, tn), lambda i, j, k: (i, j))) for dt in out_dtypes]
    res = _mm(name, (M // tm, N // tn, K // tk), [(a, a_spec, b, b_spec, dims, 0)], [(tm, tn)], outs, epilogue, ex)
    return res[0] if len(res) == 1 else res


def _rowwise(name, fn, tiles, bcasts, outs, accs=(), *, tm=256):
    rows = tiles[0].shape[0]
    tm = _tile(rows, tm)
    n_t, n_b, n_o, n_a = len(tiles), len(bcasts), len(outs), len(accs)

    def body(*refs):
        t = [r[...] for r in refs[:n_t]]
        b = [r[...] for r in refs[n_t:n_t + n_b]]
        o = refs[n_t + n_b:n_t + n_b + n_o]
        a = refs[n_t + n_b + n_o:]
        ov, av = fn(t, b)
        for r, val in zip(o, ov):
            r[...] = val.astype(r.dtype)
        if n_a:
            @pl.when(pl.program_id(0) == 0)
            def _():
                for r in a:
                    r[...] = jnp.zeros_like(r)
            for r, val in zip(a, av):
                r[...] += val

    in_specs = [_bs((tm, x.shape[1]), lambda i: (i, 0)) for x in tiles]
    in_specs += [_bs(x.shape, lambda i, nd=x.ndim: (0,) * nd) for x in bcasts]
    out_specs = [_bs((tm, w), lambda i: (i, 0)) for w, _ in outs]
    out_specs += [_bs((1, w), lambda i: (0, 0)) for w in accs]
    out_shape = [jax.ShapeDtypeStruct((rows, w), dt) for w, dt in outs]
    out_shape += [jax.ShapeDtypeStruct((1, w), F32) for w in accs]
    return _pcall(
        body, name=name, grid=(rows // tm,), in_specs=in_specs, out_specs=out_specs, out_shape=out_shape,
        compiler_params=pltpu.CompilerParams(dimension_semantics=("arbitrary",)),
    )(*tiles, *bcasts)


def _colsum(v):
    return jnp.sum(v, axis=0, keepdims=True)


def _rms_fwd(name, x, gain, out_dtype=BF16):
    def fn(t, b):
        xv = t[0]
        r = lax.rsqrt(jnp.mean(xv * xv, axis=-1, keepdims=True) + EPS)
        return (xv * r * b[0], r), ()
    w = x.shape[1]
    return _rowwise(name, fn, [x], [gain.reshape(1, w)], [(w, out_dtype), (1, F32)])


def _rms_bwd(name, dh, x, rstd, gain, dres=None):
    def fn(t, b):
        dhv, xv, r = t[0], t[1], t[2]
        xh = xv * r
        dxh = dhv * b[0]
        dx = r * (dxh - xh * jnp.mean(dxh * xh, axis=-1, keepdims=True))
        if dres is not None:
            dx = dx + t[3]
        return (dx,), (_colsum(dhv * xh),)
    w = x.shape[1]
    tiles = [dh, x, rstd] + ([dres] if dres is not None else [])
    return _rowwise(name, fn, tiles, [gain.reshape(1, w)], [(w, F32)], [w])


def _sigmoid(v):
    return 1.0 / (1.0 + jnp.exp(-v))


def _ffn_fwd(tag, x, gain, wg2, wu2, wd):
    T, D = x.shape
    fs = wg2.shape[1]
    h, rstd = _rms_fwd(tag + "_rms", x, gain)
    tm = _tile(T, 256)
    a_spec = _bs((tm, D), lambda i, j, k: (j, 0))
    w_spec = _bs((D, fs), lambda i, j, k: (i, 0))
    o_spec = _bs((tm, fs), lambda i, j, k: (j, i))

    def epi(accs, ex):
        a, b = accs
        return a, b, a * _sigmoid(a) * b

    a, b, t = _mm(
        tag + "_up", (N_SHARD, T // tm, 1),
        [(h, a_spec, wg2, w_spec, NN, 0), (h, a_spec, wu2, w_spec, NN, 1)],
        [(tm, fs), (tm, fs)],
        [((T, N_SHARD * fs), F32, o_spec), ((T, N_SHARD * fs), F32, o_spec), ((T, N_SHARD * fs), BF16, o_spec)],
        epi)
    y = _mm_simple(tag + "_down", t, wd, NN, tm=512, tn=D, tk=fs // 2 if fs % 256 == 0 else fs,
                   epilogue=lambda accs, ex: (ex[0] + 0.5 * accs[0],), extras=[x])
    return y, (x, h, rstd, a, b, t)


def _ffn_bwd(tag, dy, saved, gain, wg2, wu2, wd):
    x, h, rstd, a, b, t = saved
    T, D = x.shape
    fs = wg2.shape[1]
    F = N_SHARD * fs
    dwd = _mm_simple(tag + "_dwd", t, dy, TN, tm=fs // 2 if fs % 16 == 0 else fs, tn=D, tk=512, scale=0.5)

    def epi(accs, ex):
        dt = 0.5 * accs[0]
        av, bv = ex
        sig = _sigmoid(av)
        da = dt * bv * (sig * (1.0 + av * (1.0 - sig)))
        db = dt * (av * sig)
        return da, db

    tn = 512 if F % 512 == 0 else (256 if F % 256 == 0 else 128)
    da, db = _mm_simple(tag + "_dt", dy, wd, NT, tm=512, tn=tn, tk=D, out_dtype=(BF16, BF16), epilogue=epi,
                        extras=[a, b])
    tm = _tile(D, 1024)
    per = D // tm
    tk = _tile(T, 512)
    h_spec = _bs((tk, tm), lambda i, j, k: (k, i % per))
    d_spec = _bs((tk, fs), lambda i, j, k: (k, i // per))
    o_spec = _bs((tm, fs), lambda i, j, k: (i, 0))
    dwg2, dwu2 = _mm(
        tag + "_dwgu", (N_SHARD * per, 1, T // tk),
        [(h, h_spec, da, d_spec, TN, 0), (h, h_spec, db, d_spec, TN, 1)],
        [(tm, fs), (tm, fs)],
        [((N_SHARD * D, fs), F32, o_spec), ((N_SHARD * D, fs), F32, o_spec)],
        lambda accs, ex: tuple(accs))
    tm2 = _tile(T, 512)
    tn2 = _tile(D, 1024)
    per2 = D // tn2
    g_spec = _bs((tm2, fs), lambda i, j, k: (i, k))
    w_spec = _bs((tn2, fs), lambda i, j, k: (k * per2 + j, 0))
    o2 = _bs((tm2, tn2), lambda i, j, k: (i, j))
    dh, = _mm(
        tag + "_dh", (T // tm2, per2, N_SHARD),
        [(da, g_spec, wg2, w_spec, NT, 0), (db, g_spec, wu2, w_spec, NT, 0)],
        [(tm2, tn2)], [((T, D), F32, o2)], lambda accs, ex: (accs[0],))
    dx, dgain = _rms_bwd(tag + "_rmsb", dh, x, rstd, gain, dres=dy)
    return dx, dgain, dwg2, dwu2, dwd


def _rope_tables(positions):
    inv_freq = ROPE_THETA ** (-jnp.arange(0, 2 * ROPE_HALF, 2, dtype=F32) / (2 * ROPE_HALF))
    ang = positions.astype(F32)[:, None] * inv_freq
    c, s = jnp.cos(ang), jnp.sin(ang)
    c64 = jnp.concatenate([c, c], axis=1)
    s64 = jnp.concatenate([-s, s], axis=1)
    T = positions.shape[0]
    one, zero = jnp.ones((T, MLA_NOPE), F32), jnp.zeros((T, MLA_NOPE), F32)
    pad1, pad0 = jnp.ones((T, MLA_SLOT - MLA_NOPE - MLA_ROPE), F32), jnp.zeros((T, MLA_SLOT - MLA_NOPE - MLA_ROPE), F32)
    c256 = jnp.concatenate([one, c64, pad1], axis=1)
    s256 = jnp.concatenate([zero, s64, pad0], axis=1)
    return c64, s64, c256, s256


def _half_swap(v):
    w = v.shape[1]
    lane = lax.broadcasted_iota(jnp.int32, v.shape, 1)
    first = (lane % (2 * ROPE_HALF)) < ROPE_HALF
    return jnp.where(first, pltpu.roll(v, w - ROPE_HALF, 1), pltpu.roll(v, ROPE_HALF, 1))


def _rope(name, x, ctab, stab, backward=False, out_dtype=F32):
    reps = x.shape[1] // ctab.shape[1]

    def fn(t, b):
        xv = t[0]
        c = jnp.tile(t[1], (1, reps)) if reps > 1 else t[1]
        s = jnp.tile(t[2], (1, reps)) if reps > 1 else t[2]
        if backward:
            return (xv * c + _half_swap(xv * s),), ()
        return (xv * c + _half_swap(xv) * s,), ()
    return _rowwise(name, fn, [x, ctab, stab], [], [(x.shape[1], out_dtype)])[0]


def _s5_prep(log_dt, a_re, a_im, b_re, b_im, c_re, c_im):
    G, P = a_re.shape
    C = b_re.shape[-1]
    dt = jnp.exp(log_dt)[:, None]
    mag = jnp.exp(a_re * dt)
    abar_r = mag * jnp.cos(a_im * dt)
    abar_i = mag * jnp.sin(a_im * dt)
    den = a_re * a_re + a_im * a_im
    nr = abar_r - 1.0
    qr = (nr * a_re + abar_i * a_im) / den
    qi = (abar_i * a_re - nr * a_im) / den
    bbar_r = qr[..., None] * b_re - qi[..., None] * b_im
    bbar_i = qr[..., None] * b_im + qi[..., None] * b_re
    eye = jnp.eye(G, dtype=F32)
    b_r = jnp.einsum('gpc,gh->gchp', bbar_r, eye).reshape(G * C, G * P)
    b_i = jnp.einsum('gpc,gh->gchp', bbar_i, eye).reshape(G * C, G * P)
    c_r = jnp.einsum('gcp,gh->gphc', c_re, eye).reshape(G * P, G * C)
    c_i = jnp.einsum('gcp,gh->gphc', c_im, eye).reshape(G * P, G * C)
    return abar_r.reshape(1, G * P), abar_i.reshape(1, G * P), b_r, b_i, c_r, -c_i


def _scan_lanes(gp):
    return 512 if gp % 512 == 0 else (256 if gp % 256 == 0 else 128)


def _scan_fwd(name, bu_r, bu_i, a_r, a_i):
    T, gp = bu_r.shape
    tl = _scan_lanes(gp)
    tc = _tile(T, 256)

    def body(br_ref, bi_ref, ar_ref, ai_ref, xr_ref, xi_ref, pr_ref, pi_ref, sr, si):
        @pl.when(pl.program_id(1) == 0)
        def _():
            sr[...] = jnp.zeros_like(sr)
            si[...] = jnp.zeros_like(si)
        ar, ai = ar_ref[...], ai_ref[...]

        def step(g, carry):
            xr, xi = carry
            base = pl.multiple_of(g * 8, 8)
            b_r = br_ref[pl.ds(base, 8), :]
            b_i = bi_ref[pl.ds(base, 8), :]
            rows_r, rows_i, prev_r, prev_i = [], [], [], []
            for j in range(8):
                prev_r.append(xr)
                prev_i.append(xi)
                nr = ar * xr - ai * xi + b_r[j:j + 1, :]
                ni = ar * xi + ai * xr + b_i[j:j + 1, :]
                xr, xi = nr, ni
                rows_r.append(xr)
                rows_i.append(xi)
            xr_ref[pl.ds(base, 8), :] = jnp.concatenate(rows_r, axis=0)
            xi_ref[pl.ds(base, 8), :] = jnp.concatenate(rows_i, axis=0)
            pr_ref[pl.ds(base, 8), :] = jnp.concatenate(prev_r, axis=0)
            pi_ref[pl.ds(base, 8), :] = jnp.concatenate(prev_i, axis=0)
            return xr, xi

        xr, xi = lax.fori_loop(0, tc // 8, step, (sr[...], si[...]))
        sr[...] = xr
        si[...] = xi

    spec = _bs((tc, tl), lambda l, t: (t, l))
    a_spec = _bs((1, tl), lambda l, t: (0, l))
    return _pcall(
        body, name=name, grid=(gp // tl, T // tc),
        in_specs=[spec, spec, a_spec, a_spec], out_specs=[spec] * 4,
        out_shape=[jax.ShapeDtypeStruct((T, gp), F32)] * 4,
        scratch_shapes=[pltpu.VMEM((1, tl), F32), pltpu.VMEM((1, tl), F32)],
        compiler_params=pltpu.CompilerParams(dimension_semantics=("parallel", "arbitrary")),
    )(bu_r, bu_i, a_r, a_i)


def _scan_bwd(name, g_r, g_i, p_r, p_i, a_r, a_i):
    T, gp = g_r.shape
    tl = _scan_lanes(gp)
    tc = _tile(T, 256)
    nt = T // tc
    ng = tc // 8

    def body(gr_ref, gi_ref, pr_ref, pi_ref, ar_ref, ai_ref, lr_ref, li_ref, dar_ref, dai_ref, sr, si, accr, acci):
        t = pl.program_id(1)

        @pl.when(t == 0)
        def _():
            sr[...] = jnp.zeros_like(sr)
            si[...] = jnp.zeros_like(si)
            accr[...] = jnp.zeros_like(accr)
            acci[...] = jnp.zeros_like(acci)
        ar, ai = ar_ref[...], ai_ref[...]

        def step(kk, carry):
            lr, li = carry
            base = pl.multiple_of((ng - 1 - kk) * 8, 8)
            gr8 = gr_ref[pl.ds(base, 8), :]
            gi8 = gi_ref[pl.ds(base, 8), :]
            rows_r, rows_i = [None] * 8, [None] * 8
            for j in range(7, -1, -1):
                nr = gr8[j:j + 1, :] + ar * lr + ai * li
                ni = gi8[j:j + 1, :] - ai * lr + ar * li
                lr, li = nr, ni
                rows_r[j] = lr
                rows_i[j] = li
            lam_r = jnp.concatenate(rows_r, axis=0)
            lam_i = jnp.concatenate(rows_i, axis=0)
            lr_ref[pl.ds(base, 8), :] = lam_r
            li_ref[pl.ds(base, 8), :] = lam_i
            pr8 = pr_ref[pl.ds(base, 8), :]
            pi8 = pi_ref[pl.ds(base, 8), :]
            accr[...] += lam_r * pr8 + lam_i * pi8
            acci[...] += lam_i * pr8 - lam_r * pi8
            return lr, li

        lr, li = lax.fori_loop(0, ng, step, (sr[...], si[...]))
        sr[...] = lr
        si[...] = li

        @pl.when(t == nt - 1)
        def _():
            dar_ref[...] = jnp.sum(accr[...], axis=0, keepdims=True)
            dai_ref[...] = jnp.sum(acci[...], axis=0, keepdims=True)

    spec = _bs((tc, tl), lambda l, t: (nt - 1 - t, l))
    a_spec = _bs((1, tl), lambda l, t: (0, l))
    return _pcall(
        body, name=name, grid=(gp // tl, nt),
        in_specs=[spec] * 4 + [a_spec, a_spec], out_specs=[spec, spec, a_spec, a_spec],
        out_shape=[jax.ShapeDtypeStruct((T, gp), F32)] * 2 + [jax.ShapeDtypeStruct((1, gp), F32)] * 2,
        scratch_shapes=[pltpu.VMEM((1, tl), F32), pltpu.VMEM((1, tl), F32),
                        pltpu.VMEM((8, tl), F32), pltpu.VMEM((8, tl), F32)],
        compiler_params=pltpu.CompilerParams(dimension_semantics=("parallel", "arbitrary")),
    )(g_r, g_i, p_r, p_i, a_r, a_i)


_GELU_C = math.sqrt(2.0 / math.pi)


def _gelu(v):
    return 0.5 * v * (1.0 + jnp.tanh(_GELU_C * (v + 0.044715 * v * v * v)))


def _gelu_grad(v):
    th = jnp.tanh(_GELU_C * (v + 0.044715 * v * v * v))
    return 0.5 * (1.0 + th) + 0.5 * v * (1.0 - th * th) * _GELU_C * (1.0 + 3.0 * 0.044715 * v * v)


def _mm_shared_lhs(name, a, bs_, dims, *, tm, tn, tk):
    if dims == NN:
        (M, K), N = a.shape, bs_[0].shape[1]
    elif dims == NT:
        (M, K), N = a.shape, bs_[0].shape[0]
    else:
        (K, M), N = a.shape, bs_[0].shape[1]
    tm, tn, tk = _tile(M, tm, 128), _tile(N, tn, 128), _tile(K, tk, 128)
    if dims == NN:
        a_spec, b_spec = _bs((tm, tk), lambda i, j, k: (i, k)), _bs((tk, tn), lambda i, j, k: (k, j))
    elif dims == NT:
        a_spec, b_spec = _bs((tm, tk), lambda i, j, k: (i, k)), _bs((tn, tk), lambda i, j, k: (j, k))
    else:
        a_spec, b_spec = _bs((tk, tm), lambda i, j, k: (k, i)), _bs((tk, tn), lambda i, j, k: (k, j))
    o_spec = _bs((tm, tn), lambda i, j, k: (i, j))
    n = len(bs_)
    return _mm(name, (M // tm, N // tn, K // tk), [(a, a_spec, b, b_spec, dims, i) for i, b in enumerate(bs_)],
               [(tm, tn)] * n, [((M, N), F32, o_spec)] * n, lambda accs, ex: tuple(accs))


def _s5_fwd(tag, u, prep, d_skip, w_glu, b_glu):
    a_r, a_i, b_r, b_i, c_r, c_in = prep
    T, W = u.shape
    gp = a_r.shape[1]
    bu_r, bu_i = _mm_shared_lhs(tag + "_bu", u, [b_r, b_i], NN, tm=512, tn=1024, tk=W)
    x_r, x_i, p_r, p_i = _scan_fwd(tag + "_scan", bu_r, bu_i, a_r, a_i)
    tm, tk = _tile(T, 512), _tile(gp, 512)
    x_spec = _bs((tm, tk), lambda i, j, k: (i, k))
    c_spec = _bs((tk, W), lambda i, j, k: (k, 0))
    full = _bs((tm, W), lambda i, j, k: (i, 0))
    row = _bs((1, W), lambda i, j, k: (0, 0))

    def epi(accs, ex):
        y = accs[0] + ex[1] * ex[0]
        return y, _gelu(y)

    ypre, yg = _mm(tag + "_y", (T // tm, 1, gp // tk),
                   [(x_r, x_spec, c_r, c_spec, NN, 0), (x_i, x_spec, c_in, c_spec, NN, 0)],
                   [(tm, W)], [((T, W), F32, full)] * 2, epi, [(u, full), (d_skip.reshape(1, W), row)])

    def epi2(accs, ex):
        pre = accs[0] + ex[1]
        return ex[0] * _sigmoid(pre), pre

    out, pre = _mm_simple(tag + "_glu", yg, w_glu, NN, tm=512, tn=W, tk=W, out_dtype=(F32, F32), epilogue=epi2,
                          extras=[yg, b_glu.reshape(1, W)])
    return out, (u, x_r, x_i, p_r, p_i, ypre, yg, pre)


def _s5_bwd(tag, d_out, saved, prep, d_skip, w_glu):
    u, x_r, x_i, p_r, p_i, ypre, yg, pre = saved
    a_r, a_i, b_r, b_i, c_r, c_in = prep
    T, W = u.shape

    def gate_fn(t, b):
        gate = _sigmoid(t[2])
        dpre = t[0] * t[1] * gate * (1.0 - gate)
        return (dpre, t[0] * gate), (_colsum(dpre),)

    dpre, tmp, db_glu = _rowwise(tag + "_gateb", gate_fn, [d_out, yg, pre], [], [(W, F32), (W, F32)], [W])
    dw_glu = _mm_simple(tag + "_dwglu", yg, dpre, TN, tm=W, tn=W, tk=512)
    dy = _mm_simple(tag + "_dyg", dpre, w_glu, NT, tm=512, tn=W, tk=W,
                    epilogue=lambda accs, ex: ((accs[0] + ex[0]) * _gelu_grad(ex[1]),), extras=[tmp, ypre])
    dd, = _rowwise(tag + "_dd", lambda t, b: ((), (_colsum(t[0] * t[1]),)), [dy, u], [], [], [W])
    dx_r, dx_i = _mm_shared_lhs(tag + "_dx", dy, [c_r, c_in], NT, tm=512, tn=1024, tk=W)
    dc_r = _mm_simple(tag + "_dcr", x_r, dy, TN, tm=512, tn=W, tk=512)
    dc_in = _mm_simple(tag + "_dci", x_i, dy, TN, tm=512, tn=W, tk=512)
    lam_r, lam_i, da_r, da_i = _scan_bwd(tag + "_scanb", dx_r, dx_i, p_r, p_i, a_r, a_i)
    gp = a_r.shape[1]
    tm, tk = _tile(T, 512), _tile(gp, 512)
    l_spec = _bs((tm, tk), lambda i, j, k: (i, k))
    b_spec = _bs((W, tk), lambda i, j, k: (0, k))
    full = _bs((tm, W), lambda i, j, k: (i, 0))
    row = _bs((1, W), lambda i, j, k: (0, 0))
    du, = _mm(tag + "_du", (T // tm, 1, gp // tk),
              [(lam_r, l_spec, b_r, b_spec, NT, 0), (lam_i, l_spec, b_i, b_spec, NT, 0)],
              [(tm, W)], [((T, W), F32, full)], lambda accs, ex: (accs[0] + ex[1] * ex[0],),
              [(dy, full), (d_skip.reshape(1, W), row)])
    db_r, db_i = _mm_shared_lhs(tag + "_db", u, [lam_r, lam_i], TN, tm=W, tn=1024, tk=512)
    return du, (da_r, da_i, db_r, db_i, dc_r, dc_in), dd, dw_glu, db_glu


def _swa_mask(n):
    B = SWA_BLOCK
    r = lax.broadcasted_iota(jnp.int32, (B, 2 * B), 0)
    c = lax.broadcasted_iota(jnp.int32, (B, 2 * B), 1)
    d = r + B - c
    return (d >= 0) & (d < B) & ((n > 0) | (c >= B))


def _swa_specs(T):
    B = SWA_BLOCK
    cur = lambda w: _bs((B, w), lambda n: (n, 0))
    prev = lambda w: _bs((B, w), lambda n: (jnp.maximum(n - 1, 0), 0))
    return cur, prev


def _swa_fwd(name, q, k, v, sinks_b):
    T = q.shape[0]
    B = SWA_BLOCK
    scale = SWA_HEAD_DIM ** -0.5
    per_kv = SWA_HEADS // SWA_KV_HEADS

    def body(q_ref, kc_ref, kp_ref, vc_ref, vp_ref, s_ref, o_ref, l_ref):
        n = pl.program_id(0)
        kcat = jnp.concatenate([kp_ref[...], kc_ref[...]], axis=0).astype(BF16)
        vcat = jnp.concatenate([vp_ref[...], vc_ref[...]], axis=0).astype(BF16)
        mask = _swa_mask(n)
        lane = lax.broadcasted_iota(jnp.int32, (B, 128), 1)
        lo = lane < SWA_HEAD_DIM
        lse_out = jnp.zeros((B, 128), F32)
        for jb in range(SWA_HEADS // 2):
            qblk = q_ref[:, jb * 128:(jb + 1) * 128]
            h = (2 * jb) // per_kv
            half_h = lo if h == 0 else jnp.logical_not(lo)
            outs = []
            for e in range(2):
                j = 2 * jb + e
                qa = qblk if e == h else pltpu.roll(qblk, SWA_HEAD_DIM, 1)
                qm = jnp.where(half_h, qa, 0.0).astype(BF16)
                s = lax.dot_general(qm, kcat, (NT, ((), ())), preferred_element_type=F32) * scale
                s = jnp.where(mask, s, -jnp.inf)
                sk = s_ref[j:j + 1, 0:1]
                m = jnp.maximum(jnp.max(s, axis=1, keepdims=True), sk)
                ex = jnp.exp(s - m)
                den = jnp.sum(ex, axis=1, keepdims=True) + jnp.exp(sk - m)
                p = ex / den
                r = lax.dot_general(p.astype(BF16), vcat, (NN, ((), ())), preferred_element_type=F32)
                outs.append(r if e == h else pltpu.roll(r, SWA_HEAD_DIM, 1))
                lse_out = jnp.where(lane == j, m + jnp.log(den), lse_out)
            o_ref[:, jb * 128:(jb + 1) * 128] = jnp.where(lo, outs[0], outs[1])
        l_ref[...] = lse_out

    cur, prev = _swa_specs(T)
    return _pcall(
        body, name=name, grid=(T // B,),
        in_specs=[cur(SWA_WIDTH), cur(SWA_KV_WIDTH), prev(SWA_KV_WIDTH), cur(SWA_KV_WIDTH), prev(SWA_KV_WIDTH),
                  _bs(sinks_b.shape, lambda n: (0, 0))],
        out_specs=[cur(SWA_WIDTH), cur(128)],
        out_shape=[jax.ShapeDtypeStruct((T, SWA_WIDTH), F32), jax.ShapeDtypeStruct((T, 128), F32)],
        compiler_params=pltpu.CompilerParams(dimension_semantics=("parallel",)),
    )(q, k, k, v, v, sinks_b)


def _swa_bwd(name, q, k, v, sinks_b, o, lse, do):
    T = q.shape[0]
    B = SWA_BLOCK
    scale = SWA_HEAD_DIM ** -0.5
    per_kv = SWA_HEADS // SWA_KV_HEADS

    def body(q_ref, kc_ref, kp_ref, vc_ref, vp_ref, s_ref, o_ref, l_ref, do_ref,
             dq_ref, dkc_ref, dkp_ref, dvc_ref, dvp_ref, ds_ref):
        n = pl.program_id(0)

        @pl.when(n == 0)
        def _():
            ds_ref[...] = jnp.zeros_like(ds_ref)
        kcat = jnp.concatenate([kp_ref[...], kc_ref[...]], axis=0).astype(BF16)
        vcat = jnp.concatenate([vp_ref[...], vc_ref[...]], axis=0).astype(BF16)
        mask = _swa_mask(n)
        lane = lax.broadcasted_iota(jnp.int32, (B, 128), 1)
        lane1 = lax.broadcasted_iota(jnp.int32, (1, 128), 1)
        lo = lane < SWA_HEAD_DIM
        lblk = l_ref[...]
        dk = jnp.zeros((2 * B, 128), F32)
        dv = jnp.zeros((2 * B, 128), F32)
        dsink = jnp.zeros((1, 128), F32)
        for jb in range(SWA_HEADS // 2):
            sl = slice(jb * 128, (jb + 1) * 128)
            qblk, doblk = q_ref[:, sl], do_ref[:, sl]
            prod = doblk * o_ref[:, sl]
            h = (2 * jb) // per_kv
            half_h = lo if h == 0 else jnp.logical_not(lo)
            parts = []
            for e in range(2):
                j = 2 * jb + e
                half_e = lo if e == 0 else jnp.logical_not(lo)
                dsum = jnp.sum(jnp.where(half_e, prod, 0.0), axis=1, keepdims=True)
                lj = jnp.sum(jnp.where(lane == j, lblk, 0.0), axis=1, keepdims=True)
                qa = qblk if e == h else pltpu.roll(qblk, SWA_HEAD_DIM, 1)
                da = doblk if e == h else pltpu.roll(doblk, SWA_HEAD_DIM, 1)
                qm = jnp.where(half_h, qa, 0.0).astype(BF16)
                dm = jnp.where(half_h, da, 0.0).astype(BF16)
                s = lax.dot_general(qm, kcat, (NT, ((), ())), preferred_element_type=F32) * scale
                p = jnp.where(mask, jnp.exp(s - lj), 0.0)
                dp = lax.dot_general(dm, vcat, (NT, ((), ())), preferred_element_type=F32)
                dsb = (p * (dp - dsum) * scale).astype(BF16)
                dqa = lax.dot_general(dsb, kcat, (NN, ((), ())), preferred_element_type=F32)
                parts.append(dqa if e == h else pltpu.roll(dqa, SWA_HEAD_DIM, 1))
                dk = dk + lax.dot_general(dsb, qm, (TN, ((), ())), preferred_element_type=F32)
                dv = dv + lax.dot_general(p.astype(BF16), dm, (TN, ((), ())), preferred_element_type=F32)
                sk = s_ref[j:j + 1, 0:1]
                contrib = jnp.sum(jnp.exp(sk - lj) * dsum, axis=0, keepdims=True)
                dsink = jnp.where(lane1 == j, dsink - contrib, dsink)
            dq_ref[:, sl] = jnp.where(lo, parts[0], parts[1])
        dkp_ref[...] = dk[:B]
        dkc_ref[...] = dk[B:]
        dvp_ref[...] = dv[:B]
        dvc_ref[...] = dv[B:]
        ds_ref[...] += dsink

    cur, prev = _swa_specs(T)
    kv = jax.ShapeDtypeStruct((T, SWA_KV_WIDTH), F32)
    return _pcall(
        body, name=name, grid=(T // B,),
        in_specs=[cur(SWA_WIDTH), cur(SWA_KV_WIDTH), prev(SWA_KV_WIDTH), cur(SWA_KV_WIDTH), prev(SWA_KV_WIDTH),
                  _bs(sinks_b.shape, lambda n: (0, 0)), cur(SWA_WIDTH), cur(128), cur(SWA_WIDTH)],
        out_specs=[cur(SWA_WIDTH)] + [cur(SWA_KV_WIDTH)] * 4 + [_bs((1, 128), lambda n: (0, 0))],
        out_shape=[jax.ShapeDtypeStruct((T, SWA_WIDTH), F32), kv, kv, kv, kv, jax.ShapeDtypeStruct((1, 128), F32)],
        compiler_params=pltpu.CompilerParams(dimension_semantics=("arbitrary",)),
    )(q, k, k, v, v, sinks_b, o, lse, do)


def _shift_add(name, cur, prv):
    T, W = cur.shape
    B = SWA_BLOCK
    nb = T // B

    def body(c_ref, p_ref, o_ref):
        n = pl.program_id(0)
        o_ref[...] = c_ref[...] + jnp.where(n < nb - 1, p_ref[...], 0.0)

    return _pcall(
        body, name=name, grid=(nb,),
        in_specs=[_bs((B, W), lambda n: (n, 0)), _bs((B, W), lambda n: (jnp.minimum(n + 1, nb - 1), 0))],
        out_specs=_bs((B, W), lambda n: (n, 0)), out_shape=jax.ShapeDtypeStruct((T, W), F32),
        compiler_params=pltpu.CompilerParams(dimension_semantics=("parallel",)),
    )(cur, prv)


MLA_BLOCK = 512
_MLA_SCALE = (MLA_NOPE + MLA_ROPE) ** -0.5


def _lower_tri(tb):
    r = lax.broadcasted_iota(jnp.int32, (tb, tb), 0)
    c = lax.broadcasted_iota(jnp.int32, (tb, tb), 1)
    return c <= r


def _mla_fwd(name, qcat, kcat, v):
    T = qcat.shape[0]
    tb = _tile(T, MLA_BLOCK)
    nb = T // tb

    def body(q_ref, k_ref, v_ref, o_ref, l_ref):
        qi = pl.program_id(1)
        q = q_ref[...]

        def block(kb, carry, diagonal):
            m_prev, l_prev, acc = carry
            rows = pl.ds(pl.multiple_of(kb * tb, tb), tb)
            s = lax.dot_general(q, k_ref[rows, :], (NT, ((), ())), preferred_element_type=F32) * _MLA_SCALE
            if diagonal:
                s = jnp.where(_lower_tri(tb), s, -jnp.inf)
            m_new = jnp.maximum(m_prev, jnp.max(s, axis=1, keepdims=True))
            alpha = jnp.exp(m_prev - m_new)
            p = jnp.exp(s - m_new)
            l_new = alpha * l_prev + jnp.sum(p, axis=1, keepdims=True)
            acc = alpha * acc + lax.dot_general(p.astype(BF16), v_ref[rows, :], (NN, ((), ())),
                                                preferred_element_type=F32)
            return m_new, l_new, acc

        init = (jnp.full((tb, 1), -jnp.inf, F32), jnp.zeros((tb, 1), F32), jnp.zeros((tb, MLA_V), F32))
        carry = lax.fori_loop(0, qi, lambda kb, c: block(kb, c, False), init)
        m_fin, l_fin, acc = block(qi, carry, True)
        o_ref[...] = acc / l_fin
        l_ref[...] = m_fin + jnp.log(l_fin)

    return _pcall(
        body, name=name, grid=(MLA_HEADS, nb),
        in_specs=[_bs((tb, MLA_SLOT), lambda h, qi: (qi, h)), _bs((T, MLA_SLOT), lambda h, qi: (0, h)),
                  _bs((T, MLA_V), lambda h, qi: (0, h))],
        out_specs=[_bs((tb, MLA_V), lambda h, qi: (qi, h)), _bs((None, tb, 1), lambda h, qi: (h, qi, 0))],
        out_shape=[jax.ShapeDtypeStruct((T, MLA_WIDTH), F32), jax.ShapeDtypeStruct((MLA_HEADS, T, 1), F32)],
        compiler_params=pltpu.CompilerParams(dimension_semantics=("parallel", "arbitrary")),
    )(qcat, kcat, v)


def _mla_bwd_q(name, qcat, kcat, v, o, lse, do):
    T = qcat.shape[0]
    tb = _tile(T, MLA_BLOCK)
    nb = T // tb

    def body(q_ref, k_ref, v_ref, o_ref, do_ref, l_ref, dq_ref, ds_ref):
        qi = pl.program_id(1)
        q = q_ref[...]
        do = do_ref[...]
        dsum = jnp.sum(do * o_ref[...], axis=1, keepdims=True)
        ds_ref[...] = dsum
        dob = do.astype(BF16)
        lse_q = l_ref[...]

        def block(kb, acc, diagonal):
            rows = pl.ds(pl.multiple_of(kb * tb, tb), tb)
            k = k_ref[rows, :]
            s = lax.dot_general(q, k, (NT, ((), ())), preferred_element_type=F32) * _MLA_SCALE
            p = jnp.exp(s - lse_q)
            if diagonal:
                p = jnp.where(_lower_tri(tb), p, 0.0)
            dp = lax.dot_general(dob, v_ref[rows, :], (NT, ((), ())), preferred_element_type=F32)
            ds = (p * (dp - dsum) * _MLA_SCALE).astype(BF16)
            return acc + lax.dot_general(ds, k, (NN, ((), ())), preferred_element_type=F32)

        acc = lax.fori_loop(0, qi, lambda kb, a: block(kb, a, False), jnp.zeros((tb, MLA_SLOT), F32))
        dq_ref[...] = block(qi, acc, True)

    qs = lambda w: _bs((tb, w), lambda h, qi: (qi, h))
    col = _bs((None, tb, 1), lambda h, qi: (h, qi, 0))
    return _pcall(
        body, name=name, grid=(MLA_HEADS, nb),
        in_specs=[qs(MLA_SLOT), _bs((T, MLA_SLOT), lambda h, qi: (0, h)), _bs((T, MLA_V), lambda h, qi: (0, h)),
                  qs(MLA_V), qs(MLA_V), col],
        out_specs=[qs(MLA_SLOT), col],
        out_shape=[jax.ShapeDtypeStruct((T, MLA_HEADS * MLA_SLOT), F32), jax.ShapeDtypeStruct((MLA_HEADS, T, 1), F32)],
        compiler_params=pltpu.CompilerParams(dimension_semantics=("parallel", "arbitrary")),
    )(qcat, kcat, v, o, do, lse)


def _mla_bwd_kv(name, qcat, kcat, v, lse, dsum, do):
    T = qcat.shape[0]
    tb = _tile(T, MLA_BLOCK)
    nb = T // tb

    def body(q_ref, k_ref, v_ref, do_ref, l_ref, ds_ref, dk_ref, dv_ref):
        ki = pl.program_id(1)
        k = k_ref[...]
        vv = v_ref[...]

        def block(qb, carry, diagonal):
            dk, dv = carry
            rows = pl.ds(pl.multiple_of(qb * tb, tb), tb)
            q = q_ref[rows, :]
            dob = do_ref[rows, :].astype(BF16)
            s = lax.dot_general(q, k, (NT, ((), ())), preferred_element_type=F32) * _MLA_SCALE
            p = jnp.exp(s - l_ref[rows, :])
            if diagonal:
                p = jnp.where(_lower_tri(tb), p, 0.0)
            dp = lax.dot_general(dob, vv, (NT, ((), ())), preferred_element_type=F32)
            ds = (p * (dp - ds_ref[rows, :]) * _MLA_SCALE).astype(BF16)
            dv = dv + lax.dot_general(p.astype(BF16), dob, (TN, ((), ())), preferred_element_type=F32)
            dk = dk + lax.dot_general(ds, q, (TN, ((), ())), preferred_element_type=F32)
            return dk, dv

        carry = block(ki, (jnp.zeros((tb, MLA_SLOT), F32), jnp.zeros((tb, MLA_V), F32)), True)
        dk, dv = lax.fori_loop(ki + 1, nb, lambda qb, c: block(qb, c, False), carry)
        dk_ref[...] = dk
        dv_ref[...] = dv

    ks = lambda w: _bs((tb, w), lambda h, ki: (ki, h))
    col = _bs((None, T, 1), lambda h, ki: (h, 0, 0))
    return _pcall(
        body, name=name, grid=(MLA_HEADS, nb),
        in_specs=[_bs((T, MLA_SLOT), lambda h, ki: (0, h)), ks(MLA_SLOT), ks(MLA_V),
                  _bs((T, MLA_V), lambda h, ki: (0, h)), col, col],
        out_specs=[ks(MLA_SLOT), ks(MLA_V)],
        out_shape=[jax.ShapeDtypeStruct((T, MLA_HEADS * MLA_SLOT), F32), jax.ShapeDtypeStruct((T, MLA_WIDTH), F32)],
        compiler_params=pltpu.CompilerParams(dimension_semantics=("parallel", "arbitrary")),
    )(qcat, kcat, v, do, lse, dsum)


def _kcat_fwd(name, kpre, krs, c256, s256):
    def fn(t, b):
        kr = t[1] * t[2] + _half_swap(t[1]) * t[3]
        return (t[0] + jnp.tile(kr, (1, MLA_HEADS)),), ()
    return _rowwise(name, fn, [kpre, krs, c256, s256], [], [(kpre.shape[1], BF16)])[0]


def _kcat_bwd(name, dkcat, c256, s256):
    def fn(t, b):
        d = t[0][:, 0:MLA_SLOT]
        for h in range(1, MLA_HEADS):
            d = d + t[0][:, h * MLA_SLOT:(h + 1) * MLA_SLOT]
        return (d * t[1] + _half_swap(d * t[2]),), ()
    return _rowwise(name, fn, [dkcat, c256, s256], [], [(MLA_SLOT, F32)])[0]


def _in_widths(ws):
    return [ws, SWA_WIDTH, SWA_KV_WIDTH, SWA_KV_WIDTH, MLA_Q_RANK, MLA_KV_RANK, MLA_SLOT]


def _kr_offset(ws):
    return ws + SWA_WIDTH + 2 * SWA_KV_WIDTH + MLA_Q_RANK + MLA_KV_RANK


def _pad_w_in(w, ws):
    z = lambda n: jnp.zeros((w.shape[0], n), w.dtype)
    o = _kr_offset(ws)
    return jnp.concatenate([w[:, :o], z(MLA_NOPE), w[:, o:], z(MLA_SLOT - MLA_NOPE - MLA_ROPE)], axis=1)


def _unpad_w_in(dw, ws):
    o = _kr_offset(ws)
    return jnp.concatenate([dw[:, :o], dw[:, o + MLA_NOPE:o + MLA_NOPE + MLA_ROPE]], axis=1)


def _pad_w_uq(w):
    r = w.shape[0]
    w3 = w.reshape(r, MLA_HEADS, MLA_NOPE + MLA_ROPE)
    return jnp.pad(w3, ((0, 0), (0, 0), (0, MLA_SLOT - MLA_NOPE - MLA_ROPE))).reshape(r, MLA_HEADS * MLA_SLOT)


def _unpad_w_uq(dw):
    r = dw.shape[0]
    return dw.reshape(r, MLA_HEADS, MLA_SLOT)[..., :MLA_NOPE + MLA_ROPE].reshape(r, -1)


def _pad_w_ukv(w):
    r = w.shape[0]
    w3 = w.reshape(r, MLA_HEADS, MLA_NOPE + MLA_V)
    wk = jnp.pad(w3[..., :MLA_NOPE], ((0, 0), (0, 0), (0, MLA_SLOT - MLA_NOPE))).reshape(r, MLA_HEADS * MLA_SLOT)
    wv = w3[..., MLA_NOPE:].reshape(r, MLA_WIDTH)
    return wk, wv


def _unpad_w_ukv(dwk, dwv):
    r = dwk.shape[0]
    return jnp.concatenate([dwk.reshape(r, MLA_HEADS, MLA_SLOT)[..., :MLA_NOPE], dwv.reshape(r, MLA_HEADS, MLA_V)],
                           axis=-1).reshape(r, -1)


def _layer_prep(lw):
    ws = lw['ssm_d'].shape[0]
    p = dict(lw)
    w_in_pad = _pad_w_in(lw['w_in'], ws)
    p['w_in_pad'] = w_in_pad
    offs = np.cumsum([0] + _in_widths(ws))
    p['w_in_parts'] = [w_in_pad[:, offs[i]:offs[i + 1]] for i in range(7)]
    p['s5_prep'] = _s5_prep(lw['ssm_log_dt'], lw['ssm_a_re'], lw['ssm_a_im'], lw['ssm_b_re'], lw['ssm_b_im'],
                            lw['ssm_c_re'], lw['ssm_c_im'])
    p['sinks_b'] = jnp.broadcast_to(lw['swa_sinks'][:, None], (SWA_HEADS, 128))
    p['w_uq_pad'] = _pad_w_uq(lw['mla_w_uq'])
    p['w_k_pad'], p['w_v'] = _pad_w_ukv(lw['mla_w_ukv'])
    p['w_ukv_pad'] = jnp.concatenate([p['w_k_pad'], p['w_v']], axis=1)
    b = [0, ws, ws + SWA_WIDTH, ws + SWA_WIDTH + MLA_WIDTH]
    p['w_out_g'] = [lw['w_out'][b[g]:b[g + 1]] for g in range(3)]
    p['out_norm_g'] = [lw['out_norm'][b[g]:b[g + 1]] for g in range(3)]
    return p


def _mixer_fwd(tag, x, p, tabs):
    T, D = x.shape
    c64, s64, c256, s256 = tabs
    ws = p['ssm_d'].shape[0]
    widths = _in_widths(ws)
    n_in = sum(widths)
    offs = [int(o) for o in np.cumsum([0] + widths[:-1])]
    h, rstd = _rms_fwd(tag + "_rms", x, p['mix_norm'])
    tm = _tile(T, 256)
    row_i = lambda w: _bs((tm, w), lambda i, j, k: (i, 0))
    parts = _mm(tag + "_in", (T // tm, 1, 1),
                [(h, row_i(D), p['w_in_pad'], _bs((D, n_in), lambda i, j, k: (0, 0)), NN, 0)],
                [(tm, n_in)], [((T, w), F32, row_i(w)) for w in widths],
                lambda accs, ex: tuple(accs[0][:, o:o + w] for o, w in zip(offs, widths)))
    u, q, k, v, cq, ckv, krs = parts
    y_ssm, s5_saved = _s5_fwd(tag + "_s5", u, p['s5_prep'], p['ssm_d'], p['ssm_w_glu'], p['ssm_b_glu'])
    q_r = _rope(tag + "_ropeq", q, c64, s64)
    k_r = _rope(tag + "_ropek", k, c64, s64)
    y_swa, lse_swa = _swa_fwd(tag + "_swa", q_r, k_r, v, p['sinks_b'])
    cqn, r_q = _rms_fwd(tag + "_rmsq", cq, p['mla_q_norm'])
    ckvn, r_kv = _rms_fwd(tag + "_rmskv", ckv, p['mla_kv_norm'])
    qpre = _mm_simple(tag + "_uq", cqn, p['w_uq_pad'], NN, tm=512, tn=1024, tk=MLA_Q_RANK)
    nk_, nv_ = MLA_HEADS * MLA_SLOT, MLA_WIDTH
    kpre, vm = _mm(tag + "_ukv", (T // tm, 1, 1),
                   [(ckvn, row_i(MLA_KV_RANK), p['w_ukv_pad'], _bs((MLA_KV_RANK, nk_ + nv_), lambda i, j, k: (0, 0)),
                     NN, 0)],
                   [(tm, nk_ + nv_)], [((T, nk_), F32, row_i(nk_)), ((T, nv_), BF16, row_i(nv_))],
                   lambda accs, ex: (accs[0][:, :nk_], accs[0][:, nk_:]))
    qcat = _rope(tag + "_ropemq", qpre, c256, s256, out_dtype=BF16)
    kcat = _kcat_fwd(tag + "_kcat", kpre, krs, c256, s256)
    y_mla, lse_mla = _mla_fwd(tag + "_mla", qcat, kcat, vm)
    ys = [y_ssm, y_swa, y_mla]
    yn, rs = [], []
    for g in range(3):
        n_, r_ = _rms_fwd(f"{tag}_rmso{g}", ys[g], p['out_norm_g'][g])
        yn.append(n_)
        rs.append(r_)
    tm3, tn3 = _tile(T, 512), _tile(D, 1024)
    pairs = []
    for g in range(3):
        wg = ys[g].shape[1]
        pairs.append((yn[g], _bs((tm3, wg), lambda i, j, k: (i, 0)), p['w_out_g'][g],
                      _bs((wg, tn3), lambda i, j, k: (0, j)), NN, 0))
    o_spec = _bs((tm3, tn3), lambda i, j, k: (i, j))
    x2, = _mm(tag + "_out", (T // tm3, D // tn3, 1), pairs, [(tm3, tn3)], [((T, D), F32, o_spec)],
              lambda accs, ex: (ex[0] + accs[0],), [(x, o_spec)])
    saved = (x, h, rstd, q_r, k_r, v, cq, ckv, s5_saved, y_swa, lse_swa, cqn, r_q, ckvn, r_kv, qcat, kcat, vm,
             y_mla, lse_mla, ys, yn, rs)
    return x2, saved


def _mixer_bwd(tag, dx2, saved, p, tabs):
    (x, h, rstd, q_r, k_r, v, cq, ckv, s5_saved, y_swa, lse_swa, cqn, r_q, ckvn, r_kv, qcat, kcat, vm,
     y_mla, lse_mla, ys, yn, rs) = saved
    T, D = x.shape
    c64, s64, c256, s256 = tabs
    ws = p['ssm_d'].shape[0]
    g_ = {}
    dys, dwo, don = [], [], []
    for g in range(3):
        wg = ys[g].shape[1]
        dyn = _mm_simple(f"{tag}_dyn{g}", dx2, p['w_out_g'][g], NT, tm=512, tn=wg, tk=1024)
        dwo.append(_mm_simple(f"{tag}_dwo{g}", yn[g], dx2, TN, tm=512, tn=1024, tk=512))
        dy_g, don_g = _rms_bwd(f"{tag}_rmsob{g}", dyn, ys[g], rs[g], p['out_norm_g'][g])
        dys.append(dy_g)
        don.append(don_g)
    g_['w_out'] = jnp.concatenate(dwo, axis=0)
    g_['out_norm'] = jnp.concatenate(don, axis=1)[0]
    dqcat, dsum = _mla_bwd_q(tag + "_mlabq", qcat, kcat, vm, y_mla, lse_mla, dys[2])
    dkcat, dvm = _mla_bwd_kv(tag + "_mlabkv", qcat, kcat, vm, lse_mla, dsum, dys[2])
    dqpre = _rope(tag + "_ropemqb", dqcat, c256, s256, backward=True)
    dkrs = _kcat_bwd(tag + "_kcatb", dkcat, c256, s256)
    g_['mla_w_uq'] = _unpad_w_uq(_mm_simple(tag + "_dwuq", cqn, dqpre, TN, tm=MLA_Q_RANK, tn=1024, tk=512))
    dcqn = _mm_simple(tag + "_dcqn", dqpre, p['w_uq_pad'], NT, tm=512, tn=MLA_Q_RANK, tk=1024)
    dcq, dqn = _rms_bwd(tag + "_rmsqb", dcqn, cq, r_q, p['mla_q_norm'])
    g_['mla_q_norm'] = dqn[0]
    dwk = _mm_simple(tag + "_dwk", ckvn, dkcat, TN, tm=MLA_KV_RANK, tn=1024, tk=512)
    dwv = _mm_simple(tag + "_dwv", ckvn, dvm, TN, tm=MLA_KV_RANK, tn=1024, tk=512)
    g_['mla_w_ukv'] = _unpad_w_ukv(dwk, dwv)
    tm = _tile(T, 512)
    nk_, nv_ = MLA_HEADS * MLA_SLOT, MLA_WIDTH
    tkk = _tile(nk_, 1024)
    dckvn_k = _mm_simple(tag + "_dckvk", dkcat, p['w_k_pad'], NT, tm=512, tn=MLA_KV_RANK, tk=tkk)
    dckvn = _mm_simple(tag + "_dckvv", dvm, p['w_v'], NT, tm=512, tn=MLA_KV_RANK, tk=nv_,
                       epilogue=lambda accs, ex: (accs[0] + ex[0],), extras=[dckvn_k])
    dckv, dkvn = _rms_bwd(tag + "_rmskvb", dckvn, ckv, r_kv, p['mla_kv_norm'])
    g_['mla_kv_norm'] = dkvn[0]
    dq_r, dkc, dkp, dvc, dvp, dsinks = _swa_bwd(tag + "_swab", q_r, k_r, v, p['sinks_b'], y_swa, lse_swa, dys[1])
    g_['swa_sinks'] = dsinks[0, :SWA_HEADS]
    dk_r = _shift_add(tag + "_dksum", dkc, dkp)
    dv = _shift_add(tag + "_dvsum", dvc, dvp)
    dq = _rope(tag + "_ropeqb", dq_r, c64, s64, backward=True)
    dk = _rope(tag + "_ropekb", dk_r, c64, s64, backward=True)
    du, s5g, dd, dw_glu, db_glu = _s5_bwd(tag + "_s5b", dys[0], s5_saved, p['s5_prep'], p['ssm_d'], p['ssm_w_glu'])
    g_['ssm_d'], g_['ssm_w_glu'], g_['ssm_b_glu'] = dd[0], dw_glu, db_glu[0]
    _, pull = jax.vjp(_s5_prep, p['ssm_log_dt'], p['ssm_a_re'], p['ssm_a_im'], p['ssm_b_re'], p['ssm_b_im'],
                      p['ssm_c_re'], p['ssm_c_im'])
    for name, val in zip(['ssm_log_dt', 'ssm_a_re', 'ssm_a_im', 'ssm_b_re', 'ssm_b_im', 'ssm_c_re', 'ssm_c_im'],
                         pull(tuple(s5g))):
        g_[name] = val
    dparts = [du, dq, dk, dv, dcq, dckv, dkrs]
    widths = _in_widths(ws)
    tn = _tile(D, 1024)
    pairs = []
    for dp_, wp_, w in zip(dparts, p['w_in_parts'], widths):
        pairs.append((dp_, _bs((tm, w), lambda i, j, k: (i, 0)), wp_, _bs((tn, w), lambda i, j, k: (j, 0)), NT, 0))
    o_spec = _bs((tm, tn), lambda i, j, k: (i, j))
    dh, = _mm(tag + "_dh", (T // tm, D // tn, 1), pairs, [(tm, tn)], [((T, D), F32, o_spec)],
              lambda accs, ex: (accs[0],))
    tmw, tk = _tile(D, 512), _tile(T, 512)
    pairs = []
    for i_, (dp_, w) in enumerate(zip(dparts, widths)):
        pairs.append((h, _bs((tk, tmw), lambda i, j, k: (k, i)), dp_, _bs((tk, w), lambda i, j, k: (k, 0)), TN, i_))
    dws = _mm(tag + "_dwin", (D // tmw, 1, T // tk), pairs, [(tmw, w) for w in widths],
              [((D, w), F32, _bs((tmw, w), lambda i, j, k: (i, 0))) for w in widths], lambda accs, ex: tuple(accs))
    g_['w_in'] = _unpad_w_in(jnp.concatenate(dws, axis=1), ws)
    dx, dmix = _rms_bwd(tag + "_rmsb", dh, x, rstd, p['mix_norm'], dres=dx2)
    g_['mix_norm'] = dmix[0]
    return dx, g_


def _final_loss(name, x, target, gain):
    D = x.shape[1]

    def fn(t, b):
        xv = t[0]
        r = lax.rsqrt(jnp.mean(xv * xv, axis=-1, keepdims=True) + EPS)
        xh = xv * r
        err = xh * b[0] - t[1]
        part = 0.5 * jnp.sum(jnp.sum(err * err, axis=-1, keepdims=True), axis=0, keepdims=True) / D
        dy = err / D
        dxh = dy * b[0]
        dx = r * (dxh - xh * jnp.mean(dxh * xh, axis=-1, keepdims=True))
        return (dx,), (jnp.broadcast_to(part, (1, 128)), _colsum(dy * xh))
    dx, part, dg = _rowwise(name, fn, [x, target], [gain.reshape(1, D)], [(D, F32)], [128, D])
    return part[0, 0], dx, dg[0]


def _device_step(x, positions, target, n_layers, get_layer, final_norm, on_grads):
    tabs = _rope_tables(positions)
    preps, saved = [], []
    for l in range(n_layers):
        p = _layer_prep(get_layer(l, x))
        preps.append(p)
        x, s1 = _ffn_fwd("ffn1", x, p['ffn1_norm'], p['ffn1_wg2'], p['ffn1_wu2'], p['ffn1_wd'])
        x, s2 = _mixer_fwd("mix", x, p, tabs)
        x, s3 = _ffn_fwd("ffn2", x, p['ffn2_norm'], p['ffn2_wg2'], p['ffn2_wu2'], p['ffn2_wd'])
        saved.append((s1, s2, s3))
    loss_part, dx, dfinal = _final_loss("loss", x, target, final_norm)
    for l in range(n_layers - 1, -1, -1):
        p = preps[l]
        s1, s2, s3 = saved[l]
        g_ = {}
        dx, g_['ffn2_norm'], g_['ffn2_wg2'], g_['ffn2_wu2'], g_['ffn2_wd'] = _ffn_bwd(
            "ffn2b", dx, s3, p['ffn2_norm'], p['ffn2_wg2'], p['ffn2_wu2'], p['ffn2_wd'])
        dx, gm = _mixer_bwd("mixb", dx, s2, p, tabs)
        g_.update(gm)
        dx, g_['ffn1_norm'], g_['ffn1_wg2'], g_['ffn1_wu2'], g_['ffn1_wd'] = _ffn_bwd(
            "ffn1b", dx, s1, p['ffn1_norm'], p['ffn1_wg2'], p['ffn1_wu2'], p['ffn1_wd'])
        g_['ffn1_norm'], g_['ffn2_norm'] = g_['ffn1_norm'][0], g_['ffn2_norm'][0]
        dx = on_grads(l, g_, dx)
    return loss_part, dx, dfinal


_ANY = pl.BlockSpec(memory_space=pl.ANY)
_CHIP_MASKS = (2, 1, 3)


def _place():
    x, y, c = lax.axis_index("x"), lax.axis_index("y"), lax.axis_index("c")
    chips = [(1 - x, y), (x, 1 - y), (1 - x, 1 - y)]
    return x, y, c, 2 * x + y, chips


_HBM = pl.BlockSpec(memory_space=pltpu.HBM)
_SEMS = pl.BlockSpec(memory_space=pltpu.SEMAPHORE)
_EFFECT = pltpu.SideEffectType.DATAFLOW_SIDE_EFFECTING


def _split_start(name, bufs, n_sems, copies):
    n = len(bufs)

    def body(*refs):
        for cp in copies(refs[:n], refs[n], refs[n + 1])[0]:
            cp.start()

    outs = _pcall(
        body, name=name, in_specs=[_HBM] * n,
        out_shape=(pltpu.SemaphoreType.DMA((n_sems,)), pltpu.SemaphoreType.DMA((n_sems,)),
                   *[pltpu.HBM(b.shape, b.dtype) for b in bufs]),
        out_specs=(_SEMS, _SEMS, *[_HBM] * n), input_output_aliases={i: i + 2 for i in range(n)},
        compiler_params=pltpu.CompilerParams(has_side_effects=_EFFECT),
    )(*[pltpu.with_memory_space_constraint(b, pltpu.HBM) for b in bufs])
    return outs[0], outs[1], list(outs[2:])


def _split_wait(name, bufs, send, recv, after, copies):
    n = len(bufs)

    def body(*refs):
        _, sent, landed = copies(refs[:n], refs[n], refs[n + 1])
        for cp in sent:
            cp.wait_send()
        for cp in landed:
            cp.wait_recv()

    return list(_pcall(
        body, name=name, in_specs=[_HBM] * n + [_SEMS, _SEMS, _ANY],
        out_shape=tuple(pltpu.HBM(b.shape, b.dtype) for b in bufs), out_specs=tuple([_HBM] * n),
        input_output_aliases={i: i for i in range(n)},
        compiler_params=pltpu.CompilerParams(has_side_effects=_EFFECT),
    )(*bufs, send, recv, after))


def _gather_copies(n_pass):
    def copies(refs, send, recv):
        x, y, c, s, chips = _place()
        start, landed = [], []
        for a, buf in enumerate(refs[n_pass:]):
            for j in range(3):
                k = a * 3 + j
                to = (*chips[j], c)
                start.append(pltpu.make_async_remote_copy(
                    src_ref=buf.at[s], dst_ref=buf.at[s], send_sem=send.at[k], recv_sem=recv.at[k],
                    device_id=to, device_id_type=MESH))
                theirs = buf.at[s ^ _CHIP_MASKS[j]]
                landed.append(pltpu.make_async_remote_copy(
                    src_ref=theirs, dst_ref=theirs, send_sem=send.at[k], recv_sem=recv.at[k],
                    device_id=to, device_id_type=MESH))
        return start, start, landed
    return copies


def _chip_copies(n_pass, n):
    def copies(refs, send, recv):
        x, y, c, s, chips = _place()
        sums, lands = refs[n_pass:n_pass + n], refs[n_pass + n:]
        start, landed = [], []
        for a in range(n):
            for j in range(3):
                k = a * 3 + j
                to = (*chips[j], c)
                start.append(pltpu.make_async_remote_copy(
                    src_ref=sums[a].at[s ^ _CHIP_MASKS[j]], dst_ref=lands[a].at[j], send_sem=send.at[k],
                    recv_sem=recv.at[k], device_id=to, device_id_type=MESH))
                landed.append(pltpu.make_async_remote_copy(
                    src_ref=lands[a].at[j], dst_ref=lands[a].at[j], send_sem=send.at[k], recv_sem=recv.at[k],
                    device_id=to, device_id_type=MESH))
        return start, start, landed
    return copies


def _stage_shard(name, wl):
    r, c = wl.shape
    tm = _tile(r, 256)
    shard = lambda: 2 * lax.axis_index("x") + lax.axis_index("y")

    def body(w_ref, o_ref):
        o_ref[...] = w_ref[...].astype(o_ref.dtype)

    return _pcall(
        body, name=name, grid=(r // tm,), in_specs=[_bs((tm, c), lambda i: (i, 0))],
        out_specs=_bs((None, tm, c), lambda i: (shard(), i, 0)),
        out_shape=jax.ShapeDtypeStruct((N_SHARD, r, c), BF16),
        compiler_params=pltpu.CompilerParams(dimension_semantics=("parallel",)),
    )(wl)


def _pair_exchange_call(name, grads):
    n = len(grads)

    def body(*refs):
        ins, outs = refs[:n], refs[n:2 * n]
        send, recv = refs[2 * n:]
        x, y, c, s, chips = _place()
        cps = []
        for a in range(n):
            half = ins[a].shape[1] // 2
            cp = pltpu.make_async_remote_copy(
                src_ref=ins[a].at[:, pl.ds((1 - c) * half, half), :], dst_ref=outs[a],
                send_sem=send.at[a], recv_sem=recv.at[a], device_id=(x, y, 1 - c), device_id_type=MESH)
            cp.start()
            cps.append(cp)
        for cp in cps:
            cp.wait()

    return _pcall(
        body, name=name, in_specs=[_ANY] * n, out_specs=[_ANY] * n,
        out_shape=[jax.ShapeDtypeStruct((g.shape[0], g.shape[1] // 2, g.shape[2]), g.dtype) for g in grads],
        scratch_shapes=[pltpu.SemaphoreType.DMA((n,))] * 2,
    )(*grads)


def _half_rows(nb):
    return lambda i: lax.axis_index("c") * nb + i


def _pair_add(name, g, p1):
    ns, half, w = p1.shape
    tm = _tile(half, 256)
    nb = half // tm
    mine = _half_rows(nb)

    def body(g_ref, p_ref, o_ref):
        o_ref[...] = (g_ref[...] + p_ref[...]).astype(o_ref.dtype)

    return _pcall(
        body, name=name, grid=(ns, nb),
        in_specs=[_bs((None, tm, w), lambda s, i: (s, mine(i), 0)), _bs((None, tm, w), lambda s, i: (s, i, 0))],
        out_specs=_bs((None, tm, w), lambda s, i: (s, i, 0)), out_shape=jax.ShapeDtypeStruct(p1.shape, BF16),
        compiler_params=pltpu.CompilerParams(dimension_semantics=("parallel", "parallel")),
    )(g, p1)


def _chip_sum(name, g, p1, p2):
    ns, half, w = p1.shape
    tm = _tile(half, 256)
    nb = half // tm
    mine = _half_rows(nb)
    shard = lambda: 2 * lax.axis_index("x") + lax.axis_index("y")

    def body(g_ref, p_ref, a_ref, b_ref, c_ref, o_ref):
        o_ref[...] = (((g_ref[...] + p_ref[...]) + a_ref[...].astype(F32)) + b_ref[...].astype(F32)) \
            + c_ref[...].astype(F32)

    blk = lambda j: _bs((None, tm, w), lambda i: (j, i, 0))
    return _pcall(
        body, name=name, grid=(nb,),
        in_specs=[_bs((None, tm, w), lambda i: (shard(), mine(i), 0)), _bs((None, tm, w), lambda i: (shard(), i, 0)),
                  blk(0), blk(1), blk(2)],
        out_specs=_bs((tm, w), lambda i: (mine(i), 0)), out_shape=jax.ShapeDtypeStruct((2 * half, w), F32),
        compiler_params=pltpu.CompilerParams(dimension_semantics=("parallel",)),
    )(g, p1, p2, p2, p2)


def _pair_join_call(name, halves):
    n = len(halves)

    def body(*refs):
        outs = refs[n:2 * n]
        send, recv = refs[2 * n:]
        x, y, c, s, chips = _place()
        cps = []
        for a in range(n):
            half = outs[a].shape[0] // 2
            mine = outs[a].at[pl.ds(c * half, half)]
            cp = pltpu.make_async_remote_copy(
                src_ref=mine, dst_ref=mine, send_sem=send.at[a], recv_sem=recv.at[a],
                device_id=(x, y, 1 - c), device_id_type=MESH)
            cp.start()
            cps.append(cp)
        for a in range(n):
            half = outs[a].shape[0] // 2
            theirs = outs[a].at[pl.ds((1 - c) * half, half)]
            pltpu.make_async_remote_copy(
                src_ref=theirs, dst_ref=theirs, send_sem=send.at[a], recv_sem=recv.at[a],
                device_id=(x, y, 1 - c), device_id_type=MESH).wait_recv()
        for cp in cps:
            cp.wait_send()

    return _pcall(
        body, name=name, in_specs=[_ANY] * n, out_specs=[_ANY] * n,
        out_shape=[jax.ShapeDtypeStruct(h.shape, h.dtype) for h in halves],
        input_output_aliases={a: a for a in range(n)},
        scratch_shapes=[pltpu.SemaphoreType.DMA((n,))] * 2,
    )(*halves)


def _all_sum_small(tag, buf):
    n_dev = 8

    def body(in_ref, out_ref, send, recv, loc):
        x, y, c, s, chips = _place()
        me = 4 * x + 2 * y + c
        lc = pltpu.make_async_copy(in_ref, out_ref.at[me], loc)
        lc.start()
        cps = []
        for k in range(1, n_dev):
            to = (x ^ (k >> 2), y ^ ((k >> 1) & 1), c ^ (k & 1))
            cp = pltpu.make_async_remote_copy(
                src_ref=in_ref, dst_ref=out_ref.at[me], send_sem=send.at[k - 1], recv_sem=recv.at[k - 1],
                device_id=to, device_id_type=MESH)
            cp.start()
            cps.append(cp)
        for k in range(1, n_dev):
            theirs = out_ref.at[me ^ k]
            pltpu.make_async_remote_copy(
                src_ref=theirs, dst_ref=theirs, send_sem=send.at[k - 1], recv_sem=recv.at[k - 1],
                device_id=(x, y, c), device_id_type=MESH).wait_recv()
        for cp in cps:
            cp.wait_send()
        lc.wait()

    allb = _pcall(
        body, name=tag + "_gather", in_specs=[_ANY], out_specs=_ANY,
        out_shape=jax.ShapeDtypeStruct((n_dev,) + buf.shape, buf.dtype),
        scratch_shapes=[pltpu.SemaphoreType.DMA((n_dev - 1,))] * 2 + [pltpu.SemaphoreType.DMA(())],
    )(buf)
    rows = buf.shape[0]
    tm = _tile(rows, 512)

    def sum_body(a_ref, o_ref):
        acc = a_ref[0]
        for k in range(1, n_dev):
            acc = acc + a_ref[k]
        o_ref[...] = acc

    return _pcall(
        sum_body, name=tag + "_sum", grid=(rows // tm,),
        in_specs=[_bs((n_dev, tm, 128), lambda i: (0, i, 0))], out_specs=_bs((tm, 128), lambda i: (i, 0)),
        out_shape=jax.ShapeDtypeStruct(buf.shape, F32),
        compiler_params=pltpu.CompilerParams(dimension_semantics=("parallel",)),
    )(allb)


def _adamw(name, w, g, m, v):
    def fn(t, b):
        wv, gv, mv, vv = t
        m2 = ADAM_B1 * mv + (1.0 - ADAM_B1) * gv
        v2 = ADAM_B2 * vv + (1.0 - ADAM_B2) * (gv * gv)
        m_hat = m2 / (1.0 - ADAM_B1 ** ADAM_STEP)
        v_hat = v2 / (1.0 - ADAM_B2 ** ADAM_STEP)
        delta = -ADAM_LR * (m_hat / (jnp.sqrt(v_hat) + ADAM_EPS) + ADAM_WD * wv)
        return (delta, m2, v2), ()
    wd = w.shape[1]
    return _rowwise(name, fn, [w, g, m, v], [], [(wd, F32)] * 3, tm=256 if wd > 1024 else 512)


_WEIGHTS = ['ffn1_norm', 'ffn1_w_gate', 'ffn1_w_up', 'ffn1_w_down', 'mix_norm', 'w_in', 'ssm_log_dt', 'ssm_a_re',
            'ssm_a_im', 'ssm_b_re', 'ssm_b_im', 'ssm_c_re', 'ssm_c_im', 'ssm_d', 'ssm_w_glu', 'ssm_b_glu',
            'swa_sinks', 'mla_q_norm', 'mla_w_uq', 'mla_kv_norm', 'mla_w_ukv', 'out_norm', 'w_out', 'ffn2_norm',
            'ffn2_w_gate', 'ffn2_w_up', 'ffn2_w_down', 'final_norm']
_COL_SHARDED = ['ffn1_w_gate', 'ffn1_w_up', 'w_in', 'mla_w_uq', 'mla_w_ukv', 'ffn2_w_gate', 'ffn2_w_up']
_ROW_SHARDED = ['ffn1_w_down', 'ssm_w_glu', 'w_out', 'ffn2_w_down']
_STACKED = {'ffn1_w_gate': 'ffn1_wg2', 'ffn1_w_up': 'ffn1_wu2', 'ffn2_w_gate': 'ffn2_wg2', 'ffn2_w_up': 'ffn2_wu2'}
_RENAMED = {'ffn1_w_down': 'ffn1_wd', 'ffn2_w_down': 'ffn2_wd'}
_BIG = _COL_SHARDED + _ROW_SHARDED
_SMALL = [n for n in _WEIGHTS if n not in _BIG]


def _pack(vals):
    flat = jnp.concatenate([v.reshape(-1) for v in vals])
    pad = (-flat.shape[0]) % 1024
    return jnp.pad(flat, (0, pad)).reshape(-1, 128)


def _unpack(buf, like):
    flat = buf.reshape(-1)
    out, o = [], 0
    for v in like:
        out.append(flat[o:o + v.size].reshape(v.shape))
        o += v.size
    return out


def kernel(x, positions, ffn1_norm, ffn1_w_gate, ffn1_w_up, ffn1_w_down, mix_norm, w_in, ssm_log_dt, ssm_a_re, ssm_a_im, ssm_b_re, ssm_b_im, ssm_c_re, ssm_c_im, ssm_d, ssm_w_glu, ssm_b_glu, swa_sinks, mla_q_norm, mla_w_uq, mla_kv_norm, mla_w_ukv, out_norm, w_out, ffn2_norm, ffn2_w_gate, ffn2_w_up, ffn2_w_down, final_norm, loss_target, m_ffn1_norm, m_ffn1_w_gate, m_ffn1_w_up, m_ffn1_w_down, m_mix_norm, m_w_in, m_ssm_log_dt, m_ssm_a_re, m_ssm_a_im, m_ssm_b_re, m_ssm_b_im, m_ssm_c_re, m_ssm_c_im, m_ssm_d, m_ssm_w_glu, m_ssm_b_glu, m_swa_sinks, m_mla_q_norm, m_mla_w_uq, m_mla_kv_norm, m_mla_w_ukv, m_out_norm, m_w_out, m_ffn2_norm, m_ffn2_w_gate, m_ffn2_w_up, m_ffn2_w_down, m_final_norm, v_ffn1_norm, v_ffn1_w_gate, v_ffn1_w_up, v_ffn1_w_down, v_mix_norm, v_w_in, v_ssm_log_dt, v_ssm_a_re, v_ssm_a_im, v_ssm_b_re, v_ssm_b_im, v_ssm_c_re, v_ssm_c_im, v_ssm_d, v_ssm_w_glu, v_ssm_b_glu, v_swa_sinks, v_mla_q_norm, v_mla_w_uq, v_mla_kv_norm, v_mla_w_ukv, v_out_norm, v_w_out, v_ffn2_norm, v_ffn2_w_gate, v_ffn2_w_up, v_ffn2_w_down, v_final_norm):
    w = dict(zip(_WEIGHTS, (ffn1_norm, ffn1_w_gate, ffn1_w_up, ffn1_w_down, mix_norm, w_in, ssm_log_dt, ssm_a_re, ssm_a_im, ssm_b_re, ssm_b_im, ssm_c_re, ssm_c_im, ssm_d, ssm_w_glu, ssm_b_glu, swa_sinks, mla_q_norm, mla_w_uq, mla_kv_norm, mla_w_ukv, out_norm, w_out, ffn2_norm, ffn2_w_gate, ffn2_w_up, ffn2_w_down, final_norm)))
    m = dict(zip(_WEIGHTS, (m_ffn1_norm, m_ffn1_w_gate, m_ffn1_w_up, m_ffn1_w_down, m_mix_norm, m_w_in, m_ssm_log_dt, m_ssm_a_re, m_ssm_a_im, m_ssm_b_re, m_ssm_b_im, m_ssm_c_re, m_ssm_c_im, m_ssm_d, m_ssm_w_glu, m_ssm_b_glu, m_swa_sinks, m_mla_q_norm, m_mla_w_uq, m_mla_kv_norm, m_mla_w_ukv, m_out_norm, m_w_out, m_ffn2_norm, m_ffn2_w_gate, m_ffn2_w_up, m_ffn2_w_down, m_final_norm)))
    v = dict(zip(_WEIGHTS, (v_ffn1_norm, v_ffn1_w_gate, v_ffn1_w_up, v_ffn1_w_down, v_mix_norm, v_w_in, v_ssm_log_dt, v_ssm_a_re, v_ssm_a_im, v_ssm_b_re, v_ssm_b_im, v_ssm_c_re, v_ssm_c_im, v_ssm_d, v_ssm_w_glu, v_ssm_b_glu, v_swa_sinks, v_mla_q_norm, v_mla_w_uq, v_mla_kv_norm, v_mla_w_ukv, v_out_norm, v_w_out, v_ffn2_norm, v_ffn2_w_gate, v_ffn2_w_up, v_ffn2_w_down, v_final_norm)))
    n_layers = ffn1_norm.shape[0]

    n_big = len(_BIG)

    x0 = x[0]
    travelling = []
    for l in range(n_layers):
        staged = [_stage_shard("stage_" + n, w[n][l]) for n in _BIG]
        send, recv, thru = _split_start(f"gstart{l}", [x0] + staged, 3 * n_big, _gather_copies(1))
        x0 = thru[0]
        travelling.append((send, recv, thru[1:]))

    def get_layer(l, x_now):
        send, recv, staged = travelling[l]
        full = _split_wait(f"gwait{l}", staged, send, recv, x_now, _gather_copies(0))
        lw = {n: w[n][l] for n in _SMALL if n != 'final_norm'}
        for n, g in zip(_BIG, full):
            ns, r, c = g.shape
            if n in _STACKED:
                lw[_STACKED[n]] = g.reshape(ns * r, c)
            elif n in _ROW_SHARDED:
                lw[_RENAMED.get(n, n)] = g.reshape(ns * r, c)
            else:
                lw[n] = jnp.moveaxis(g, 0, 1).reshape(r, ns * c)
        return lw

    reduced = {n: [None] * n_layers for n in _BIG}
    grads = [None] * n_layers
    pending = {}

    def finish(l, after):
        contrib, p1, send, recv, sums, lands = pending.pop(l)
        landed = _split_wait(f"cwait{l}", sums + lands, send, recv, after, _chip_copies(0, n_big))[n_big:]
        halves = [_chip_sum("rs_chipsum", g, p, q) for g, p, q in zip(contrib, p1, landed)]
        for n, g in zip(_BIG, _pair_join_call("rs_join", halves)):
            reduced[n][l] = g

    def on_grads(l, g_, dx_now):
        if l + 1 in pending:
            finish(l + 1, dx_now)
        grads[l] = g_
        contrib = []
        for n in _BIG:
            r, c = w[n].shape[1:]
            if n in _STACKED:
                g = g_[_STACKED[n]].reshape(N_SHARD, r, c)
            elif n in _ROW_SHARDED:
                g = g_[_RENAMED.get(n, n)].reshape(N_SHARD, r, c)
            else:
                g = jnp.moveaxis(g_[n].reshape(r, N_SHARD, c), 1, 0)
            contrib.append(g)
        p1 = _pair_exchange_call("rs_pairx", contrib)
        sums = [_pair_add("rs_pairadd", g, p) for g, p in zip(contrib, p1)]
        lands = [lax.empty((3,) + s_.shape[1:], s_.dtype) for s_ in sums]
        send, recv, thru = _split_start(f"cstart{l}", [dx_now] + sums + lands, 3 * n_big, _chip_copies(1, n_big))
        pending[l] = (contrib, p1, send, recv, thru[1:1 + n_big], thru[1 + n_big:])
        return thru[0]

    loss_part, dx, dfinal = _device_step(x0, positions[0], loss_target[0], n_layers, get_layer, final_norm, on_grads)
    finish(0, dx)
    loss = lax.psum(loss_part, ("x", "y", "c"))
    grad = {n: jnp.stack(reduced[n]) for n in _BIG}

    small_like = [w[n] for n in _SMALL]
    small_g = [jnp.stack([grads[l][n] for l in range(n_layers)]) for n in _SMALL if n != 'final_norm'] + [dfinal]
    g_small = _all_sum_small("small", _pack(small_g))
    d_small, m_small, v_small = _adamw("adam_small", _pack(small_like), g_small, _pack([m[n] for n in _SMALL]),
                                       _pack([v[n] for n in _SMALL]))
    delta, new_m, new_v = {}, {}, {}
    for n, gv, dv_, mv, vv in zip(_SMALL, _unpack(g_small, small_like), _unpack(d_small, small_like),
                                  _unpack(m_small, small_like), _unpack(v_small, small_like)):
        grad[n], delta[n], new_m[n], new_v[n] = gv, dv_, mv, vv
    for n in _BIG:
        shp = w[n].shape
        two = lambda t: t.reshape(shp[0] * shp[1], shp[2])
        d2, m2, v2 = _adamw("adam_" + n, two(w[n]), two(grad[n]), two(m[n]), two(v[n]))
        delta[n], new_m[n], new_v[n] = d2.reshape(shp), m2.reshape(shp), v2.reshape(shp)
    return (loss, dx[None], *[grad[n] for n in _WEIGHTS], *[delta[n] for n in _WEIGHTS],
            *[new_m[n] for n in _WEIGHTS], *[new_v[n] for n in _WEIGHTS])
```

```python
import functools
import math

import jax
import jax.numpy as jnp
import numpy as np
from jax import lax
from jax.experimental import pallas as pl
from jax.experimental.pallas import tpu as pltpu

F32 = jnp.float32
BF16 = jnp.bfloat16

EPS = 1e-6
ROPE_THETA = 10000.0
SSM_GROUP = 16
SSM_STATE = 64
SWA_HEADS = 8
SWA_KV_HEADS = 2
SWA_HEAD_DIM = 64
SWA_BLOCK = 128
SWA_WIDTH = SWA_HEADS * SWA_HEAD_DIM
SWA_KV_WIDTH = SWA_KV_HEADS * SWA_HEAD_DIM
MLA_HEADS = 8
MLA_Q_RANK = 512
MLA_KV_RANK = 256
MLA_NOPE = 128
MLA_ROPE = 64
MLA_V = 128
MLA_SLOT = 256
MLA_WIDTH = MLA_HEADS * MLA_V
ROPE_HALF = 32

ADAM_LR = 0.001
ADAM_B1 = 0.9
ADAM_B2 = 0.999
ADAM_EPS = 1e-08
ADAM_WD = 0.01
ADAM_STEP = 10

N_SHARD = 4
MESH = pl.DeviceIdType.MESH

NN = ((1,), (0,))
NT = ((1,), (1,))
TN = ((0,), (0,))


def _pcall(body, **kw):
    return pl.pallas_call(body, **kw)


def _tile(n, want, align=16):
    if n <= want:
        return n
    t = want - want % align
    while t >= align:
        if n % t == 0:
            return t
        t -= align
    return n


def _mm(name, grid, pairs, acc_shapes, outs, epilogue, extras=()):
    nk = grid[2]
    n_p, n_e, n_o, n_a = len(pairs), len(extras), len(outs), len(acc_shapes)
    dims_idx = [(p[4], p[5]) for p in pairs]

    def body(*refs):
        ab = refs[:2 * n_p]
        ex = refs[2 * n_p:2 * n_p + n_e]
        o = refs[2 * n_p + n_e:2 * n_p + n_e + n_o]
        accs = refs[2 * n_p + n_e + n_o:]

        def partial_sums():
            sums = [None] * n_a
            for p, (dims, ai) in enumerate(dims_idx):
                a = ab[2 * p][...].astype(BF16)
                b = ab[2 * p + 1][...].astype(BF16)
                d = lax.dot_general(a, b, (dims, ((), ())), preferred_element_type=F32)
                sums[ai] = d if sums[ai] is None else sums[ai] + d
            return sums

        def finish(vals):
            res = epilogue(vals, [e[...] for e in ex])
            for r, oref in zip(res, o):
                oref[...] = r.astype(oref.dtype)

        if nk == 1:
            finish(partial_sums())
        else:
            k = pl.program_id(2)

            @pl.when(k == 0)
            def _():
                for acc in accs:
                    acc[...] = jnp.zeros_like(acc)

            for acc, s in zip(accs, partial_sums()):
                acc[...] += s

            @pl.when(k == nk - 1)
            def _():
                finish([acc[...] for acc in accs])

    in_arrays, in_specs = [], []
    for a, a_spec, b, b_spec, _, _ in pairs:
        in_arrays += [a, b]
        in_specs += [a_spec, b_spec]
    for e, e_spec in extras:
        in_arrays.append(e)
        in_specs.append(e_spec)
    res = _pcall(
        body, name=name, grid=grid, in_specs=in_specs,
        out_specs=[o[2] for o in outs],
        out_shape=[jax.ShapeDtypeStruct(o[0], o[1]) for o in outs],
        scratch_shapes=[] if nk == 1 else [pltpu.VMEM(s, F32) for s in acc_shapes],
        compiler_params=pltpu.CompilerParams(dimension_semantics=("parallel", "parallel", "arbitrary")),
    )(*in_arrays)
    return res


def _bs(shape, fn):
    return pl.BlockSpec(shape, fn)


def _mm_simple(name, a, b, dims, *, tm=512, tn=512, tk=512, out_dtype=F32, epilogue=None, extras=(), scale=None):
    if dims == NN:
        (M, K), N = a.shape, b.shape[1]
    elif dims == NT:
        (M, K), N = a.shape, b.shape[0]
    else:
        (K, M), N = a.shape, b.shape[1]
    tm, tn, tk = _tile(M, tm, 128), _tile(N, tn, 128), _tile(K, tk, 128)
    if dims == NN:
        a_spec, b_spec = _bs((tm, tk), lambda i, j, k: (i, k)), _bs((tk, tn), lambda i, j, k: (k, j))
    elif dims == NT:
        a_spec, b_spec = _bs((tm, tk), lambda i, j, k: (i, k)), _bs((tn, tk), lambda i, j, k: (j, k))
    else:
        a_spec, b_spec = _bs((tk, tm), lambda i, j, k: (k, i)), _bs((tk, tn), lambda i, j, k: (k, j))
    ex = []
    for e in extras:
        if e.shape[0] == 1:
            ex.append((e, _bs((1, tn), lambda i, j, k: (0, j))))
        else:
            ex.append((e, _bs((tm, tn), lambda i, j, k: (i, j))))
    if epilogue is None:
        if scale is None:
            epilogue = lambda accs, ex_: (accs[0],)
        else:
            epilogue = lambda accs, ex_: (accs[0] * scale,)
        out_dtypes = (out_dtype,)
    else:
        out_dtypes = out_dtype if isinstance(out_dtype, tuple) else (out_dtype,)
    outs = [((M, N), dt, _bs((tm, tn), lambda i, j, k: (i, j))) for dt in out_dtypes]
    res = _mm(name, (M // tm, N // tn, K // tk), [(a, a_spec, b, b_spec, dims, 0)], [(tm, tn)], outs, epilogue, ex)
    return res[0] if len(res) == 1 else res


def _rowwise(name, fn, tiles, bcasts, outs, accs=(), *, tm=256):
    rows = tiles[0].shape[0]
    tm = _tile(rows, tm)
    n_t, n_b, n_o, n_a = len(tiles), len(bcasts), len(outs), len(accs)

    def body(*refs):
        t = [r[...] for r in refs[:n_t]]
        b = [r[...] for r in refs[n_t:n_t + n_b]]
        o = refs[n_t + n_b:n_t + n_b + n_o]
        a = refs[n_t + n_b + n_o:]
        ov, av = fn(t, b)
        for r, val in zip(o, ov):
            r[...] = val.astype(r.dtype)
        if n_a:
            @pl.when(pl.program_id(0) == 0)
            def _():
                for r in a:
                    r[...] = jnp.zeros_like(r)
            for r, val in zip(a, av):
                r[...] += val

    in_specs = [_bs((tm, x.shape[1]), lambda i: (i, 0)) for x in tiles]
    in_specs += [_bs(x.shape, lambda i, nd=x.ndim: (0,) * nd) for x in bcasts]
    out_specs = [_bs((tm, w), lambda i: (i, 0)) for w, _ in outs]
    out_specs += [_bs((1, w), lambda i: (0, 0)) for w in accs]
    out_shape = [jax.ShapeDtypeStruct((rows, w), dt) for w, dt in outs]
    out_shape += [jax.ShapeDtypeStruct((1, w), F32) for w in accs]
    return _pcall(
        body, name=name, grid=(rows // tm,), in_specs=in_specs, out_specs=out_specs, out_shape=out_shape,
        compiler_params=pltpu.CompilerParams(dimension_semantics=("arbitrary",)),
    )(*tiles, *bcasts)


def _colsum(v):
    return jnp.sum(v, axis=0, keepdims=True)


def _rms_fwd(name, x, gain, out_dtype=BF16):
    def fn(t, b):
        xv = t[0]
        r = lax.rsqrt(jnp.mean(xv * xv, axis=-1, keepdims=True) + EPS)
        return (xv * r * b[0], r), ()
    w = x.shape[1]
    return _rowwise(name, fn, [x], [gain.reshape(1, w)], [(w, out_dtype), (1, F32)])


def _rms_bwd(name, dh, x, rstd, gain, dres=None, with_bf16=False):
    def fn(t, b):
        dhv, xv, r = t[0], t[1], t[2]
        xh = xv * r
        dxh = dhv * b[0]
        dx = r * (dxh - xh * jnp.mean(dxh * xh, axis=-1, keepdims=True))
        if dres is not None:
            dx = dx + t[3]
        return ((dx, dx) if with_bf16 else (dx,)), (_colsum(dhv * xh),)
    w = x.shape[1]
    tiles = [dh, x, rstd] + ([dres] if dres is not None else [])
    outs = [(w, F32), (w, BF16)] if with_bf16 else [(w, F32)]
    return _rowwise(name, fn, tiles, [gain.reshape(1, w)], outs, [w])


def _sigmoid(v):
    return 1.0 / (1.0 + jnp.exp(-v))


def _ffn_fwd(tag, x, gain, wg2, wu2, wd):
    T, D = x.shape
    fs = wg2.shape[1]
    h, rstd = _rms_fwd(tag + "_rms", x, gain)
    tm = _tile(T, 256)
    a_spec = _bs((tm, D), lambda i, j, k: (j, 0))
    w_spec = _bs((D, fs), lambda i, j, k: (i, 0))
    o_spec = _bs((tm, fs), lambda i, j, k: (j, i))

    def epi(accs, ex):
        a, b = accs
        return a, b, a * _sigmoid(a) * b

    a, b, t = _mm(
        tag + "_up", (N_SHARD, T // tm, 1),
        [(h, a_spec, wg2, w_spec, NN, 0), (h, a_spec, wu2, w_spec, NN, 1)],
        [(tm, fs), (tm, fs)],
        [((T, N_SHARD * fs), F32, o_spec), ((T, N_SHARD * fs), F32, o_spec), ((T, N_SHARD * fs), BF16, o_spec)],
        epi)
    y = _mm_simple(tag + "_down", t, wd, NN, tm=512, tn=D, tk=fs // 2 if fs % 256 == 0 else fs,
                   epilogue=lambda accs, ex: (ex[0] + 0.5 * accs[0],), extras=[x])
    return y, (x, h, rstd, a, b, t)


def _ffn_bwd(tag, dy, dy16, saved, gain, wg2, wu2, wd):
    x, h, rstd, a, b, t = saved
    T, D = x.shape
    fs = wg2.shape[1]
    F = N_SHARD * fs
    dwd = _mm_simple(tag + "_dwd", t, dy16, TN, tm=512, tn=D, tk=512, scale=0.5, out_dtype=BF16)

    def epi(accs, ex):
        dt = 0.5 * accs[0]
        av, bv = ex
        sig = _sigmoid(av)
        da = dt * bv * (sig * (1.0 + av * (1.0 - sig)))
        db = dt * (av * sig)
        return da, db

    tn = 512 if F % 512 == 0 else (256 if F % 256 == 0 else 128)
    da, db = _mm_simple(tag + "_dt", dy16, wd, NT, tm=512, tn=tn, tk=D, out_dtype=(BF16, BF16), epilogue=epi,
                        extras=[a, b])
    tm = _tile(D, 1024)
    per = D // tm
    tk = _tile(T, 512)
    h_spec = _bs((tk, tm), lambda i, j, k: (k, i % per))
    d_spec = _bs((tk, fs), lambda i, j, k: (k, i // per))
    o_spec = _bs((tm, fs), lambda i, j, k: (i, 0))
    dwg2, dwu2 = _mm(
        tag + "_dwgu", (N_SHARD * per, 1, T // tk),
        [(h, h_spec, da, d_spec, TN, 0), (h, h_spec, db, d_spec, TN, 1)],
        [(tm, fs), (tm, fs)],
        [((N_SHARD * D, fs), BF16, o_spec), ((N_SHARD * D, fs), BF16, o_spec)],
        lambda accs, ex: tuple(accs))
    tm2 = _tile(T, 512)
    tn2 = _tile(D, 1024)
    per2 = D // tn2
    g_spec = _bs((tm2, fs), lambda i, j, k: (i, k))
    w_spec = _bs((tn2, fs), lambda i, j, k: (k * per2 + j, 0))
    o2 = _bs((tm2, tn2), lambda i, j, k: (i, j))
    dh, = _mm(
        tag + "_dh", (T // tm2, per2, N_SHARD),
        [(da, g_spec, wg2, w_spec, NT, 0), (db, g_spec, wu2, w_spec, NT, 0)],
        [(tm2, tn2)], [((T, D), F32, o2)], lambda accs, ex: (accs[0],))
    dx, dx16, dgain = _rms_bwd(tag + "_rmsb", dh, x, rstd, gain, dres=dy, with_bf16=True)
    return dx, dx16, dgain, dwg2, dwu2, dwd


def _rope_tables(positions):
    inv_freq = ROPE_THETA ** (-jnp.arange(0, 2 * ROPE_HALF, 2, dtype=F32) / (2 * ROPE_HALF))
    ang = positions.astype(F32)[:, None] * inv_freq
    c, s = jnp.cos(ang), jnp.sin(ang)
    c64 = jnp.concatenate([c, c], axis=1)
    s64 = jnp.concatenate([-s, s], axis=1)
    T = positions.shape[0]
    one, zero = jnp.ones((T, MLA_NOPE), F32), jnp.zeros((T, MLA_NOPE), F32)
    pad1, pad0 = jnp.ones((T, MLA_SLOT - MLA_NOPE - MLA_ROPE), F32), jnp.zeros((T, MLA_SLOT - MLA_NOPE - MLA_ROPE), F32)
    c256 = jnp.concatenate([one, c64, pad1], axis=1)
    s256 = jnp.concatenate([zero, s64, pad0], axis=1)
    return c64, s64, c256, s256


def _half_swap(v):
    w = v.shape[1]
    lane = lax.broadcasted_iota(jnp.int32, v.shape, 1)
    first = (lane % (2 * ROPE_HALF)) < ROPE_HALF
    return jnp.where(first, pltpu.roll(v, w - ROPE_HALF, 1), pltpu.roll(v, ROPE_HALF, 1))


def _rope(name, x, ctab, stab, backward=False, out_dtype=F32):
    reps = x.shape[1] // ctab.shape[1]

    def fn(t, b):
        xv = t[0]
        c = jnp.tile(t[1], (1, reps)) if reps > 1 else t[1]
        s = jnp.tile(t[2], (1, reps)) if reps > 1 else t[2]
        if backward:
            return (xv * c + _half_swap(xv * s),), ()
        return (xv * c + _half_swap(xv) * s,), ()
    return _rowwise(name, fn, [x, ctab, stab], [], [(x.shape[1], out_dtype)])[0]


def _s5_prep(log_dt, a_re, a_im, b_re, b_im, c_re, c_im):
    G, P = a_re.shape
    C = b_re.shape[-1]
    dt = jnp.exp(log_dt)[:, None]
    mag = jnp.exp(a_re * dt)
    abar_r = mag * jnp.cos(a_im * dt)
    abar_i = mag * jnp.sin(a_im * dt)
    den = a_re * a_re + a_im * a_im
    nr = abar_r - 1.0
    qr = (nr * a_re + abar_i * a_im) / den
    qi = (abar_i * a_re - nr * a_im) / den
    bbar_r = qr[..., None] * b_re - qi[..., None] * b_im
    bbar_i = qr[..., None] * b_im + qi[..., None] * b_re
    eye = jnp.eye(G, dtype=F32)
    b_r = jnp.einsum('gpc,gh->gchp', bbar_r, eye).reshape(G * C, G * P)
    b_i = jnp.einsum('gpc,gh->gchp', bbar_i, eye).reshape(G * C, G * P)
    c_r = jnp.einsum('gcp,gh->gphc', c_re, eye).reshape(G * P, G * C)
    c_i = jnp.einsum('gcp,gh->gphc', c_im, eye).reshape(G * P, G * C)
    return abar_r.reshape(1, G * P), abar_i.reshape(1, G * P), b_r, b_i, c_r, -c_i


def _scan_lanes(gp):
    for tl in (1024, 512, 256):
        if gp % tl == 0:
            return tl
    return 128


def _scan_fwd(name, bu_r, bu_i, a_r, a_i):
    T, gp = bu_r.shape
    tl = _scan_lanes(gp)
    tc = _tile(T, 256)

    def body(br_ref, bi_ref, ar_ref, ai_ref, xr_ref, xi_ref, pr_ref, pi_ref, sr, si):
        @pl.when(pl.program_id(1) == 0)
        def _():
            sr[...] = jnp.zeros_like(sr)
            si[...] = jnp.zeros_like(si)
        ar, ai = ar_ref[...], ai_ref[...]

        def step(g, carry):
            xr, xi = carry
            base = pl.multiple_of(g * 8, 8)
            b_r = br_ref[pl.ds(base, 8), :]
            b_i = bi_ref[pl.ds(base, 8), :]
            rows_r, rows_i, prev_r, prev_i = [], [], [], []
            for j in range(8):
                prev_r.append(xr)
                prev_i.append(xi)
                nr = ar * xr - ai * xi + b_r[j:j + 1, :]
                ni = ar * xi + ai * xr + b_i[j:j + 1, :]
                xr, xi = nr, ni
                rows_r.append(xr)
                rows_i.append(xi)
            xr_ref[pl.ds(base, 8), :] = jnp.concatenate(rows_r, axis=0)
            xi_ref[pl.ds(base, 8), :] = jnp.concatenate(rows_i, axis=0)
            pr_ref[pl.ds(base, 8), :] = jnp.concatenate(prev_r, axis=0)
            pi_ref[pl.ds(base, 8), :] = jnp.concatenate(prev_i, axis=0)
            return xr, xi

        xr, xi = lax.fori_loop(0, tc // 8, step, (sr[...], si[...]))
        sr[...] = xr
        si[...] = xi

    spec = _bs((tc, tl), lambda l, t: (t, l))
    a_spec = _bs((1, tl), lambda l, t: (0, l))
    return _pcall(
        body, name=name, grid=(gp // tl, T // tc),
        in_specs=[spec, spec, a_spec, a_spec], out_specs=[spec] * 4,
        out_shape=[jax.ShapeDtypeStruct((T, gp), F32)] * 4,
        scratch_shapes=[pltpu.VMEM((1, tl), F32), pltpu.VMEM((1, tl), F32)],
        compiler_params=pltpu.CompilerParams(dimension_semantics=("parallel", "arbitrary")),
    )(bu_r, bu_i, a_r, a_i)


def _scan_bwd(name, g_r, g_i, p_r, p_i, a_r, a_i):
    T, gp = g_r.shape
    tl = _scan_lanes(gp)
    tc = _tile(T, 256)
    nt = T // tc
    ng = tc // 8

    def body(gr_ref, gi_ref, pr_ref, pi_ref, ar_ref, ai_ref, lr_ref, li_ref, dar_ref, dai_ref, sr, si, accr, acci):
        t = pl.program_id(1)

        @pl.when(t == 0)
        def _():
            sr[...] = jnp.zeros_like(sr)
            si[...] = jnp.zeros_like(si)
            accr[...] = jnp.zeros_like(accr)
            acci[...] = jnp.zeros_like(acci)
        ar, ai = ar_ref[...], ai_ref[...]

        def step(kk, carry):
            lr, li = carry
            base = pl.multiple_of((ng - 1 - kk) * 8, 8)
            gr8 = gr_ref[pl.ds(base, 8), :]
            gi8 = gi_ref[pl.ds(base, 8), :]
            rows_r, rows_i = [None] * 8, [None] * 8
            for j in range(7, -1, -1):
                nr = gr8[j:j + 1, :] + ar * lr + ai * li
                ni = gi8[j:j + 1, :] - ai * lr + ar * li
                lr, li = nr, ni
                rows_r[j] = lr
                rows_i[j] = li
            lam_r = jnp.concatenate(rows_r, axis=0)
            lam_i = jnp.concatenate(rows_i, axis=0)
            lr_ref[pl.ds(base, 8), :] = lam_r
            li_ref[pl.ds(base, 8), :] = lam_i
            pr8 = pr_ref[pl.ds(base, 8), :]
            pi8 = pi_ref[pl.ds(base, 8), :]
            accr[...] += lam_r * pr8 + lam_i * pi8
            acci[...] += lam_i * pr8 - lam_r * pi8
            return lr, li

        lr, li = lax.fori_loop(0, ng, step, (sr[...], si[...]))
        sr[...] = lr
        si[...] = li

        @pl.when(t == nt - 1)
        def _():
            dar_ref[...] = jnp.sum(accr[...], axis=0, keepdims=True)
            dai_ref[...] = jnp.sum(acci[...], axis=0, keepdims=True)

    spec = _bs((tc, tl), lambda l, t: (nt - 1 - t, l))
    a_spec = _bs((1, tl), lambda l, t: (0, l))
    return _pcall(
        body, name=name, grid=(gp // tl, nt),
        in_specs=[spec] * 4 + [a_spec, a_spec], out_specs=[spec, spec, a_spec, a_spec],
        out_shape=[jax.ShapeDtypeStruct((T, gp), F32)] * 2 + [jax.ShapeDtypeStruct((1, gp), F32)] * 2,
        scratch_shapes=[pltpu.VMEM((1, tl), F32), pltpu.VMEM((1, tl), F32),
                        pltpu.VMEM((8, tl), F32), pltpu.VMEM((8, tl), F32)],
        compiler_params=pltpu.CompilerParams(dimension_semantics=("parallel", "arbitrary")),
    )(g_r, g_i, p_r, p_i, a_r, a_i)


_GELU_C = math.sqrt(2.0 / math.pi)


def _gelu(v):
    return 0.5 * v * (1.0 + jnp.tanh(_GELU_C * (v + 0.044715 * v * v * v)))


def _gelu_grad(v):
    th = jnp.tanh(_GELU_C * (v + 0.044715 * v * v * v))
    return 0.5 * (1.0 + th) + 0.5 * v * (1.0 - th * th) * _GELU_C * (1.0 + 3.0 * 0.044715 * v * v)


def _mm_shared_lhs(name, a, bs_, dims, *, tm, tn, tk):
    if dims == NN:
        (M, K), N = a.shape, bs_[0].shape[1]
    elif dims == NT:
        (M, K), N = a.shape, bs_[0].shape[0]
    else:
        (K, M), N = a.shape, bs_[0].shape[1]
    tm, tn, tk = _tile(M, tm, 128), _tile(N, tn, 128), _tile(K, tk, 128)
    if dims == NN:
        a_spec, b_spec = _bs((tm, tk), lambda i, j, k: (i, k)), _bs((tk, tn), lambda i, j, k: (k, j))
    elif dims == NT:
        a_spec, b_spec = _bs((tm, tk), lambda i, j, k: (i, k)), _bs((tn, tk), lambda i, j, k: (j, k))
    else:
        a_spec, b_spec = _bs((tk, tm), lambda i, j, k: (k, i)), _bs((tk, tn), lambda i, j, k: (k, j))
    o_spec = _bs((tm, tn), lambda i, j, k: (i, j))
    n = len(bs_)
    return _mm(name, (M // tm, N // tn, K // tk), [(a, a_spec, b, b_spec, dims, i) for i, b in enumerate(bs_)],
               [(tm, tn)] * n, [((M, N), F32, o_spec)] * n, lambda accs, ex: tuple(accs))


def _s5_fwd(tag, u, prep, d_skip, w_glu, b_glu):
    a_r, a_i, b_r, b_i, c_r, c_in = prep
    T, W = u.shape
    gp = a_r.shape[1]
    bu_r, bu_i = _mm_shared_lhs(tag + "_bu", u, [b_r, b_i], NN, tm=512, tn=1024, tk=W)
    x_r, x_i, p_r, p_i = _scan_fwd(tag + "_scan", bu_r, bu_i, a_r, a_i)
    tm, tk = _tile(T, 512), _tile(gp, 512)
    x_spec = _bs((tm, tk), lambda i, j, k: (i, k))
    c_spec = _bs((tk, W), lambda i, j, k: (k, 0))
    full = _bs((tm, W), lambda i, j, k: (i, 0))
    row = _bs((1, W), lambda i, j, k: (0, 0))

    def epi(accs, ex):
        y = accs[0] + ex[1] * ex[0]
        return y, _gelu(y)

    ypre, yg = _mm(tag + "_y", (T // tm, 1, gp // tk),
                   [(x_r, x_spec, c_r, c_spec, NN, 0), (x_i, x_spec, c_in, c_spec, NN, 0)],
                   [(tm, W)], [((T, W), F32, full)] * 2, epi, [(u, full), (d_skip.reshape(1, W), row)])

    def epi2(accs, ex):
        pre = accs[0] + ex[1]
        return ex[0] * _sigmoid(pre), pre

    out, pre = _mm_simple(tag + "_glu", yg, w_glu, NN, tm=512, tn=W, tk=W, out_dtype=(F32, F32), epilogue=epi2,
                          extras=[yg, b_glu.reshape(1, W)])
    return out, (u, x_r, x_i, p_r, p_i, ypre, yg, pre)


def _s5_bwd(tag, d_out, saved, prep, d_skip, w_glu):
    u, x_r, x_i, p_r, p_i, ypre, yg, pre = saved
    a_r, a_i, b_r, b_i, c_r, c_in = prep
    T, W = u.shape

    def gate_fn(t, b):
        gate = _sigmoid(t[2])
        dpre = t[0] * t[1] * gate * (1.0 - gate)
        return (dpre, t[0] * gate), (_colsum(dpre),)

    dpre, tmp, db_glu = _rowwise(tag + "_gateb", gate_fn, [d_out, yg, pre], [], [(W, F32), (W, F32)], [W])
    dw_glu = _mm_simple(tag + "_dwglu", yg, dpre, TN, tm=W, tn=W, tk=512, out_dtype=BF16)
    dy = _mm_simple(tag + "_dyg", dpre, w_glu, NT, tm=512, tn=W, tk=W,
                    epilogue=lambda accs, ex: ((accs[0] + ex[0]) * _gelu_grad(ex[1]),), extras=[tmp, ypre])
    dd, = _rowwise(tag + "_dd", lambda t, b: ((), (_colsum(t[0] * t[1]),)), [dy, u], [], [], [W])
    dx_r, dx_i = _mm_shared_lhs(tag + "_dx", dy, [c_r, c_in], NT, tm=512, tn=1024, tk=W)
    dc_r = _mm_simple(tag + "_dcr", x_r, dy, TN, tm=512, tn=W, tk=512)
    dc_in = _mm_simple(tag + "_dci", x_i, dy, TN, tm=512, tn=W, tk=512)
    lam_r, lam_i, da_r, da_i = _scan_bwd(tag + "_scanb", dx_r, dx_i, p_r, p_i, a_r, a_i)
    gp = a_r.shape[1]
    tm, tk = _tile(T, 512), _tile(gp, 512)
    l_spec = _bs((tm, tk), lambda i, j, k: (i, k))
    b_spec = _bs((W, tk), lambda i, j, k: (0, k))
    full = _bs((tm, W), lambda i, j, k: (i, 0))
    row = _bs((1, W), lambda i, j, k: (0, 0))
    du, = _mm(tag + "_du", (T // tm, 1, gp // tk),
              [(lam_r, l_spec, b_r, b_spec, NT, 0), (lam_i, l_spec, b_i, b_spec, NT, 0)],
              [(tm, W)], [((T, W), F32, full)], lambda accs, ex: (accs[0] + ex[1] * ex[0],),
              [(dy, full), (d_skip.reshape(1, W), row)])
    db_r, db_i = _mm_shared_lhs(tag + "_db", u, [lam_r, lam_i], TN, tm=W, tn=1024, tk=512)
    return du, (da_r, da_i, db_r, db_i, dc_r, dc_in), dd, dw_glu, db_glu


def _swa_mask(n):
    B = SWA_BLOCK
    r = lax.broadcasted_iota(jnp.int32, (B, 2 * B), 0)
    c = lax.broadcasted_iota(jnp.int32, (B, 2 * B), 1)
    d = r + B - c
    return (d >= 0) & (d < B) & ((n > 0) | (c >= B))


def _swa_specs(T):
    B = SWA_BLOCK
    cur = lambda w: _bs((B, w), lambda n: (n, 0))
    prev = lambda w: _bs((B, w), lambda n: (jnp.maximum(n - 1, 0), 0))
    return cur, prev


def _swa_fwd(name, q, k, v, sinks_b):
    T = q.shape[0]
    B = SWA_BLOCK
    scale = SWA_HEAD_DIM ** -0.5
    per_kv = SWA_HEADS // SWA_KV_HEADS

    def body(q_ref, kc_ref, kp_ref, vc_ref, vp_ref, s_ref, o_ref, l_ref):
        n = pl.program_id(0)
        kcat = jnp.concatenate([kp_ref[...], kc_ref[...]], axis=0).astype(BF16)
        vcat = jnp.concatenate([vp_ref[...], vc_ref[...]], axis=0).astype(BF16)
        mask = _swa_mask(n)
        lane = lax.broadcasted_iota(jnp.int32, (B, 128), 1)
        lo = lane < SWA_HEAD_DIM
        lse_out = jnp.zeros((B, 128), F32)
        for jb in range(SWA_HEADS // 2):
            qblk = q_ref[:, jb * 128:(jb + 1) * 128]
            h = (2 * jb) // per_kv
            half_h = lo if h == 0 else jnp.logical_not(lo)
            outs = []
            for e in range(2):
                j = 2 * jb + e
                qa = qblk if e == h else pltpu.roll(qblk, SWA_HEAD_DIM, 1)
                qm = jnp.where(half_h, qa, 0.0).astype(BF16)
                s = lax.dot_general(qm, kcat, (NT, ((), ())), preferred_element_type=F32) * scale
                s = jnp.where(mask, s, -jnp.inf)
                sk = s_ref[j:j + 1, 0:1]
                m = jnp.maximum(jnp.max(s, axis=1, keepdims=True), sk)
                ex = jnp.exp(s - m)
                den = jnp.sum(ex, axis=1, keepdims=True) + jnp.exp(sk - m)
                p = ex / den
                r = lax.dot_general(p.astype(BF16), vcat, (NN, ((), ())), preferred_element_type=F32)
                outs.append(r if e == h else pltpu.roll(r, SWA_HEAD_DIM, 1))
                lse_out = jnp.where(lane == j, m + jnp.log(den), lse_out)
            o_ref[:, jb * 128:(jb + 1) * 128] = jnp.where(lo, outs[0], outs[1])
        l_ref[...] = lse_out

    cur, prev = _swa_specs(T)
    return _pcall(
        body, name=name, grid=(T // B,),
        in_specs=[cur(SWA_WIDTH), cur(SWA_KV_WIDTH), prev(SWA_KV_WIDTH), cur(SWA_KV_WIDTH), prev(SWA_KV_WIDTH),
                  _bs(sinks_b.shape, lambda n: (0, 0))],
        out_specs=[cur(SWA_WIDTH), cur(128)],
        out_shape=[jax.ShapeDtypeStruct((T, SWA_WIDTH), F32), jax.ShapeDtypeStruct((T, 128), F32)],
        compiler_params=pltpu.CompilerParams(dimension_semantics=("parallel",)),
    )(q, k, k, v, v, sinks_b)


def _swa_bwd(name, q, k, v, sinks_b, o, lse, do):
    T = q.shape[0]
    B = SWA_BLOCK
    scale = SWA_HEAD_DIM ** -0.5
    per_kv = SWA_HEADS // SWA_KV_HEADS

    def body(q_ref, kc_ref, kp_ref, vc_ref, vp_ref, s_ref, o_ref, l_ref, do_ref,
             dq_ref, dkc_ref, dkp_ref, dvc_ref, dvp_ref, ds_ref):
        n = pl.program_id(0)

        @pl.when(n == 0)
        def _():
            ds_ref[...] = jnp.zeros_like(ds_ref)
        kcat = jnp.concatenate([kp_ref[...], kc_ref[...]], axis=0).astype(BF16)
        vcat = jnp.concatenate([vp_ref[...], vc_ref[...]], axis=0).astype(BF16)
        mask = _swa_mask(n)
        lane = lax.broadcasted_iota(jnp.int32, (B, 128), 1)
        lane1 = lax.broadcasted_iota(jnp.int32, (1, 128), 1)
        lo = lane < SWA_HEAD_DIM
        lblk = l_ref[...]
        dk = jnp.zeros((2 * B, 128), F32)
        dv = jnp.zeros((2 * B, 128), F32)
        dsink = jnp.zeros((1, 128), F32)
        for jb in range(SWA_HEADS // 2):
            sl = slice(jb * 128, (jb + 1) * 128)
            qblk, doblk = q_ref[:, sl], do_ref[:, sl]
            prod = doblk * o_ref[:, sl]
            h = (2 * jb) // per_kv
            half_h = lo if h == 0 else jnp.logical_not(lo)
            parts = []
            for e in range(2):
                j = 2 * jb + e
                half_e = lo if e == 0 else jnp.logical_not(lo)
                dsum = jnp.sum(jnp.where(half_e, prod, 0.0), axis=1, keepdims=True)
                lj = jnp.sum(jnp.where(lane == j, lblk, 0.0), axis=1, keepdims=True)
                qa = qblk if e == h else pltpu.roll(qblk, SWA_HEAD_DIM, 1)
                da = doblk if e == h else pltpu.roll(doblk, SWA_HEAD_DIM, 1)
                qm = jnp.where(half_h, qa, 0.0).astype(BF16)
                dm = jnp.where(half_h, da, 0.0).astype(BF16)
                s = lax.dot_general(qm, kcat, (NT, ((), ())), preferred_element_type=F32) * scale
                p = jnp.where(mask, jnp.exp(s - lj), 0.0)
                dp = lax.dot_general(dm, vcat, (NT, ((), ())), preferred_element_type=F32)
                dsb = (p * (dp - dsum) * scale).astype(BF16)
                dqa = lax.dot_general(dsb, kcat, (NN, ((), ())), preferred_element_type=F32)
                parts.append(dqa if e == h else pltpu.roll(dqa, SWA_HEAD_DIM, 1))
                dk = dk + lax.dot_general(dsb, qm, (TN, ((), ())), preferred_element_type=F32)
                dv = dv + lax.dot_general(p.astype(BF16), dm, (TN, ((), ())), preferred_element_type=F32)
                sk = s_ref[j:j + 1, 0:1]
                contrib = jnp.sum(jnp.exp(sk - lj) * dsum, axis=0, keepdims=True)
                dsink = jnp.where(lane1 == j, dsink - contrib, dsink)
            dq_ref[:, sl] = jnp.where(lo, parts[0], parts[1])
        dkp_ref[...] = dk[:B]
        dkc_ref[...] = dk[B:]
        dvp_ref[...] = dv[:B]
        dvc_ref[...] = dv[B:]
        ds_ref[...] += dsink

    cur, prev = _swa_specs(T)
    kv = jax.ShapeDtypeStruct((T, SWA_KV_WIDTH), F32)
    return _pcall(
        body, name=name, grid=(T // B,),
        in_specs=[cur(SWA_WIDTH), cur(SWA_KV_WIDTH), prev(SWA_KV_WIDTH), cur(SWA_KV_WIDTH), prev(SWA_KV_WIDTH),
                  _bs(sinks_b.shape, lambda n: (0, 0)), cur(SWA_WIDTH), cur(128), cur(SWA_WIDTH)],
        out_specs=[cur(SWA_WIDTH)] + [cur(SWA_KV_WIDTH)] * 4 + [_bs((1, 128), lambda n: (0, 0))],
        out_shape=[jax.ShapeDtypeStruct((T, SWA_WIDTH), F32), kv, kv, kv, kv, jax.ShapeDtypeStruct((1, 128), F32)],
        compiler_params=pltpu.CompilerParams(dimension_semantics=("arbitrary",)),
    )(q, k, k, v, v, sinks_b, o, lse, do)


def _shift_add(name, cur, prv):
    T, W = cur.shape
    B = SWA_BLOCK
    nb = T // B

    def body(c_ref, p_ref, o_ref):
        n = pl.program_id(0)
        o_ref[...] = c_ref[...] + jnp.where(n < nb - 1, p_ref[...], 0.0)

    return _pcall(
        body, name=name, grid=(nb,),
        in_specs=[_bs((B, W), lambda n: (n, 0)), _bs((B, W), lambda n: (jnp.minimum(n + 1, nb - 1), 0))],
        out_specs=_bs((B, W), lambda n: (n, 0)), out_shape=jax.ShapeDtypeStruct((T, W), F32),
        compiler_params=pltpu.CompilerParams(dimension_semantics=("parallel",)),
    )(cur, prv)


MLA_BLOCK = 512
_MLA_SCALE = (MLA_NOPE + MLA_ROPE) ** -0.5


def _lower_tri(tb):
    r = lax.broadcasted_iota(jnp.int32, (tb, tb), 0)
    c = lax.broadcasted_iota(jnp.int32, (tb, tb), 1)
    return c <= r


def _mla_fwd(name, qcat, kcat, v):
    T = qcat.shape[0]
    tb = _tile(T, MLA_BLOCK)
    nb = T // tb

    def body(q_ref, k_ref, v_ref, o_ref, l_ref):
        qi = pl.program_id(1)
        q = q_ref[...]

        def block(kb, carry, diagonal):
            m_prev, l_prev, acc = carry
            rows = pl.ds(pl.multiple_of(kb * tb, tb), tb)
            s = lax.dot_general(q, k_ref[rows, :], (NT, ((), ())), preferred_element_type=F32) * _MLA_SCALE
            if diagonal:
                s = jnp.where(_lower_tri(tb), s, -jnp.inf)
            m_new = jnp.maximum(m_prev, jnp.max(s, axis=1, keepdims=True))
            alpha = jnp.exp(m_prev - m_new)
            p = jnp.exp(s - m_new)
            l_new = alpha * l_prev + jnp.sum(p, axis=1, keepdims=True)
            acc = alpha * acc + lax.dot_general(p.astype(BF16), v_ref[rows, :], (NN, ((), ())),
                                                preferred_element_type=F32)
            return m_new, l_new, acc

        init = (jnp.full((tb, 1), -jnp.inf, F32), jnp.zeros((tb, 1), F32), jnp.zeros((tb, MLA_V), F32))
        carry = lax.fori_loop(0, qi, lambda kb, c: block(kb, c, False), init)
        m_fin, l_fin, acc = block(qi, carry, True)
        o_ref[...] = acc / l_fin
        l_ref[...] = m_fin + jnp.log(l_fin)

    return _pcall(
        body, name=name, grid=(MLA_HEADS, nb),
        in_specs=[_bs((tb, MLA_SLOT), lambda h, qi: (qi, h)), _bs((T, MLA_SLOT), lambda h, qi: (0, h)),
                  _bs((T, MLA_V), lambda h, qi: (0, h))],
        out_specs=[_bs((tb, MLA_V), lambda h, qi: (qi, h)), _bs((None, tb, 1), lambda h, qi: (h, qi, 0))],
        out_shape=[jax.ShapeDtypeStruct((T, MLA_WIDTH), F32), jax.ShapeDtypeStruct((MLA_HEADS, T, 1), F32)],
        compiler_params=pltpu.CompilerParams(dimension_semantics=("parallel", "arbitrary")),
    )(qcat, kcat, v)


def _mla_bwd_q(name, qcat, kcat, v, o, lse, do):
    T = qcat.shape[0]
    tb = _tile(T, MLA_BLOCK)
    nb = T // tb

    def body(q_ref, k_ref, v_ref, o_ref, do_ref, l_ref, dq_ref, ds_ref):
        qi = pl.program_id(1)
        q = q_ref[...]
        do = do_ref[...]
        dsum = jnp.sum(do * o_ref[...], axis=1, keepdims=True)
        ds_ref[...] = dsum
        dob = do.astype(BF16)
        lse_q = l_ref[...]

        def block(kb, acc, diagonal):
            rows = pl.ds(pl.multiple_of(kb * tb, tb), tb)
            k = k_ref[rows, :]
            s = lax.dot_general(q, k, (NT, ((), ())), preferred_element_type=F32) * _MLA_SCALE
            p = jnp.exp(s - lse_q)
            if diagonal:
                p = jnp.where(_lower_tri(tb), p, 0.0)
            dp = lax.dot_general(dob, v_ref[rows, :], (NT, ((), ())), preferred_element_type=F32)
            ds = (p * (dp - dsum) * _MLA_SCALE).astype(BF16)
            return acc + lax.dot_general(ds, k, (NN, ((), ())), preferred_element_type=F32)

        acc = lax.fori_loop(0, qi, lambda kb, a: block(kb, a, False), jnp.zeros((tb, MLA_SLOT), F32))
        dq_ref[...] = block(qi, acc, True)

    qs = lambda w: _bs((tb, w), lambda h, qi: (qi, h))
    col = _bs((None, tb, 1), lambda h, qi: (h, qi, 0))
    return _pcall(
        body, name=name, grid=(MLA_HEADS, nb),
        in_specs=[qs(MLA_SLOT), _bs((T, MLA_SLOT), lambda h, qi: (0, h)), _bs((T, MLA_V), lambda h, qi: (0, h)),
                  qs(MLA_V), qs(MLA_V), col],
        out_specs=[qs(MLA_SLOT), col],
        out_shape=[jax.ShapeDtypeStruct((T, MLA_HEADS * MLA_SLOT), F32), jax.ShapeDtypeStruct((MLA_HEADS, T, 1), F32)],
        compiler_params=pltpu.CompilerParams(dimension_semantics=("parallel", "arbitrary")),
    )(qcat, kcat, v, o, do, lse)


def _mla_bwd_kv(name, qcat, kcat, v, lse, dsum, do):
    T = qcat.shape[0]
    tb = _tile(T, MLA_BLOCK)
    nb = T // tb

    def body(q_ref, k_ref, v_ref, do_ref, l_ref, ds_ref, dk_ref, dv_ref):
        ki = pl.program_id(1)
        k = k_ref[...]
        vv = v_ref[...]

        def block(qb, carry, diagonal):
            dk, dv = carry
            rows = pl.ds(pl.multiple_of(qb * tb, tb), tb)
            q = q_ref[rows, :]
            dob = do_ref[rows, :].astype(BF16)
            s = lax.dot_general(q, k, (NT, ((), ())), preferred_element_type=F32) * _MLA_SCALE
            p = jnp.exp(s - l_ref[rows, :])
            if diagonal:
                p = jnp.where(_lower_tri(tb), p, 0.0)
            dp = lax.dot_general(dob, vv, (NT, ((), ())), preferred_element_type=F32)
            ds = (p * (dp - ds_ref[rows, :]) * _MLA_SCALE).astype(BF16)
            dv = dv + lax.dot_general(p.astype(BF16), dob, (TN, ((), ())), preferred_element_type=F32)
            dk = dk + lax.dot_general(ds, q, (TN, ((), ())), preferred_element_type=F32)
            return dk, dv

        carry = block(ki, (jnp.zeros((tb, MLA_SLOT), F32), jnp.zeros((tb, MLA_V), F32)), True)
        dk, dv = lax.fori_loop(ki + 1, nb, lambda qb, c: block(qb, c, False), carry)
        dk_ref[...] = dk
        dv_ref[...] = dv

    ks = lambda w: _bs((tb, w), lambda h, ki: (ki, h))
    col = _bs((None, T, 1), lambda h, ki: (h, 0, 0))
    return _pcall(
        body, name=name, grid=(MLA_HEADS, nb),
        in_specs=[_bs((T, MLA_SLOT), lambda h, ki: (0, h)), ks(MLA_SLOT), ks(MLA_V),
                  _bs((T, MLA_V), lambda h, ki: (0, h)), col, col],
        out_specs=[ks(MLA_SLOT), ks(MLA_V)],
        out_shape=[jax.ShapeDtypeStruct((T, MLA_HEADS * MLA_SLOT), F32), jax.ShapeDtypeStruct((T, MLA_WIDTH), F32)],
        compiler_params=pltpu.CompilerParams(dimension_semantics=("parallel", "arbitrary")),
    )(qcat, kcat, v, do, lse, dsum)


def _kcat_fwd(name, kpre, krs, c256, s256):
    def fn(t, b):
        kr = t[1] * t[2] + _half_swap(t[1]) * t[3]
        return (t[0] + jnp.tile(kr, (1, MLA_HEADS)),), ()
    return _rowwise(name, fn, [kpre, krs, c256, s256], [], [(kpre.shape[1], BF16)])[0]


def _kcat_bwd(name, dkcat, c256, s256):
    def fn(t, b):
        d = t[0][:, 0:MLA_SLOT]
        for h in range(1, MLA_HEADS):
            d = d + t[0][:, h * MLA_SLOT:(h + 1) * MLA_SLOT]
        return (d * t[1] + _half_swap(d * t[2]),), ()
    return _rowwise(name, fn, [dkcat, c256, s256], [], [(MLA_SLOT, F32)])[0]


def _in_widths(ws):
    return [ws, SWA_WIDTH, SWA_KV_WIDTH, SWA_KV_WIDTH, MLA_Q_RANK, MLA_KV_RANK, MLA_SLOT]


def _kr_offset(ws):
    return ws + SWA_WIDTH + 2 * SWA_KV_WIDTH + MLA_Q_RANK + MLA_KV_RANK


def _pad_w_in(w, ws):
    z = lambda n: jnp.zeros((w.shape[0], n), w.dtype)
    o = _kr_offset(ws)
    return jnp.concatenate([w[:, :o], z(MLA_NOPE), w[:, o:], z(MLA_SLOT - MLA_NOPE - MLA_ROPE)], axis=1)


def _unpad_w_in(dw, ws):
    o = _kr_offset(ws)
    return jnp.concatenate([dw[:, :o], dw[:, o + MLA_NOPE:o + MLA_NOPE + MLA_ROPE]], axis=1)


def _pad_w_uq(w):
    r = w.shape[0]
    w3 = w.reshape(r, MLA_HEADS, MLA_NOPE + MLA_ROPE)
    return jnp.pad(w3, ((0, 0), (0, 0), (0, MLA_SLOT - MLA_NOPE - MLA_ROPE))).reshape(r, MLA_HEADS * MLA_SLOT)


def _unpad_w_uq(dw):
    r = dw.shape[0]
    return dw.reshape(r, MLA_HEADS, MLA_SLOT)[..., :MLA_NOPE + MLA_ROPE].reshape(r, -1)


def _pad_w_ukv(w):
    r = w.shape[0]
    w3 = w.reshape(r, MLA_HEADS, MLA_NOPE + MLA_V)
    wk = jnp.pad(w3[..., :MLA_NOPE], ((0, 0), (0, 0), (0, MLA_SLOT - MLA_NOPE))).reshape(r, MLA_HEADS * MLA_SLOT)
    wv = w3[..., MLA_NOPE:].reshape(r, MLA_WIDTH)
    return wk, wv


def _unpad_w_ukv(dwk, dwv):
    r = dwk.shape[0]
    return jnp.concatenate([dwk.reshape(r, MLA_HEADS, MLA_SLOT)[..., :MLA_NOPE], dwv.reshape(r, MLA_HEADS, MLA_V)],
                           axis=-1).reshape(r, -1)


def _layer_prep(lw):
    ws = lw['ssm_d'].shape[0]
    p = dict(lw)
    w_in_pad = _pad_w_in(lw['w_in'], ws)
    p['w_in_pad'] = w_in_pad
    offs = np.cumsum([0] + _in_widths(ws))
    p['w_in_parts'] = [w_in_pad[:, offs[i]:offs[i + 1]] for i in range(7)]
    p['s5_prep'] = _s5_prep(lw['ssm_log_dt'], lw['ssm_a_re'], lw['ssm_a_im'], lw['ssm_b_re'], lw['ssm_b_im'],
                            lw['ssm_c_re'], lw['ssm_c_im'])
    p['sinks_b'] = jnp.broadcast_to(lw['swa_sinks'][:, None], (SWA_HEADS, 128))
    p['w_uq_pad'] = _pad_w_uq(lw['mla_w_uq'])
    p['w_k_pad'], p['w_v'] = _pad_w_ukv(lw['mla_w_ukv'])
    p['w_ukv_pad'] = jnp.concatenate([p['w_k_pad'], p['w_v']], axis=1)
    b = [0, ws, ws + SWA_WIDTH, ws + SWA_WIDTH + MLA_WIDTH]
    p['w_out_g'] = [lw['w_out'][b[g]:b[g + 1]] for g in range(3)]
    p['out_norm_g'] = [lw['out_norm'][b[g]:b[g + 1]] for g in range(3)]
    return p


def _mixer_fwd(tag, x, p, tabs):
    T, D = x.shape
    c64, s64, c256, s256 = tabs
    ws = p['ssm_d'].shape[0]
    widths = _in_widths(ws)
    n_in = sum(widths)
    offs = [int(o) for o in np.cumsum([0] + widths[:-1])]
    h, rstd = _rms_fwd(tag + "_rms", x, p['mix_norm'])
    tm = _tile(T, 256)
    row_i = lambda w: _bs((tm, w), lambda i, j, k: (i, 0))
    parts = _mm(tag + "_in", (T // tm, 1, 1),
                [(h, row_i(D), p['w_in_pad'], _bs((D, n_in), lambda i, j, k: (0, 0)), NN, 0)],
                [(tm, n_in)], [((T, w), F32, row_i(w)) for w in widths],
                lambda accs, ex: tuple(accs[0][:, o:o + w] for o, w in zip(offs, widths)))
    u, q, k, v, cq, ckv, krs = parts
    y_ssm, s5_saved = _s5_fwd(tag + "_s5", u, p['s5_prep'], p['ssm_d'], p['ssm_w_glu'], p['ssm_b_glu'])
    q_r = _rope(tag + "_ropeq", q, c64, s64)
    k_r = _rope(tag + "_ropek", k, c64, s64)
    y_swa, lse_swa = _swa_fwd(tag + "_swa", q_r, k_r, v, p['sinks_b'])
    cqn, r_q = _rms_fwd(tag + "_rmsq", cq, p['mla_q_norm'])
    ckvn, r_kv = _rms_fwd(tag + "_rmskv", ckv, p['mla_kv_norm'])
    qpre = _mm_simple(tag + "_uq", cqn, p['w_uq_pad'], NN, tm=512, tn=1024, tk=MLA_Q_RANK)
    nk_, nv_ = MLA_HEADS * MLA_SLOT, MLA_WIDTH
    kpre, vm = _mm(tag + "_ukv", (T // tm, 1, 1),
                   [(ckvn, row_i(MLA_KV_RANK), p['w_ukv_pad'], _bs((MLA_KV_RANK, nk_ + nv_), lambda i, j, k: (0, 0)),
                     NN, 0)],
                   [(tm, nk_ + nv_)], [((T, nk_), F32, row_i(nk_)), ((T, nv_), BF16, row_i(nv_))],
                   lambda accs, ex: (accs[0][:, :nk_], accs[0][:, nk_:]))
    qcat = _rope(tag + "_ropemq", qpre, c256, s256, out_dtype=BF16)
    kcat = _kcat_fwd(tag + "_kcat", kpre, krs, c256, s256)
    y_mla, lse_mla = _mla_fwd(tag + "_mla", qcat, kcat, vm)
    ys = [y_ssm, y_swa, y_mla]
    yn, rs = [], []
    for g in range(3):
        n_, r_ = _rms_fwd(f"{tag}_rmso{g}", ys[g], p['out_norm_g'][g])
        yn.append(n_)
        rs.append(r_)
    tm3, tn3 = _tile(T, 512), _tile(D, 1024)
    pairs = []
    for g in range(3):
        wg = ys[g].shape[1]
        pairs.append((yn[g], _bs((tm3, wg), lambda i, j, k: (i, 0)), p['w_out_g'][g],
                      _bs((wg, tn3), lambda i, j, k: (0, j)), NN, 0))
    o_spec = _bs((tm3, tn3), lambda i, j, k: (i, j))
    x2, = _mm(tag + "_out", (T // tm3, D // tn3, 1), pairs, [(tm3, tn3)], [((T, D), F32, o_spec)],
              lambda accs, ex: (ex[0] + accs[0],), [(x, o_spec)])
    saved = (x, h, rstd, q_r, k_r, v, cq, ckv, s5_saved, y_swa, lse_swa, cqn, r_q, ckvn, r_kv, qcat, kcat, vm,
             y_mla, lse_mla, ys, yn, rs)
    return x2, saved


def _mixer_bwd(tag, dx2, dx2_16, saved, p, tabs):
    (x, h, rstd, q_r, k_r, v, cq, ckv, s5_saved, y_swa, lse_swa, cqn, r_q, ckvn, r_kv, qcat, kcat, vm,
     y_mla, lse_mla, ys, yn, rs) = saved
    T, D = x.shape
    c64, s64, c256, s256 = tabs
    ws = p['ssm_d'].shape[0]
    g_ = {}
    dys, dwo, don = [], [], []
    for g in range(3):
        wg = ys[g].shape[1]
        dyn = _mm_simple(f"{tag}_dyn{g}", dx2_16, p['w_out_g'][g], NT, tm=512, tn=wg, tk=1024)
        dwo.append(_mm_simple(f"{tag}_dwo{g}", yn[g], dx2_16, TN, tm=512, tn=1024, tk=512, out_dtype=BF16))
        dy_g, don_g = _rms_bwd(f"{tag}_rmsob{g}", dyn, ys[g], rs[g], p['out_norm_g'][g])
        dys.append(dy_g)
        don.append(don_g)
    g_['w_out'] = jnp.concatenate(dwo, axis=0)
    g_['out_norm'] = jnp.concatenate(don, axis=1)[0]
    dqcat, dsum = _mla_bwd_q(tag + "_mlabq", qcat, kcat, vm, y_mla, lse_mla, dys[2])
    dkcat, dvm = _mla_bwd_kv(tag + "_mlabkv", qcat, kcat, vm, lse_mla, dsum, dys[2])
    dqpre = _rope(tag + "_ropemqb", dqcat, c256, s256, backward=True)
    dkrs = _kcat_bwd(tag + "_kcatb", dkcat, c256, s256)
    g_['mla_w_uq'] = _unpad_w_uq(_mm_simple(tag + "_dwuq", cqn, dqpre, TN, tm=MLA_Q_RANK, tn=1024, tk=512,
                                            out_dtype=BF16))
    dcqn = _mm_simple(tag + "_dcqn", dqpre, p['w_uq_pad'], NT, tm=512, tn=MLA_Q_RANK, tk=1024)
    dcq, dqn = _rms_bwd(tag + "_rmsqb", dcqn, cq, r_q, p['mla_q_norm'])
    g_['mla_q_norm'] = dqn[0]
    dwk = _mm_simple(tag + "_dwk", ckvn, dkcat, TN, tm=MLA_KV_RANK, tn=1024, tk=512, out_dtype=BF16)
    dwv = _mm_simple(tag + "_dwv", ckvn, dvm, TN, tm=MLA_KV_RANK, tn=1024, tk=512, out_dtype=BF16)
    g_['mla_w_ukv'] = _unpad_w_ukv(dwk, dwv)
    tm = _tile(T, 512)
    nk_, nv_ = MLA_HEADS * MLA_SLOT, MLA_WIDTH
    tkk = _tile(nk_, 1024)
    dckvn_k = _mm_simple(tag + "_dckvk", dkcat, p['w_k_pad'], NT, tm=512, tn=MLA_KV_RANK, tk=tkk)
    dckvn = _mm_simple(tag + "_dckvv", dvm, p['w_v'], NT, tm=512, tn=MLA_KV_RANK, tk=nv_,
                       epilogue=lambda accs, ex: (accs[0] + ex[0],), extras=[dckvn_k])
    dckv, dkvn = _rms_bwd(tag + "_rmskvb", dckvn, ckv, r_kv, p['mla_kv_norm'])
    g_['mla_kv_norm'] = dkvn[0]
    dq_r, dkc, dkp, dvc, dvp, dsinks = _swa_bwd(tag + "_swab", q_r, k_r, v, p['sinks_b'], y_swa, lse_swa, dys[1])
    g_['swa_sinks'] = dsinks[0, :SWA_HEADS]
    dk_r = _shift_add(tag + "_dksum", dkc, dkp)
    dv = _shift_add(tag + "_dvsum", dvc, dvp)
    dq = _rope(tag + "_ropeqb", dq_r, c64, s64, backward=True)
    dk = _rope(tag + "_ropekb", dk_r, c64, s64, backward=True)
    du, s5g, dd, dw_glu, db_glu = _s5_bwd(tag + "_s5b", dys[0], s5_saved, p['s5_prep'], p['ssm_d'], p['ssm_w_glu'])
    g_['ssm_d'], g_['ssm_w_glu'], g_['ssm_b_glu'] = dd[0], dw_glu, db_glu[0]
    _, pull = jax.vjp(_s5_prep, p['ssm_log_dt'], p['ssm_a_re'], p['ssm_a_im'], p['ssm_b_re'], p['ssm_b_im'],
                      p['ssm_c_re'], p['ssm_c_im'])
    for name, val in zip(['ssm_log_dt', 'ssm_a_re', 'ssm_a_im', 'ssm_b_re', 'ssm_b_im', 'ssm_c_re', 'ssm_c_im'],
                         pull(tuple(s5g))):
        g_[name] = val
    dparts = [du, dq, dk, dv, dcq, dckv, dkrs]
    widths = _in_widths(ws)
    tn = _tile(D, 1024)
    pairs = []
    for dp_, wp_, w in zip(dparts, p['w_in_parts'], widths):
        pairs.append((dp_, _bs((tm, w), lambda i, j, k: (i, 0)), wp_, _bs((tn, w), lambda i, j, k: (j, 0)), NT, 0))
    o_spec = _bs((tm, tn), lambda i, j, k: (i, j))
    dh, = _mm(tag + "_dh", (T // tm, D // tn, 1), pairs, [(tm, tn)], [((T, D), F32, o_spec)],
              lambda accs, ex: (accs[0],))
    tmw, tk = _tile(D, 512), _tile(T, 512)
    pairs = []
    for i_, (dp_, w) in enumerate(zip(dparts, widths)):
        pairs.append((h, _bs((tk, tmw), lambda i, j, k: (k, i)), dp_, _bs((tk, w), lambda i, j, k: (k, 0)), TN, i_))
    dws = _mm(tag + "_dwin", (D // tmw, 1, T // tk), pairs, [(tmw, w) for w in widths],
              [((D, w), BF16, _bs((tmw, w), lambda i, j, k: (i, 0))) for w in widths], lambda accs, ex: tuple(accs))
    g_['w_in'] = _unpad_w_in(jnp.concatenate(dws, axis=1), ws)
    dx, dx16, dmix = _rms_bwd(tag + "_rmsb", dh, x, rstd, p['mix_norm'], dres=dx2, with_bf16=True)
    g_['mix_norm'] = dmix[0]
    return dx, dx16, g_


def _final_loss(name, x, target, gain):
    D = x.shape[1]

    def fn(t, b):
        xv = t[0]
        r = lax.rsqrt(jnp.mean(xv * xv, axis=-1, keepdims=True) + EPS)
        xh = xv * r
        err = xh * b[0] - t[1]
        part = 0.5 * jnp.sum(jnp.sum(err * err, axis=-1, keepdims=True), axis=0, keepdims=True) / D
        dy = err / D
        dxh = dy * b[0]
        dx = r * (dxh - xh * jnp.mean(dxh * xh, axis=-1, keepdims=True))
        return (dx, dx), (jnp.broadcast_to(part, (1, 128)), _colsum(dy * xh))
    dx, dx16, part, dg = _rowwise(name, fn, [x, target], [gain.reshape(1, D)], [(D, F32), (D, BF16)], [128, D])
    return part[0, 0], dx, dx16, dg[0]


def _device_step(x, positions, target, n_layers, get_block, final_norm, on_grads):
    tabs = _rope_tables(positions)
    saved = []
    for l in range(n_layers):
        w1 = get_block(l, 'ffn1', x)
        x, s1 = _ffn_fwd("ffn1", x, w1['ffn1_norm'], w1['ffn1_wg2'], w1['ffn1_wu2'], w1['ffn1_wd'])
        p = _layer_prep(get_block(l, 'mix', x))
        x, s2 = _mixer_fwd("mix", x, p, tabs)
        w3 = get_block(l, 'ffn2', x)
        x, s3 = _ffn_fwd("ffn2", x, w3['ffn2_norm'], w3['ffn2_wg2'], w3['ffn2_wu2'], w3['ffn2_wd'])
        saved.append((w1, s1, p, s2, w3, s3))
    loss_part, dx, dx16, dfinal = _final_loss("loss", x, target, final_norm)
    for l in range(n_layers - 1, -1, -1):
        w1, s1, p, s2, w3, s3 = saved[l]
        dx, dx16, gn, gwg, gwu, gwd = _ffn_bwd("ffn2b", dx, dx16, s3, w3['ffn2_norm'], w3['ffn2_wg2'], w3['ffn2_wu2'],
                                               w3['ffn2_wd'])
        dx, dx16 = on_grads(l, 'ffn2', {'ffn2_norm': gn[0], 'ffn2_wg2': gwg, 'ffn2_wu2': gwu, 'ffn2_wd': gwd},
                            (dx, dx16))
        dx, dx16, gm = _mixer_bwd("mixb", dx, dx16, s2, p, tabs)
        dx, dx16 = on_grads(l, 'mix', gm, (dx, dx16))
        dx, dx16, gn, gwg, gwu, gwd = _ffn_bwd("ffn1b", dx, dx16, s1, w1['ffn1_norm'], w1['ffn1_wg2'], w1['ffn1_wu2'],
                                               w1['ffn1_wd'])
        dx, dx16 = on_grads(l, 'ffn1', {'ffn1_norm': gn[0], 'ffn1_wg2': gwg, 'ffn1_wu2': gwu, 'ffn1_wd': gwd},
                            (dx, dx16))
    return loss_part, dx, dfinal


_ANY = pl.BlockSpec(memory_space=pl.ANY)
_CHIP_MASKS = (2, 1, 3)


def _place():
    x, y, c = lax.axis_index("x"), lax.axis_index("y"), lax.axis_index("c")
    chips = [(1 - x, y), (x, 1 - y), (1 - x, 1 - y)]
    return x, y, c, 2 * x + y, chips


_HBM = pl.BlockSpec(memory_space=pltpu.HBM)
_SEMS = pl.BlockSpec(memory_space=pltpu.SEMAPHORE)
_EFFECT = pltpu.SideEffectType.DATAFLOW_SIDE_EFFECTING


def _split_start(name, bufs, sem_sizes, copies):
    n, ns = len(bufs), len(sem_sizes)

    def body(*refs):
        for cp in copies(refs[:n], refs[n:n + ns])[0]:
            cp.start()

    outs = _pcall(
        body, name=name, in_specs=[_HBM] * n,
        out_shape=(*[pltpu.SemaphoreType.DMA((k,)) for k in sem_sizes], *[pltpu.HBM(b.shape, b.dtype) for b in bufs]),
        out_specs=(*[_SEMS] * ns, *[_HBM] * n), input_output_aliases={i: i + ns for i in range(n)},
        compiler_params=pltpu.CompilerParams(has_side_effects=_EFFECT),
    )(*[pltpu.with_memory_space_constraint(b, pltpu.HBM) for b in bufs])
    return list(outs[:ns]), list(outs[ns:])


def _split_wait(name, bufs, sems, after, copies):
    n, ns = len(bufs), len(sems)

    def body(*refs):
        _, sent, landed = copies(refs[:n], refs[n:n + ns])
        for cp in sent:
            cp.wait_send()
        for cp in landed:
            cp.wait_recv()

    return list(_pcall(
        body, name=name, in_specs=[_HBM] * n + [_SEMS] * ns + [_ANY],
        out_shape=tuple(pltpu.HBM(b.shape, b.dtype) for b in bufs), out_specs=tuple([_HBM] * n),
        input_output_aliases={i: i for i in range(n)},
        compiler_params=pltpu.CompilerParams(has_side_effects=_EFFECT),
    )(*bufs, *sems, after))


def _gather_group(bufs, send, recv):
    x, y, c, s, chips = _place()
    start, landed = [], []
    for a, buf in enumerate(bufs):
        for j in range(3):
            k = a * 3 + j
            to = (*chips[j], c)
            start.append(pltpu.make_async_remote_copy(
                src_ref=buf.at[s], dst_ref=buf.at[s], send_sem=send.at[k], recv_sem=recv.at[k],
                device_id=to, device_id_type=MESH))
            theirs = buf.at[s ^ _CHIP_MASKS[j]]
            landed.append(pltpu.make_async_remote_copy(
                src_ref=theirs, dst_ref=theirs, send_sem=send.at[k], recv_sem=recv.at[k],
                device_id=to, device_id_type=MESH))
    return start, start, landed


def _gather_start_copies(n_pass, sizes):
    def copies(refs, sems):
        start, o = [], n_pass
        for g, k in enumerate(sizes):
            start += _gather_group(refs[o:o + k], sems[2 * g], sems[2 * g + 1])[0]
            o += k
        return start, start, []
    return copies


def _gather_wait_copies(refs, sems):
    return _gather_group(refs, sems[0], sems[1])


_N_SLOTS = 7


def _reduce_copies(n_pass, n):
    def copies(refs, sems):
        send, recv = sems
        x, y, c, s, chips = _place()
        contrib, lands = refs[n_pass:n_pass + n], refs[n_pass + n:]
        start, landed = [], []
        for a in range(n):
            half = contrib[a].shape[1] // 2
            base = a * _N_SLOTS
            for j in range(3):
                for h in range(2):
                    start.append(pltpu.make_async_remote_copy(
                        src_ref=contrib[a].at[s ^ _CHIP_MASKS[j], pl.ds(h * half, half)], dst_ref=lands[a].at[2 * j + c],
                        send_sem=send.at[base + 2 * j + h], recv_sem=recv.at[base + 2 * j + c],
                        device_id=(*chips[j], h), device_id_type=MESH))
            start.append(pltpu.make_async_remote_copy(
                src_ref=contrib[a].at[s, pl.ds((1 - c) * half, half)], dst_ref=lands[a].at[6],
                send_sem=send.at[base + 6], recv_sem=recv.at[base + 6],
                device_id=(x, y, 1 - c), device_id_type=MESH))
            for slot in range(_N_SLOTS):
                landed.append(pltpu.make_async_remote_copy(
                    src_ref=lands[a].at[slot], dst_ref=lands[a].at[slot], send_sem=send.at[base + slot],
                    recv_sem=recv.at[base + slot], device_id=(x, y, c), device_id_type=MESH))
        return start, start, landed
    return copies


def _sum_partials(name, contrib, lands):
    ns, half, w = lands.shape
    tm = _tile(half, 256)
    nb = half // tm
    mine = _half_rows(nb)
    shard = lambda: 2 * lax.axis_index("x") + lax.axis_index("y")

    def body(g_ref, *refs):
        o_ref = refs[-1]
        acc = g_ref[...].astype(F32) + refs[6][...].astype(F32)
        for slot in range(6):
            acc = acc + refs[slot][...].astype(F32)
        o_ref[...] = acc

    blk = lambda j: _bs((None, tm, w), lambda i: (j, i, 0))
    return _pcall(
        body, name=name, grid=(nb,),
        in_specs=[_bs((None, tm, w), lambda i: (shard(), mine(i), 0))] + [blk(j) for j in range(_N_SLOTS)],
        out_specs=_bs((tm, w), lambda i: (mine(i), 0)), out_shape=jax.ShapeDtypeStruct((2 * half, w), F32),
        compiler_params=pltpu.CompilerParams(dimension_semantics=("parallel",)),
    )(contrib, *[lands] * _N_SLOTS)


def _stage_shard(name, wl):
    r, c = wl.shape
    tm = _tile(r, 256)
    shard = lambda: 2 * lax.axis_index("x") + lax.axis_index("y")

    def body(w_ref, o_ref):
        o_ref[...] = w_ref[...].astype(o_ref.dtype)

    return _pcall(
        body, name=name, grid=(r // tm,), in_specs=[_bs((tm, c), lambda i: (i, 0))],
        out_specs=_bs((None, tm, c), lambda i: (shard(), i, 0)),
        out_shape=jax.ShapeDtypeStruct((N_SHARD, r, c), BF16),
        compiler_params=pltpu.CompilerParams(dimension_semantics=("parallel",)),
    )(wl)


def _half_rows(nb):
    return lambda i: lax.axis_index("c") * nb + i


def _pair_join_call(name, halves):
    n = len(halves)

    def body(*refs):
        outs = refs[n:2 * n]
        send, recv = refs[2 * n:]
        x, y, c, s, chips = _place()
        cps = []
        for a in range(n):
            half = outs[a].shape[0] // 2
            mine = outs[a].at[pl.ds(c * half, half)]
            cp = pltpu.make_async_remote_copy(
                src_ref=mine, dst_ref=mine, send_sem=send.at[a], recv_sem=recv.at[a],
                device_id=(x, y, 1 - c), device_id_type=MESH)
            cp.start()
            cps.append(cp)
        for a in range(n):
            half = outs[a].shape[0] // 2
            theirs = outs[a].at[pl.ds((1 - c) * half, half)]
            pltpu.make_async_remote_copy(
                src_ref=theirs, dst_ref=theirs, send_sem=send.at[a], recv_sem=recv.at[a],
                device_id=(x, y, 1 - c), device_id_type=MESH).wait_recv()
        for cp in cps:
            cp.wait_send()

    return _pcall(
        body, name=name, in_specs=[_ANY] * n, out_specs=[_ANY] * n,
        out_shape=[jax.ShapeDtypeStruct(h.shape, h.dtype) for h in halves],
        input_output_aliases={a: a for a in range(n)},
        scratch_shapes=[pltpu.SemaphoreType.DMA((n,))] * 2,
    )(*halves)


def _all_sum_small(tag, buf):
    n_dev = 8

    def body(in_ref, out_ref, send, recv, loc):
        x, y, c, s, chips = _place()
        me = 4 * x + 2 * y + c
        lc = pltpu.make_async_copy(in_ref, out_ref.at[me], loc)
        lc.start()
        cps = []
        for k in range(1, n_dev):
            to = (x ^ (k >> 2), y ^ ((k >> 1) & 1), c ^ (k & 1))
            cp = pltpu.make_async_remote_copy(
                src_ref=in_ref, dst_ref=out_ref.at[me], send_sem=send.at[k - 1], recv_sem=recv.at[k - 1],
                device_id=to, device_id_type=MESH)
            cp.start()
            cps.append(cp)
        for k in range(1, n_dev):
            theirs = out_ref.at[me ^ k]
            pltpu.make_async_remote_copy(
                src_ref=theirs, dst_ref=theirs, send_sem=send.at[k - 1], recv_sem=recv.at[k - 1],
                device_id=(x, y, c), device_id_type=MESH).wait_recv()
        for cp in cps:
            cp.wait_send()
        lc.wait()

    allb = _pcall(
        body, name=tag + "_gather", in_specs=[_ANY], out_specs=_ANY,
        out_shape=jax.ShapeDtypeStruct((n_dev,) + buf.shape, buf.dtype),
        scratch_shapes=[pltpu.SemaphoreType.DMA((n_dev - 1,))] * 2 + [pltpu.SemaphoreType.DMA(())],
    )(buf)
    rows = buf.shape[0]
    tm = _tile(rows, 512)

    def sum_body(a_ref, o_ref):
        acc = a_ref[0]
        for k in range(1, n_dev):
            acc = acc + a_ref[k]
        o_ref[...] = acc

    return _pcall(
        sum_body, name=tag + "_sum", grid=(rows // tm,),
        in_specs=[_bs((n_dev, tm, 128), lambda i: (0, i, 0))], out_specs=_bs((tm, 128), lambda i: (i, 0)),
        out_shape=jax.ShapeDtypeStruct(buf.shape, F32),
        compiler_params=pltpu.CompilerParams(dimension_semantics=("parallel",)),
    )(allb)


def _adamw(name, w, g, m, v):
    def fn(t, b):
        wv, gv, mv, vv = t
        m2 = ADAM_B1 * mv + (1.0 - ADAM_B1) * gv
        v2 = ADAM_B2 * vv + (1.0 - ADAM_B2) * (gv * gv)
        m_hat = m2 / (1.0 - ADAM_B1 ** ADAM_STEP)
        v_hat = v2 / (1.0 - ADAM_B2 ** ADAM_STEP)
        delta = -ADAM_LR * (m_hat / (jnp.sqrt(v_hat) + ADAM_EPS) + ADAM_WD * wv)
        return (delta, m2, v2), ()
    wd = w.shape[1]
    return _rowwise(name, fn, [w, g, m, v], [], [(wd, F32)] * 3, tm=256 if wd > 1024 else 512)


_WEIGHTS = ['ffn1_norm', 'ffn1_w_gate', 'ffn1_w_up', 'ffn1_w_down', 'mix_norm', 'w_in', 'ssm_log_dt', 'ssm_a_re',
            'ssm_a_im', 'ssm_b_re', 'ssm_b_im', 'ssm_c_re', 'ssm_c_im', 'ssm_d', 'ssm_w_glu', 'ssm_b_glu',
            'swa_sinks', 'mla_q_norm', 'mla_w_uq', 'mla_kv_norm', 'mla_w_ukv', 'out_norm', 'w_out', 'ffn2_norm',
            'ffn2_w_gate', 'ffn2_w_up', 'ffn2_w_down', 'final_norm']
_COL_SHARDED = ['ffn1_w_gate', 'ffn1_w_up', 'w_in', 'mla_w_uq', 'mla_w_ukv', 'ffn2_w_gate', 'ffn2_w_up']
_ROW_SHARDED = ['ffn1_w_down', 'ssm_w_glu', 'w_out', 'ffn2_w_down']
_STACKED = {'ffn1_w_gate': 'ffn1_wg2', 'ffn1_w_up': 'ffn1_wu2', 'ffn2_w_gate': 'ffn2_wg2', 'ffn2_w_up': 'ffn2_wu2'}
_RENAMED = {'ffn1_w_down': 'ffn1_wd', 'ffn2_w_down': 'ffn2_wd'}
_BLOCKS = ('ffn1', 'mix', 'ffn2')
_BLOCK_BIG = {'ffn1': ['ffn1_w_gate', 'ffn1_w_up', 'ffn1_w_down'],
              'mix': ['w_in', 'ssm_w_glu', 'mla_w_uq', 'mla_w_ukv', 'w_out'],
              'ffn2': ['ffn2_w_gate', 'ffn2_w_up', 'ffn2_w_down']}
_BLOCK_SMALL = {'ffn1': ['ffn1_norm'],
                'mix': ['mix_norm', 'ssm_log_dt', 'ssm_a_re', 'ssm_a_im', 'ssm_b_re', 'ssm_b_im', 'ssm_c_re', 'ssm_c_im',
                        'ssm_d', 'ssm_b_glu', 'swa_sinks', 'mla_q_norm', 'mla_kv_norm', 'out_norm'],
                'ffn2': ['ffn2_norm']}
_BIG = [n for b in _BLOCKS for n in _BLOCK_BIG[b]]
_SMALL = [n for n in _WEIGHTS if n not in _BIG]


def _pack(vals):
    flat = jnp.concatenate([v.reshape(-1) for v in vals])
    pad = (-flat.shape[0]) % 1024
    return jnp.pad(flat, (0, pad)).reshape(-1, 128)


def _unpack(buf, like):
    flat = buf.reshape(-1)
    out, o = [], 0
    for v in like:
        out.append(flat[o:o + v.size].reshape(v.shape))
        o += v.size
    return out


def kernel(x, positions, ffn1_norm, ffn1_w_gate, ffn1_w_up, ffn1_w_down, mix_norm, w_in, ssm_log_dt, ssm_a_re, ssm_a_im, ssm_b_re, ssm_b_im, ssm_c_re, ssm_c_im, ssm_d, ssm_w_glu, ssm_b_glu, swa_sinks, mla_q_norm, mla_w_uq, mla_kv_norm, mla_w_ukv, out_norm, w_out, ffn2_norm, ffn2_w_gate, ffn2_w_up, ffn2_w_down, final_norm, loss_target, m_ffn1_norm, m_ffn1_w_gate, m_ffn1_w_up, m_ffn1_w_down, m_mix_norm, m_w_in, m_ssm_log_dt, m_ssm_a_re, m_ssm_a_im, m_ssm_b_re, m_ssm_b_im, m_ssm_c_re, m_ssm_c_im, m_ssm_d, m_ssm_w_glu, m_ssm_b_glu, m_swa_sinks, m_mla_q_norm, m_mla_w_uq, m_mla_kv_norm, m_mla_w_ukv, m_out_norm, m_w_out, m_ffn2_norm, m_ffn2_w_gate, m_ffn2_w_up, m_ffn2_w_down, m_final_norm, v_ffn1_norm, v_ffn1_w_gate, v_ffn1_w_up, v_ffn1_w_down, v_mix_norm, v_w_in, v_ssm_log_dt, v_ssm_a_re, v_ssm_a_im, v_ssm_b_re, v_ssm_b_im, v_ssm_c_re, v_ssm_c_im, v_ssm_d, v_ssm_w_glu, v_ssm_b_glu, v_swa_sinks, v_mla_q_norm, v_mla_w_uq, v_mla_kv_norm, v_mla_w_ukv, v_out_norm, v_w_out, v_ffn2_norm, v_ffn2_w_gate, v_ffn2_w_up, v_ffn2_w_down, v_final_norm):
    w = dict(zip(_WEIGHTS, (ffn1_norm, ffn1_w_gate, ffn1_w_up, ffn1_w_down, mix_norm, w_in, ssm_log_dt, ssm_a_re, ssm_a_im, ssm_b_re, ssm_b_im, ssm_c_re, ssm_c_im, ssm_d, ssm_w_glu, ssm_b_glu, swa_sinks, mla_q_norm, mla_w_uq, mla_kv_norm, mla_w_ukv, out_norm, w_out, ffn2_norm, ffn2_w_gate, ffn2_w_up, ffn2_w_down, final_norm)))
    m = dict(zip(_WEIGHTS, (m_ffn1_norm, m_ffn1_w_gate, m_ffn1_w_up, m_ffn1_w_down, m_mix_norm, m_w_in, m_ssm_log_dt, m_ssm_a_re, m_ssm_a_im, m_ssm_b_re, m_ssm_b_im, m_ssm_c_re, m_ssm_c_im, m_ssm_d, m_ssm_w_glu, m_ssm_b_glu, m_swa_sinks, m_mla_q_norm, m_mla_w_uq, m_mla_kv_norm, m_mla_w_ukv, m_out_norm, m_w_out, m_ffn2_norm, m_ffn2_w_gate, m_ffn2_w_up, m_ffn2_w_down, m_final_norm)))
    v = dict(zip(_WEIGHTS, (v_ffn1_norm, v_ffn1_w_gate, v_ffn1_w_up, v_ffn1_w_down, v_mix_norm, v_w_in, v_ssm_log_dt, v_ssm_a_re, v_ssm_a_im, v_ssm_b_re, v_ssm_b_im, v_ssm_c_re, v_ssm_c_im, v_ssm_d, v_ssm_w_glu, v_ssm_b_glu, v_swa_sinks, v_mla_q_norm, v_mla_w_uq, v_mla_kv_norm, v_mla_w_ukv, v_out_norm, v_w_out, v_ffn2_norm, v_ffn2_w_gate, v_ffn2_w_up, v_ffn2_w_down, v_final_norm)))
    n_layers = ffn1_norm.shape[0]

    sizes = [len(_BLOCK_BIG[b]) for b in _BLOCKS]

    x0 = x[0]
    travelling = []
    for l in range(n_layers):
        staged = [_stage_shard("stage_" + n, w[n][l]) for n in _BIG]
        sems, thru = _split_start(f"gstart{l}", [x0] + staged, [3 * k for k in sizes for _ in range(2)],
                                  _gather_start_copies(1, sizes))
        x0 = thru[0]
        travelling.append((sems, thru[1:]))

    def get_block(l, blk, x_now):
        g = _BLOCKS.index(blk)
        sems, staged = travelling[l]
        o = sum(sizes[:g])
        full = _split_wait(f"gwait{l}{blk}", staged[o:o + sizes[g]], sems[2 * g:2 * g + 2], x_now,
                           _gather_wait_copies)
        lw = {n: w[n][l] for n in _BLOCK_SMALL[blk]}
        for n, gth in zip(_BLOCK_BIG[blk], full):
            ns, r, c = gth.shape
            if n in _STACKED:
                lw[_STACKED[n]] = gth.reshape(ns * r, c)
            elif n in _ROW_SHARDED:
                lw[_RENAMED.get(n, n)] = gth.reshape(ns * r, c)
            else:
                lw[n] = jnp.moveaxis(gth, 0, 1).reshape(r, ns * c)
        return lw

    reduced = {n: [None] * n_layers for n in _BIG}
    grads = [dict() for _ in range(n_layers)]
    pending = []

    def finish(after):
        l, blk, sems, contrib, lands = pending.pop()
        k = len(contrib)
        landed = _split_wait(f"cwait{l}{blk}", contrib + lands, sems, after, _reduce_copies(0, k))[k:]
        halves = [_sum_partials("rs_sum", g, q) for g, q in zip(contrib, landed)]
        for n, g in zip(_BLOCK_BIG[blk], _pair_join_call("rs_join", halves)):
            reduced[n][l] = g

    def on_grads(l, blk, g_, dxs):
        if pending:
            finish(dxs[0])
        for n in _BLOCK_SMALL[blk]:
            grads[l][n] = g_[n]
        contrib = []
        for n in _BLOCK_BIG[blk]:
            r, c = w[n].shape[1:]
            if n in _STACKED:
                g = g_[_STACKED[n]].reshape(N_SHARD, r, c)
            elif n in _ROW_SHARDED:
                g = g_[_RENAMED.get(n, n)].reshape(N_SHARD, r, c)
            else:
                g = jnp.moveaxis(g_[n].reshape(r, N_SHARD, c), 1, 0)
            contrib.append(g)
        k = len(contrib)
        lands = [lax.empty((_N_SLOTS, g.shape[1] // 2, g.shape[2]), g.dtype) for g in contrib]
        sems, thru = _split_start(f"cstart{l}{blk}", list(dxs) + contrib + lands, [_N_SLOTS * k] * 2,
                                  _reduce_copies(2, k))
        pending.append((l, blk, sems, thru[2:2 + k], thru[2 + k:]))
        return thru[0], thru[1]

    loss_part, dx, dfinal = _device_step(x0, positions[0], loss_target[0], n_layers, get_block, final_norm, on_grads)
    finish(dx)
    loss = lax.psum(loss_part, ("x", "y", "c"))
    grad = {n: jnp.stack(reduced[n]) for n in _BIG}

    small_like = [w[n] for n in _SMALL]
    small_g = [jnp.stack([grads[l][n] for l in range(n_layers)]) for n in _SMALL if n != 'final_norm'] + [dfinal]
    g_small = _all_sum_small("small", _pack(small_g))
    d_small, m_small, v_small = _adamw("adam_small", _pack(small_like), g_small, _pack([m[n] for n in _SMALL]),
                                       _pack([v[n] for n in _SMALL]))
    delta, new_m, new_v = {}, {}, {}
    for n, gv, dv_, mv, vv in zip(_SMALL, _unpack(g_small, small_like), _unpack(d_small, small_like),
                                  _unpack(m_small, small_like), _unpack(v_small, small_like)):
        grad[n], delta[n], new_m[n], new_v[n] = gv, dv_, mv, vv
    for n in _BIG:
        shp = w[n].shape
        two = lambda t: t.reshape(shp[0] * shp[1], shp[2])
        d2, m2, v2 = _adamw("adam_" + n, two(w[n]), two(grad[n]), two(m[n]), two(v[n]))
        delta[n], new_m[n], new_v[n] = d2.reshape(shp), m2.reshape(shp), v2.reshape(shp)
    return (loss, dx[None], *[grad[n] for n in _WEIGHTS], *[delta[n] for n in _WEIGHTS],
            *[new_m[n] for n in _WEIGHTS], *[new_v[n] for n in _WEIGHTS])
```

```python
import functools
import math

import jax
import jax.numpy as jnp
import numpy as np
from jax import lax
from jax.experimental import pallas as pl
from jax.experimental.pallas import tpu as pltpu

F32 = jnp.float32
BF16 = jnp.bfloat16

EPS = 1e-6
ROPE_THETA = 10000.0
SSM_GROUP = 16
SSM_STATE = 64
S5_TILE = 8
SWA_HEADS = 8
SWA_KV_HEADS = 2
SWA_HEAD_DIM = 64
SWA_BLOCK = 128
SWA_WIDTH = SWA_HEADS * SWA_HEAD_DIM
SWA_KV_WIDTH = SWA_KV_HEADS * SWA_HEAD_DIM
MLA_HEADS = 8
MLA_Q_RANK = 512
MLA_KV_RANK = 256
MLA_NOPE = 128
MLA_ROPE = 64
MLA_V = 128
MLA_SLOT = 256
MLA_WIDTH = MLA_HEADS * MLA_V
ROPE_HALF = 32

ADAM_LR = 0.001
ADAM_B1 = 0.9
ADAM_B2 = 0.999
ADAM_EPS = 1e-08
ADAM_WD = 0.01
ADAM_STEP = 10

N_SHARD = 4
MESH = pl.DeviceIdType.MESH

NN = ((1,), (0,))
NT = ((1,), (1,))
TN = ((0,), (0,))


def _pcall(body, **kw):
    return pl.pallas_call(body, **kw)


def _tile(n, want, align=16):
    if n <= want:
        return n
    t = want - want % align
    while t >= align:
        if n % t == 0:
            return t
        t -= align
    return n


def _mm(name, grid, pairs, acc_shapes, outs, epilogue, extras=()):
    nk = grid[2]
    n_p, n_e, n_o, n_a = len(pairs), len(extras), len(outs), len(acc_shapes)
    dims_idx = [(p[4], p[5]) for p in pairs]

    def body(*refs):
        ab = refs[:2 * n_p]
        ex = refs[2 * n_p:2 * n_p + n_e]
        o = refs[2 * n_p + n_e:2 * n_p + n_e + n_o]
        accs = refs[2 * n_p + n_e + n_o:]

        def partial_sums():
            sums = [None] * n_a
            for p, (dims, ai) in enumerate(dims_idx):
                a = ab[2 * p][...].astype(BF16)
                b = ab[2 * p + 1][...].astype(BF16)
                d = lax.dot_general(a, b, (dims, ((), ())), preferred_element_type=F32)
                sums[ai] = d if sums[ai] is None else sums[ai] + d
            return sums

        def finish(vals):
            res = epilogue(vals, [e[...] for e in ex])
            for r, oref in zip(res, o):
                oref[...] = r.astype(oref.dtype)

        if nk == 1:
            finish(partial_sums())
        else:
            k = pl.program_id(2)

            @pl.when(k == 0)
            def _():
                for acc in accs:
                    acc[...] = jnp.zeros_like(acc)

            for acc, s in zip(accs, partial_sums()):
                acc[...] += s

            @pl.when(k == nk - 1)
            def _():
                finish([acc[...] for acc in accs])

    in_arrays, in_specs = [], []
    for a, a_spec, b, b_spec, _, _ in pairs:
        in_arrays += [a, b]
        in_specs += [a_spec, b_spec]
    for e, e_spec in extras:
        in_arrays.append(e)
        in_specs.append(e_spec)
    res = _pcall(
        body, name=name, grid=grid, in_specs=in_specs,
        out_specs=[o[2] for o in outs],
        out_shape=[jax.ShapeDtypeStruct(o[0], o[1]) for o in outs],
        scratch_shapes=[] if nk == 1 else [pltpu.VMEM(s, F32) for s in acc_shapes],
        compiler_params=pltpu.CompilerParams(dimension_semantics=("parallel", "parallel", "arbitrary")),
    )(*in_arrays)
    return res


def _bs(shape, fn):
    return pl.BlockSpec(shape, fn)


def _mm_simple(name, a, b, dims, *, tm=512, tn=512, tk=512, out_dtype=F32, epilogue=None, extras=(), scale=None):
    if dims == NN:
        (M, K), N = a.shape, b.shape[1]
    elif dims == NT:
        (M, K), N = a.shape, b.shape[0]
    else:
        (K, M), N = a.shape, b.shape[1]
    tm, tn, tk = _tile(M, tm, 128), _tile(N, tn, 128), _tile(K, tk, 128)
    if dims == NN:
        a_spec, b_spec = _bs((tm, tk), lambda i, j, k: (i, k)), _bs((tk, tn), lambda i, j, k: (k, j))
    elif dims == NT:
        a_spec, b_spec = _bs((tm, tk), lambda i, j, k: (i, k)), _bs((tn, tk), lambda i, j, k: (j, k))
    else:
        a_spec, b_spec = _bs((tk, tm), lambda i, j, k: (k, i)), _bs((tk, tn), lambda i, j, k: (k, j))
    ex = []
    for e in extras:
        if e.shape[0] == 1:
            ex.append((e, _bs((1, tn), lambda i, j, k: (0, j))))
        else:
            ex.append((e, _bs((tm, tn), lambda i, j, k: (i, j))))
    if epilogue is None:
        if scale is None:
            epilogue = lambda accs, ex_: (accs[0],)
        else:
            epilogue = lambda accs, ex_: (accs[0] * scale,)
        out_dtypes = (out_dtype,)
    else:
        out_dtypes = out_dtype if isinstance(out_dtype, tuple) else (out_dtype,)
    outs = [((M, N), dt, _bs((tm, tn), lambda i, j, k: (i, j))) for dt in out_dtypes]
    res = _mm(name, (M // tm, N // tn, K // tk), [(a, a_spec, b, b_spec, dims, 0)], [(tm, tn)], outs, epilogue, ex)
    return res[0] if len(res) == 1 else res


def _rowwise(name, fn, tiles, bcasts, outs, accs=(), *, tm=256):
    rows = tiles[0].shape[0]
    tm = _tile(rows, tm)
    n_t, n_b, n_o, n_a = len(tiles), len(bcasts), len(outs), len(accs)

    def body(*refs):
        t = [r[...] for r in refs[:n_t]]
        b = [r[...] for r in refs[n_t:n_t + n_b]]
        o = refs[n_t + n_b:n_t + n_b + n_o]
        a = refs[n_t + n_b + n_o:]
        ov, av = fn(t, b)
        for r, val in zip(o, ov):
            r[...] = val.astype(r.dtype)
        if n_a:
            @pl.when(pl.program_id(0) == 0)
            def _():
                for r in a:
                    r[...] = jnp.zeros_like(r)
            for r, val in zip(a, av):
                r[...] += val

    in_specs = [_bs((tm, x.shape[1]), lambda i: (i, 0)) for x in tiles]
    in_specs += [_bs(x.shape, lambda i, nd=x.ndim: (0,) * nd) for x in bcasts]
    out_specs = [_bs((tm, w), lambda i: (i, 0)) for w, _ in outs]
    out_specs += [_bs((1, w), lambda i: (0, 0)) for w in accs]
    out_shape = [jax.ShapeDtypeStruct((rows, w), dt) for w, dt in outs]
    out_shape += [jax.ShapeDtypeStruct((1, w), F32) for w in accs]
    return _pcall(
        body, name=name, grid=(rows // tm,), in_specs=in_specs, out_specs=out_specs, out_shape=out_shape,
        compiler_params=pltpu.CompilerParams(dimension_semantics=("arbitrary",)),
    )(*tiles, *bcasts)


def _colsum(v):
    return jnp.sum(v, axis=0, keepdims=True)


def _rms_fwd(name, x, gain, out_dtype=BF16):
    def fn(t, b):
        xv = t[0]
        r = lax.rsqrt(jnp.mean(xv * xv, axis=-1, keepdims=True) + EPS)
        return (xv * r * b[0], r), ()
    w = x.shape[1]
    return _rowwise(name, fn, [x], [gain.reshape(1, w)], [(w, out_dtype), (1, F32)])


def _rms_bwd(name, dh, x, rstd, gain, dres=None, with_bf16=False):
    def fn(t, b):
        dhv, xv, r = t[0], t[1], t[2]
        xh = xv * r
        dxh = dhv * b[0]
        dx = r * (dxh - xh * jnp.mean(dxh * xh, axis=-1, keepdims=True))
        if dres is not None:
            dx = dx + t[3]
        return ((dx, dx) if with_bf16 else (dx,)), (_colsum(dhv * xh),)
    w = x.shape[1]
    tiles = [dh, x, rstd] + ([dres] if dres is not None else [])
    outs = [(w, F32), (w, BF16)] if with_bf16 else [(w, F32)]
    return _rowwise(name, fn, tiles, [gain.reshape(1, w)], outs, [w])


def _sigmoid(v):
    return 1.0 / (1.0 + jnp.exp(-v))


def _ffn_fwd(tag, x, gain, wg2, wu2, wd):
    T, D = x.shape
    fs = wg2.shape[1]
    h, rstd = _rms_fwd(tag + "_rms", x, gain)
    tm = _tile(T, 256)
    a_spec = _bs((tm, D), lambda i, j, k: (j, 0))
    w_spec = _bs((D, fs), lambda i, j, k: (i, 0))
    o_spec = _bs((tm, fs), lambda i, j, k: (j, i))

    def epi(accs, ex):
        a, b = accs
        return a, b, a * _sigmoid(a) * b

    a, b, t = _mm(
        tag + "_up", (N_SHARD, T // tm, 1),
        [(h, a_spec, wg2, w_spec, NN, 0), (h, a_spec, wu2, w_spec, NN, 1)],
        [(tm, fs), (tm, fs)],
        [((T, N_SHARD * fs), F32, o_spec), ((T, N_SHARD * fs), F32, o_spec), ((T, N_SHARD * fs), BF16, o_spec)],
        epi)
    y = _mm_simple(tag + "_down", t, wd, NN, tm=512, tn=D, tk=fs // 2 if fs % 256 == 0 else fs,
                   epilogue=lambda accs, ex: (ex[0] + 0.5 * accs[0],), extras=[x])
    return y, (x, h, rstd, a, b, t)


def _ffn_bwd(tag, dy, dy16, saved, gain, wg2, wu2, wd):
    x, h, rstd, a, b, t = saved
    T, D = x.shape
    fs = wg2.shape[1]
    F = N_SHARD * fs
    dwd = _mm_simple(tag + "_dwd", t, dy16, TN, tm=512, tn=D, tk=512, scale=0.5, out_dtype=BF16)

    def epi(accs, ex):
        dt = 0.5 * accs[0]
        av, bv = ex
        sig = _sigmoid(av)
        da = dt * bv * (sig * (1.0 + av * (1.0 - sig)))
        db = dt * (av * sig)
        return da, db

    tn = 512 if F % 512 == 0 else (256 if F % 256 == 0 else 128)
    da, db = _mm_simple(tag + "_dt", dy16, wd, NT, tm=512, tn=tn, tk=D, out_dtype=(BF16, BF16), epilogue=epi,
                        extras=[a, b])
    tm = _tile(D, 1024)
    per = D // tm
    tk = _tile(T, 512)
    h_spec = _bs((tk, tm), lambda i, j, k: (k, i % per))
    d_spec = _bs((tk, fs), lambda i, j, k: (k, i // per))
    o_spec = _bs((tm, fs), lambda i, j, k: (i, 0))
    dwg2, dwu2 = _mm(
        tag + "_dwgu", (N_SHARD * per, 1, T // tk),
        [(h, h_spec, da, d_spec, TN, 0), (h, h_spec, db, d_spec, TN, 1)],
        [(tm, fs), (tm, fs)],
        [((N_SHARD * D, fs), BF16, o_spec), ((N_SHARD * D, fs), BF16, o_spec)],
        lambda accs, ex: tuple(accs))
    tm2 = _tile(T, 512)
    tn2 = _tile(D, 1024)
    per2 = D // tn2
    g_spec = _bs((tm2, fs), lambda i, j, k: (i, k))
    w_spec = _bs((tn2, fs), lambda i, j, k: (k * per2 + j, 0))
    o2 = _bs((tm2, tn2), lambda i, j, k: (i, j))
    dh, = _mm(
        tag + "_dh", (T // tm2, per2, N_SHARD),
        [(da, g_spec, wg2, w_spec, NT, 0), (db, g_spec, wu2, w_spec, NT, 0)],
        [(tm2, tn2)], [((T, D), F32, o2)], lambda accs, ex: (accs[0],))
    dx, dx16, dgain = _rms_bwd(tag + "_rmsb", dh, x, rstd, gain, dres=dy, with_bf16=True)
    return dx, dx16, dgain, dwg2, dwu2, dwd


def _rope_tables(positions):
    inv_freq = ROPE_THETA ** (-jnp.arange(0, 2 * ROPE_HALF, 2, dtype=F32) / (2 * ROPE_HALF))
    ang = positions.astype(F32)[:, None] * inv_freq
    c, s = jnp.cos(ang), jnp.sin(ang)
    c64 = jnp.concatenate([c, c], axis=1)
    s64 = jnp.concatenate([-s, s], axis=1)
    T = positions.shape[0]
    one, zero = jnp.ones((T, MLA_NOPE), F32), jnp.zeros((T, MLA_NOPE), F32)
    pad1, pad0 = jnp.ones((T, MLA_SLOT - MLA_NOPE - MLA_ROPE), F32), jnp.zeros((T, MLA_SLOT - MLA_NOPE - MLA_ROPE), F32)
    c256 = jnp.concatenate([one, c64, pad1], axis=1)
    s256 = jnp.concatenate([zero, s64, pad0], axis=1)
    return c64, s64, c256, s256


def _half_swap(v):
    w = v.shape[1]
    lane = lax.broadcasted_iota(jnp.int32, v.shape, 1)
    first = (lane % (2 * ROPE_HALF)) < ROPE_HALF
    return jnp.where(first, pltpu.roll(v, w - ROPE_HALF, 1), pltpu.roll(v, ROPE_HALF, 1))


def _rope(name, x, ctab, stab, backward=False, out_dtype=F32):
    reps = x.shape[1] // ctab.shape[1]

    def fn(t, b):
        xv = t[0]
        c = jnp.tile(t[1], (1, reps)) if reps > 1 else t[1]
        s = jnp.tile(t[2], (1, reps)) if reps > 1 else t[2]
        if backward:
            return (xv * c + _half_swap(xv * s),), ()
        return (xv * c + _half_swap(xv) * s,), ()
    return _rowwise(name, fn, [x, ctab, stab], [], [(x.shape[1], out_dtype)])[0]


def _s5_prep(log_dt, a_re, a_im, b_re, b_im, c_re, c_im):
    G, P = a_re.shape
    C = b_re.shape[-1]
    dt = jnp.exp(log_dt)[:, None]
    mag = jnp.exp(a_re * dt)
    abar_r = mag * jnp.cos(a_im * dt)
    abar_i = mag * jnp.sin(a_im * dt)
    den = a_re * a_re + a_im * a_im
    nr = abar_r - 1.0
    qr = (nr * a_re + abar_i * a_im) / den
    qi = (abar_i * a_re - nr * a_im) / den
    bbar_r = qr[..., None] * b_re - qi[..., None] * b_im
    bbar_i = qr[..., None] * b_im + qi[..., None] * b_re
    nt = G // S5_TILE
    eye = jnp.eye(S5_TILE, dtype=F32)
    tiles_b = lambda t: jnp.einsum('tgpc,gh->tgchp', t.reshape(nt, S5_TILE, P, C), eye).reshape(
        nt, S5_TILE * C, S5_TILE * P)
    tiles_c = lambda t: jnp.einsum('tgcp,gh->tgphc', t.reshape(nt, S5_TILE, C, P), eye).reshape(
        nt, S5_TILE * P, S5_TILE * C)
    return (abar_r.reshape(1, G * P), abar_i.reshape(1, G * P), tiles_b(bbar_r), tiles_b(bbar_i), tiles_c(c_re),
            -tiles_c(c_im))


def _scan_lanes(gp):
    for tl in (1024, 512, 256):
        if gp % tl == 0:
            return tl
    return 128


def _scan_fwd(name, bu_r, bu_i, a_r, a_i):
    T, gp = bu_r.shape
    tl = _scan_lanes(gp)
    tc = _tile(T, 256)

    def body(br_ref, bi_ref, ar_ref, ai_ref, xr_ref, xi_ref, pr_ref, pi_ref, sr, si):
        @pl.when(pl.program_id(1) == 0)
        def _():
            sr[...] = jnp.zeros_like(sr)
            si[...] = jnp.zeros_like(si)
        ar, ai = ar_ref[...], ai_ref[...]

        def step(g, carry):
            xr, xi = carry
            base = pl.multiple_of(g * 8, 8)
            b_r = br_ref[pl.ds(base, 8), :]
            b_i = bi_ref[pl.ds(base, 8), :]
            rows_r, rows_i, prev_r, prev_i = [], [], [], []
            for j in range(8):
                prev_r.append(xr)
                prev_i.append(xi)
                nr = ar * xr - ai * xi + b_r[j:j + 1, :]
                ni = ar * xi + ai * xr + b_i[j:j + 1, :]
                xr, xi = nr, ni
                rows_r.append(xr)
                rows_i.append(xi)
            xr_ref[pl.ds(base, 8), :] = jnp.concatenate(rows_r, axis=0)
            xi_ref[pl.ds(base, 8), :] = jnp.concatenate(rows_i, axis=0)
            pr_ref[pl.ds(base, 8), :] = jnp.concatenate(prev_r, axis=0)
            pi_ref[pl.ds(base, 8), :] = jnp.concatenate(prev_i, axis=0)
            return xr, xi

        xr, xi = lax.fori_loop(0, tc // 8, step, (sr[...], si[...]))
        sr[...] = xr
        si[...] = xi

    spec = _bs((tc, tl), lambda l, t: (t, l))
    a_spec = _bs((1, tl), lambda l, t: (0, l))
    return _pcall(
        body, name=name, grid=(gp // tl, T // tc),
        in_specs=[spec, spec, a_spec, a_spec], out_specs=[spec] * 4,
        out_shape=[jax.ShapeDtypeStruct((T, gp), F32)] * 4,
        scratch_shapes=[pltpu.VMEM((1, tl), F32), pltpu.VMEM((1, tl), F32)],
        compiler_params=pltpu.CompilerParams(dimension_semantics=("parallel", "arbitrary")),
    )(bu_r, bu_i, a_r, a_i)


def _scan_bwd(name, g_r, g_i, p_r, p_i, a_r, a_i):
    T, gp = g_r.shape
    tl = _scan_lanes(gp)
    tc = _tile(T, 256)
    nt = T // tc
    ng = tc // 8

    def body(gr_ref, gi_ref, pr_ref, pi_ref, ar_ref, ai_ref, lr_ref, li_ref, dar_ref, dai_ref, sr, si, accr, acci):
        t = pl.program_id(1)

        @pl.when(t == 0)
        def _():
            sr[...] = jnp.zeros_like(sr)
            si[...] = jnp.zeros_like(si)
            accr[...] = jnp.zeros_like(accr)
            acci[...] = jnp.zeros_like(acci)
        ar, ai = ar_ref[...], ai_ref[...]

        def step(kk, carry):
            lr, li = carry
            base = pl.multiple_of((ng - 1 - kk) * 8, 8)
            gr8 = gr_ref[pl.ds(base, 8), :]
            gi8 = gi_ref[pl.ds(base, 8), :]
            rows_r, rows_i = [None] * 8, [None] * 8
            for j in range(7, -1, -1):
                nr = gr8[j:j + 1, :] + ar * lr + ai * li
                ni = gi8[j:j + 1, :] - ai * lr + ar * li
                lr, li = nr, ni
                rows_r[j] = lr
                rows_i[j] = li
            lam_r = jnp.concatenate(rows_r, axis=0)
            lam_i = jnp.concatenate(rows_i, axis=0)
            lr_ref[pl.ds(base, 8), :] = lam_r
            li_ref[pl.ds(base, 8), :] = lam_i
            pr8 = pr_ref[pl.ds(base, 8), :]
            pi8 = pi_ref[pl.ds(base, 8), :]
            accr[...] += lam_r * pr8 + lam_i * pi8
            acci[...] += lam_i * pr8 - lam_r * pi8
            return lr, li

        lr, li = lax.fori_loop(0, ng, step, (sr[...], si[...]))
        sr[...] = lr
        si[...] = li

        @pl.when(t == nt - 1)
        def _():
            dar_ref[...] = jnp.sum(accr[...], axis=0, keepdims=True)
            dai_ref[...] = jnp.sum(acci[...], axis=0, keepdims=True)

    spec = _bs((tc, tl), lambda l, t: (nt - 1 - t, l))
    a_spec = _bs((1, tl), lambda l, t: (0, l))
    return _pcall(
        body, name=name, grid=(gp // tl, nt),
        in_specs=[spec] * 4 + [a_spec, a_spec], out_specs=[spec, spec, a_spec, a_spec],
        out_shape=[jax.ShapeDtypeStruct((T, gp), F32)] * 2 + [jax.ShapeDtypeStruct((1, gp), F32)] * 2,
        scratch_shapes=[pltpu.VMEM((1, tl), F32), pltpu.VMEM((1, tl), F32),
                        pltpu.VMEM((8, tl), F32), pltpu.VMEM((8, tl), F32)],
        compiler_params=pltpu.CompilerParams(dimension_semantics=("parallel", "arbitrary")),
    )(g_r, g_i, p_r, p_i, a_r, a_i)


_GELU_C = math.sqrt(2.0 / math.pi)


def _gelu(v):
    return 0.5 * v * (1.0 + jnp.tanh(_GELU_C * (v + 0.044715 * v * v * v)))


def _gelu_grad(v):
    th = jnp.tanh(_GELU_C * (v + 0.044715 * v * v * v))
    return 0.5 * (1.0 + th) + 0.5 * v * (1.0 - th * th) * _GELU_C * (1.0 + 3.0 * 0.044715 * v * v)


def _s5_specs(T, nt, cu, cs):
    tm = _tile(T, 512)
    return dict(
        tm=tm,
        chan=_bs((tm, cu), lambda i, j, k: (i, j)), state=_bs((tm, cs), lambda i, j, k: (i, j)),
        b=_bs((None, cu, cs), lambda i, j, k: (j, 0, 0)), c=_bs((None, cs, cu), lambda i, j, k: (j, 0, 0)),
        row=_bs((1, cu), lambda i, j, k: (0, j)))


def _s5_fwd(tag, u, prep, d_skip, w_glu, b_glu):
    a_r, a_i, b_r, b_i, c_r, c_in = prep
    T, W = u.shape
    gp = a_r.shape[1]
    nt, cu, cs = b_r.shape
    sp = _s5_specs(T, nt, cu, cs)
    tm = sp['tm']
    bu_r, bu_i = _mm(tag + "_bu", (T // tm, nt, 1),
                     [(u, sp['chan'], b_r, sp['b'], NN, 0), (u, sp['chan'], b_i, sp['b'], NN, 1)],
                     [(tm, cs)] * 2, [((T, gp), F32, sp['state'])] * 2, lambda accs, ex: tuple(accs))
    x_r, x_i, p_r, p_i = _scan_fwd(tag + "_scan", bu_r, bu_i, a_r, a_i)

    def epi(accs, ex):
        y = accs[0] + ex[1] * ex[0]
        return y, _gelu(y)

    ypre, yg = _mm(tag + "_y", (T // tm, nt, 1),
                   [(x_r, sp['state'], c_r, sp['c'], NN, 0), (x_i, sp['state'], c_in, sp['c'], NN, 0)],
                   [(tm, cu)], [((T, W), F32, sp['chan'])] * 2, epi,
                   [(u, sp['chan']), (d_skip.reshape(1, W), sp['row'])])

    def epi2(accs, ex):
        pre = accs[0] + ex[1]
        return ex[0] * _sigmoid(pre), pre

    out, pre = _mm_simple(tag + "_glu", yg, w_glu, NN, tm=512, tn=W, tk=W, out_dtype=(F32, F32), epilogue=epi2,
                          extras=[yg, b_glu.reshape(1, W)])
    return out, (u, x_r, x_i, p_r, p_i, ypre, yg, pre)


def _s5_bwd(tag, d_out, saved, prep, d_skip, w_glu):
    u, x_r, x_i, p_r, p_i, ypre, yg, pre = saved
    a_r, a_i, b_r, b_i, c_r, c_in = prep
    T, W = u.shape

    def gate_fn(t, b):
        gate = _sigmoid(t[2])
        dpre = t[0] * t[1] * gate * (1.0 - gate)
        return (dpre, t[0] * gate), (_colsum(dpre),)

    dpre, tmp, db_glu = _rowwise(tag + "_gateb", gate_fn, [d_out, yg, pre], [], [(W, F32), (W, F32)], [W])
    dw_glu = _mm_simple(tag + "_dwglu", yg, dpre, TN, tm=W, tn=W, tk=512, out_dtype=BF16)
    dy = _mm_simple(tag + "_dyg", dpre, w_glu, NT, tm=512, tn=W, tk=W,
                    epilogue=lambda accs, ex: ((accs[0] + ex[0]) * _gelu_grad(ex[1]),), extras=[tmp, ypre])
    dd, = _rowwise(tag + "_dd", lambda t, b: ((), (_colsum(t[0] * t[1]),)), [dy, u], [], [], [W])
    gp = a_r.shape[1]
    nt, cu, cs = b_r.shape
    sp = _s5_specs(T, nt, cu, cs)
    tm = sp['tm']
    dx_r, dx_i = _mm(tag + "_dx", (T // tm, nt, 1),
                     [(dy, sp['chan'], c_r, sp['c'], NT, 0), (dy, sp['chan'], c_in, sp['c'], NT, 1)],
                     [(tm, cs)] * 2, [((T, gp), F32, sp['state'])] * 2, lambda accs, ex: tuple(accs))
    tk = _tile(T, 512)
    t_chan = _bs((tk, cu), lambda i, j, k: (k, i))
    t_state = _bs((tk, cs), lambda i, j, k: (k, i))
    o_b = _bs((None, cu, cs), lambda i, j, k: (i, 0, 0))
    o_c = _bs((None, cs, cu), lambda i, j, k: (i, 0, 0))
    dc_r, dc_in = _mm(tag + "_dc", (nt, 1, T // tk),
                      [(x_r, t_state, dy, t_chan, TN, 0), (x_i, t_state, dy, t_chan, TN, 1)],
                      [(cs, cu)] * 2, [((nt, cs, cu), F32, o_c)] * 2, lambda accs, ex: tuple(accs))
    lam_r, lam_i, da_r, da_i = _scan_bwd(tag + "_scanb", dx_r, dx_i, p_r, p_i, a_r, a_i)
    du, = _mm(tag + "_du", (T // tm, nt, 1),
              [(lam_r, sp['state'], b_r, sp['b'], NT, 0), (lam_i, sp['state'], b_i, sp['b'], NT, 0)],
              [(tm, cu)], [((T, W), F32, sp['chan'])], lambda accs, ex: (accs[0] + ex[1] * ex[0],),
              [(dy, sp['chan']), (d_skip.reshape(1, W), sp['row'])])
    db_r, db_i = _mm(tag + "_db", (nt, 1, T // tk),
                     [(u, t_chan, lam_r, t_state, TN, 0), (u, t_chan, lam_i, t_state, TN, 1)],
                     [(cu, cs)] * 2, [((nt, cu, cs), F32, o_b)] * 2, lambda accs, ex: tuple(accs))
    return du, (da_r, da_i, db_r, db_i, dc_r, dc_in), dd, dw_glu, db_glu


def _swa_mask(n):
    B = SWA_BLOCK
    r = lax.broadcasted_iota(jnp.int32, (B, 2 * B), 0)
    c = lax.broadcasted_iota(jnp.int32, (B, 2 * B), 1)
    d = r + B - c
    return (d >= 0) & (d < B) & ((n > 0) | (c >= B))


def _swa_specs(T):
    B = SWA_BLOCK
    cur = lambda w: _bs((B, w), lambda n: (n, 0))
    prev = lambda w: _bs((B, w), lambda n: (jnp.maximum(n - 1, 0), 0))
    return cur, prev


def _swa_fwd(name, q, k, v, sinks_b):
    T = q.shape[0]
    B = SWA_BLOCK
    scale = SWA_HEAD_DIM ** -0.5
    per_kv = SWA_HEADS // SWA_KV_HEADS

    def body(q_ref, kc_ref, kp_ref, vc_ref, vp_ref, s_ref, o_ref, l_ref):
        n = pl.program_id(0)
        kcat = jnp.concatenate([kp_ref[...], kc_ref[...]], axis=0).astype(BF16)
        vcat = jnp.concatenate([vp_ref[...], vc_ref[...]], axis=0).astype(BF16)
        mask = _swa_mask(n)
        lane = lax.broadcasted_iota(jnp.int32, (B, 128), 1)
        lo = lane < SWA_HEAD_DIM
        lse_out = jnp.zeros((B, 128), F32)
        for jb in range(SWA_HEADS // 2):
            qblk = q_ref[:, jb * 128:(jb + 1) * 128]
            h = (2 * jb) // per_kv
            half_h = lo if h == 0 else jnp.logical_not(lo)
            outs = []
            for e in range(2):
                j = 2 * jb + e
                qa = qblk if e == h else pltpu.roll(qblk, SWA_HEAD_DIM, 1)
                qm = jnp.where(half_h, qa, 0.0).astype(BF16)
                s = lax.dot_general(qm, kcat, (NT, ((), ())), preferred_element_type=F32) * scale
                s = jnp.where(mask, s, -jnp.inf)
                sk = s_ref[j:j + 1, 0:1]
                m = jnp.maximum(jnp.max(s, axis=1, keepdims=True), sk)
                ex = jnp.exp(s - m)
                den = jnp.sum(ex, axis=1, keepdims=True) + jnp.exp(sk - m)
                p = ex / den
                r = lax.dot_general(p.astype(BF16), vcat, (NN, ((), ())), preferred_element_type=F32)
                outs.append(r if e == h else pltpu.roll(r, SWA_HEAD_DIM, 1))
                lse_out = jnp.where(lane == j, m + jnp.log(den), lse_out)
            o_ref[:, jb * 128:(jb + 1) * 128] = jnp.where(lo, outs[0], outs[1])
        l_ref[...] = lse_out

    cur, prev = _swa_specs(T)
    return _pcall(
        body, name=name, grid=(T // B,),
        in_specs=[cur(SWA_WIDTH), cur(SWA_KV_WIDTH), prev(SWA_KV_WIDTH), cur(SWA_KV_WIDTH), prev(SWA_KV_WIDTH),
                  _bs(sinks_b.shape, lambda n: (0, 0))],
        out_specs=[cur(SWA_WIDTH), cur(128)],
        out_shape=[jax.ShapeDtypeStruct((T, SWA_WIDTH), F32), jax.ShapeDtypeStruct((T, 128), F32)],
        compiler_params=pltpu.CompilerParams(dimension_semantics=("parallel",)),
    )(q, k, k, v, v, sinks_b)


def _swa_bwd(name, q, k, v, sinks_b, o, lse, do):
    T = q.shape[0]
    B = SWA_BLOCK
    scale = SWA_HEAD_DIM ** -0.5
    per_kv = SWA_HEADS // SWA_KV_HEADS

    def body(q_ref, kc_ref, kp_ref, vc_ref, vp_ref, s_ref, o_ref, l_ref, do_ref,
             dq_ref, dkc_ref, dkp_ref, dvc_ref, dvp_ref, ds_ref):
        n = pl.program_id(0)

        @pl.when(n == 0)
        def _():
            ds_ref[...] = jnp.zeros_like(ds_ref)
        kcat = jnp.concatenate([kp_ref[...], kc_ref[...]], axis=0).astype(BF16)
        vcat = jnp.concatenate([vp_ref[...], vc_ref[...]], axis=0).astype(BF16)
        mask = _swa_mask(n)
        lane = lax.broadcasted_iota(jnp.int32, (B, 128), 1)
        lane1 = lax.broadcasted_iota(jnp.int32, (1, 128), 1)
        lo = lane < SWA_HEAD_DIM
        lblk = l_ref[...]
        dk = jnp.zeros((2 * B, 128), F32)
        dv = jnp.zeros((2 * B, 128), F32)
        dsink = jnp.zeros((1, 128), F32)
        for jb in range(SWA_HEADS // 2):
            sl = slice(jb * 128, (jb + 1) * 128)
            qblk, doblk = q_ref[:, sl], do_ref[:, sl]
            prod = doblk * o_ref[:, sl]
            h = (2 * jb) // per_kv
            half_h = lo if h == 0 else jnp.logical_not(lo)
            parts = []
            for e in range(2):
                j = 2 * jb + e
                half_e = lo if e == 0 else jnp.logical_not(lo)
                dsum = jnp.sum(jnp.where(half_e, prod, 0.0), axis=1, keepdims=True)
                lj = jnp.sum(jnp.where(lane == j, lblk, 0.0), axis=1, keepdims=True)
                qa = qblk if e == h else pltpu.roll(qblk, SWA_HEAD_DIM, 1)
                da = doblk if e == h else pltpu.roll(doblk, SWA_HEAD_DIM, 1)
                qm = jnp.where(half_h, qa, 0.0).astype(BF16)
                dm = jnp.where(half_h, da, 0.0).astype(BF16)
                s = lax.dot_general(qm, kcat, (NT, ((), ())), preferred_element_type=F32) * scale
                p = jnp.where(mask, jnp.exp(s - lj), 0.0)
                dp = lax.dot_general(dm, vcat, (NT, ((), ())), preferred_element_type=F32)
                dsb = (p * (dp - dsum) * scale).astype(BF16)
                dqa = lax.dot_general(dsb, kcat, (NN, ((), ())), preferred_element_type=F32)
                parts.append(dqa if e == h else pltpu.roll(dqa, SWA_HEAD_DIM, 1))
                dk = dk + lax.dot_general(dsb, qm, (TN, ((), ())), preferred_element_type=F32)
                dv = dv + lax.dot_general(p.astype(BF16), dm, (TN, ((), ())), preferred_element_type=F32)
                sk = s_ref[j:j + 1, 0:1]
                contrib = jnp.sum(jnp.exp(sk - lj) * dsum, axis=0, keepdims=True)
                dsink = jnp.where(lane1 == j, dsink - contrib, dsink)
            dq_ref[:, sl] = jnp.where(lo, parts[0], parts[1])
        dkp_ref[...] = dk[:B]
        dkc_ref[...] = dk[B:]
        dvp_ref[...] = dv[:B]
        dvc_ref[...] = dv[B:]
        ds_ref[...] += dsink

    cur, prev = _swa_specs(T)
    kv = jax.ShapeDtypeStruct((T, SWA_KV_WIDTH), F32)
    return _pcall(
        body, name=name, grid=(T // B,),
        in_specs=[cur(SWA_WIDTH), cur(SWA_KV_WIDTH), prev(SWA_KV_WIDTH), cur(SWA_KV_WIDTH), prev(SWA_KV_WIDTH),
                  _bs(sinks_b.shape, lambda n: (0, 0)), cur(SWA_WIDTH), cur(128), cur(SWA_WIDTH)],
        out_specs=[cur(SWA_WIDTH)] + [cur(SWA_KV_WIDTH)] * 4 + [_bs((1, 128), lambda n: (0, 0))],
        out_shape=[jax.ShapeDtypeStruct((T, SWA_WIDTH), F32), kv, kv, kv, kv, jax.ShapeDtypeStruct((1, 128), F32)],
        compiler_params=pltpu.CompilerParams(dimension_semantics=("arbitrary",)),
    )(q, k, k, v, v, sinks_b, o, lse, do)


def _shift_add(name, cur, prv):
    T, W = cur.shape
    B = SWA_BLOCK
    nb = T // B

    def body(c_ref, p_ref, o_ref):
        n = pl.program_id(0)
        o_ref[...] = c_ref[...] + jnp.where(n < nb - 1, p_ref[...], 0.0)

    return _pcall(
        body, name=name, grid=(nb,),
        in_specs=[_bs((B, W), lambda n: (n, 0)), _bs((B, W), lambda n: (jnp.minimum(n + 1, nb - 1), 0))],
        out_specs=_bs((B, W), lambda n: (n, 0)), out_shape=jax.ShapeDtypeStruct((T, W), F32),
        compiler_params=pltpu.CompilerParams(dimension_semantics=("parallel",)),
    )(cur, prv)


MLA_BLOCK = 512
_MLA_SCALE = (MLA_NOPE + MLA_ROPE) ** -0.5


def _lower_tri(tb):
    r = lax.broadcasted_iota(jnp.int32, (tb, tb), 0)
    c = lax.broadcasted_iota(jnp.int32, (tb, tb), 1)
    return c <= r


def _mla_fwd(name, qcat, kcat, v):
    T = qcat.shape[0]
    tb = _tile(T, MLA_BLOCK)
    nb = T // tb

    def body(q_ref, k_ref, v_ref, o_ref, l_ref):
        qi = pl.program_id(1)
        q = q_ref[...]

        def block(kb, carry, diagonal):
            m_prev, l_prev, acc = carry
            rows = pl.ds(pl.multiple_of(kb * tb, tb), tb)
            s = lax.dot_general(q, k_ref[rows, :], (NT, ((), ())), preferred_element_type=F32) * _MLA_SCALE
            if diagonal:
                s = jnp.where(_lower_tri(tb), s, -jnp.inf)
            m_new = jnp.maximum(m_prev, jnp.max(s, axis=1, keepdims=True))
            alpha = jnp.exp(m_prev - m_new)
            p = jnp.exp(s - m_new)
            l_new = alpha * l_prev + jnp.sum(p, axis=1, keepdims=True)
            acc = alpha * acc + lax.dot_general(p.astype(BF16), v_ref[rows, :], (NN, ((), ())),
                                                preferred_element_type=F32)
            return m_new, l_new, acc

        init = (jnp.full((tb, 1), -jnp.inf, F32), jnp.zeros((tb, 1), F32), jnp.zeros((tb, MLA_V), F32))
        carry = lax.fori_loop(0, qi, lambda kb, c: block(kb, c, False), init)
        m_fin, l_fin, acc = block(qi, carry, True)
        o_ref[...] = acc / l_fin
        l_ref[...] = m_fin + jnp.log(l_fin)

    return _pcall(
        body, name=name, grid=(MLA_HEADS, nb),
        in_specs=[_bs((tb, MLA_SLOT), lambda h, qi: (qi, h)), _bs((T, MLA_SLOT), lambda h, qi: (0, h)),
                  _bs((T, MLA_V), lambda h, qi: (0, h))],
        out_specs=[_bs((tb, MLA_V), lambda h, qi: (qi, h)), _bs((None, tb, 1), lambda h, qi: (h, qi, 0))],
        out_shape=[jax.ShapeDtypeStruct((T, MLA_WIDTH), F32), jax.ShapeDtypeStruct((MLA_HEADS, T, 1), F32)],
        compiler_params=pltpu.CompilerParams(dimension_semantics=("parallel", "arbitrary")),
    )(qcat, kcat, v)


def _mla_bwd_q(name, qcat, kcat, v, o, lse, do):
    T = qcat.shape[0]
    tb = _tile(T, MLA_BLOCK)
    nb = T // tb

    def body(q_ref, k_ref, v_ref, o_ref, do_ref, l_ref, dq_ref, ds_ref):
        qi = pl.program_id(1)
        q = q_ref[...]
        do = do_ref[...]
        dsum = jnp.sum(do * o_ref[...], axis=1, keepdims=True)
        ds_ref[...] = dsum
        dob = do.astype(BF16)
        lse_q = l_ref[...]

        def block(kb, acc, diagonal):
            rows = pl.ds(pl.multiple_of(kb * tb, tb), tb)
            k = k_ref[rows, :]
            s = lax.dot_general(q, k, (NT, ((), ())), preferred_element_type=F32) * _MLA_SCALE
            p = jnp.exp(s - lse_q)
            if diagonal:
                p = jnp.where(_lower_tri(tb), p, 0.0)
            dp = lax.dot_general(dob, v_ref[rows, :], (NT, ((), ())), preferred_element_type=F32)
            ds = (p * (dp - dsum) * _MLA_SCALE).astype(BF16)
            return acc + lax.dot_general(ds, k, (NN, ((), ())), preferred_element_type=F32)

        acc = lax.fori_loop(0, qi, lambda kb, a: block(kb, a, False), jnp.zeros((tb, MLA_SLOT), F32))
        dq_ref[...] = block(qi, acc, True)

    qs = lambda w: _bs((tb, w), lambda h, qi: (qi, h))
    col = _bs((None, tb, 1), lambda h, qi: (h, qi, 0))
    return _pcall(
        body, name=name, grid=(MLA_HEADS, nb),
        in_specs=[qs(MLA_SLOT), _bs((T, MLA_SLOT), lambda h, qi: (0, h)), _bs((T, MLA_V), lambda h, qi: (0, h)),
                  qs(MLA_V), qs(MLA_V), col],
        out_specs=[qs(MLA_SLOT), col],
        out_shape=[jax.ShapeDtypeStruct((T, MLA_HEADS * MLA_SLOT), F32), jax.ShapeDtypeStruct((MLA_HEADS, T, 1), F32)],
        compiler_params=pltpu.CompilerParams(dimension_semantics=("parallel", "arbitrary")),
    )(qcat, kcat, v, o, do, lse)


def _mla_bwd_kv(name, qcat, kcat, v, lse, dsum, do):
    T = qcat.shape[0]
    tb = _tile(T, MLA_BLOCK)
    nb = T // tb

    def body(q_ref, k_ref, v_ref, do_ref, l_ref, ds_ref, dk_ref, dv_ref):
        ki = pl.program_id(1)
        k = k_ref[...]
        vv = v_ref[...]

        def block(qb, carry, diagonal):
            dk, dv = carry
            rows = pl.ds(pl.multiple_of(qb * tb, tb), tb)
            q = q_ref[rows, :]
            dob = do_ref[rows, :].astype(BF16)
            s = lax.dot_general(q, k, (NT, ((), ())), preferred_element_type=F32) * _MLA_SCALE
            p = jnp.exp(s - l_ref[rows, :])
            if diagonal:
                p = jnp.where(_lower_tri(tb), p, 0.0)
            dp = lax.dot_general(dob, vv, (NT, ((), ())), preferred_element_type=F32)
            ds = (p * (dp - ds_ref[rows, :]) * _MLA_SCALE).astype(BF16)
            dv = dv + lax.dot_general(p.astype(BF16), dob, (TN, ((), ())), preferred_element_type=F32)
            dk = dk + lax.dot_general(ds, q, (TN, ((), ())), preferred_element_type=F32)
            return dk, dv

        carry = block(ki, (jnp.zeros((tb, MLA_SLOT), F32), jnp.zeros((tb, MLA_V), F32)), True)
        dk, dv = lax.fori_loop(ki + 1, nb, lambda qb, c: block(qb, c, False), carry)
        dk_ref[...] = dk
        dv_ref[...] = dv

    ks = lambda w: _bs((tb, w), lambda h, ki: (ki, h))
    col = _bs((None, T, 1), lambda h, ki: (h, 0, 0))
    return _pcall(
        body, name=name, grid=(MLA_HEADS, nb),
        in_specs=[_bs((T, MLA_SLOT), lambda h, ki: (0, h)), ks(MLA_SLOT), ks(MLA_V),
                  _bs((T, MLA_V), lambda h, ki: (0, h)), col, col],
        out_specs=[ks(MLA_SLOT), ks(MLA_V)],
        out_shape=[jax.ShapeDtypeStruct((T, MLA_HEADS * MLA_SLOT), F32), jax.ShapeDtypeStruct((T, MLA_WIDTH), F32)],
        compiler_params=pltpu.CompilerParams(dimension_semantics=("parallel", "arbitrary")),
    )(qcat, kcat, v, do, lse, dsum)


def _kcat_fwd(name, kpre, krs, c256, s256):
    def fn(t, b):
        kr = t[1] * t[2] + _half_swap(t[1]) * t[3]
        return (t[0] + jnp.tile(kr, (1, MLA_HEADS)),), ()
    return _rowwise(name, fn, [kpre, krs, c256, s256], [], [(kpre.shape[1], BF16)])[0]


def _kcat_bwd(name, dkcat, c256, s256):
    def fn(t, b):
        d = t[0][:, 0:MLA_SLOT]
        for h in range(1, MLA_HEADS):
            d = d + t[0][:, h * MLA_SLOT:(h + 1) * MLA_SLOT]
        return (d * t[1] + _half_swap(d * t[2]),), ()
    return _rowwise(name, fn, [dkcat, c256, s256], [], [(MLA_SLOT, F32)])[0]


def _in_widths(ws):
    return [ws, SWA_WIDTH, SWA_KV_WIDTH, SWA_KV_WIDTH, MLA_Q_RANK, MLA_KV_RANK, MLA_SLOT]


def _kr_offset(ws):
    return ws + SWA_WIDTH + 2 * SWA_KV_WIDTH + MLA_Q_RANK + MLA_KV_RANK


def _pad_w_in(w, ws):
    z = lambda n: jnp.zeros((w.shape[0], n), w.dtype)
    o = _kr_offset(ws)
    return jnp.concatenate([w[:, :o], z(MLA_NOPE), w[:, o:], z(MLA_SLOT - MLA_NOPE - MLA_ROPE)], axis=1)


def _unpad_w_in(dw, ws):
    o = _kr_offset(ws)
    return jnp.concatenate([dw[:, :o], dw[:, o + MLA_NOPE:o + MLA_NOPE + MLA_ROPE]], axis=1)


def _pad_w_uq(w):
    r = w.shape[0]
    w3 = w.reshape(r, MLA_HEADS, MLA_NOPE + MLA_ROPE)
    return jnp.pad(w3, ((0, 0), (0, 0), (0, MLA_SLOT - MLA_NOPE - MLA_ROPE))).reshape(r, MLA_HEADS * MLA_SLOT)


def _unpad_w_uq(dw):
    r = dw.shape[0]
    return dw.reshape(r, MLA_HEADS, MLA_SLOT)[..., :MLA_NOPE + MLA_ROPE].reshape(r, -1)


def _pad_w_ukv(w):
    r = w.shape[0]
    w3 = w.reshape(r, MLA_HEADS, MLA_NOPE + MLA_V)
    wk = jnp.pad(w3[..., :MLA_NOPE], ((0, 0), (0, 0), (0, MLA_SLOT - MLA_NOPE))).reshape(r, MLA_HEADS * MLA_SLOT)
    wv = w3[..., MLA_NOPE:].reshape(r, MLA_WIDTH)
    return wk, wv


def _unpad_w_ukv(dwk, dwv):
    r = dwk.shape[0]
    return jnp.concatenate([dwk.reshape(r, MLA_HEADS, MLA_SLOT)[..., :MLA_NOPE], dwv.reshape(r, MLA_HEADS, MLA_V)],
                           axis=-1).reshape(r, -1)


def _layer_prep(lw):
    ws = lw['ssm_d'].shape[0]
    p = dict(lw)
    w_in_pad = _pad_w_in(lw['w_in'], ws)
    p['w_in_pad'] = w_in_pad
    offs = np.cumsum([0] + _in_widths(ws))
    p['w_in_parts'] = [w_in_pad[:, offs[i]:offs[i + 1]] for i in range(7)]
    p['s5_prep'] = _s5_prep(lw['ssm_log_dt'], lw['ssm_a_re'], lw['ssm_a_im'], lw['ssm_b_re'], lw['ssm_b_im'],
                            lw['ssm_c_re'], lw['ssm_c_im'])
    p['sinks_b'] = jnp.broadcast_to(lw['swa_sinks'][:, None], (SWA_HEADS, 128))
    p['w_uq_pad'] = _pad_w_uq(lw['mla_w_uq'])
    p['w_k_pad'], p['w_v'] = _pad_w_ukv(lw['mla_w_ukv'])
    p['w_ukv_pad'] = jnp.concatenate([p['w_k_pad'], p['w_v']], axis=1)
    b = [0, ws, ws + SWA_WIDTH, ws + SWA_WIDTH + MLA_WIDTH]
    p['w_out_g'] = [lw['w_out'][b[g]:b[g + 1]] for g in range(3)]
    p['out_norm_g'] = [lw['out_norm'][b[g]:b[g + 1]] for g in range(3)]
    return p


def _mixer_fwd(tag, x, p, tabs):
    T, D = x.shape
    c64, s64, c256, s256 = tabs
    ws = p['ssm_d'].shape[0]
    widths = _in_widths(ws)
    n_in = sum(widths)
    offs = [int(o) for o in np.cumsum([0] + widths[:-1])]
    h, rstd = _rms_fwd(tag + "_rms", x, p['mix_norm'])
    tm = _tile(T, 256)
    row_i = lambda w: _bs((tm, w), lambda i, j, k: (i, 0))
    parts = _mm(tag + "_in", (T // tm, 1, 1),
                [(h, row_i(D), p['w_in_pad'], _bs((D, n_in), lambda i, j, k: (0, 0)), NN, 0)],
                [(tm, n_in)], [((T, w), F32, row_i(w)) for w in widths],
                lambda accs, ex: tuple(accs[0][:, o:o + w] for o, w in zip(offs, widths)))
    u, q, k, v, cq, ckv, krs = parts
    y_ssm, s5_saved = _s5_fwd(tag + "_s5", u, p['s5_prep'], p['ssm_d'], p['ssm_w_glu'], p['ssm_b_glu'])
    q_r = _rope(tag + "_ropeq", q, c64, s64)
    k_r = _rope(tag + "_ropek", k, c64, s64)
    y_swa, lse_swa = _swa_fwd(tag + "_swa", q_r, k_r, v, p['sinks_b'])
    cqn, r_q = _rms_fwd(tag + "_rmsq", cq, p['mla_q_norm'])
    ckvn, r_kv = _rms_fwd(tag + "_rmskv", ckv, p['mla_kv_norm'])
    qpre = _mm_simple(tag + "_uq", cqn, p['w_uq_pad'], NN, tm=512, tn=1024, tk=MLA_Q_RANK)
    nk_, nv_ = MLA_HEADS * MLA_SLOT, MLA_WIDTH
    kpre, vm = _mm(tag + "_ukv", (T // tm, 1, 1),
                   [(ckvn, row_i(MLA_KV_RANK), p['w_ukv_pad'], _bs((MLA_KV_RANK, nk_ + nv_), lambda i, j, k: (0, 0)),
                     NN, 0)],
                   [(tm, nk_ + nv_)], [((T, nk_), F32, row_i(nk_)), ((T, nv_), BF16, row_i(nv_))],
                   lambda accs, ex: (accs[0][:, :nk_], accs[0][:, nk_:]))
    qcat = _rope(tag + "_ropemq", qpre, c256, s256, out_dtype=BF16)
    kcat = _kcat_fwd(tag + "_kcat", kpre, krs, c256, s256)
    y_mla, lse_mla = _mla_fwd(tag + "_mla", qcat, kcat, vm)
    ys = [y_ssm, y_swa, y_mla]
    yn, rs = [], []
    for g in range(3):
        n_, r_ = _rms_fwd(f"{tag}_rmso{g}", ys[g], p['out_norm_g'][g])
        yn.append(n_)
        rs.append(r_)
    tm3, tn3 = _tile(T, 512), _tile(D, 1024)
    pairs = []
    for g in range(3):
        wg = ys[g].shape[1]
        pairs.append((yn[g], _bs((tm3, wg), lambda i, j, k: (i, 0)), p['w_out_g'][g],
                      _bs((wg, tn3), lambda i, j, k: (0, j)), NN, 0))
    o_spec = _bs((tm3, tn3), lambda i, j, k: (i, j))
    x2, = _mm(tag + "_out", (T // tm3, D // tn3, 1), pairs, [(tm3, tn3)], [((T, D), F32, o_spec)],
              lambda accs, ex: (ex[0] + accs[0],), [(x, o_spec)])
    saved = (x, h, rstd, q_r, k_r, v, cq, ckv, s5_saved, y_swa, lse_swa, cqn, r_q, ckvn, r_kv, qcat, kcat, vm,
             y_mla, lse_mla, ys, yn, rs)
    return x2, saved


def _mixer_bwd(tag, dx2, dx2_16, saved, p, tabs):
    (x, h, rstd, q_r, k_r, v, cq, ckv, s5_saved, y_swa, lse_swa, cqn, r_q, ckvn, r_kv, qcat, kcat, vm,
     y_mla, lse_mla, ys, yn, rs) = saved
    T, D = x.shape
    c64, s64, c256, s256 = tabs
    ws = p['ssm_d'].shape[0]
    g_ = {}
    dys, dwo, don = [], [], []
    for g in range(3):
        wg = ys[g].shape[1]
        dyn = _mm_simple(f"{tag}_dyn{g}", dx2_16, p['w_out_g'][g], NT, tm=512, tn=wg, tk=1024)
        dwo.append(_mm_simple(f"{tag}_dwo{g}", yn[g], dx2_16, TN, tm=512, tn=1024, tk=512, out_dtype=BF16))
        dy_g, don_g = _rms_bwd(f"{tag}_rmsob{g}", dyn, ys[g], rs[g], p['out_norm_g'][g])
        dys.append(dy_g)
        don.append(don_g)
    g_['w_out'] = jnp.concatenate(dwo, axis=0)
    g_['out_norm'] = jnp.concatenate(don, axis=1)[0]
    dqcat, dsum = _mla_bwd_q(tag + "_mlabq", qcat, kcat, vm, y_mla, lse_mla, dys[2])
    dkcat, dvm = _mla_bwd_kv(tag + "_mlabkv", qcat, kcat, vm, lse_mla, dsum, dys[2])
    dqpre = _rope(tag + "_ropemqb", dqcat, c256, s256, backward=True)
    dkrs = _kcat_bwd(tag + "_kcatb", dkcat, c256, s256)
    g_['mla_w_uq'] = _unpad_w_uq(_mm_simple(tag + "_dwuq", cqn, dqpre, TN, tm=MLA_Q_RANK, tn=1024, tk=512,
                                            out_dtype=BF16))
    dcqn = _mm_simple(tag + "_dcqn", dqpre, p['w_uq_pad'], NT, tm=512, tn=MLA_Q_RANK, tk=1024)
    dcq, dqn = _rms_bwd(tag + "_rmsqb", dcqn, cq, r_q, p['mla_q_norm'])
    g_['mla_q_norm'] = dqn[0]
    dwk = _mm_simple(tag + "_dwk", ckvn, dkcat, TN, tm=MLA_KV_RANK, tn=1024, tk=512, out_dtype=BF16)
    dwv = _mm_simple(tag + "_dwv", ckvn, dvm, TN, tm=MLA_KV_RANK, tn=1024, tk=512, out_dtype=BF16)
    g_['mla_w_ukv'] = _unpad_w_ukv(dwk, dwv)
    tm = _tile(T, 512)
    nk_, nv_ = MLA_HEADS * MLA_SLOT, MLA_WIDTH
    tkk = _tile(nk_, 1024)
    dckvn_k = _mm_simple(tag + "_dckvk", dkcat, p['w_k_pad'], NT, tm=512, tn=MLA_KV_RANK, tk=tkk)
    dckvn = _mm_simple(tag + "_dckvv", dvm, p['w_v'], NT, tm=512, tn=MLA_KV_RANK, tk=nv_,
                       epilogue=lambda accs, ex: (accs[0] + ex[0],), extras=[dckvn_k])
    dckv, dkvn = _rms_bwd(tag + "_rmskvb", dckvn, ckv, r_kv, p['mla_kv_norm'])
    g_['mla_kv_norm'] = dkvn[0]
    dq_r, dkc, dkp, dvc, dvp, dsinks = _swa_bwd(tag + "_swab", q_r, k_r, v, p['sinks_b'], y_swa, lse_swa, dys[1])
    g_['swa_sinks'] = dsinks[0, :SWA_HEADS]
    dk_r = _shift_add(tag + "_dksum", dkc, dkp)
    dv = _shift_add(tag + "_dvsum", dvc, dvp)
    dq = _rope(tag + "_ropeqb", dq_r, c64, s64, backward=True)
    dk = _rope(tag + "_ropekb", dk_r, c64, s64, backward=True)
    du, s5g, dd, dw_glu, db_glu = _s5_bwd(tag + "_s5b", dys[0], s5_saved, p['s5_prep'], p['ssm_d'], p['ssm_w_glu'])
    g_['ssm_d'], g_['ssm_w_glu'], g_['ssm_b_glu'] = dd[0], dw_glu, db_glu[0]
    _, pull = jax.vjp(_s5_prep, p['ssm_log_dt'], p['ssm_a_re'], p['ssm_a_im'], p['ssm_b_re'], p['ssm_b_im'],
                      p['ssm_c_re'], p['ssm_c_im'])
    for name, val in zip(['ssm_log_dt', 'ssm_a_re', 'ssm_a_im', 'ssm_b_re', 'ssm_b_im', 'ssm_c_re', 'ssm_c_im'],
                         pull(tuple(s5g))):
        g_[name] = val
    dparts = [du, dq, dk, dv, dcq, dckv, dkrs]
    widths = _in_widths(ws)
    tn = _tile(D, 1024)
    pairs = []
    for dp_, wp_, w in zip(dparts, p['w_in_parts'], widths):
        pairs.append((dp_, _bs((tm, w), lambda i, j, k: (i, 0)), wp_, _bs((tn, w), lambda i, j, k: (j, 0)), NT, 0))
    o_spec = _bs((tm, tn), lambda i, j, k: (i, j))
    dh, = _mm(tag + "_dh", (T // tm, D // tn, 1), pairs, [(tm, tn)], [((T, D), F32, o_spec)],
              lambda accs, ex: (accs[0],))
    tmw, tk = _tile(D, 512), _tile(T, 512)
    pairs = []
    for i_, (dp_, w) in enumerate(zip(dparts, widths)):
        pairs.append((h, _bs((tk, tmw), lambda i, j, k: (k, i)), dp_, _bs((tk, w), lambda i, j, k: (k, 0)), TN, i_))
    dws = _mm(tag + "_dwin", (D // tmw, 1, T // tk), pairs, [(tmw, w) for w in widths],
              [((D, w), BF16, _bs((tmw, w), lambda i, j, k: (i, 0))) for w in widths], lambda accs, ex: tuple(accs))
    g_['w_in'] = _unpad_w_in(jnp.concatenate(dws, axis=1), ws)
    dx, dx16, dmix = _rms_bwd(tag + "_rmsb", dh, x, rstd, p['mix_norm'], dres=dx2, with_bf16=True)
    g_['mix_norm'] = dmix[0]
    return dx, dx16, g_


def _final_loss(name, x, target, gain):
    D = x.shape[1]

    def fn(t, b):
        xv = t[0]
        r = lax.rsqrt(jnp.mean(xv * xv, axis=-1, keepdims=True) + EPS)
        xh = xv * r
        err = xh * b[0] - t[1]
        part = 0.5 * jnp.sum(jnp.sum(err * err, axis=-1, keepdims=True), axis=0, keepdims=True) / D
        dy = err / D
        dxh = dy * b[0]
        dx = r * (dxh - xh * jnp.mean(dxh * xh, axis=-1, keepdims=True))
        return (dx, dx), (jnp.broadcast_to(part, (1, 128)), _colsum(dy * xh))
    dx, dx16, part, dg = _rowwise(name, fn, [x, target], [gain.reshape(1, D)], [(D, F32), (D, BF16)], [128, D])
    return part[0, 0], dx, dx16, dg[0]


def _device_step(x, positions, target, n_layers, get_block, final_norm, on_grads):
    tabs = _rope_tables(positions)
    saved = []
    for l in range(n_layers):
        w1 = get_block(l, 'ffn1', x)
        x, s1 = _ffn_fwd("ffn1", x, w1['ffn1_norm'], w1['ffn1_wg2'], w1['ffn1_wu2'], w1['ffn1_wd'])
        p = _layer_prep(get_block(l, 'mix', x))
        x, s2 = _mixer_fwd("mix", x, p, tabs)
        w3 = get_block(l, 'ffn2', x)
        x, s3 = _ffn_fwd("ffn2", x, w3['ffn2_norm'], w3['ffn2_wg2'], w3['ffn2_wu2'], w3['ffn2_wd'])
        saved.append((w1, s1, p, s2, w3, s3))
    loss_part, dx, dx16, dfinal = _final_loss("loss", x, target, final_norm)
    for l in range(n_layers - 1, -1, -1):
        w1, s1, p, s2, w3, s3 = saved[l]
        dx, dx16, gn, gwg, gwu, gwd = _ffn_bwd("ffn2b", dx, dx16, s3, w3['ffn2_norm'], w3['ffn2_wg2'], w3['ffn2_wu2'],
                                               w3['ffn2_wd'])
        dx, dx16 = on_grads(l, 'ffn2', {'ffn2_norm': gn[0], 'ffn2_wg2': gwg, 'ffn2_wu2': gwu, 'ffn2_wd': gwd},
                            (dx, dx16))
        dx, dx16, gm = _mixer_bwd("mixb", dx, dx16, s2, p, tabs)
        dx, dx16 = on_grads(l, 'mix', gm, (dx, dx16))
        dx, dx16, gn, gwg, gwu, gwd = _ffn_bwd("ffn1b", dx, dx16, s1, w1['ffn1_norm'], w1['ffn1_wg2'], w1['ffn1_wu2'],
                                               w1['ffn1_wd'])
        dx, dx16 = on_grads(l, 'ffn1', {'ffn1_norm': gn[0], 'ffn1_wg2': gwg, 'ffn1_wu2': gwu, 'ffn1_wd': gwd},
                            (dx, dx16))
    return loss_part, dx, dfinal


_ANY = pl.BlockSpec(memory_space=pl.ANY)
_CHIP_MASKS = (2, 1, 3)


def _place():
    x, y, c = lax.axis_index("x"), lax.axis_index("y"), lax.axis_index("c")
    chips = [(1 - x, y), (x, 1 - y), (1 - x, 1 - y)]
    return x, y, c, 2 * x + y, chips


_HBM = pl.BlockSpec(memory_space=pltpu.HBM)
_SEMS = pl.BlockSpec(memory_space=pltpu.SEMAPHORE)
_EFFECT = pltpu.SideEffectType.DATAFLOW_SIDE_EFFECTING


def _split_start(name, bufs, sem_sizes, copies):
    n, ns = len(bufs), len(sem_sizes)

    def body(*refs):
        for cp in copies(refs[:n], refs[n:n + ns])[0]:
            cp.start()

    outs = _pcall(
        body, name=name, in_specs=[_HBM] * n,
        out_shape=(*[pltpu.SemaphoreType.DMA((k,)) for k in sem_sizes], *[pltpu.HBM(b.shape, b.dtype) for b in bufs]),
        out_specs=(*[_SEMS] * ns, *[_HBM] * n), input_output_aliases={i: i + ns for i in range(n)},
        compiler_params=pltpu.CompilerParams(has_side_effects=_EFFECT),
    )(*[pltpu.with_memory_space_constraint(b, pltpu.HBM) for b in bufs])
    return list(outs[:ns]), list(outs[ns:])


def _split_wait(name, bufs, sems, after, copies):
    n, ns = len(bufs), len(sems)

    def body(*refs):
        _, sent, landed = copies(refs[:n], refs[n:n + ns])
        for cp in sent:
            cp.wait_send()
        for cp in landed:
            cp.wait_recv()

    return list(_pcall(
        body, name=name, in_specs=[_HBM] * n + [_SEMS] * ns + [_ANY],
        out_shape=tuple(pltpu.HBM(b.shape, b.dtype) for b in bufs), out_specs=tuple([_HBM] * n),
        input_output_aliases={i: i for i in range(n)},
        compiler_params=pltpu.CompilerParams(has_side_effects=_EFFECT),
    )(*bufs, *sems, after))


def _gather_group(bufs, send, recv):
    x, y, c, s, chips = _place()
    start, landed = [], []
    for a, buf in enumerate(bufs):
        for j in range(3):
            k = a * 3 + j
            to = (*chips[j], c)
            start.append(pltpu.make_async_remote_copy(
                src_ref=buf.at[s], dst_ref=buf.at[s], send_sem=send.at[k], recv_sem=recv.at[k],
                device_id=to, device_id_type=MESH))
            theirs = buf.at[s ^ _CHIP_MASKS[j]]
            landed.append(pltpu.make_async_remote_copy(
                src_ref=theirs, dst_ref=theirs, send_sem=send.at[k], recv_sem=recv.at[k],
                device_id=to, device_id_type=MESH))
    return start, start, landed


def _gather_start_copies(n_pass, sizes):
    def copies(refs, sems):
        start, o = [], n_pass
        for g, k in enumerate(sizes):
            start += _gather_group(refs[o:o + k], sems[2 * g], sems[2 * g + 1])[0]
            o += k
        return start, start, []
    return copies


def _gather_wait_copies(refs, sems):
    return _gather_group(refs, sems[0], sems[1])


_N_SLOTS = 7


def _reduce_copies(n_pass, n):
    def copies(refs, sems):
        send, recv = sems
        x, y, c, s, chips = _place()
        contrib, lands = refs[n_pass:n_pass + n], refs[n_pass + n:]
        start, landed = [], []
        for a in range(n):
            half = contrib[a].shape[1] // 2
            base = a * _N_SLOTS
            for j in range(3):
                for h in range(2):
                    start.append(pltpu.make_async_remote_copy(
                        src_ref=contrib[a].at[s ^ _CHIP_MASKS[j], pl.ds(h * half, half)], dst_ref=lands[a].at[2 * j + c],
                        send_sem=send.at[base + 2 * j + h], recv_sem=recv.at[base + 2 * j + c],
                        device_id=(*chips[j], h), device_id_type=MESH))
            start.append(pltpu.make_async_remote_copy(
                src_ref=contrib[a].at[s, pl.ds((1 - c) * half, half)], dst_ref=lands[a].at[6],
                send_sem=send.at[base + 6], recv_sem=recv.at[base + 6],
                device_id=(x, y, 1 - c), device_id_type=MESH))
            for slot in range(_N_SLOTS):
                landed.append(pltpu.make_async_remote_copy(
                    src_ref=lands[a].at[slot], dst_ref=lands[a].at[slot], send_sem=send.at[base + slot],
                    recv_sem=recv.at[base + slot], device_id=(x, y, c), device_id_type=MESH))
        return start, start, landed
    return copies


def _sum_partials(name, contrib, lands):
    ns, half, w = lands.shape
    tm = _tile(half, 256)
    nb = half // tm
    mine = _half_rows(nb)
    shard = lambda: 2 * lax.axis_index("x") + lax.axis_index("y")

    def body(g_ref, *refs):
        o_ref = refs[-1]
        acc = g_ref[...].astype(F32) + refs[6][...].astype(F32)
        for slot in range(6):
            acc = acc + refs[slot][...].astype(F32)
        o_ref[...] = acc

    blk = lambda j: _bs((None, tm, w), lambda i: (j, i, 0))
    return _pcall(
        body, name=name, grid=(nb,),
        in_specs=[_bs((None, tm, w), lambda i: (shard(), mine(i), 0))] + [blk(j) for j in range(_N_SLOTS)],
        out_specs=_bs((tm, w), lambda i: (mine(i), 0)), out_shape=jax.ShapeDtypeStruct((2 * half, w), F32),
        compiler_params=pltpu.CompilerParams(dimension_semantics=("parallel",)),
    )(contrib, *[lands] * _N_SLOTS)


def _stage_shard(name, wl):
    r, c = wl.shape
    tm = _tile(r, 256)
    shard = lambda: 2 * lax.axis_index("x") + lax.axis_index("y")

    def body(w_ref, o_ref):
        o_ref[...] = w_ref[...].astype(o_ref.dtype)

    return _pcall(
        body, name=name, grid=(r // tm,), in_specs=[_bs((tm, c), lambda i: (i, 0))],
        out_specs=_bs((None, tm, c), lambda i: (shard(), i, 0)),
        out_shape=jax.ShapeDtypeStruct((N_SHARD, r, c), BF16),
        compiler_params=pltpu.CompilerParams(dimension_semantics=("parallel",)),
    )(wl)


def _half_rows(nb):
    return lambda i: lax.axis_index("c") * nb + i


def _pair_join_call(name, halves):
    n = len(halves)

    def body(*refs):
        outs = refs[n:2 * n]
        send, recv = refs[2 * n:]
        x, y, c, s, chips = _place()
        cps = []
        for a in range(n):
            half = outs[a].shape[0] // 2
            mine = outs[a].at[pl.ds(c * half, half)]
            cp = pltpu.make_async_remote_copy(
                src_ref=mine, dst_ref=mine, send_sem=send.at[a], recv_sem=recv.at[a],
                device_id=(x, y, 1 - c), device_id_type=MESH)
            cp.start()
            cps.append(cp)
        for a in range(n):
            half = outs[a].shape[0] // 2
            theirs = outs[a].at[pl.ds((1 - c) * half, half)]
            pltpu.make_async_remote_copy(
                src_ref=theirs, dst_ref=theirs, send_sem=send.at[a], recv_sem=recv.at[a],
                device_id=(x, y, 1 - c), device_id_type=MESH).wait_recv()
        for cp in cps:
            cp.wait_send()

    return _pcall(
        body, name=name, in_specs=[_ANY] * n, out_specs=[_ANY] * n,
        out_shape=[jax.ShapeDtypeStruct(h.shape, h.dtype) for h in halves],
        input_output_aliases={a: a for a in range(n)},
        scratch_shapes=[pltpu.SemaphoreType.DMA((n,))] * 2,
    )(*halves)


def _all_sum_small(tag, buf):
    n_dev = 8

    def body(in_ref, out_ref, send, recv, loc):
        x, y, c, s, chips = _place()
        me = 4 * x + 2 * y + c
        lc = pltpu.make_async_copy(in_ref, out_ref.at[me], loc)
        lc.start()
        cps = []
        for k in range(1, n_dev):
            to = (x ^ (k >> 2), y ^ ((k >> 1) & 1), c ^ (k & 1))
            cp = pltpu.make_async_remote_copy(
                src_ref=in_ref, dst_ref=out_ref.at[me], send_sem=send.at[k - 1], recv_sem=recv.at[k - 1],
                device_id=to, device_id_type=MESH)
            cp.start()
            cps.append(cp)
        for k in range(1, n_dev):
            theirs = out_ref.at[me ^ k]
            pltpu.make_async_remote_copy(
                src_ref=theirs, dst_ref=theirs, send_sem=send.at[k - 1], recv_sem=recv.at[k - 1],
                device_id=(x, y, c), device_id_type=MESH).wait_recv()
        for cp in cps:
            cp.wait_send()
        lc.wait()

    allb = _pcall(
        body, name=tag + "_gather", in_specs=[_ANY], out_specs=_ANY,
        out_shape=jax.ShapeDtypeStruct((n_dev,) + buf.shape, buf.dtype),
        scratch_shapes=[pltpu.SemaphoreType.DMA((n_dev - 1,))] * 2 + [pltpu.SemaphoreType.DMA(())],
    )(buf)
    rows = buf.shape[0]
    tm = _tile(rows, 512)

    def sum_body(a_ref, o_ref):
        acc = a_ref[0]
        for k in range(1, n_dev):
            acc = acc + a_ref[k]
        o_ref[...] = acc

    return _pcall(
        sum_body, name=tag + "_sum", grid=(rows // tm,),
        in_specs=[_bs((n_dev, tm, 128), lambda i: (0, i, 0))], out_specs=_bs((tm, 128), lambda i: (i, 0)),
        out_shape=jax.ShapeDtypeStruct(buf.shape, F32),
        compiler_params=pltpu.CompilerParams(dimension_semantics=("parallel",)),
    )(allb)


def _adamw(name, w, g, m, v):
    def fn(t, b):
        wv, gv, mv, vv = t
        m2 = ADAM_B1 * mv + (1.0 - ADAM_B1) * gv
        v2 = ADAM_B2 * vv + (1.0 - ADAM_B2) * (gv * gv)
        m_hat = m2 / (1.0 - ADAM_B1 ** ADAM_STEP)
        v_hat = v2 / (1.0 - ADAM_B2 ** ADAM_STEP)
        delta = -ADAM_LR * (m_hat / (jnp.sqrt(v_hat) + ADAM_EPS) + ADAM_WD * wv)
        return (delta, m2, v2), ()
    wd = w.shape[1]
    return _rowwise(name, fn, [w, g, m, v], [], [(wd, F32)] * 3, tm=256 if wd > 1024 else 512)


_WEIGHTS = ['ffn1_norm', 'ffn1_w_gate', 'ffn1_w_up', 'ffn1_w_down', 'mix_norm', 'w_in', 'ssm_log_dt', 'ssm_a_re',
            'ssm_a_im', 'ssm_b_re', 'ssm_b_im', 'ssm_c_re', 'ssm_c_im', 'ssm_d', 'ssm_w_glu', 'ssm_b_glu',
            'swa_sinks', 'mla_q_norm', 'mla_w_uq', 'mla_kv_norm', 'mla_w_ukv', 'out_norm', 'w_out', 'ffn2_norm',
            'ffn2_w_gate', 'ffn2_w_up', 'ffn2_w_down', 'final_norm']
_COL_SHARDED = ['ffn1_w_gate', 'ffn1_w_up', 'w_in', 'mla_w_uq', 'mla_w_ukv', 'ffn2_w_gate', 'ffn2_w_up']
_ROW_SHARDED = ['ffn1_w_down', 'ssm_w_glu', 'w_out', 'ffn2_w_down']
_STACKED = {'ffn1_w_gate': 'ffn1_wg2', 'ffn1_w_up': 'ffn1_wu2', 'ffn2_w_gate': 'ffn2_wg2', 'ffn2_w_up': 'ffn2_wu2'}
_RENAMED = {'ffn1_w_down': 'ffn1_wd', 'ffn2_w_down': 'ffn2_wd'}
_BLOCKS = ('ffn1', 'mix', 'ffn2')
_BLOCK_BIG = {'ffn1': ['ffn1_w_gate', 'ffn1_w_up', 'ffn1_w_down'],
              'mix': ['w_in', 'ssm_w_glu', 'mla_w_uq', 'mla_w_ukv', 'w_out'],
              'ffn2': ['ffn2_w_gate', 'ffn2_w_up', 'ffn2_w_down']}
_BLOCK_SMALL = {'ffn1': ['ffn1_norm'],
                'mix': ['mix_norm', 'ssm_log_dt', 'ssm_a_re', 'ssm_a_im', 'ssm_b_re', 'ssm_b_im', 'ssm_c_re', 'ssm_c_im',
                        'ssm_d', 'ssm_b_glu', 'swa_sinks', 'mla_q_norm', 'mla_kv_norm', 'out_norm'],
                'ffn2': ['ffn2_norm']}
_BIG = [n for b in _BLOCKS for n in _BLOCK_BIG[b]]
_SMALL = [n for n in _WEIGHTS if n not in _BIG]


def _pack(vals):
    flat = jnp.concatenate([v.reshape(-1) for v in vals])
    pad = (-flat.shape[0]) % 1024
    return jnp.pad(flat, (0, pad)).reshape(-1, 128)


def _unpack(buf, like):
    flat = buf.reshape(-1)
    out, o = [], 0
    for v in like:
        out.append(flat[o:o + v.size].reshape(v.shape))
        o += v.size
    return out


def kernel(x, positions, ffn1_norm, ffn1_w_gate, ffn1_w_up, ffn1_w_down, mix_norm, w_in, ssm_log_dt, ssm_a_re, ssm_a_im, ssm_b_re, ssm_b_im, ssm_c_re, ssm_c_im, ssm_d, ssm_w_glu, ssm_b_glu, swa_sinks, mla_q_norm, mla_w_uq, mla_kv_norm, mla_w_ukv, out_norm, w_out, ffn2_norm, ffn2_w_gate, ffn2_w_up, ffn2_w_down, final_norm, loss_target, m_ffn1_norm, m_ffn1_w_gate, m_ffn1_w_up, m_ffn1_w_down, m_mix_norm, m_w_in, m_ssm_log_dt, m_ssm_a_re, m_ssm_a_im, m_ssm_b_re, m_ssm_b_im, m_ssm_c_re, m_ssm_c_im, m_ssm_d, m_ssm_w_glu, m_ssm_b_glu, m_swa_sinks, m_mla_q_norm, m_mla_w_uq, m_mla_kv_norm, m_mla_w_ukv, m_out_norm, m_w_out, m_ffn2_norm, m_ffn2_w_gate, m_ffn2_w_up, m_ffn2_w_down, m_final_norm, v_ffn1_norm, v_ffn1_w_gate, v_ffn1_w_up, v_ffn1_w_down, v_mix_norm, v_w_in, v_ssm_log_dt, v_ssm_a_re, v_ssm_a_im, v_ssm_b_re, v_ssm_b_im, v_ssm_c_re, v_ssm_c_im, v_ssm_d, v_ssm_w_glu, v_ssm_b_glu, v_swa_sinks, v_mla_q_norm, v_mla_w_uq, v_mla_kv_norm, v_mla_w_ukv, v_out_norm, v_w_out, v_ffn2_norm, v_ffn2_w_gate, v_ffn2_w_up, v_ffn2_w_down, v_final_norm):
    w = dict(zip(_WEIGHTS, (ffn1_norm, ffn1_w_gate, ffn1_w_up, ffn1_w_down, mix_norm, w_in, ssm_log_dt, ssm_a_re, ssm_a_im, ssm_b_re, ssm_b_im, ssm_c_re, ssm_c_im, ssm_d, ssm_w_glu, ssm_b_glu, swa_sinks, mla_q_norm, mla_w_uq, mla_kv_norm, mla_w_ukv, out_norm, w_out, ffn2_norm, ffn2_w_gate, ffn2_w_up, ffn2_w_down, final_norm)))
    m = dict(zip(_WEIGHTS, (m_ffn1_norm, m_ffn1_w_gate, m_ffn1_w_up, m_ffn1_w_down, m_mix_norm, m_w_in, m_ssm_log_dt, m_ssm_a_re, m_ssm_a_im, m_ssm_b_re, m_ssm_b_im, m_ssm_c_re, m_ssm_c_im, m_ssm_d, m_ssm_w_glu, m_ssm_b_glu, m_swa_sinks, m_mla_q_norm, m_mla_w_uq, m_mla_kv_norm, m_mla_w_ukv, m_out_norm, m_w_out, m_ffn2_norm, m_ffn2_w_gate, m_ffn2_w_up, m_ffn2_w_down, m_final_norm)))
    v = dict(zip(_WEIGHTS, (v_ffn1_norm, v_ffn1_w_gate, v_ffn1_w_up, v_ffn1_w_down, v_mix_norm, v_w_in, v_ssm_log_dt, v_ssm_a_re, v_ssm_a_im, v_ssm_b_re, v_ssm_b_im, v_ssm_c_re, v_ssm_c_im, v_ssm_d, v_ssm_w_glu, v_ssm_b_glu, v_swa_sinks, v_mla_q_norm, v_mla_w_uq, v_mla_kv_norm, v_mla_w_ukv, v_out_norm, v_w_out, v_ffn2_norm, v_ffn2_w_gate, v_ffn2_w_up, v_ffn2_w_down, v_final_norm)))
    n_layers = ffn1_norm.shape[0]

    sizes = [len(_BLOCK_BIG[b]) for b in _BLOCKS]

    x0 = x[0]
    travelling = []
    for l in range(n_layers):
        staged = [_stage_shard("stage_" + n, w[n][l]) for n in _BIG]
        sems, thru = _split_start(f"gstart{l}", [x0] + staged, [3 * k for k in sizes for _ in range(2)],
                                  _gather_start_copies(1, sizes))
        x0 = thru[0]
        travelling.append((sems, thru[1:]))

    def get_block(l, blk, x_now):
        g = _BLOCKS.index(blk)
        sems, staged = travelling[l]
        o = sum(sizes[:g])
        full = _split_wait(f"gwait{l}{blk}", staged[o:o + sizes[g]], sems[2 * g:2 * g + 2], x_now,
                           _gather_wait_copies)
        lw = {n: w[n][l] for n in _BLOCK_SMALL[blk]}
        for n, gth in zip(_BLOCK_BIG[blk], full):
            ns, r, c = gth.shape
            if n in _STACKED:
                lw[_STACKED[n]] = gth.reshape(ns * r, c)
            elif n in _ROW_SHARDED:
                lw[_RENAMED.get(n, n)] = gth.reshape(ns * r, c)
            else:
                lw[n] = jnp.moveaxis(gth, 0, 1).reshape(r, ns * c)
        return lw

    reduced = {n: [None] * n_layers for n in _BIG}
    grads = [dict() for _ in range(n_layers)]
    pending = []

    def finish(after):
        l, blk, sems, contrib, lands = pending.pop()
        k = len(contrib)
        done = _split_wait(f"cwait{l}{blk}", contrib + lands, sems, after, _reduce_copies(0, k))
        halves = [_sum_partials("rs_sum", g, q) for g, q in zip(done[:k], done[k:])]
        for n, g in zip(_BLOCK_BIG[blk], _pair_join_call("rs_join", halves)):
            reduced[n][l] = g

    def on_grads(l, blk, g_, dxs):
        if pending:
            finish(g_[_BLOCK_SMALL[blk][0]])
        for n in _BLOCK_SMALL[blk]:
            grads[l][n] = g_[n]
        contrib = []
        for n in _BLOCK_BIG[blk]:
            r, c = w[n].shape[1:]
            if n in _STACKED:
                g = g_[_STACKED[n]].reshape(N_SHARD, r, c)
            elif n in _ROW_SHARDED:
                g = g_[_RENAMED.get(n, n)].reshape(N_SHARD, r, c)
            else:
                g = jnp.moveaxis(g_[n].reshape(r, N_SHARD, c), 1, 0)
            contrib.append(g)
        k = len(contrib)
        lands = [lax.empty((_N_SLOTS, g.shape[1] // 2, g.shape[2]), g.dtype) for g in contrib]
        sems, thru = _split_start(f"cstart{l}{blk}", list(dxs) + contrib + lands, [_N_SLOTS * k] * 2,
                                  _reduce_copies(2, k))
        pending.append((l, blk, sems, thru[2:2 + k], thru[2 + k:]))
        return thru[0], thru[1]

    loss_part, dx, dfinal = _device_step(x0, positions[0], loss_target[0], n_layers, get_block, final_norm, on_grads)
    finish(dx)
    loss = lax.psum(loss_part, ("x", "y", "c"))
    grad = {n: jnp.stack(reduced[n]) for n in _BIG}

    small_like = [w[n] for n in _SMALL]
    small_g = [jnp.stack([grads[l][n] for l in range(n_layers)]) for n in _SMALL if n != 'final_norm'] + [dfinal]
    g_small = _all_sum_small("small", _pack(small_g))
    d_small, m_small, v_small = _adamw("adam_small", _pack(small_like), g_small, _pack([m[n] for n in _SMALL]),
                                       _pack([v[n] for n in _SMALL]))
    delta, new_m, new_v = {}, {}, {}
    for n, gv, dv_, mv, vv in zip(_SMALL, _unpack(g_small, small_like), _unpack(d_small, small_like),
                                  _unpack(m_small, small_like), _unpack(v_small, small_like)):
        grad[n], delta[n], new_m[n], new_v[n] = gv, dv_, mv, vv
    for n in _BIG:
        shp = w[n].shape
        two = lambda t: t.reshape(shp[0] * shp[1], shp[2])
        d2, m2, v2 = _adamw("adam_" + n, two(w[n]), two(grad[n]), two(m[n]), two(v[n]))
        delta[n], new_m[n], new_v[n] = d2.reshape(shp), m2.reshape(shp), v2.reshape(shp)
    return (loss, dx[None], *[grad[n] for n in _WEIGHTS], *[delta[n] for n in _WEIGHTS],
            *[new_m[n] for n in _WEIGHTS], *[new_v[n] for n in _WEIGHTS])
```

```python
import functools
import math

import jax
import jax.numpy as jnp
import numpy as np
from jax import lax
from jax.experimental import pallas as pl
from jax.experimental.pallas import tpu as pltpu

F32 = jnp.float32
BF16 = jnp.bfloat16

EPS = 1e-6
ROPE_THETA = 10000.0
SSM_GROUP = 16
SSM_STATE = 64
S5_TILE = 8
SWA_HEADS = 8
SWA_KV_HEADS = 2
SWA_HEAD_DIM = 64
SWA_BLOCK = 128
SWA_WIDTH = SWA_HEADS * SWA_HEAD_DIM
SWA_KV_WIDTH = SWA_KV_HEADS * SWA_HEAD_DIM
MLA_HEADS = 8
MLA_Q_RANK = 512
MLA_KV_RANK = 256
MLA_NOPE = 128
MLA_ROPE = 64
MLA_V = 128
MLA_SLOT = 256
MLA_WIDTH = MLA_HEADS * MLA_V
ROPE_HALF = 32

ADAM_LR = 0.001
ADAM_B1 = 0.9
ADAM_B2 = 0.999
ADAM_EPS = 1e-08
ADAM_WD = 0.01
ADAM_STEP = 10

N_SHARD = 4
MESH = pl.DeviceIdType.MESH

NN = ((1,), (0,))
NT = ((1,), (1,))
TN = ((0,), (0,))


def _pcall(body, **kw):
    return pl.pallas_call(body, **kw)


def _tile(n, want, align=16):
    if n <= want:
        return n
    t = want - want % align
    while t >= align:
        if n % t == 0:
            return t
        t -= align
    return n


def _mm(name, grid, pairs, acc_shapes, outs, epilogue, extras=()):
    nk = grid[2]
    n_p, n_e, n_o, n_a = len(pairs), len(extras), len(outs), len(acc_shapes)
    dims_idx = [(p[4], p[5]) for p in pairs]

    def body(*refs):
        ab = refs[:2 * n_p]
        ex = refs[2 * n_p:2 * n_p + n_e]
        o = refs[2 * n_p + n_e:2 * n_p + n_e + n_o]
        accs = refs[2 * n_p + n_e + n_o:]

        def partial_sums():
            sums = [None] * n_a
            for p, (dims, ai) in enumerate(dims_idx):
                a = ab[2 * p][...].astype(BF16)
                b = ab[2 * p + 1][...].astype(BF16)
                d = lax.dot_general(a, b, (dims, ((), ())), preferred_element_type=F32)
                sums[ai] = d if sums[ai] is None else sums[ai] + d
            return sums

        def finish(vals):
            res = epilogue(vals, [e[...] for e in ex])
            for r, oref in zip(res, o):
                oref[...] = r.astype(oref.dtype)

        if nk == 1:
            finish(partial_sums())
        else:
            k = pl.program_id(2)

            @pl.when(k == 0)
            def _():
                for acc in accs:
                    acc[...] = jnp.zeros_like(acc)

            for acc, s in zip(accs, partial_sums()):
                acc[...] += s

            @pl.when(k == nk - 1)
            def _():
                finish([acc[...] for acc in accs])

    in_arrays, in_specs = [], []
    for a, a_spec, b, b_spec, _, _ in pairs:
        in_arrays += [a, b]
        in_specs += [a_spec, b_spec]
    for e, e_spec in extras:
        in_arrays.append(e)
        in_specs.append(e_spec)
    res = _pcall(
        body, name=name, grid=grid, in_specs=in_specs,
        out_specs=[o[2] for o in outs],
        out_shape=[jax.ShapeDtypeStruct(o[0], o[1]) for o in outs],
        scratch_shapes=[] if nk == 1 else [pltpu.VMEM(s, F32) for s in acc_shapes],
        compiler_params=pltpu.CompilerParams(dimension_semantics=("parallel", "parallel", "arbitrary")),
    )(*in_arrays)
    return res


def _bs(shape, fn):
    return pl.BlockSpec(shape, fn)


def _mm_simple(name, a, b, dims, *, tm=512, tn=512, tk=512, out_dtype=F32, epilogue=None, extras=(), scale=None):
    if dims == NN:
        (M, K), N = a.shape, b.shape[1]
    elif dims == NT:
        (M, K), N = a.shape, b.shape[0]
    else:
        (K, M), N = a.shape, b.shape[1]
    tm, tn, tk = _tile(M, tm, 128), _tile(N, tn, 128), _tile(K, tk, 128)
    if dims == NN:
        a_spec, b_spec = _bs((tm, tk), lambda i, j, k: (i, k)), _bs((tk, tn), lambda i, j, k: (k, j))
    elif dims == NT:
        a_spec, b_spec = _bs((tm, tk), lambda i, j, k: (i, k)), _bs((tn, tk), lambda i, j, k: (j, k))
    else:
        a_spec, b_spec = _bs((tk, tm), lambda i, j, k: (k, i)), _bs((tk, tn), lambda i, j, k: (k, j))
    ex = []
    for e in extras:
        if e.shape[0] == 1:
            ex.append((e, _bs((1, tn), lambda i, j, k: (0, j))))
        else:
            ex.append((e, _bs((tm, tn), lambda i, j, k: (i, j))))
    if epilogue is None:
        if scale is None:
            epilogue = lambda accs, ex_: (accs[0],)
        else:
            epilogue = lambda accs, ex_: (accs[0] * scale,)
        out_dtypes = (out_dtype,)
    else:
        out_dtypes = out_dtype if isinstance(out_dtype, tuple) else (out_dtype,)
    outs = [((M, N), dt, _bs((tm, tn), lambda i, j, k: (i, j))) for dt in out_dtypes]
    res = _mm(name, (M // tm, N // tn, K // tk), [(a, a_spec, b, b_spec, dims, 0)], [(tm, tn)], outs, epilogue, ex)
    return res[0] if len(res) == 1 else res


def _rowwise(name, fn, tiles, bcasts, outs, accs=(), *, tm=256):
    rows = tiles[0].shape[0]
    tm = _tile(rows, tm)
    n_t, n_b, n_o, n_a = len(tiles), len(bcasts), len(outs), len(accs)

    def body(*refs):
        t = [r[...] for r in refs[:n_t]]
        b = [r[...] for r in refs[n_t:n_t + n_b]]
        o = refs[n_t + n_b:n_t + n_b + n_o]
        a = refs[n_t + n_b + n_o:]
        ov, av = fn(t, b)
        for r, val in zip(o, ov):
            r[...] = val.astype(r.dtype)
        if n_a:
            @pl.when(pl.program_id(0) == 0)
            def _():
                for r in a:
                    r[...] = jnp.zeros_like(r)
            for r, val in zip(a, av):
                r[...] += val

    in_specs = [_bs((tm, x.shape[1]), lambda i: (i, 0)) for x in tiles]
    in_specs += [_bs(x.shape, lambda i, nd=x.ndim: (0,) * nd) for x in bcasts]
    out_specs = [_bs((tm, w), lambda i: (i, 0)) for w, _ in outs]
    out_specs += [_bs((1, w), lambda i: (0, 0)) for w in accs]
    out_shape = [jax.ShapeDtypeStruct((rows, w), dt) for w, dt in outs]
    out_shape += [jax.ShapeDtypeStruct((1, w), F32) for w in accs]
    return _pcall(
        body, name=name, grid=(rows // tm,), in_specs=in_specs, out_specs=out_specs, out_shape=out_shape,
        compiler_params=pltpu.CompilerParams(dimension_semantics=("arbitrary",)),
    )(*tiles, *bcasts)


def _colsum(v):
    return jnp.sum(v, axis=0, keepdims=True)


def _rms_fwd(name, x, gain, out_dtype=BF16):
    def fn(t, b):
        xv = t[0]
        r = lax.rsqrt(jnp.mean(xv * xv, axis=-1, keepdims=True) + EPS)
        return (xv * r * b[0], r), ()
    w = x.shape[1]
    return _rowwise(name, fn, [x], [gain.reshape(1, w)], [(w, out_dtype), (1, F32)])


def _rms_bwd(name, dh, x, rstd, gain, dres=None, with_bf16=False):
    def fn(t, b):
        dhv, xv, r = t[0], t[1], t[2]
        xh = xv * r
        dxh = dhv * b[0]
        dx = r * (dxh - xh * jnp.mean(dxh * xh, axis=-1, keepdims=True))
        if dres is not None:
            dx = dx + t[3]
        return ((dx, dx) if with_bf16 else (dx,)), (_colsum(dhv * xh),)
    w = x.shape[1]
    tiles = [dh, x, rstd] + ([dres] if dres is not None else [])
    outs = [(w, F32), (w, BF16)] if with_bf16 else [(w, F32)]
    return _rowwise(name, fn, tiles, [gain.reshape(1, w)], outs, [w])


def _sigmoid(v):
    return 1.0 / (1.0 + jnp.exp(-v))


def _ffn_fwd(tag, x, gain, wg2, wu2, wd):
    T, D = x.shape
    fs = wg2.shape[1]
    h, rstd = _rms_fwd(tag + "_rms", x, gain)
    tm = _tile(T, 256)
    a_spec = _bs((tm, D), lambda i, j, k: (j, 0))
    w_spec = _bs((D, fs), lambda i, j, k: (i, 0))
    o_spec = _bs((tm, fs), lambda i, j, k: (j, i))

    def epi(accs, ex):
        a, b = accs
        return a, b, a * _sigmoid(a) * b

    a, b, t = _mm(
        tag + "_up", (N_SHARD, T // tm, 1),
        [(h, a_spec, wg2, w_spec, NN, 0), (h, a_spec, wu2, w_spec, NN, 1)],
        [(tm, fs), (tm, fs)],
        [((T, N_SHARD * fs), BF16, o_spec)] * 3,
        epi)
    y = _mm_simple(tag + "_down", t, wd, NN, tm=512, tn=D, tk=fs // 2 if fs % 256 == 0 else fs,
                   epilogue=lambda accs, ex: (ex[0] + 0.5 * accs[0],), extras=[x])
    return y, (x, h, rstd, a, b, t)


def _ffn_bwd(tag, dy, dy16, saved, gain, wg2, wu2, wd):
    x, h, rstd, a, b, t = saved
    T, D = x.shape
    fs = wg2.shape[1]
    F = N_SHARD * fs
    dwd = _mm_simple(tag + "_dwd", t, dy16, TN, tm=fs, tn=D, tk=512, scale=0.5, out_dtype=BF16)

    def epi(accs, ex):
        dt = 0.5 * accs[0]
        av, bv = ex[0].astype(F32), ex[1].astype(F32)
        sig = _sigmoid(av)
        da = dt * bv * (sig * (1.0 + av * (1.0 - sig)))
        db = dt * (av * sig)
        return da, db

    tn = 512 if F % 512 == 0 else (256 if F % 256 == 0 else 128)
    da, db = _mm_simple(tag + "_dt", dy16, wd, NT, tm=1024, tn=tn, tk=D, out_dtype=(BF16, BF16), epilogue=epi,
                        extras=[a, b])
    tm = _tile(D, 1024)
    per = D // tm
    tk = _tile(T, 512)
    h_spec = _bs((tk, tm), lambda i, j, k: (k, i % per))
    d_spec = _bs((tk, fs), lambda i, j, k: (k, i // per))
    o_spec = _bs((tm, fs), lambda i, j, k: (i, 0))
    dwg2, dwu2 = _mm(
        tag + "_dwgu", (N_SHARD * per, 1, T // tk),
        [(h, h_spec, da, d_spec, TN, 0), (h, h_spec, db, d_spec, TN, 1)],
        [(tm, fs), (tm, fs)],
        [((N_SHARD * D, fs), BF16, o_spec), ((N_SHARD * D, fs), BF16, o_spec)],
        lambda accs, ex: tuple(accs))
    tm2 = _tile(T, 512)
    tn2 = _tile(D, 1024)
    per2 = D // tn2
    g_spec = _bs((tm2, fs), lambda i, j, k: (i, k))
    w_spec = _bs((tn2, fs), lambda i, j, k: (k * per2 + j, 0))
    o2 = _bs((tm2, tn2), lambda i, j, k: (i, j))
    dh, = _mm(
        tag + "_dh", (T // tm2, per2, N_SHARD),
        [(da, g_spec, wg2, w_spec, NT, 0), (db, g_spec, wu2, w_spec, NT, 0)],
        [(tm2, tn2)], [((T, D), F32, o2)], lambda accs, ex: (accs[0],))
    dx, dx16, dgain = _rms_bwd(tag + "_rmsb", dh, x, rstd, gain, dres=dy, with_bf16=True)
    return dx, dx16, dgain, dwg2, dwu2, dwd


def _rope_tables(positions):
    inv_freq = ROPE_THETA ** (-jnp.arange(0, 2 * ROPE_HALF, 2, dtype=F32) / (2 * ROPE_HALF))
    ang = positions.astype(F32)[:, None] * inv_freq
    c, s = jnp.cos(ang), jnp.sin(ang)
    c64 = jnp.concatenate([c, c], axis=1)
    s64 = jnp.concatenate([-s, s], axis=1)
    T = positions.shape[0]
    one, zero = jnp.ones((T, MLA_NOPE), F32), jnp.zeros((T, MLA_NOPE), F32)
    pad1, pad0 = jnp.ones((T, MLA_SLOT - MLA_NOPE - MLA_ROPE), F32), jnp.zeros((T, MLA_SLOT - MLA_NOPE - MLA_ROPE), F32)
    c256 = jnp.concatenate([one, c64, pad1], axis=1)
    s256 = jnp.concatenate([zero, s64, pad0], axis=1)
    return c64, s64, c256, s256


def _half_swap(v):
    w = v.shape[1]
    lane = lax.broadcasted_iota(jnp.int32, v.shape, 1)
    first = (lane % (2 * ROPE_HALF)) < ROPE_HALF
    return jnp.where(first, pltpu.roll(v, w - ROPE_HALF, 1), pltpu.roll(v, ROPE_HALF, 1))


def _rope(name, x, ctab, stab, backward=False, out_dtype=F32):
    reps = x.shape[1] // ctab.shape[1]

    def fn(t, b):
        xv = t[0]
        c = jnp.tile(t[1], (1, reps)) if reps > 1 else t[1]
        s = jnp.tile(t[2], (1, reps)) if reps > 1 else t[2]
        if backward:
            return (xv * c + _half_swap(xv * s),), ()
        return (xv * c + _half_swap(xv) * s,), ()
    return _rowwise(name, fn, [x, ctab, stab], [], [(x.shape[1], out_dtype)])[0]


def _s5_prep(log_dt, a_re, a_im, b_re, b_im, c_re, c_im):
    G, P = a_re.shape
    C = b_re.shape[-1]
    dt = jnp.exp(log_dt)[:, None]
    mag = jnp.exp(a_re * dt)
    abar_r = mag * jnp.cos(a_im * dt)
    abar_i = mag * jnp.sin(a_im * dt)
    den = a_re * a_re + a_im * a_im
    nr = abar_r - 1.0
    qr = (nr * a_re + abar_i * a_im) / den
    qi = (abar_i * a_re - nr * a_im) / den
    bbar_r = qr[..., None] * b_re - qi[..., None] * b_im
    bbar_i = qr[..., None] * b_im + qi[..., None] * b_re
    nt = G // S5_TILE
    eye = jnp.eye(S5_TILE, dtype=F32)
    tiles_b = lambda t: jnp.einsum('tgpc,gh->tgchp', t.reshape(nt, S5_TILE, P, C), eye).reshape(
        nt, S5_TILE * C, S5_TILE * P)
    tiles_c = lambda t: jnp.einsum('tgcp,gh->tgphc', t.reshape(nt, S5_TILE, C, P), eye).reshape(
        nt, S5_TILE * P, S5_TILE * C)
    return (abar_r.reshape(1, G * P), abar_i.reshape(1, G * P), tiles_b(bbar_r), tiles_b(bbar_i), tiles_c(c_re),
            -tiles_c(c_im))


def _scan_lanes(gp):
    for tl in (1024, 512, 256):
        if gp % tl == 0:
            return tl
    return 128


def _scan_fwd(name, bu_r, bu_i, a_r, a_i):
    T, gp = bu_r.shape
    tl = _scan_lanes(gp)
    tc = _tile(T, 256)

    def body(br_ref, bi_ref, ar_ref, ai_ref, xr_ref, xi_ref, pr_ref, pi_ref, sr, si):
        @pl.when(pl.program_id(1) == 0)
        def _():
            sr[...] = jnp.zeros_like(sr)
            si[...] = jnp.zeros_like(si)
        ar, ai = ar_ref[...], ai_ref[...]

        def step(g, carry):
            xr, xi = carry
            base = pl.multiple_of(g * 8, 8)
            b_r = br_ref[pl.ds(base, 8), :]
            b_i = bi_ref[pl.ds(base, 8), :]
            rows_r, rows_i, prev_r, prev_i = [], [], [], []
            for j in range(8):
                prev_r.append(xr)
                prev_i.append(xi)
                nr = ar * xr - ai * xi + b_r[j:j + 1, :]
                ni = ar * xi + ai * xr + b_i[j:j + 1, :]
                xr, xi = nr, ni
                rows_r.append(xr)
                rows_i.append(xi)
            xr_ref[pl.ds(base, 8), :] = jnp.concatenate(rows_r, axis=0)
            xi_ref[pl.ds(base, 8), :] = jnp.concatenate(rows_i, axis=0)
            pr_ref[pl.ds(base, 8), :] = jnp.concatenate(prev_r, axis=0)
            pi_ref[pl.ds(base, 8), :] = jnp.concatenate(prev_i, axis=0)
            return xr, xi

        xr, xi = lax.fori_loop(0, tc // 8, step, (sr[...], si[...]))
        sr[...] = xr
        si[...] = xi

    spec = _bs((tc, tl), lambda l, t: (t, l))
    a_spec = _bs((1, tl), lambda l, t: (0, l))
    return _pcall(
        body, name=name, grid=(gp // tl, T // tc),
        in_specs=[spec, spec, a_spec, a_spec], out_specs=[spec] * 4,
        out_shape=[jax.ShapeDtypeStruct((T, gp), F32)] * 4,
        scratch_shapes=[pltpu.VMEM((1, tl), F32), pltpu.VMEM((1, tl), F32)],
        compiler_params=pltpu.CompilerParams(dimension_semantics=("parallel", "arbitrary")),
    )(bu_r, bu_i, a_r, a_i)


def _scan_bwd(name, g_r, g_i, p_r, p_i, a_r, a_i):
    T, gp = g_r.shape
    tl = _scan_lanes(gp)
    tc = _tile(T, 256)
    nt = T // tc
    ng = tc // 8

    def body(gr_ref, gi_ref, pr_ref, pi_ref, ar_ref, ai_ref, lr_ref, li_ref, dar_ref, dai_ref, sr, si, accr, acci):
        t = pl.program_id(1)

        @pl.when(t == 0)
        def _():
            sr[...] = jnp.zeros_like(sr)
            si[...] = jnp.zeros_like(si)
            accr[...] = jnp.zeros_like(accr)
            acci[...] = jnp.zeros_like(acci)
        ar, ai = ar_ref[...], ai_ref[...]

        def step(kk, carry):
            lr, li = carry
            base = pl.multiple_of((ng - 1 - kk) * 8, 8)
            gr8 = gr_ref[pl.ds(base, 8), :]
            gi8 = gi_ref[pl.ds(base, 8), :]
            rows_r, rows_i = [None] * 8, [None] * 8
            for j in range(7, -1, -1):
                nr = gr8[j:j + 1, :] + ar * lr + ai * li
                ni = gi8[j:j + 1, :] - ai * lr + ar * li
                lr, li = nr, ni
                rows_r[j] = lr
                rows_i[j] = li
            lam_r = jnp.concatenate(rows_r, axis=0)
            lam_i = jnp.concatenate(rows_i, axis=0)
            lr_ref[pl.ds(base, 8), :] = lam_r
            li_ref[pl.ds(base, 8), :] = lam_i
            pr8 = pr_ref[pl.ds(base, 8), :]
            pi8 = pi_ref[pl.ds(base, 8), :]
            accr[...] += lam_r * pr8 + lam_i * pi8
            acci[...] += lam_i * pr8 - lam_r * pi8
            return lr, li

        lr, li = lax.fori_loop(0, ng, step, (sr[...], si[...]))
        sr[...] = lr
        si[...] = li

        @pl.when(t == nt - 1)
        def _():
            dar_ref[...] = jnp.sum(accr[...], axis=0, keepdims=True)
            dai_ref[...] = jnp.sum(acci[...], axis=0, keepdims=True)

    spec = _bs((tc, tl), lambda l, t: (nt - 1 - t, l))
    a_spec = _bs((1, tl), lambda l, t: (0, l))
    return _pcall(
        body, name=name, grid=(gp // tl, nt),
        in_specs=[spec] * 4 + [a_spec, a_spec], out_specs=[spec, spec, a_spec, a_spec],
        out_shape=[jax.ShapeDtypeStruct((T, gp), F32)] * 2 + [jax.ShapeDtypeStruct((1, gp), F32)] * 2,
        scratch_shapes=[pltpu.VMEM((1, tl), F32), pltpu.VMEM((1, tl), F32),
                        pltpu.VMEM((8, tl), F32), pltpu.VMEM((8, tl), F32)],
        compiler_params=pltpu.CompilerParams(dimension_semantics=("parallel", "arbitrary")),
    )(g_r, g_i, p_r, p_i, a_r, a_i)


_GELU_C = math.sqrt(2.0 / math.pi)


def _gelu(v):
    return 0.5 * v * (1.0 + jnp.tanh(_GELU_C * (v + 0.044715 * v * v * v)))


def _gelu_grad(v):
    th = jnp.tanh(_GELU_C * (v + 0.044715 * v * v * v))
    return 0.5 * (1.0 + th) + 0.5 * v * (1.0 - th * th) * _GELU_C * (1.0 + 3.0 * 0.044715 * v * v)


def _s5_specs(T, nt, cu, cs):
    tm = _tile(T, 512)
    return dict(
        tm=tm,
        chan=_bs((tm, cu), lambda i, j, k: (i, j)), state=_bs((tm, cs), lambda i, j, k: (i, j)),
        b=_bs((None, cu, cs), lambda i, j, k: (j, 0, 0)), c=_bs((None, cs, cu), lambda i, j, k: (j, 0, 0)),
        row=_bs((1, cu), lambda i, j, k: (0, j)))


def _s5_fwd(tag, u, prep, d_skip, w_glu, b_glu):
    a_r, a_i, b_r, b_i, c_r, c_in = prep
    T, W = u.shape
    gp = a_r.shape[1]
    nt, cu, cs = b_r.shape
    sp = _s5_specs(T, nt, cu, cs)
    tm = sp['tm']
    bu_r, bu_i = _mm(tag + "_bu", (T // tm, nt, 1),
                     [(u, sp['chan'], b_r, sp['b'], NN, 0), (u, sp['chan'], b_i, sp['b'], NN, 1)],
                     [(tm, cs)] * 2, [((T, gp), F32, sp['state'])] * 2, lambda accs, ex: tuple(accs))
    x_r, x_i, p_r, p_i = _scan_fwd(tag + "_scan", bu_r, bu_i, a_r, a_i)

    def epi(accs, ex):
        y = accs[0] + ex[1] * ex[0]
        return y, _gelu(y)

    ypre, yg = _mm(tag + "_y", (T // tm, nt, 1),
                   [(x_r, sp['state'], c_r, sp['c'], NN, 0), (x_i, sp['state'], c_in, sp['c'], NN, 0)],
                   [(tm, cu)], [((T, W), F32, sp['chan'])] * 2, epi,
                   [(u, sp['chan']), (d_skip.reshape(1, W), sp['row'])])

    def epi2(accs, ex):
        pre = accs[0] + ex[1]
        return ex[0] * _sigmoid(pre), pre

    out, pre = _mm_simple(tag + "_glu", yg, w_glu, NN, tm=512, tn=W, tk=W, out_dtype=(F32, F32), epilogue=epi2,
                          extras=[yg, b_glu.reshape(1, W)])
    return out, (u, x_r, x_i, p_r, p_i, ypre, yg, pre)


def _s5_bwd(tag, d_out, saved, prep, d_skip, w_glu):
    u, x_r, x_i, p_r, p_i, ypre, yg, pre = saved
    a_r, a_i, b_r, b_i, c_r, c_in = prep
    T, W = u.shape

    def gate_fn(t, b):
        gate = _sigmoid(t[2])
        dpre = t[0] * t[1] * gate * (1.0 - gate)
        return (dpre, t[0] * gate), (_colsum(dpre),)

    dpre, tmp, db_glu = _rowwise(tag + "_gateb", gate_fn, [d_out, yg, pre], [], [(W, F32), (W, F32)], [W])
    dw_glu = _mm_simple(tag + "_dwglu", yg, dpre, TN, tm=W, tn=W, tk=512, out_dtype=BF16)
    dy = _mm_simple(tag + "_dyg", dpre, w_glu, NT, tm=512, tn=W, tk=W,
                    epilogue=lambda accs, ex: ((accs[0] + ex[0]) * _gelu_grad(ex[1]),), extras=[tmp, ypre])
    dd, = _rowwise(tag + "_dd", lambda t, b: ((), (_colsum(t[0] * t[1]),)), [dy, u], [], [], [W])
    gp = a_r.shape[1]
    nt, cu, cs = b_r.shape
    sp = _s5_specs(T, nt, cu, cs)
    tm = sp['tm']
    dx_r, dx_i = _mm(tag + "_dx", (T // tm, nt, 1),
                     [(dy, sp['chan'], c_r, sp['c'], NT, 0), (dy, sp['chan'], c_in, sp['c'], NT, 1)],
                     [(tm, cs)] * 2, [((T, gp), F32, sp['state'])] * 2, lambda accs, ex: tuple(accs))
    tk = _tile(T, 512)
    t_chan = _bs((tk, cu), lambda i, j, k: (k, i))
    t_state = _bs((tk, cs), lambda i, j, k: (k, i))
    o_b = _bs((None, cu, cs), lambda i, j, k: (i, 0, 0))
    o_c = _bs((None, cs, cu), lambda i, j, k: (i, 0, 0))
    dc_r, dc_in = _mm(tag + "_dc", (nt, 1, T // tk),
                      [(x_r, t_state, dy, t_chan, TN, 0), (x_i, t_state, dy, t_chan, TN, 1)],
                      [(cs, cu)] * 2, [((nt, cs, cu), F32, o_c)] * 2, lambda accs, ex: tuple(accs))
    lam_r, lam_i, da_r, da_i = _scan_bwd(tag + "_scanb", dx_r, dx_i, p_r, p_i, a_r, a_i)
    du, = _mm(tag + "_du", (T // tm, nt, 1),
              [(lam_r, sp['state'], b_r, sp['b'], NT, 0), (lam_i, sp['state'], b_i, sp['b'], NT, 0)],
              [(tm, cu)], [((T, W), F32, sp['chan'])], lambda accs, ex: (accs[0] + ex[1] * ex[0],),
              [(dy, sp['chan']), (d_skip.reshape(1, W), sp['row'])])
    db_r, db_i = _mm(tag + "_db", (nt, 1, T // tk),
                     [(u, t_chan, lam_r, t_state, TN, 0), (u, t_chan, lam_i, t_state, TN, 1)],
                     [(cu, cs)] * 2, [((nt, cu, cs), F32, o_b)] * 2, lambda accs, ex: tuple(accs))
    return du, (da_r, da_i, db_r, db_i, dc_r, dc_in), dd, dw_glu, db_glu


def _swa_mask(n):
    B = SWA_BLOCK
    r = lax.broadcasted_iota(jnp.int32, (B, 2 * B), 0)
    c = lax.broadcasted_iota(jnp.int32, (B, 2 * B), 1)
    d = r + B - c
    return (d >= 0) & (d < B) & ((n > 0) | (c >= B))


def _swa_specs(T):
    B = SWA_BLOCK
    cur = lambda w: _bs((B, w), lambda n: (n, 0))
    prev = lambda w: _bs((B, w), lambda n: (jnp.maximum(n - 1, 0), 0))
    return cur, prev


def _swa_fwd(name, q, k, v, sinks_b):
    T = q.shape[0]
    B = SWA_BLOCK
    scale = SWA_HEAD_DIM ** -0.5
    per_kv = SWA_HEADS // SWA_KV_HEADS

    def body(q_ref, kc_ref, kp_ref, vc_ref, vp_ref, s_ref, o_ref, l_ref):
        n = pl.program_id(0)
        kcat = jnp.concatenate([kp_ref[...], kc_ref[...]], axis=0).astype(BF16)
        vcat = jnp.concatenate([vp_ref[...], vc_ref[...]], axis=0).astype(BF16)
        mask = _swa_mask(n)
        lane = lax.broadcasted_iota(jnp.int32, (B, 128), 1)
        lo = lane < SWA_HEAD_DIM
        lse_out = jnp.zeros((B, 128), F32)
        for jb in range(SWA_HEADS // 2):
            qblk = q_ref[:, jb * 128:(jb + 1) * 128]
            h = (2 * jb) // per_kv
            half_h = lo if h == 0 else jnp.logical_not(lo)
            outs = []
            for e in range(2):
                j = 2 * jb + e
                qa = qblk if e == h else pltpu.roll(qblk, SWA_HEAD_DIM, 1)
                qm = jnp.where(half_h, qa, 0.0).astype(BF16)
                s = lax.dot_general(qm, kcat, (NT, ((), ())), preferred_element_type=F32) * scale
                s = jnp.where(mask, s, -jnp.inf)
                sk = s_ref[j:j + 1, 0:1]
                m = jnp.maximum(jnp.max(s, axis=1, keepdims=True), sk)
                ex = jnp.exp(s - m)
                den = jnp.sum(ex, axis=1, keepdims=True) + jnp.exp(sk - m)
                p = ex / den
                r = lax.dot_general(p.astype(BF16), vcat, (NN, ((), ())), preferred_element_type=F32)
                outs.append(r if e == h else pltpu.roll(r, SWA_HEAD_DIM, 1))
                lse_out = jnp.where(lane == j, m + jnp.log(den), lse_out)
            o_ref[:, jb * 128:(jb + 1) * 128] = jnp.where(lo, outs[0], outs[1])
        l_ref[...] = lse_out

    cur, prev = _swa_specs(T)
    return _pcall(
        body, name=name, grid=(T // B,),
        in_specs=[cur(SWA_WIDTH), cur(SWA_KV_WIDTH), prev(SWA_KV_WIDTH), cur(SWA_KV_WIDTH), prev(SWA_KV_WIDTH),
                  _bs(sinks_b.shape, lambda n: (0, 0))],
        out_specs=[cur(SWA_WIDTH), cur(128)],
        out_shape=[jax.ShapeDtypeStruct((T, SWA_WIDTH), F32), jax.ShapeDtypeStruct((T, 128), F32)],
        compiler_params=pltpu.CompilerParams(dimension_semantics=("parallel",)),
    )(q, k, k, v, v, sinks_b)


def _swa_bwd(name, q, k, v, sinks_b, o, lse, do):
    T = q.shape[0]
    B = SWA_BLOCK
    scale = SWA_HEAD_DIM ** -0.5
    per_kv = SWA_HEADS // SWA_KV_HEADS

    def body(q_ref, kc_ref, kp_ref, vc_ref, vp_ref, s_ref, o_ref, l_ref, do_ref,
             dq_ref, dkc_ref, dkp_ref, dvc_ref, dvp_ref, ds_ref):
        n = pl.program_id(0)

        @pl.when(n == 0)
        def _():
            ds_ref[...] = jnp.zeros_like(ds_ref)
        kcat = jnp.concatenate([kp_ref[...], kc_ref[...]], axis=0).astype(BF16)
        vcat = jnp.concatenate([vp_ref[...], vc_ref[...]], axis=0).astype(BF16)
        mask = _swa_mask(n)
        lane = lax.broadcasted_iota(jnp.int32, (B, 128), 1)
        lane1 = lax.broadcasted_iota(jnp.int32, (1, 128), 1)
        lo = lane < SWA_HEAD_DIM
        lblk = l_ref[...]
        dk = jnp.zeros((2 * B, 128), F32)
        dv = jnp.zeros((2 * B, 128), F32)
        dsink = jnp.zeros((1, 128), F32)
        for jb in range(SWA_HEADS // 2):
            sl = slice(jb * 128, (jb + 1) * 128)
            qblk, doblk = q_ref[:, sl], do_ref[:, sl]
            prod = doblk * o_ref[:, sl]
            h = (2 * jb) // per_kv
            half_h = lo if h == 0 else jnp.logical_not(lo)
            parts = []
            for e in range(2):
                j = 2 * jb + e
                half_e = lo if e == 0 else jnp.logical_not(lo)
                dsum = jnp.sum(jnp.where(half_e, prod, 0.0), axis=1, keepdims=True)
                lj = jnp.sum(jnp.where(lane == j, lblk, 0.0), axis=1, keepdims=True)
                qa = qblk if e == h else pltpu.roll(qblk, SWA_HEAD_DIM, 1)
                da = doblk if e == h else pltpu.roll(doblk, SWA_HEAD_DIM, 1)
                qm = jnp.where(half_h, qa, 0.0).astype(BF16)
                dm = jnp.where(half_h, da, 0.0).astype(BF16)
                s = lax.dot_general(qm, kcat, (NT, ((), ())), preferred_element_type=F32) * scale
                p = jnp.where(mask, jnp.exp(s - lj), 0.0)
                dp = lax.dot_general(dm, vcat, (NT, ((), ())), preferred_element_type=F32)
                dsb = (p * (dp - dsum) * scale).astype(BF16)
                dqa = lax.dot_general(dsb, kcat, (NN, ((), ())), preferred_element_type=F32)
                parts.append(dqa if e == h else pltpu.roll(dqa, SWA_HEAD_DIM, 1))
                dk = dk + lax.dot_general(dsb, qm, (TN, ((), ())), preferred_element_type=F32)
                dv = dv + lax.dot_general(p.astype(BF16), dm, (TN, ((), ())), preferred_element_type=F32)
                sk = s_ref[j:j + 1, 0:1]
                contrib = jnp.sum(jnp.exp(sk - lj) * dsum, axis=0, keepdims=True)
                dsink = jnp.where(lane1 == j, dsink - contrib, dsink)
            dq_ref[:, sl] = jnp.where(lo, parts[0], parts[1])
        dkp_ref[...] = dk[:B]
        dkc_ref[...] = dk[B:]
        dvp_ref[...] = dv[:B]
        dvc_ref[...] = dv[B:]
        ds_ref[...] += dsink

    cur, prev = _swa_specs(T)
    kv = jax.ShapeDtypeStruct((T, SWA_KV_WIDTH), F32)
    return _pcall(
        body, name=name, grid=(T // B,),
        in_specs=[cur(SWA_WIDTH), cur(SWA_KV_WIDTH), prev(SWA_KV_WIDTH), cur(SWA_KV_WIDTH), prev(SWA_KV_WIDTH),
                  _bs(sinks_b.shape, lambda n: (0, 0)), cur(SWA_WIDTH), cur(128), cur(SWA_WIDTH)],
        out_specs=[cur(SWA_WIDTH)] + [cur(SWA_KV_WIDTH)] * 4 + [_bs((1, 128), lambda n: (0, 0))],
        out_shape=[jax.ShapeDtypeStruct((T, SWA_WIDTH), F32), kv, kv, kv, kv, jax.ShapeDtypeStruct((1, 128), F32)],
        compiler_params=pltpu.CompilerParams(dimension_semantics=("arbitrary",)),
    )(q, k, k, v, v, sinks_b, o, lse, do)


def _shift_add(name, cur, prv):
    T, W = cur.shape
    B = SWA_BLOCK
    nb = T // B

    def body(c_ref, p_ref, o_ref):
        n = pl.program_id(0)
        o_ref[...] = c_ref[...] + jnp.where(n < nb - 1, p_ref[...], 0.0)

    return _pcall(
        body, name=name, grid=(nb,),
        in_specs=[_bs((B, W), lambda n: (n, 0)), _bs((B, W), lambda n: (jnp.minimum(n + 1, nb - 1), 0))],
        out_specs=_bs((B, W), lambda n: (n, 0)), out_shape=jax.ShapeDtypeStruct((T, W), F32),
        compiler_params=pltpu.CompilerParams(dimension_semantics=("parallel",)),
    )(cur, prv)


MLA_BLOCK = 512
_MLA_SCALE = (MLA_NOPE + MLA_ROPE) ** -0.5


def _lower_tri(tb):
    r = lax.broadcasted_iota(jnp.int32, (tb, tb), 0)
    c = lax.broadcasted_iota(jnp.int32, (tb, tb), 1)
    return c <= r


def _mla_fwd(name, qcat, kcat, v):
    T = qcat.shape[0]
    tb = _tile(T, MLA_BLOCK)
    nb = T // tb

    def body(q_ref, k_ref, v_ref, o_ref, l_ref):
        qi = pl.program_id(1)
        q = q_ref[...]

        def block(kb, carry, diagonal):
            m_prev, l_prev, acc = carry
            rows = pl.ds(pl.multiple_of(kb * tb, tb), tb)
            s = lax.dot_general(q, k_ref[rows, :], (NT, ((), ())), preferred_element_type=F32) * _MLA_SCALE
            if diagonal:
                s = jnp.where(_lower_tri(tb), s, -jnp.inf)
            m_new = jnp.maximum(m_prev, jnp.max(s, axis=1, keepdims=True))
            alpha = jnp.exp(m_prev - m_new)
            p = jnp.exp(s - m_new)
            l_new = alpha * l_prev + jnp.sum(p, axis=1, keepdims=True)
            acc = alpha * acc + lax.dot_general(p.astype(BF16), v_ref[rows, :], (NN, ((), ())),
                                                preferred_element_type=F32)
            return m_new, l_new, acc

        init = (jnp.full((tb, 1), -jnp.inf, F32), jnp.zeros((tb, 1), F32), jnp.zeros((tb, MLA_V), F32))
        carry = lax.fori_loop(0, qi, lambda kb, c: block(kb, c, False), init)
        m_fin, l_fin, acc = block(qi, carry, True)
        o_ref[...] = acc / l_fin
        l_ref[...] = m_fin + jnp.log(l_fin)

    return _pcall(
        body, name=name, grid=(MLA_HEADS, nb),
        in_specs=[_bs((tb, MLA_SLOT), lambda h, qi: (qi, h)), _bs((T, MLA_SLOT), lambda h, qi: (0, h)),
                  _bs((T, MLA_V), lambda h, qi: (0, h))],
        out_specs=[_bs((tb, MLA_V), lambda h, qi: (qi, h)), _bs((None, tb, 1), lambda h, qi: (h, qi, 0))],
        out_shape=[jax.ShapeDtypeStruct((T, MLA_WIDTH), F32), jax.ShapeDtypeStruct((MLA_HEADS, T, 1), F32)],
        compiler_params=pltpu.CompilerParams(dimension_semantics=("parallel", "arbitrary")),
    )(qcat, kcat, v)


def _mla_bwd_q(name, qcat, kcat, v, o, lse, do):
    T = qcat.shape[0]
    tb = _tile(T, MLA_BLOCK)
    nb = T // tb

    def body(q_ref, k_ref, v_ref, o_ref, do_ref, l_ref, dq_ref, ds_ref):
        qi = pl.program_id(1)
        q = q_ref[...]
        do = do_ref[...]
        dsum = jnp.sum(do * o_ref[...], axis=1, keepdims=True)
        ds_ref[...] = dsum
        dob = do.astype(BF16)
        lse_q = l_ref[...]

        def block(kb, acc, diagonal):
            rows = pl.ds(pl.multiple_of(kb * tb, tb), tb)
            k = k_ref[rows, :]
            s = lax.dot_general(q, k, (NT, ((), ())), preferred_element_type=F32) * _MLA_SCALE
            p = jnp.exp(s - lse_q)
            if diagonal:
                p = jnp.where(_lower_tri(tb), p, 0.0)
            dp = lax.dot_general(dob, v_ref[rows, :], (NT, ((), ())), preferred_element_type=F32)
            ds = (p * (dp - dsum) * _MLA_SCALE).astype(BF16)
            return acc + lax.dot_general(ds, k, (NN, ((), ())), preferred_element_type=F32)

        acc = lax.fori_loop(0, qi, lambda kb, a: block(kb, a, False), jnp.zeros((tb, MLA_SLOT), F32))
        dq_ref[...] = block(qi, acc, True)

    qs = lambda w: _bs((tb, w), lambda h, qi: (qi, h))
    col = _bs((None, tb, 1), lambda h, qi: (h, qi, 0))
    return _pcall(
        body, name=name, grid=(MLA_HEADS, nb),
        in_specs=[qs(MLA_SLOT), _bs((T, MLA_SLOT), lambda h, qi: (0, h)), _bs((T, MLA_V), lambda h, qi: (0, h)),
                  qs(MLA_V), qs(MLA_V), col],
        out_specs=[qs(MLA_SLOT), col],
        out_shape=[jax.ShapeDtypeStruct((T, MLA_HEADS * MLA_SLOT), F32), jax.ShapeDtypeStruct((MLA_HEADS, T, 1), F32)],
        compiler_params=pltpu.CompilerParams(dimension_semantics=("parallel", "arbitrary")),
    )(qcat, kcat, v, o, do, lse)


def _mla_bwd_kv(name, qcat, kcat, v, lse, dsum, do):
    T = qcat.shape[0]
    tb = _tile(T, MLA_BLOCK)
    nb = T // tb

    def body(q_ref, k_ref, v_ref, do_ref, l_ref, ds_ref, dk_ref, dv_ref):
        ki = pl.program_id(1)
        k = k_ref[...]
        vv = v_ref[...]

        def block(qb, carry, diagonal):
            dk, dv = carry
            rows = pl.ds(pl.multiple_of(qb * tb, tb), tb)
            q = q_ref[rows, :]
            dob = do_ref[rows, :].astype(BF16)
            s = lax.dot_general(q, k, (NT, ((), ())), preferred_element_type=F32) * _MLA_SCALE
            p = jnp.exp(s - l_ref[rows, :])
            if diagonal:
                p = jnp.where(_lower_tri(tb), p, 0.0)
            dp = lax.dot_general(dob, vv, (NT, ((), ())), preferred_element_type=F32)
            ds = (p * (dp - ds_ref[rows, :]) * _MLA_SCALE).astype(BF16)
            dv = dv + lax.dot_general(p.astype(BF16), dob, (TN, ((), ())), preferred_element_type=F32)
            dk = dk + lax.dot_general(ds, q, (TN, ((), ())), preferred_element_type=F32)
            return dk, dv

        carry = block(ki, (jnp.zeros((tb, MLA_SLOT), F32), jnp.zeros((tb, MLA_V), F32)), True)
        dk, dv = lax.fori_loop(ki + 1, nb, lambda qb, c: block(qb, c, False), carry)
        dk_ref[...] = dk
        dv_ref[...] = dv

    ks = lambda w: _bs((tb, w), lambda h, ki: (ki, h))
    col = _bs((None, T, 1), lambda h, ki: (h, 0, 0))
    return _pcall(
        body, name=name, grid=(MLA_HEADS, nb),
        in_specs=[_bs((T, MLA_SLOT), lambda h, ki: (0, h)), ks(MLA_SLOT), ks(MLA_V),
                  _bs((T, MLA_V), lambda h, ki: (0, h)), col, col],
        out_specs=[ks(MLA_SLOT), ks(MLA_V)],
        out_shape=[jax.ShapeDtypeStruct((T, MLA_HEADS * MLA_SLOT), F32), jax.ShapeDtypeStruct((T, MLA_WIDTH), F32)],
        compiler_params=pltpu.CompilerParams(dimension_semantics=("parallel", "arbitrary")),
    )(qcat, kcat, v, do, lse, dsum)


def _kcat_fwd(name, kpre, krs, c256, s256):
    def fn(t, b):
        kr = t[1] * t[2] + _half_swap(t[1]) * t[3]
        return (t[0] + jnp.tile(kr, (1, MLA_HEADS)),), ()
    return _rowwise(name, fn, [kpre, krs, c256, s256], [], [(kpre.shape[1], BF16)])[0]


def _kcat_bwd(name, dkcat, c256, s256):
    def fn(t, b):
        d = t[0][:, 0:MLA_SLOT]
        for h in range(1, MLA_HEADS):
            d = d + t[0][:, h * MLA_SLOT:(h + 1) * MLA_SLOT]
        return (d * t[1] + _half_swap(d * t[2]),), ()
    return _rowwise(name, fn, [dkcat, c256, s256], [], [(MLA_SLOT, F32)])[0]


def _in_widths(ws):
    return [ws, SWA_WIDTH, SWA_KV_WIDTH, SWA_KV_WIDTH, MLA_Q_RANK, MLA_KV_RANK, MLA_SLOT]


def _kr_offset(ws):
    return ws + SWA_WIDTH + 2 * SWA_KV_WIDTH + MLA_Q_RANK + MLA_KV_RANK


def _pad_w_in(w, ws):
    z = lambda n: jnp.zeros((w.shape[0], n), w.dtype)
    o = _kr_offset(ws)
    return jnp.concatenate([w[:, :o], z(MLA_NOPE), w[:, o:], z(MLA_SLOT - MLA_NOPE - MLA_ROPE)], axis=1)


def _unpad_w_in(dw, ws):
    o = _kr_offset(ws)
    return jnp.concatenate([dw[:, :o], dw[:, o + MLA_NOPE:o + MLA_NOPE + MLA_ROPE]], axis=1)


def _pad_w_uq(w):
    r = w.shape[0]
    w3 = w.reshape(r, MLA_HEADS, MLA_NOPE + MLA_ROPE)
    return jnp.pad(w3, ((0, 0), (0, 0), (0, MLA_SLOT - MLA_NOPE - MLA_ROPE))).reshape(r, MLA_HEADS * MLA_SLOT)


def _unpad_w_uq(dw):
    r = dw.shape[0]
    return dw.reshape(r, MLA_HEADS, MLA_SLOT)[..., :MLA_NOPE + MLA_ROPE].reshape(r, -1)


def _pad_w_ukv(w):
    r = w.shape[0]
    w3 = w.reshape(r, MLA_HEADS, MLA_NOPE + MLA_V)
    wk = jnp.pad(w3[..., :MLA_NOPE], ((0, 0), (0, 0), (0, MLA_SLOT - MLA_NOPE))).reshape(r, MLA_HEADS * MLA_SLOT)
    wv = w3[..., MLA_NOPE:].reshape(r, MLA_WIDTH)
    return wk, wv


def _unpad_w_ukv(dwk, dwv):
    r = dwk.shape[0]
    return jnp.concatenate([dwk.reshape(r, MLA_HEADS, MLA_SLOT)[..., :MLA_NOPE], dwv.reshape(r, MLA_HEADS, MLA_V)],
                           axis=-1).reshape(r, -1)


def _layer_prep(lw):
    ws = lw['ssm_d'].shape[0]
    p = dict(lw)
    w_in_pad = _pad_w_in(lw['w_in'], ws)
    p['w_in_pad'] = w_in_pad
    offs = np.cumsum([0] + _in_widths(ws))
    p['w_in_parts'] = [w_in_pad[:, offs[i]:offs[i + 1]] for i in range(7)]
    p['s5_prep'] = _s5_prep(lw['ssm_log_dt'], lw['ssm_a_re'], lw['ssm_a_im'], lw['ssm_b_re'], lw['ssm_b_im'],
                            lw['ssm_c_re'], lw['ssm_c_im'])
    p['sinks_b'] = jnp.broadcast_to(lw['swa_sinks'][:, None], (SWA_HEADS, 128))
    p['w_uq_pad'] = _pad_w_uq(lw['mla_w_uq'])
    p['w_k_pad'], p['w_v'] = _pad_w_ukv(lw['mla_w_ukv'])
    p['w_ukv_pad'] = jnp.concatenate([p['w_k_pad'], p['w_v']], axis=1)
    b = [0, ws, ws + SWA_WIDTH, ws + SWA_WIDTH + MLA_WIDTH]
    p['w_out_g'] = [lw['w_out'][b[g]:b[g + 1]] for g in range(3)]
    p['out_norm_g'] = [lw['out_norm'][b[g]:b[g + 1]] for g in range(3)]
    return p


def _mixer_fwd(tag, x, p, tabs):
    T, D = x.shape
    c64, s64, c256, s256 = tabs
    ws = p['ssm_d'].shape[0]
    widths = _in_widths(ws)
    n_in = sum(widths)
    offs = [int(o) for o in np.cumsum([0] + widths[:-1])]
    h, rstd = _rms_fwd(tag + "_rms", x, p['mix_norm'])
    tm = _tile(T, 256)
    row_i = lambda w: _bs((tm, w), lambda i, j, k: (i, 0))
    parts = _mm(tag + "_in", (T // tm, 1, 1),
                [(h, row_i(D), p['w_in_pad'], _bs((D, n_in), lambda i, j, k: (0, 0)), NN, 0)],
                [(tm, n_in)], [((T, w), F32, row_i(w)) for w in widths],
                lambda accs, ex: tuple(accs[0][:, o:o + w] for o, w in zip(offs, widths)))
    u, q, k, v, cq, ckv, krs = parts
    y_ssm, s5_saved = _s5_fwd(tag + "_s5", u, p['s5_prep'], p['ssm_d'], p['ssm_w_glu'], p['ssm_b_glu'])
    q_r = _rope(tag + "_ropeq", q, c64, s64)
    k_r = _rope(tag + "_ropek", k, c64, s64)
    y_swa, lse_swa = _swa_fwd(tag + "_swa", q_r, k_r, v, p['sinks_b'])
    cqn, r_q = _rms_fwd(tag + "_rmsq", cq, p['mla_q_norm'])
    ckvn, r_kv = _rms_fwd(tag + "_rmskv", ckv, p['mla_kv_norm'])
    qpre = _mm_simple(tag + "_uq", cqn, p['w_uq_pad'], NN, tm=512, tn=1024, tk=MLA_Q_RANK)
    nk_, nv_ = MLA_HEADS * MLA_SLOT, MLA_WIDTH
    kpre, vm = _mm(tag + "_ukv", (T // tm, 1, 1),
                   [(ckvn, row_i(MLA_KV_RANK), p['w_ukv_pad'], _bs((MLA_KV_RANK, nk_ + nv_), lambda i, j, k: (0, 0)),
                     NN, 0)],
                   [(tm, nk_ + nv_)], [((T, nk_), F32, row_i(nk_)), ((T, nv_), BF16, row_i(nv_))],
                   lambda accs, ex: (accs[0][:, :nk_], accs[0][:, nk_:]))
    qcat = _rope(tag + "_ropemq", qpre, c256, s256, out_dtype=BF16)
    kcat = _kcat_fwd(tag + "_kcat", kpre, krs, c256, s256)
    y_mla, lse_mla = _mla_fwd(tag + "_mla", qcat, kcat, vm)
    ys = [y_ssm, y_swa, y_mla]
    yn, rs = [], []
    for g in range(3):
        n_, r_ = _rms_fwd(f"{tag}_rmso{g}", ys[g], p['out_norm_g'][g])
        yn.append(n_)
        rs.append(r_)
    tm3, tn3 = _tile(T, 512), _tile(D, 1024)
    pairs = []
    for g in range(3):
        wg = ys[g].shape[1]
        pairs.append((yn[g], _bs((tm3, wg), lambda i, j, k: (i, 0)), p['w_out_g'][g],
                      _bs((wg, tn3), lambda i, j, k: (0, j)), NN, 0))
    o_spec = _bs((tm3, tn3), lambda i, j, k: (i, j))
    x2, = _mm(tag + "_out", (T // tm3, D // tn3, 1), pairs, [(tm3, tn3)], [((T, D), F32, o_spec)],
              lambda accs, ex: (ex[0] + accs[0],), [(x, o_spec)])
    saved = (x, h, rstd, q_r, k_r, v, cq, ckv, s5_saved, y_swa, lse_swa, cqn, r_q, ckvn, r_kv, qcat, kcat, vm,
             y_mla, lse_mla, ys, yn, rs)
    return x2, saved


def _mixer_bwd(tag, dx2, dx2_16, saved, p, tabs):
    (x, h, rstd, q_r, k_r, v, cq, ckv, s5_saved, y_swa, lse_swa, cqn, r_q, ckvn, r_kv, qcat, kcat, vm,
     y_mla, lse_mla, ys, yn, rs) = saved
    T, D = x.shape
    c64, s64, c256, s256 = tabs
    ws = p['ssm_d'].shape[0]
    g_ = {}
    dys, dwo, don = [], [], []
    for g in range(3):
        wg = ys[g].shape[1]
        dyn = _mm_simple(f"{tag}_dyn{g}", dx2_16, p['w_out_g'][g], NT, tm=512, tn=wg, tk=1024)
        dwo.append(_mm_simple(f"{tag}_dwo{g}", yn[g], dx2_16, TN, tm=512, tn=1024, tk=512, out_dtype=BF16))
        dy_g, don_g = _rms_bwd(f"{tag}_rmsob{g}", dyn, ys[g], rs[g], p['out_norm_g'][g])
        dys.append(dy_g)
        don.append(don_g)
    g_['w_out'] = jnp.concatenate(dwo, axis=0)
    g_['out_norm'] = jnp.concatenate(don, axis=1)[0]
    dqcat, dsum = _mla_bwd_q(tag + "_mlabq", qcat, kcat, vm, y_mla, lse_mla, dys[2])
    dkcat, dvm = _mla_bwd_kv(tag + "_mlabkv", qcat, kcat, vm, lse_mla, dsum, dys[2])
    dqpre = _rope(tag + "_ropemqb", dqcat, c256, s256, backward=True)
    dkrs = _kcat_bwd(tag + "_kcatb", dkcat, c256, s256)
    g_['mla_w_uq'] = _unpad_w_uq(_mm_simple(tag + "_dwuq", cqn, dqpre, TN, tm=MLA_Q_RANK, tn=1024, tk=512,
                                            out_dtype=BF16))
    dcqn = _mm_simple(tag + "_dcqn", dqpre, p['w_uq_pad'], NT, tm=512, tn=MLA_Q_RANK, tk=1024)
    dcq, dqn = _rms_bwd(tag + "_rmsqb", dcqn, cq, r_q, p['mla_q_norm'])
    g_['mla_q_norm'] = dqn[0]
    dwk = _mm_simple(tag + "_dwk", ckvn, dkcat, TN, tm=MLA_KV_RANK, tn=1024, tk=512, out_dtype=BF16)
    dwv = _mm_simple(tag + "_dwv", ckvn, dvm, TN, tm=MLA_KV_RANK, tn=1024, tk=512, out_dtype=BF16)
    g_['mla_w_ukv'] = _unpad_w_ukv(dwk, dwv)
    tm = _tile(T, 512)
    nk_, nv_ = MLA_HEADS * MLA_SLOT, MLA_WIDTH
    tkk = _tile(nk_, 1024)
    dckvn_k = _mm_simple(tag + "_dckvk", dkcat, p['w_k_pad'], NT, tm=512, tn=MLA_KV_RANK, tk=tkk)
    dckvn = _mm_simple(tag + "_dckvv", dvm, p['w_v'], NT, tm=512, tn=MLA_KV_RANK, tk=nv_,
                       epilogue=lambda accs, ex: (accs[0] + ex[0],), extras=[dckvn_k])
    dckv, dkvn = _rms_bwd(tag + "_rmskvb", dckvn, ckv, r_kv, p['mla_kv_norm'])
    g_['mla_kv_norm'] = dkvn[0]
    dq_r, dkc, dkp, dvc, dvp, dsinks = _swa_bwd(tag + "_swab", q_r, k_r, v, p['sinks_b'], y_swa, lse_swa, dys[1])
    g_['swa_sinks'] = dsinks[0, :SWA_HEADS]
    dk_r = _shift_add(tag + "_dksum", dkc, dkp)
    dv = _shift_add(tag + "_dvsum", dvc, dvp)
    dq = _rope(tag + "_ropeqb", dq_r, c64, s64, backward=True)
    dk = _rope(tag + "_ropekb", dk_r, c64, s64, backward=True)
    du, s5g, dd, dw_glu, db_glu = _s5_bwd(tag + "_s5b", dys[0], s5_saved, p['s5_prep'], p['ssm_d'], p['ssm_w_glu'])
    g_['ssm_d'], g_['ssm_w_glu'], g_['ssm_b_glu'] = dd[0], dw_glu, db_glu[0]
    _, pull = jax.vjp(_s5_prep, p['ssm_log_dt'], p['ssm_a_re'], p['ssm_a_im'], p['ssm_b_re'], p['ssm_b_im'],
                      p['ssm_c_re'], p['ssm_c_im'])
    for name, val in zip(['ssm_log_dt', 'ssm_a_re', 'ssm_a_im', 'ssm_b_re', 'ssm_b_im', 'ssm_c_re', 'ssm_c_im'],
                         pull(tuple(s5g))):
        g_[name] = val
    dparts = [du, dq, dk, dv, dcq, dckv, dkrs]
    widths = _in_widths(ws)
    tn = _tile(D, 1024)
    pairs = []
    for dp_, wp_, w in zip(dparts, p['w_in_parts'], widths):
        pairs.append((dp_, _bs((tm, w), lambda i, j, k: (i, 0)), wp_, _bs((tn, w), lambda i, j, k: (j, 0)), NT, 0))
    o_spec = _bs((tm, tn), lambda i, j, k: (i, j))
    dh, = _mm(tag + "_dh", (T // tm, D // tn, 1), pairs, [(tm, tn)], [((T, D), F32, o_spec)],
              lambda accs, ex: (accs[0],))
    tmw, tk = _tile(D, 512), _tile(T, 512)
    pairs = []
    for i_, (dp_, w) in enumerate(zip(dparts, widths)):
        pairs.append((h, _bs((tk, tmw), lambda i, j, k: (k, i)), dp_, _bs((tk, w), lambda i, j, k: (k, 0)), TN, i_))
    dws = _mm(tag + "_dwin", (D // tmw, 1, T // tk), pairs, [(tmw, w) for w in widths],
              [((D, w), BF16, _bs((tmw, w), lambda i, j, k: (i, 0))) for w in widths], lambda accs, ex: tuple(accs))
    g_['w_in'] = _unpad_w_in(jnp.concatenate(dws, axis=1), ws)
    dx, dx16, dmix = _rms_bwd(tag + "_rmsb", dh, x, rstd, p['mix_norm'], dres=dx2, with_bf16=True)
    g_['mix_norm'] = dmix[0]
    return dx, dx16, g_


def _final_loss(name, x, target, gain):
    D = x.shape[1]

    def fn(t, b):
        xv = t[0]
        r = lax.rsqrt(jnp.mean(xv * xv, axis=-1, keepdims=True) + EPS)
        xh = xv * r
        err = xh * b[0] - t[1]
        part = 0.5 * jnp.sum(jnp.sum(err * err, axis=-1, keepdims=True), axis=0, keepdims=True) / D
        dy = err / D
        dxh = dy * b[0]
        dx = r * (dxh - xh * jnp.mean(dxh * xh, axis=-1, keepdims=True))
        return (dx, dx), (jnp.broadcast_to(part, (1, 128)), _colsum(dy * xh))
    dx, dx16, part, dg = _rowwise(name, fn, [x, target], [gain.reshape(1, D)], [(D, F32), (D, BF16)], [128, D])
    return part[0, 0], dx, dx16, dg[0]


def _device_step(x, positions, target, n_layers, get_block, final_norm, on_grads):
    tabs = _rope_tables(positions)
    saved = []
    for l in range(n_layers):
        w1 = get_block(l, 'ffn1', x)
        x, s1 = _ffn_fwd("ffn1", x, w1['ffn1_norm'], w1['ffn1_wg2'], w1['ffn1_wu2'], w1['ffn1_wd'])
        p = _layer_prep(get_block(l, 'mix', x))
        x, s2 = _mixer_fwd("mix", x, p, tabs)
        w3 = get_block(l, 'ffn2', x)
        x, s3 = _ffn_fwd("ffn2", x, w3['ffn2_norm'], w3['ffn2_wg2'], w3['ffn2_wu2'], w3['ffn2_wd'])
        saved.append((w1, s1, p, s2, w3, s3))
    loss_part, dx, dx16, dfinal = _final_loss("loss", x, target, final_norm)
    for l in range(n_layers - 1, -1, -1):
        w1, s1, p, s2, w3, s3 = saved[l]
        dx, dx16, gn, gwg, gwu, gwd = _ffn_bwd("ffn2b", dx, dx16, s3, w3['ffn2_norm'], w3['ffn2_wg2'], w3['ffn2_wu2'],
                                               w3['ffn2_wd'])
        dx, dx16 = on_grads(l, 'ffn2', {'ffn2_norm': gn[0], 'ffn2_wg2': gwg, 'ffn2_wu2': gwu, 'ffn2_wd': gwd},
                            (dx, dx16))
        dx, dx16, gm = _mixer_bwd("mixb", dx, dx16, s2, p, tabs)
        dx, dx16 = on_grads(l, 'mix', gm, (dx, dx16))
        dx, dx16, gn, gwg, gwu, gwd = _ffn_bwd("ffn1b", dx, dx16, s1, w1['ffn1_norm'], w1['ffn1_wg2'], w1['ffn1_wu2'],
                                               w1['ffn1_wd'])
        dx, dx16 = on_grads(l, 'ffn1', {'ffn1_norm': gn[0], 'ffn1_wg2': gwg, 'ffn1_wu2': gwu, 'ffn1_wd': gwd},
                            (dx, dx16))
    return loss_part, dx, dfinal


_ANY = pl.BlockSpec(memory_space=pl.ANY)
_CHIP_MASKS = (2, 1, 3)


def _place():
    x, y, c = lax.axis_index("x"), lax.axis_index("y"), lax.axis_index("c")
    chips = [(1 - x, y), (x, 1 - y), (1 - x, 1 - y)]
    return x, y, c, 2 * x + y, chips


_HBM = pl.BlockSpec(memory_space=pltpu.HBM)
_SEMS = pl.BlockSpec(memory_space=pltpu.SEMAPHORE)
_EFFECT = pltpu.SideEffectType.DATAFLOW_SIDE_EFFECTING


def _split_start(name, bufs, sem_sizes, copies):
    n, ns = len(bufs), len(sem_sizes)

    def body(*refs):
        for cp in copies(refs[:n], refs[n:n + ns])[0]:
            cp.start()

    outs = _pcall(
        body, name=name, in_specs=[_HBM] * n,
        out_shape=(*[pltpu.SemaphoreType.DMA((k,)) for k in sem_sizes], *[pltpu.HBM(b.shape, b.dtype) for b in bufs]),
        out_specs=(*[_SEMS] * ns, *[_HBM] * n), input_output_aliases={i: i + ns for i in range(n)},
        compiler_params=pltpu.CompilerParams(has_side_effects=_EFFECT),
    )(*[pltpu.with_memory_space_constraint(b, pltpu.HBM) for b in bufs])
    return list(outs[:ns]), list(outs[ns:])


def _split_wait(name, bufs, sems, after, copies):
    n, ns = len(bufs), len(sems)

    def body(*refs):
        _, sent, landed = copies(refs[:n], refs[n:n + ns])
        for cp in sent:
            cp.wait_send()
        for cp in landed:
            cp.wait_recv()

    return list(_pcall(
        body, name=name, in_specs=[_HBM] * n + [_SEMS] * ns + [_ANY],
        out_shape=tuple(pltpu.HBM(b.shape, b.dtype) for b in bufs), out_specs=tuple([_HBM] * n),
        input_output_aliases={i: i for i in range(n)},
        compiler_params=pltpu.CompilerParams(has_side_effects=_EFFECT),
    )(*bufs, *sems, after))


def _gather_group(bufs, send, recv):
    x, y, c, s, chips = _place()
    start, landed = [], []
    for a, buf in enumerate(bufs):
        for j in range(3):
            k = a * 3 + j
            to = (*chips[j], c)
            start.append(pltpu.make_async_remote_copy(
                src_ref=buf.at[s], dst_ref=buf.at[s], send_sem=send.at[k], recv_sem=recv.at[k],
                device_id=to, device_id_type=MESH))
            theirs = buf.at[s ^ _CHIP_MASKS[j]]
            landed.append(pltpu.make_async_remote_copy(
                src_ref=theirs, dst_ref=theirs, send_sem=send.at[k], recv_sem=recv.at[k],
                device_id=to, device_id_type=MESH))
    return start, start, landed


def _gather_start_copies(n_pass, sizes):
    def copies(refs, sems):
        start, o = [], n_pass
        for g, k in enumerate(sizes):
            start += _gather_group(refs[o:o + k], sems[2 * g], sems[2 * g + 1])[0]
            o += k
        return start, start, []
    return copies


def _gather_wait_copies(refs, sems):
    return _gather_group(refs, sems[0], sems[1])


_N_SLOTS = 7


def _reduce_copies(n_pass, n):
    def copies(refs, sems):
        send, recv = sems
        x, y, c, s, chips = _place()
        contrib, lands = refs[n_pass:n_pass + n], refs[n_pass + n:]
        start, landed = [], []
        for a in range(n):
            half = contrib[a].shape[1] // 2
            base = a * _N_SLOTS
            for j in range(3):
                for h in range(2):
                    start.append(pltpu.make_async_remote_copy(
                        src_ref=contrib[a].at[s ^ _CHIP_MASKS[j], pl.ds(h * half, half)], dst_ref=lands[a].at[2 * j + c],
                        send_sem=send.at[base + 2 * j + h], recv_sem=recv.at[base + 2 * j + c],
                        device_id=(*chips[j], h), device_id_type=MESH))
            start.append(pltpu.make_async_remote_copy(
                src_ref=contrib[a].at[s, pl.ds((1 - c) * half, half)], dst_ref=lands[a].at[6],
                send_sem=send.at[base + 6], recv_sem=recv.at[base + 6],
                device_id=(x, y, 1 - c), device_id_type=MESH))
            for slot in range(_N_SLOTS):
                landed.append(pltpu.make_async_remote_copy(
                    src_ref=lands[a].at[slot], dst_ref=lands[a].at[slot], send_sem=send.at[base + slot],
                    recv_sem=recv.at[base + slot], device_id=(x, y, c), device_id_type=MESH))
        return start, start, landed
    return copies


def _sum_partials(name, contrib, lands):
    ns, half, w = lands.shape
    tm = _tile(half, 256)
    nb = half // tm
    mine = _half_rows(nb)
    shard = lambda: 2 * lax.axis_index("x") + lax.axis_index("y")

    def body(g_ref, *refs):
        o_ref = refs[-1]
        acc = g_ref[...].astype(F32) + refs[6][...].astype(F32)
        for slot in range(6):
            acc = acc + refs[slot][...].astype(F32)
        o_ref[...] = acc

    blk = lambda j: _bs((None, tm, w), lambda i: (j, i, 0))
    return _pcall(
        body, name=name, grid=(nb,),
        in_specs=[_bs((None, tm, w), lambda i: (shard(), mine(i), 0))] + [blk(j) for j in range(_N_SLOTS)],
        out_specs=_bs((tm, w), lambda i: (mine(i), 0)), out_shape=jax.ShapeDtypeStruct((2 * half, w), F32),
        compiler_params=pltpu.CompilerParams(dimension_semantics=("parallel",)),
    )(contrib, *[lands] * _N_SLOTS)


def _stage_shard(name, wl):
    r, c = wl.shape
    tm = _tile(r, 256)
    shard = lambda: 2 * lax.axis_index("x") + lax.axis_index("y")

    def body(w_ref, o_ref):
        o_ref[...] = w_ref[...].astype(o_ref.dtype)

    return _pcall(
        body, name=name, grid=(r // tm,), in_specs=[_bs((tm, c), lambda i: (i, 0))],
        out_specs=_bs((None, tm, c), lambda i: (shard(), i, 0)),
        out_shape=jax.ShapeDtypeStruct((N_SHARD, r, c), BF16),
        compiler_params=pltpu.CompilerParams(dimension_semantics=("parallel",)),
    )(wl)


def _half_rows(nb):
    return lambda i: lax.axis_index("c") * nb + i


def _pair_join_call(name, halves):
    n = len(halves)

    def body(*refs):
        outs = refs[n:2 * n]
        send, recv = refs[2 * n:]
        x, y, c, s, chips = _place()
        cps = []
        for a in range(n):
            half = outs[a].shape[0] // 2
            mine = outs[a].at[pl.ds(c * half, half)]
            cp = pltpu.make_async_remote_copy(
                src_ref=mine, dst_ref=mine, send_sem=send.at[a], recv_sem=recv.at[a],
                device_id=(x, y, 1 - c), device_id_type=MESH)
            cp.start()
            cps.append(cp)
        for a in range(n):
            half = outs[a].shape[0] // 2
            theirs = outs[a].at[pl.ds((1 - c) * half, half)]
            pltpu.make_async_remote_copy(
                src_ref=theirs, dst_ref=theirs, send_sem=send.at[a], recv_sem=recv.at[a],
                device_id=(x, y, 1 - c), device_id_type=MESH).wait_recv()
        for cp in cps:
            cp.wait_send()

    return _pcall(
        body, name=name, in_specs=[_ANY] * n, out_specs=[_ANY] * n,
        out_shape=[jax.ShapeDtypeStruct(h.shape, h.dtype) for h in halves],
        input_output_aliases={a: a for a in range(n)},
        scratch_shapes=[pltpu.SemaphoreType.DMA((n,))] * 2,
    )(*halves)


_N_DEV = 8


def _small_copies(n_pass):
    def copies(refs, sems):
        send, recv = sems
        buf, land = refs[n_pass], refs[n_pass + 1]
        x, y, c, s, chips = _place()
        me = 4 * x + 2 * y + c
        start, landed = [], []
        for k in range(1, _N_DEV):
            to = (x ^ (k >> 2), y ^ ((k >> 1) & 1), c ^ (k & 1))
            start.append(pltpu.make_async_remote_copy(
                src_ref=buf, dst_ref=land.at[me], send_sem=send.at[k - 1], recv_sem=recv.at[k - 1],
                device_id=to, device_id_type=MESH))
            theirs = land.at[me ^ k]
            landed.append(pltpu.make_async_remote_copy(
                src_ref=theirs, dst_ref=theirs, send_sem=send.at[k - 1], recv_sem=recv.at[k - 1],
                device_id=to, device_id_type=MESH))
        return start, start, landed
    return copies


def _sum_devices(name, land):
    rows = land.shape[1]
    tm = _tile(rows, 512)

    def sum_body(a_ref, o_ref):
        acc = a_ref[0]
        for k in range(1, _N_DEV):
            acc = acc + a_ref[k]
        o_ref[...] = acc

    return _pcall(
        sum_body, name=name, grid=(rows // tm,),
        in_specs=[_bs((_N_DEV, tm, 128), lambda i: (0, i, 0))], out_specs=_bs((tm, 128), lambda i: (i, 0)),
        out_shape=jax.ShapeDtypeStruct(land.shape[1:], F32),
        compiler_params=pltpu.CompilerParams(dimension_semantics=("parallel",)),
    )(land)


def _adamw(name, w, g, m, v, token=None):
    def fn(t, b):
        wv, gv, mv, vv = t
        m2 = ADAM_B1 * mv + (1.0 - ADAM_B1) * gv
        v2 = ADAM_B2 * vv + (1.0 - ADAM_B2) * (gv * gv)
        m_hat = m2 / (1.0 - ADAM_B1 ** ADAM_STEP)
        v_hat = v2 / (1.0 - ADAM_B2 ** ADAM_STEP)
        delta = -ADAM_LR * (m_hat / (jnp.sqrt(v_hat) + ADAM_EPS) + ADAM_WD * wv)
        return (delta, m2, v2), ()
    wd = w.shape[1]
    return _rowwise(name, fn, [w, g, m, v], [] if token is None else [token], [(wd, F32)] * 3,
                    tm=256 if wd > 1024 else 512)


_WEIGHTS = ['ffn1_norm', 'ffn1_w_gate', 'ffn1_w_up', 'ffn1_w_down', 'mix_norm', 'w_in', 'ssm_log_dt', 'ssm_a_re',
            'ssm_a_im', 'ssm_b_re', 'ssm_b_im', 'ssm_c_re', 'ssm_c_im', 'ssm_d', 'ssm_w_glu', 'ssm_b_glu',
            'swa_sinks', 'mla_q_norm', 'mla_w_uq', 'mla_kv_norm', 'mla_w_ukv', 'out_norm', 'w_out', 'ffn2_norm',
            'ffn2_w_gate', 'ffn2_w_up', 'ffn2_w_down', 'final_norm']
_COL_SHARDED = ['ffn1_w_gate', 'ffn1_w_up', 'w_in', 'mla_w_uq', 'mla_w_ukv', 'ffn2_w_gate', 'ffn2_w_up']
_ROW_SHARDED = ['ffn1_w_down', 'ssm_w_glu', 'w_out', 'ffn2_w_down']
_STACKED = {'ffn1_w_gate': 'ffn1_wg2', 'ffn1_w_up': 'ffn1_wu2', 'ffn2_w_gate': 'ffn2_wg2', 'ffn2_w_up': 'ffn2_wu2'}
_RENAMED = {'ffn1_w_down': 'ffn1_wd', 'ffn2_w_down': 'ffn2_wd'}
_BLOCKS = ('ffn1', 'mix', 'ffn2')
_BLOCK_BIG = {'ffn1': ['ffn1_w_gate', 'ffn1_w_up', 'ffn1_w_down'],
              'mix': ['w_in', 'ssm_w_glu', 'mla_w_uq', 'mla_w_ukv', 'w_out'],
              'ffn2': ['ffn2_w_gate', 'ffn2_w_up', 'ffn2_w_down']}
_BLOCK_SMALL = {'ffn1': ['ffn1_norm'],
                'mix': ['mix_norm', 'ssm_log_dt', 'ssm_a_re', 'ssm_a_im', 'ssm_b_re', 'ssm_b_im', 'ssm_c_re', 'ssm_c_im',
                        'ssm_d', 'ssm_b_glu', 'swa_sinks', 'mla_q_norm', 'mla_kv_norm', 'out_norm'],
                'ffn2': ['ffn2_norm']}
_BIG = [n for b in _BLOCKS for n in _BLOCK_BIG[b]]
_SMALL = [n for n in _WEIGHTS if n not in _BIG]


def _pack(vals):
    flat = jnp.concatenate([v.reshape(-1) for v in vals])
    pad = (-flat.shape[0]) % 1024
    return jnp.pad(flat, (0, pad)).reshape(-1, 128)


def _unpack(buf, like):
    flat = buf.reshape(-1)
    out, o = [], 0
    for v in like:
        out.append(flat[o:o + v.size].reshape(v.shape))
        o += v.size
    return out


def kernel(x, positions, ffn1_norm, ffn1_w_gate, ffn1_w_up, ffn1_w_down, mix_norm, w_in, ssm_log_dt, ssm_a_re, ssm_a_im, ssm_b_re, ssm_b_im, ssm_c_re, ssm_c_im, ssm_d, ssm_w_glu, ssm_b_glu, swa_sinks, mla_q_norm, mla_w_uq, mla_kv_norm, mla_w_ukv, out_norm, w_out, ffn2_norm, ffn2_w_gate, ffn2_w_up, ffn2_w_down, final_norm, loss_target, m_ffn1_norm, m_ffn1_w_gate, m_ffn1_w_up, m_ffn1_w_down, m_mix_norm, m_w_in, m_ssm_log_dt, m_ssm_a_re, m_ssm_a_im, m_ssm_b_re, m_ssm_b_im, m_ssm_c_re, m_ssm_c_im, m_ssm_d, m_ssm_w_glu, m_ssm_b_glu, m_swa_sinks, m_mla_q_norm, m_mla_w_uq, m_mla_kv_norm, m_mla_w_ukv, m_out_norm, m_w_out, m_ffn2_norm, m_ffn2_w_gate, m_ffn2_w_up, m_ffn2_w_down, m_final_norm, v_ffn1_norm, v_ffn1_w_gate, v_ffn1_w_up, v_ffn1_w_down, v_mix_norm, v_w_in, v_ssm_log_dt, v_ssm_a_re, v_ssm_a_im, v_ssm_b_re, v_ssm_b_im, v_ssm_c_re, v_ssm_c_im, v_ssm_d, v_ssm_w_glu, v_ssm_b_glu, v_swa_sinks, v_mla_q_norm, v_mla_w_uq, v_mla_kv_norm, v_mla_w_ukv, v_out_norm, v_w_out, v_ffn2_norm, v_ffn2_w_gate, v_ffn2_w_up, v_ffn2_w_down, v_final_norm):
    w = dict(zip(_WEIGHTS, (ffn1_norm, ffn1_w_gate, ffn1_w_up, ffn1_w_down, mix_norm, w_in, ssm_log_dt, ssm_a_re, ssm_a_im, ssm_b_re, ssm_b_im, ssm_c_re, ssm_c_im, ssm_d, ssm_w_glu, ssm_b_glu, swa_sinks, mla_q_norm, mla_w_uq, mla_kv_norm, mla_w_ukv, out_norm, w_out, ffn2_norm, ffn2_w_gate, ffn2_w_up, ffn2_w_down, final_norm)))
    m = dict(zip(_WEIGHTS, (m_ffn1_norm, m_ffn1_w_gate, m_ffn1_w_up, m_ffn1_w_down, m_mix_norm, m_w_in, m_ssm_log_dt, m_ssm_a_re, m_ssm_a_im, m_ssm_b_re, m_ssm_b_im, m_ssm_c_re, m_ssm_c_im, m_ssm_d, m_ssm_w_glu, m_ssm_b_glu, m_swa_sinks, m_mla_q_norm, m_mla_w_uq, m_mla_kv_norm, m_mla_w_ukv, m_out_norm, m_w_out, m_ffn2_norm, m_ffn2_w_gate, m_ffn2_w_up, m_ffn2_w_down, m_final_norm)))
    v = dict(zip(_WEIGHTS, (v_ffn1_norm, v_ffn1_w_gate, v_ffn1_w_up, v_ffn1_w_down, v_mix_norm, v_w_in, v_ssm_log_dt, v_ssm_a_re, v_ssm_a_im, v_ssm_b_re, v_ssm_b_im, v_ssm_c_re, v_ssm_c_im, v_ssm_d, v_ssm_w_glu, v_ssm_b_glu, v_swa_sinks, v_mla_q_norm, v_mla_w_uq, v_mla_kv_norm, v_mla_w_ukv, v_out_norm, v_w_out, v_ffn2_norm, v_ffn2_w_gate, v_ffn2_w_up, v_ffn2_w_down, v_final_norm)))
    n_layers = ffn1_norm.shape[0]

    sizes = [len(_BLOCK_BIG[b]) for b in _BLOCKS]

    x0 = x[0]
    travelling = []
    for l in range(n_layers):
        staged = [_stage_shard("stage_" + n, w[n][l]) for n in _BIG]
        sems, thru = _split_start(f"gstart{l}", [x0] + staged, [3 * k for k in sizes for _ in range(2)],
                                  _gather_start_copies(1, sizes))
        x0 = thru[0]
        travelling.append((sems, thru[1:]))

    def get_block(l, blk, x_now):
        g = _BLOCKS.index(blk)
        sems, staged = travelling[l]
        o = sum(sizes[:g])
        full = _split_wait(f"gwait{l}{blk}", staged[o:o + sizes[g]], sems[2 * g:2 * g + 2], x_now,
                           _gather_wait_copies)
        lw = {n: w[n][l] for n in _BLOCK_SMALL[blk]}
        for n, gth in zip(_BLOCK_BIG[blk], full):
            ns, r, c = gth.shape
            if n in _STACKED:
                lw[_STACKED[n]] = gth.reshape(ns * r, c)
            elif n in _ROW_SHARDED:
                lw[_RENAMED.get(n, n)] = gth.reshape(ns * r, c)
            else:
                lw[n] = jnp.moveaxis(gth, 0, 1).reshape(r, ns * c)
        return lw

    reduced = {n: [None] * n_layers for n in _BIG}
    grads = [dict() for _ in range(n_layers)]
    pending = []

    def finish(after):
        l, blk, sems, contrib, lands = pending.pop()
        k = len(contrib)
        done = _split_wait(f"cwait{l}{blk}", contrib + lands, sems, after, _reduce_copies(0, k))
        halves = [_sum_partials("rs_sum", g, q) for g, q in zip(done[:k], done[k:])]
        for n, g in zip(_BLOCK_BIG[blk], _pair_join_call("rs_join", halves)):
            reduced[n][l] = g

    def on_grads(l, blk, g_, dxs):
        if pending:
            finish(g_[_BLOCK_SMALL[blk][0]])
        for n in _BLOCK_SMALL[blk]:
            grads[l][n] = g_[n]
        contrib = []
        for n in _BLOCK_BIG[blk]:
            r, c = w[n].shape[1:]
            if n in _STACKED:
                g = g_[_STACKED[n]].reshape(N_SHARD, r, c)
            elif n in _ROW_SHARDED:
                g = g_[_RENAMED.get(n, n)].reshape(N_SHARD, r, c)
            else:
                g = jnp.moveaxis(g_[n].reshape(r, N_SHARD, c), 1, 0)
            contrib.append(g)
        k = len(contrib)
        lands = [lax.empty((_N_SLOTS, g.shape[1] // 2, g.shape[2]), g.dtype) for g in contrib]
        sems, thru = _split_start(f"cstart{l}{blk}", list(dxs) + [token[0]] + contrib + lands, [_N_SLOTS * k] * 2,
                                  _reduce_copies(3, k))
        token[0] = thru[2]
        pending.append((l, blk, sems, thru[3:3 + k], thru[3 + k:]))
        return thru[0], thru[1]

    token = [jnp.zeros((8, 128), F32)]
    loss_part, dx, dfinal = _device_step(x0, positions[0], loss_target[0], n_layers, get_block, final_norm, on_grads)
    loss = lax.psum(loss_part, ("x", "y", "c"))

    small_like = [w[n] for n in _SMALL]
    small_g = [jnp.stack([grads[l][n] for l in range(n_layers)]) for n in _SMALL if n != 'final_norm'] + [dfinal]
    packed = _pack(small_g)
    me = 4 * lax.axis_index("x") + 2 * lax.axis_index("y") + lax.axis_index("c")
    land = lax.dynamic_update_slice(lax.empty((_N_DEV,) + packed.shape, F32), packed[None], (me, 0, 0))
    s_sems, s_thru = _split_start("sstart", [token[0], packed, land], [_N_DEV - 1] * 2, _small_copies(1))
    token[0] = s_thru[0]

    grad, delta, new_m, new_v = {}, {}, {}, {}

    def update(n):
        shp = w[n].shape
        two = lambda t: t.reshape(shp[0] * shp[1], shp[2])
        grad[n] = jnp.stack(reduced[n])
        d2, m2, v2 = _adamw("adam_" + n, two(w[n]), two(grad[n]), two(m[n]), two(v[n]), token=token[0])
        delta[n], new_m[n], new_v[n] = d2.reshape(shp), m2.reshape(shp), v2.reshape(shp)
        return d2

    last_block = pending[-1][1]
    for n in _BIG:
        if n not in _BLOCK_BIG[last_block]:
            meanwhile = update(n)
    finish(meanwhile)
    g_small = _sum_devices("small_sum", _split_wait("swait", s_thru[1:], s_sems, meanwhile, _small_copies(0))[1])
    d_small, m_small, v_small = _adamw("adam_small", _pack(small_like), g_small, _pack([m[n] for n in _SMALL]),
                                       _pack([v[n] for n in _SMALL]))
    for n, gv, dv_, mv, vv in zip(_SMALL, _unpack(g_small, small_like), _unpack(d_small, small_like),
                                  _unpack(m_small, small_like), _unpack(v_small, small_like)):
        grad[n], delta[n], new_m[n], new_v[n] = gv, dv_, mv, vv
    for n in _BLOCK_BIG[last_block]:
        update(n)
    return (loss, dx[None], *[grad[n] for n in _WEIGHTS], *[delta[n] for n in _WEIGHTS],
            *[new_m[n] for n in _WEIGHTS], *[new_v[n] for n in _WEIGHTS])
```

```python
import functools
import math

import jax
import jax.numpy as jnp
import numpy as np
from jax import lax
from jax.experimental import pallas as pl
from jax.experimental.pallas import tpu as pltpu

F32 = jnp.float32
BF16 = jnp.bfloat16

EPS = 1e-6
ROPE_THETA = 10000.0
SSM_GROUP = 16
SSM_STATE = 64
S5_TILE = 8
SWA_HEADS = 8
SWA_KV_HEADS = 2
SWA_HEAD_DIM = 64
SWA_BLOCK = 128
SWA_WIDTH = SWA_HEADS * SWA_HEAD_DIM
SWA_KV_WIDTH = SWA_KV_HEADS * SWA_HEAD_DIM
MLA_HEADS = 8
MLA_Q_RANK = 512
MLA_KV_RANK = 256
MLA_NOPE = 128
MLA_ROPE = 64
MLA_V = 128
MLA_SLOT = 256
MLA_WIDTH = MLA_HEADS * MLA_V
ROPE_HALF = 32

ADAM_LR = 0.001
ADAM_B1 = 0.9
ADAM_B2 = 0.999
ADAM_EPS = 1e-08
ADAM_WD = 0.01
ADAM_STEP = 10

N_SHARD = 4
MESH = pl.DeviceIdType.MESH

NN = ((1,), (0,))
NT = ((1,), (1,))
TN = ((0,), (0,))


def _pcall(body, **kw):
    return pl.pallas_call(body, **kw)


def _tile(n, want, align=16):
    if n <= want:
        return n
    t = want - want % align
    while t >= align:
        if n % t == 0:
            return t
        t -= align
    return n


def _mm(name, grid, pairs, acc_shapes, outs, epilogue, extras=()):
    nk = grid[2]
    n_p, n_e, n_o, n_a = len(pairs), len(extras), len(outs), len(acc_shapes)
    dims_idx = [(p[4], p[5]) for p in pairs]

    def body(*refs):
        ab = refs[:2 * n_p]
        ex = refs[2 * n_p:2 * n_p + n_e]
        o = refs[2 * n_p + n_e:2 * n_p + n_e + n_o]
        accs = refs[2 * n_p + n_e + n_o:]

        def partial_sums():
            sums = [None] * n_a
            for p, (dims, ai) in enumerate(dims_idx):
                a = ab[2 * p][...].astype(BF16)
                b = ab[2 * p + 1][...].astype(BF16)
                d = lax.dot_general(a, b, (dims, ((), ())), preferred_element_type=F32)
                sums[ai] = d if sums[ai] is None else sums[ai] + d
            return sums

        def finish(vals):
            res = epilogue(vals, [e[...] for e in ex])
            for r, oref in zip(res, o):
                oref[...] = r.astype(oref.dtype)

        if nk == 1:
            finish(partial_sums())
        else:
            k = pl.program_id(2)

            @pl.when(k == 0)
            def _():
                for acc in accs:
                    acc[...] = jnp.zeros_like(acc)

            for acc, s in zip(accs, partial_sums()):
                acc[...] += s

            @pl.when(k == nk - 1)
            def _():
                finish([acc[...] for acc in accs])

    in_arrays, in_specs = [], []
    for a, a_spec, b, b_spec, _, _ in pairs:
        in_arrays += [a, b]
        in_specs += [a_spec, b_spec]
    for e, e_spec in extras:
        in_arrays.append(e)
        in_specs.append(e_spec)
    res = _pcall(
        body, name=name, grid=grid, in_specs=in_specs,
        out_specs=[o[2] for o in outs],
        out_shape=[jax.ShapeDtypeStruct(o[0], o[1]) for o in outs],
        scratch_shapes=[] if nk == 1 else [pltpu.VMEM(s, F32) for s in acc_shapes],
        compiler_params=pltpu.CompilerParams(dimension_semantics=("parallel", "parallel", "arbitrary")),
    )(*in_arrays)
    return res


def _bs(shape, fn):
    return pl.BlockSpec(shape, fn)


def _mm_simple(name, a, b, dims, *, tm=512, tn=512, tk=512, out_dtype=F32, epilogue=None, extras=(), scale=None):
    if dims == NN:
        (M, K), N = a.shape, b.shape[1]
    elif dims == NT:
        (M, K), N = a.shape, b.shape[0]
    else:
        (K, M), N = a.shape, b.shape[1]
    tm, tn, tk = _tile(M, tm, 128), _tile(N, tn, 128), _tile(K, tk, 128)
    if dims == NN:
        a_spec, b_spec = _bs((tm, tk), lambda i, j, k: (i, k)), _bs((tk, tn), lambda i, j, k: (k, j))
    elif dims == NT:
        a_spec, b_spec = _bs((tm, tk), lambda i, j, k: (i, k)), _bs((tn, tk), lambda i, j, k: (j, k))
    else:
        a_spec, b_spec = _bs((tk, tm), lambda i, j, k: (k, i)), _bs((tk, tn), lambda i, j, k: (k, j))
    ex = []
    for e in extras:
        if e.shape[0] == 1:
            ex.append((e, _bs((1, tn), lambda i, j, k: (0, j))))
        else:
            ex.append((e, _bs((tm, tn), lambda i, j, k: (i, j))))
    if epilogue is None:
        if scale is None:
            epilogue = lambda accs, ex_: (accs[0],)
        else:
            epilogue = lambda accs, ex_: (accs[0] * scale,)
        out_dtypes = (out_dtype,)
    else:
        out_dtypes = out_dtype if isinstance(out_dtype, tuple) else (out_dtype,)
    outs = [((M, N), dt, _bs((tm, tn), lambda i, j, k: (i, j))) for dt in out_dtypes]
    res = _mm(name, (M // tm, N // tn, K // tk), [(a, a_spec, b, b_spec, dims, 0)], [(tm, tn)], outs, epilogue, ex)
    return res[0] if len(res) == 1 else res


def _rowwise(name, fn, tiles, bcasts, outs, accs=(), *, tm=256):
    rows = tiles[0].shape[0]
    tm = _tile(rows, tm)
    n_t, n_b, n_o, n_a = len(tiles), len(bcasts), len(outs), len(accs)

    def body(*refs):
        t = [r[...] for r in refs[:n_t]]
        b = [r[...] for r in refs[n_t:n_t + n_b]]
        o = refs[n_t + n_b:n_t + n_b + n_o]
        a = refs[n_t + n_b + n_o:]
        ov, av = fn(t, b)
        for r, val in zip(o, ov):
            r[...] = val.astype(r.dtype)
        if n_a:
            @pl.when(pl.program_id(0) == 0)
            def _():
                for r in a:
                    r[...] = jnp.zeros_like(r)
            for r, val in zip(a, av):
                r[...] += val

    in_specs = [_bs((tm, x.shape[1]), lambda i: (i, 0)) for x in tiles]
    in_specs += [_bs(x.shape, lambda i, nd=x.ndim: (0,) * nd) for x in bcasts]
    out_specs = [_bs((tm, w), lambda i: (i, 0)) for w, _ in outs]
    out_specs += [_bs((1, w), lambda i: (0, 0)) for w in accs]
    out_shape = [jax.ShapeDtypeStruct((rows, w), dt) for w, dt in outs]
    out_shape += [jax.ShapeDtypeStruct((1, w), F32) for w in accs]
    return _pcall(
        body, name=name, grid=(rows // tm,), in_specs=in_specs, out_specs=out_specs, out_shape=out_shape,
        compiler_params=pltpu.CompilerParams(dimension_semantics=("arbitrary",)),
    )(*tiles, *bcasts)


def _colsum(v):
    return jnp.sum(v, axis=0, keepdims=True)


def _rms_fwd(name, x, gain, out_dtype=BF16):
    def fn(t, b):
        xv = t[0]
        r = lax.rsqrt(jnp.mean(xv * xv, axis=-1, keepdims=True) + EPS)
        return (xv * r * b[0], r), ()
    w = x.shape[1]
    return _rowwise(name, fn, [x], [gain.reshape(1, w)], [(w, out_dtype), (1, F32)])


def _rms_bwd(name, dh, x, rstd, gain, dres=None, with_bf16=False):
    def fn(t, b):
        dhv, xv, r = t[0], t[1], t[2]
        xh = xv * r
        dxh = dhv * b[0]
        dx = r * (dxh - xh * jnp.mean(dxh * xh, axis=-1, keepdims=True))
        if dres is not None:
            dx = dx + t[3]
        return ((dx, dx) if with_bf16 else (dx,)), (_colsum(dhv * xh),)
    w = x.shape[1]
    tiles = [dh, x, rstd] + ([dres] if dres is not None else [])
    outs = [(w, F32), (w, BF16)] if with_bf16 else [(w, F32)]
    return _rowwise(name, fn, tiles, [gain.reshape(1, w)], outs, [w])


def _sigmoid(v):
    return 1.0 / (1.0 + jnp.exp(-v))


def _ffn_fwd(tag, x, gain, wg2, wu2, wd):
    T, D = x.shape
    fs = wg2.shape[1]
    h, rstd = _rms_fwd(tag + "_rms", x, gain)
    tm = _tile(T, 256)
    a_spec = _bs((tm, D), lambda i, j, k: (j, 0))
    w_spec = _bs((D, fs), lambda i, j, k: (i, 0))
    o_spec = _bs((tm, fs), lambda i, j, k: (j, i))

    def epi(accs, ex):
        a, b = accs
        return a, b, a * _sigmoid(a) * b

    a, b, t = _mm(
        tag + "_up", (N_SHARD, T // tm, 1),
        [(h, a_spec, wg2, w_spec, NN, 0), (h, a_spec, wu2, w_spec, NN, 1)],
        [(tm, fs), (tm, fs)],
        [((T, N_SHARD * fs), BF16, o_spec)] * 3,
        epi)
    if callable(wd):
        wd = wd(t)
    y = _mm_simple(tag + "_down", t, wd, NN, tm=512, tn=D, tk=fs // 2 if fs % 256 == 0 else fs,
                   epilogue=lambda accs, ex: (ex[0] + 0.5 * accs[0],), extras=[x])
    return y, (x, h, rstd, a, b, t), wd


def _ffn_bwd(tag, dy, dy16, saved, gain, wg2, wu2, wd, early=None, keys=None):
    x, h, rstd, a, b, t = saved
    T, D = x.shape
    fs = wg2.shape[1]
    F = N_SHARD * fs
    dwd = _mm_simple(tag + "_dwd", t, dy16, TN, tm=fs, tn=D, tk=512, scale=0.5, out_dtype=BF16)
    if early is not None:
        dy16 = early({keys[2]: dwd}, dy16)

    def epi(accs, ex):
        dt = 0.5 * accs[0]
        av, bv = ex[0].astype(F32), ex[1].astype(F32)
        sig = _sigmoid(av)
        da = dt * bv * (sig * (1.0 + av * (1.0 - sig)))
        db = dt * (av * sig)
        return da, db

    tn = 512 if F % 512 == 0 else (256 if F % 256 == 0 else 128)
    da, db = _mm_simple(tag + "_dt", dy16, wd, NT, tm=1024, tn=tn, tk=D, out_dtype=(BF16, BF16), epilogue=epi,
                        extras=[a, b])
    tm = _tile(D, 1024)
    per = D // tm
    tk = _tile(T, 512)
    h_spec = _bs((tk, tm), lambda i, j, k: (k, i % per))
    d_spec = _bs((tk, fs), lambda i, j, k: (k, i // per))
    o_spec = _bs((tm, fs), lambda i, j, k: (i, 0))
    dwg2, dwu2 = _mm(
        tag + "_dwgu", (N_SHARD * per, 1, T // tk),
        [(h, h_spec, da, d_spec, TN, 0), (h, h_spec, db, d_spec, TN, 1)],
        [(tm, fs), (tm, fs)],
        [((N_SHARD * D, fs), BF16, o_spec), ((N_SHARD * D, fs), BF16, o_spec)],
        lambda accs, ex: tuple(accs))
    tm2 = _tile(T, 512)
    tn2 = _tile(D, 1024)
    per2 = D // tn2
    if early is not None:
        da = early({keys[0]: dwg2, keys[1]: dwu2}, da)
    g_spec = _bs((tm2, fs), lambda i, j, k: (i, k))
    w_spec = _bs((tn2, fs), lambda i, j, k: (k * per2 + j, 0))
    o2 = _bs((tm2, tn2), lambda i, j, k: (i, j))
    dh, = _mm(
        tag + "_dh", (T // tm2, per2, N_SHARD),
        [(da, g_spec, wg2, w_spec, NT, 0), (db, g_spec, wu2, w_spec, NT, 0)],
        [(tm2, tn2)], [((T, D), F32, o2)], lambda accs, ex: (accs[0],))
    dx, dx16, dgain = _rms_bwd(tag + "_rmsb", dh, x, rstd, gain, dres=dy, with_bf16=True)
    return dx, dx16, dgain, dwg2, dwu2, dwd


def _rope_tables(positions):
    inv_freq = ROPE_THETA ** (-jnp.arange(0, 2 * ROPE_HALF, 2, dtype=F32) / (2 * ROPE_HALF))
    ang = positions.astype(F32)[:, None] * inv_freq
    c, s = jnp.cos(ang), jnp.sin(ang)
    c64 = jnp.concatenate([c, c], axis=1)
    s64 = jnp.concatenate([-s, s], axis=1)
    T = positions.shape[0]
    one, zero = jnp.ones((T, MLA_NOPE), F32), jnp.zeros((T, MLA_NOPE), F32)
    pad1, pad0 = jnp.ones((T, MLA_SLOT - MLA_NOPE - MLA_ROPE), F32), jnp.zeros((T, MLA_SLOT - MLA_NOPE - MLA_ROPE), F32)
    c256 = jnp.concatenate([one, c64, pad1], axis=1)
    s256 = jnp.concatenate([zero, s64, pad0], axis=1)
    return c64, s64, c256, s256


def _half_swap(v):
    w = v.shape[1]
    lane = lax.broadcasted_iota(jnp.int32, v.shape, 1)
    first = (lane % (2 * ROPE_HALF)) < ROPE_HALF
    return jnp.where(first, pltpu.roll(v, w - ROPE_HALF, 1), pltpu.roll(v, ROPE_HALF, 1))


def _rope(name, x, ctab, stab, backward=False, out_dtype=F32):
    reps = x.shape[1] // ctab.shape[1]

    def fn(t, b):
        xv = t[0]
        c = jnp.tile(t[1], (1, reps)) if reps > 1 else t[1]
        s = jnp.tile(t[2], (1, reps)) if reps > 1 else t[2]
        if backward:
            return (xv * c + _half_swap(xv * s),), ()
        return (xv * c + _half_swap(xv) * s,), ()
    return _rowwise(name, fn, [x, ctab, stab], [], [(x.shape[1], out_dtype)])[0]


def _s5_prep(log_dt, a_re, a_im, b_re, b_im, c_re, c_im):
    G, P = a_re.shape
    C = b_re.shape[-1]
    dt = jnp.exp(log_dt)[:, None]
    mag = jnp.exp(a_re * dt)
    abar_r = mag * jnp.cos(a_im * dt)
    abar_i = mag * jnp.sin(a_im * dt)
    den = a_re * a_re + a_im * a_im
    nr = abar_r - 1.0
    qr = (nr * a_re + abar_i * a_im) / den
    qi = (abar_i * a_re - nr * a_im) / den
    bbar_r = qr[..., None] * b_re - qi[..., None] * b_im
    bbar_i = qr[..., None] * b_im + qi[..., None] * b_re
    nt = G // S5_TILE
    eye = jnp.eye(S5_TILE, dtype=F32)
    tiles_b = lambda t: jnp.einsum('tgpc,gh->tgchp', t.reshape(nt, S5_TILE, P, C), eye).reshape(
        nt, S5_TILE * C, S5_TILE * P)
    tiles_c = lambda t: jnp.einsum('tgcp,gh->tgphc', t.reshape(nt, S5_TILE, C, P), eye).reshape(
        nt, S5_TILE * P, S5_TILE * C)
    return (abar_r.reshape(1, G * P), abar_i.reshape(1, G * P), tiles_b(bbar_r), tiles_b(bbar_i), tiles_c(c_re),
            -tiles_c(c_im))


def _scan_lanes(gp):
    for tl in (1024, 512, 256):
        if gp % tl == 0:
            return tl
    return 128


def _scan_fwd(name, bu_r, bu_i, a_r, a_i):
    T, gp = bu_r.shape
    tl = _scan_lanes(gp)
    tc = _tile(T, 256)

    def body(br_ref, bi_ref, ar_ref, ai_ref, xr_ref, xi_ref, pr_ref, pi_ref, sr, si):
        @pl.when(pl.program_id(1) == 0)
        def _():
            sr[...] = jnp.zeros_like(sr)
            si[...] = jnp.zeros_like(si)
        ar, ai = ar_ref[...], ai_ref[...]

        def step(g, carry):
            xr, xi = carry
            base = pl.multiple_of(g * 8, 8)
            b_r = br_ref[pl.ds(base, 8), :]
            b_i = bi_ref[pl.ds(base, 8), :]
            rows_r, rows_i, prev_r, prev_i = [], [], [], []
            for j in range(8):
                prev_r.append(xr)
                prev_i.append(xi)
                nr = ar * xr - ai * xi + b_r[j:j + 1, :]
                ni = ar * xi + ai * xr + b_i[j:j + 1, :]
                xr, xi = nr, ni
                rows_r.append(xr)
                rows_i.append(xi)
            xr_ref[pl.ds(base, 8), :] = jnp.concatenate(rows_r, axis=0)
            xi_ref[pl.ds(base, 8), :] = jnp.concatenate(rows_i, axis=0)
            pr_ref[pl.ds(base, 8), :] = jnp.concatenate(prev_r, axis=0)
            pi_ref[pl.ds(base, 8), :] = jnp.concatenate(prev_i, axis=0)
            return xr, xi

        xr, xi = lax.fori_loop(0, tc // 8, step, (sr[...], si[...]))
        sr[...] = xr
        si[...] = xi

    spec = _bs((tc, tl), lambda l, t: (t, l))
    a_spec = _bs((1, tl), lambda l, t: (0, l))
    return _pcall(
        body, name=name, grid=(gp // tl, T // tc),
        in_specs=[spec, spec, a_spec, a_spec], out_specs=[spec] * 4,
        out_shape=[jax.ShapeDtypeStruct((T, gp), F32)] * 4,
        scratch_shapes=[pltpu.VMEM((1, tl), F32), pltpu.VMEM((1, tl), F32)],
        compiler_params=pltpu.CompilerParams(dimension_semantics=("parallel", "arbitrary")),
    )(bu_r, bu_i, a_r, a_i)


def _scan_bwd(name, g_r, g_i, p_r, p_i, a_r, a_i):
    T, gp = g_r.shape
    tl = _scan_lanes(gp)
    tc = _tile(T, 256)
    nt = T // tc
    ng = tc // 8

    def body(gr_ref, gi_ref, pr_ref, pi_ref, ar_ref, ai_ref, lr_ref, li_ref, dar_ref, dai_ref, sr, si, accr, acci):
        t = pl.program_id(1)

        @pl.when(t == 0)
        def _():
            sr[...] = jnp.zeros_like(sr)
            si[...] = jnp.zeros_like(si)
            accr[...] = jnp.zeros_like(accr)
            acci[...] = jnp.zeros_like(acci)
        ar, ai = ar_ref[...], ai_ref[...]

        def step(kk, carry):
            lr, li = carry
            base = pl.multiple_of((ng - 1 - kk) * 8, 8)
            gr8 = gr_ref[pl.ds(base, 8), :]
            gi8 = gi_ref[pl.ds(base, 8), :]
            rows_r, rows_i = [None] * 8, [None] * 8
            for j in range(7, -1, -1):
                nr = gr8[j:j + 1, :] + ar * lr + ai * li
                ni = gi8[j:j + 1, :] - ai * lr + ar * li
                lr, li = nr, ni
                rows_r[j] = lr
                rows_i[j] = li
            lam_r = jnp.concatenate(rows_r, axis=0)
            lam_i = jnp.concatenate(rows_i, axis=0)
            lr_ref[pl.ds(base, 8), :] = lam_r
            li_ref[pl.ds(base, 8), :] = lam_i
            pr8 = pr_ref[pl.ds(base, 8), :]
            pi8 = pi_ref[pl.ds(base, 8), :]
            accr[...] += lam_r * pr8 + lam_i * pi8
            acci[...] += lam_i * pr8 - lam_r * pi8
            return lr, li

        lr, li = lax.fori_loop(0, ng, step, (sr[...], si[...]))
        sr[...] = lr
        si[...] = li

        @pl.when(t == nt - 1)
        def _():
            dar_ref[...] = jnp.sum(accr[...], axis=0, keepdims=True)
            dai_ref[...] = jnp.sum(acci[...], axis=0, keepdims=True)

    spec = _bs((tc, tl), lambda l, t: (nt - 1 - t, l))
    a_spec = _bs((1, tl), lambda l, t: (0, l))
    return _pcall(
        body, name=name, grid=(gp // tl, nt),
        in_specs=[spec] * 4 + [a_spec, a_spec], out_specs=[spec, spec, a_spec, a_spec],
        out_shape=[jax.ShapeDtypeStruct((T, gp), F32)] * 2 + [jax.ShapeDtypeStruct((1, gp), F32)] * 2,
        scratch_shapes=[pltpu.VMEM((1, tl), F32), pltpu.VMEM((1, tl), F32),
                        pltpu.VMEM((8, tl), F32), pltpu.VMEM((8, tl), F32)],
        compiler_params=pltpu.CompilerParams(dimension_semantics=("parallel", "arbitrary")),
    )(g_r, g_i, p_r, p_i, a_r, a_i)


_GELU_C = math.sqrt(2.0 / math.pi)


def _gelu(v):
    return 0.5 * v * (1.0 + jnp.tanh(_GELU_C * (v + 0.044715 * v * v * v)))


def _gelu_grad(v):
    th = jnp.tanh(_GELU_C * (v + 0.044715 * v * v * v))
    return 0.5 * (1.0 + th) + 0.5 * v * (1.0 - th * th) * _GELU_C * (1.0 + 3.0 * 0.044715 * v * v)


def _s5_specs(T, nt, cu, cs):
    tm = _tile(T, 512)
    return dict(
        tm=tm,
        chan=_bs((tm, cu), lambda i, j, k: (i, j)), state=_bs((tm, cs), lambda i, j, k: (i, j)),
        b=_bs((None, cu, cs), lambda i, j, k: (j, 0, 0)), c=_bs((None, cs, cu), lambda i, j, k: (j, 0, 0)),
        row=_bs((1, cu), lambda i, j, k: (0, j)))


def _s5_fwd(tag, u, prep, d_skip, w_glu, b_glu):
    a_r, a_i, b_r, b_i, c_r, c_in = prep
    T, W = u.shape
    gp = a_r.shape[1]
    nt, cu, cs = b_r.shape
    sp = _s5_specs(T, nt, cu, cs)
    tm = sp['tm']
    bu_r, bu_i = _mm(tag + "_bu", (T // tm, nt, 1),
                     [(u, sp['chan'], b_r, sp['b'], NN, 0), (u, sp['chan'], b_i, sp['b'], NN, 1)],
                     [(tm, cs)] * 2, [((T, gp), F32, sp['state'])] * 2, lambda accs, ex: tuple(accs))
    x_r, x_i, p_r, p_i = _scan_fwd(tag + "_scan", bu_r, bu_i, a_r, a_i)

    def epi(accs, ex):
        y = accs[0] + ex[1] * ex[0]
        return y, _gelu(y)

    ypre, yg = _mm(tag + "_y", (T // tm, nt, 1),
                   [(x_r, sp['state'], c_r, sp['c'], NN, 0), (x_i, sp['state'], c_in, sp['c'], NN, 0)],
                   [(tm, cu)], [((T, W), F32, sp['chan'])] * 2, epi,
                   [(u, sp['chan']), (d_skip.reshape(1, W), sp['row'])])

    def epi2(accs, ex):
        pre = accs[0] + ex[1]
        return ex[0] * _sigmoid(pre), pre

    out, pre = _mm_simple(tag + "_glu", yg, w_glu, NN, tm=512, tn=W, tk=W, out_dtype=(F32, F32), epilogue=epi2,
                          extras=[yg, b_glu.reshape(1, W)])
    return out, (u, x_r, x_i, p_r, p_i, ypre, yg, pre)


def _s5_bwd(tag, d_out, saved, prep, d_skip, w_glu):
    u, x_r, x_i, p_r, p_i, ypre, yg, pre = saved
    a_r, a_i, b_r, b_i, c_r, c_in = prep
    T, W = u.shape

    def gate_fn(t, b):
        gate = _sigmoid(t[2])
        dpre = t[0] * t[1] * gate * (1.0 - gate)
        return (dpre, t[0] * gate), (_colsum(dpre),)

    dpre, tmp, db_glu = _rowwise(tag + "_gateb", gate_fn, [d_out, yg, pre], [], [(W, F32), (W, F32)], [W])
    dw_glu = _mm_simple(tag + "_dwglu", yg, dpre, TN, tm=W, tn=W, tk=512, out_dtype=BF16)
    dy = _mm_simple(tag + "_dyg", dpre, w_glu, NT, tm=512, tn=W, tk=W,
                    epilogue=lambda accs, ex: ((accs[0] + ex[0]) * _gelu_grad(ex[1]),), extras=[tmp, ypre])
    dd, = _rowwise(tag + "_dd", lambda t, b: ((), (_colsum(t[0] * t[1]),)), [dy, u], [], [], [W])
    gp = a_r.shape[1]
    nt, cu, cs = b_r.shape
    sp = _s5_specs(T, nt, cu, cs)
    tm = sp['tm']
    dx_r, dx_i = _mm(tag + "_dx", (T // tm, nt, 1),
                     [(dy, sp['chan'], c_r, sp['c'], NT, 0), (dy, sp['chan'], c_in, sp['c'], NT, 1)],
                     [(tm, cs)] * 2, [((T, gp), F32, sp['state'])] * 2, lambda accs, ex: tuple(accs))
    tk = _tile(T, 512)
    t_chan = _bs((tk, cu), lambda i, j, k: (k, i))
    t_state = _bs((tk, cs), lambda i, j, k: (k, i))
    o_b = _bs((None, cu, cs), lambda i, j, k: (i, 0, 0))
    o_c = _bs((None, cs, cu), lambda i, j, k: (i, 0, 0))
    dc_r, dc_in = _mm(tag + "_dc", (nt, 1, T // tk),
                      [(x_r, t_state, dy, t_chan, TN, 0), (x_i, t_state, dy, t_chan, TN, 1)],
                      [(cs, cu)] * 2, [((nt, cs, cu), F32, o_c)] * 2, lambda accs, ex: tuple(accs))
    lam_r, lam_i, da_r, da_i = _scan_bwd(tag + "_scanb", dx_r, dx_i, p_r, p_i, a_r, a_i)
    du, = _mm(tag + "_du", (T // tm, nt, 1),
              [(lam_r, sp['state'], b_r, sp['b'], NT, 0), (lam_i, sp['state'], b_i, sp['b'], NT, 0)],
              [(tm, cu)], [((T, W), F32, sp['chan'])], lambda accs, ex: (accs[0] + ex[1] * ex[0],),
              [(dy, sp['chan']), (d_skip.reshape(1, W), sp['row'])])
    db_r, db_i = _mm(tag + "_db", (nt, 1, T // tk),
                     [(u, t_chan, lam_r, t_state, TN, 0), (u, t_chan, lam_i, t_state, TN, 1)],
                     [(cu, cs)] * 2, [((nt, cu, cs), F32, o_b)] * 2, lambda accs, ex: tuple(accs))
    return du, (da_r, da_i, db_r, db_i, dc_r, dc_in), dd, dw_glu, db_glu


def _swa_mask(n):
    B = SWA_BLOCK
    r = lax.broadcasted_iota(jnp.int32, (B, 2 * B), 0)
    c = lax.broadcasted_iota(jnp.int32, (B, 2 * B), 1)
    d = r + B - c
    return (d >= 0) & (d < B) & ((n > 0) | (c >= B))


def _swa_specs(T):
    B = SWA_BLOCK
    cur = lambda w: _bs((B, w), lambda n: (n, 0))
    prev = lambda w: _bs((B, w), lambda n: (jnp.maximum(n - 1, 0), 0))
    return cur, prev


def _swa_fwd(name, q, k, v, sinks_b):
    T = q.shape[0]
    B = SWA_BLOCK
    scale = SWA_HEAD_DIM ** -0.5
    per_kv = SWA_HEADS // SWA_KV_HEADS

    def body(q_ref, kc_ref, kp_ref, vc_ref, vp_ref, s_ref, o_ref, l_ref):
        n = pl.program_id(0)
        kcat = jnp.concatenate([kp_ref[...], kc_ref[...]], axis=0).astype(BF16)
        vcat = jnp.concatenate([vp_ref[...], vc_ref[...]], axis=0).astype(BF16)
        mask = _swa_mask(n)
        lane = lax.broadcasted_iota(jnp.int32, (B, 128), 1)
        lo = lane < SWA_HEAD_DIM
        lse_out = jnp.zeros((B, 128), F32)
        for jb in range(SWA_HEADS // 2):
            qblk = q_ref[:, jb * 128:(jb + 1) * 128]
            h = (2 * jb) // per_kv
            half_h = lo if h == 0 else jnp.logical_not(lo)
            outs = []
            for e in range(2):
                j = 2 * jb + e
                qa = qblk if e == h else pltpu.roll(qblk, SWA_HEAD_DIM, 1)
                qm = jnp.where(half_h, qa, 0.0).astype(BF16)
                s = lax.dot_general(qm, kcat, (NT, ((), ())), preferred_element_type=F32) * scale
                s = jnp.where(mask, s, -jnp.inf)
                sk = s_ref[j:j + 1, 0:1]
                m = jnp.maximum(jnp.max(s, axis=1, keepdims=True), sk)
                ex = jnp.exp(s - m)
                den = jnp.sum(ex, axis=1, keepdims=True) + jnp.exp(sk - m)
                p = ex / den
                r = lax.dot_general(p.astype(BF16), vcat, (NN, ((), ())), preferred_element_type=F32)
                outs.append(r if e == h else pltpu.roll(r, SWA_HEAD_DIM, 1))
                lse_out = jnp.where(lane == j, m + jnp.log(den), lse_out)
            o_ref[:, jb * 128:(jb + 1) * 128] = jnp.where(lo, outs[0], outs[1])
        l_ref[...] = lse_out

    cur, prev = _swa_specs(T)
    return _pcall(
        body, name=name, grid=(T // B,),
        in_specs=[cur(SWA_WIDTH), cur(SWA_KV_WIDTH), prev(SWA_KV_WIDTH), cur(SWA_KV_WIDTH), prev(SWA_KV_WIDTH),
                  _bs(sinks_b.shape, lambda n: (0, 0))],
        out_specs=[cur(SWA_WIDTH), cur(128)],
        out_shape=[jax.ShapeDtypeStruct((T, SWA_WIDTH), F32), jax.ShapeDtypeStruct((T, 128), F32)],
        compiler_params=pltpu.CompilerParams(dimension_semantics=("parallel",)),
    )(q, k, k, v, v, sinks_b)


def _swa_bwd(name, q, k, v, sinks_b, o, lse, do):
    T = q.shape[0]
    B = SWA_BLOCK
    scale = SWA_HEAD_DIM ** -0.5
    per_kv = SWA_HEADS // SWA_KV_HEADS

    def body(q_ref, kc_ref, kp_ref, vc_ref, vp_ref, s_ref, o_ref, l_ref, do_ref,
             dq_ref, dkc_ref, dkp_ref, dvc_ref, dvp_ref, ds_ref):
        n = pl.program_id(0)

        @pl.when(n == 0)
        def _():
            ds_ref[...] = jnp.zeros_like(ds_ref)
        kcat = jnp.concatenate([kp_ref[...], kc_ref[...]], axis=0).astype(BF16)
        vcat = jnp.concatenate([vp_ref[...], vc_ref[...]], axis=0).astype(BF16)
        mask = _swa_mask(n)
        lane = lax.broadcasted_iota(jnp.int32, (B, 128), 1)
        lane1 = lax.broadcasted_iota(jnp.int32, (1, 128), 1)
        lo = lane < SWA_HEAD_DIM
        lblk = l_ref[...]
        dk = jnp.zeros((2 * B, 128), F32)
        dv = jnp.zeros((2 * B, 128), F32)
        dsink = jnp.zeros((1, 128), F32)
        for jb in range(SWA_HEADS // 2):
            sl = slice(jb * 128, (jb + 1) * 128)
            qblk, doblk = q_ref[:, sl], do_ref[:, sl]
            prod = doblk * o_ref[:, sl]
            h = (2 * jb) // per_kv
            half_h = lo if h == 0 else jnp.logical_not(lo)
            parts = []
            for e in range(2):
                j = 2 * jb + e
                half_e = lo if e == 0 else jnp.logical_not(lo)
                dsum = jnp.sum(jnp.where(half_e, prod, 0.0), axis=1, keepdims=True)
                lj = jnp.sum(jnp.where(lane == j, lblk, 0.0), axis=1, keepdims=True)
                qa = qblk if e == h else pltpu.roll(qblk, SWA_HEAD_DIM, 1)
                da = doblk if e == h else pltpu.roll(doblk, SWA_HEAD_DIM, 1)
                qm = jnp.where(half_h, qa, 0.0).astype(BF16)
                dm = jnp.where(half_h, da, 0.0).astype(BF16)
                s = lax.dot_general(qm, kcat, (NT, ((), ())), preferred_element_type=F32) * scale
                p = jnp.where(mask, jnp.exp(s - lj), 0.0)
                dp = lax.dot_general(dm, vcat, (NT, ((), ())), preferred_element_type=F32)
                dsb = (p * (dp - dsum) * scale).astype(BF16)
                dqa = lax.dot_general(dsb, kcat, (NN, ((), ())), preferred_element_type=F32)
                parts.append(dqa if e == h else pltpu.roll(dqa, SWA_HEAD_DIM, 1))
                dk = dk + lax.dot_general(dsb, qm, (TN, ((), ())), preferred_element_type=F32)
                dv = dv + lax.dot_general(p.astype(BF16), dm, (TN, ((), ())), preferred_element_type=F32)
                sk = s_ref[j:j + 1, 0:1]
                contrib = jnp.sum(jnp.exp(sk - lj) * dsum, axis=0, keepdims=True)
                dsink = jnp.where(lane1 == j, dsink - contrib, dsink)
            dq_ref[:, sl] = jnp.where(lo, parts[0], parts[1])
        dkp_ref[...] = dk[:B]
        dkc_ref[...] = dk[B:]
        dvp_ref[...] = dv[:B]
        dvc_ref[...] = dv[B:]
        ds_ref[...] += dsink

    cur, prev = _swa_specs(T)
    kv = jax.ShapeDtypeStruct((T, SWA_KV_WIDTH), F32)
    return _pcall(
        body, name=name, grid=(T // B,),
        in_specs=[cur(SWA_WIDTH), cur(SWA_KV_WIDTH), prev(SWA_KV_WIDTH), cur(SWA_KV_WIDTH), prev(SWA_KV_WIDTH),
                  _bs(sinks_b.shape, lambda n: (0, 0)), cur(SWA_WIDTH), cur(128), cur(SWA_WIDTH)],
        out_specs=[cur(SWA_WIDTH)] + [cur(SWA_KV_WIDTH)] * 4 + [_bs((1, 128), lambda n: (0, 0))],
        out_shape=[jax.ShapeDtypeStruct((T, SWA_WIDTH), F32), kv, kv, kv, kv, jax.ShapeDtypeStruct((1, 128), F32)],
        compiler_params=pltpu.CompilerParams(dimension_semantics=("arbitrary",)),
    )(q, k, k, v, v, sinks_b, o, lse, do)


def _shift_add(name, cur, prv):
    T, W = cur.shape
    B = SWA_BLOCK
    nb = T // B

    def body(c_ref, p_ref, o_ref):
        n = pl.program_id(0)
        o_ref[...] = c_ref[...] + jnp.where(n < nb - 1, p_ref[...], 0.0)

    return _pcall(
        body, name=name, grid=(nb,),
        in_specs=[_bs((B, W), lambda n: (n, 0)), _bs((B, W), lambda n: (jnp.minimum(n + 1, nb - 1), 0))],
        out_specs=_bs((B, W), lambda n: (n, 0)), out_shape=jax.ShapeDtypeStruct((T, W), F32),
        compiler_params=pltpu.CompilerParams(dimension_semantics=("parallel",)),
    )(cur, prv)


MLA_BLOCK = 512
_MLA_SCALE = (MLA_NOPE + MLA_ROPE) ** -0.5


def _lower_tri(tb):
    r = lax.broadcasted_iota(jnp.int32, (tb, tb), 0)
    c = lax.broadcasted_iota(jnp.int32, (tb, tb), 1)
    return c <= r


def _mla_fwd(name, qcat, kcat, v):
    T = qcat.shape[0]
    tb = _tile(T, MLA_BLOCK)
    nb = T // tb

    def body(q_ref, k_ref, v_ref, o_ref, l_ref):
        qi = pl.program_id(1)
        q = q_ref[...]

        def block(kb, carry, diagonal):
            m_prev, l_prev, acc = carry
            rows = pl.ds(pl.multiple_of(kb * tb, tb), tb)
            s = lax.dot_general(q, k_ref[rows, :], (NT, ((), ())), preferred_element_type=F32) * _MLA_SCALE
            if diagonal:
                s = jnp.where(_lower_tri(tb), s, -jnp.inf)
            m_new = jnp.maximum(m_prev, jnp.max(s, axis=1, keepdims=True))
            alpha = jnp.exp(m_prev - m_new)
            p = jnp.exp(s - m_new)
            l_new = alpha * l_prev + jnp.sum(p, axis=1, keepdims=True)
            acc = alpha * acc + lax.dot_general(p.astype(BF16), v_ref[rows, :], (NN, ((), ())),
                                                preferred_element_type=F32)
            return m_new, l_new, acc

        init = (jnp.full((tb, 1), -jnp.inf, F32), jnp.zeros((tb, 1), F32), jnp.zeros((tb, MLA_V), F32))
        carry = lax.fori_loop(0, qi, lambda kb, c: block(kb, c, False), init)
        m_fin, l_fin, acc = block(qi, carry, True)
        o_ref[...] = acc / l_fin
        l_ref[...] = m_fin + jnp.log(l_fin)

    return _pcall(
        body, name=name, grid=(MLA_HEADS, nb),
        in_specs=[_bs((tb, MLA_SLOT), lambda h, qi: (qi, h)), _bs((T, MLA_SLOT), lambda h, qi: (0, h)),
                  _bs((T, MLA_V), lambda h, qi: (0, h))],
        out_specs=[_bs((tb, MLA_V), lambda h, qi: (qi, h)), _bs((None, tb, 1), lambda h, qi: (h, qi, 0))],
        out_shape=[jax.ShapeDtypeStruct((T, MLA_WIDTH), F32), jax.ShapeDtypeStruct((MLA_HEADS, T, 1), F32)],
        compiler_params=pltpu.CompilerParams(dimension_semantics=("parallel", "arbitrary")),
    )(qcat, kcat, v)


def _mla_bwd_q(name, qcat, kcat, v, o, lse, do):
    T = qcat.shape[0]
    tb = _tile(T, MLA_BLOCK)
    nb = T // tb

    def body(q_ref, k_ref, v_ref, o_ref, do_ref, l_ref, dq_ref, ds_ref):
        qi = pl.program_id(1)
        q = q_ref[...]
        do = do_ref[...]
        dsum = jnp.sum(do * o_ref[...], axis=1, keepdims=True)
        ds_ref[...] = dsum
        dob = do.astype(BF16)
        lse_q = l_ref[...]

        def block(kb, acc, diagonal):
            rows = pl.ds(pl.multiple_of(kb * tb, tb), tb)
            k = k_ref[rows, :]
            s = lax.dot_general(q, k, (NT, ((), ())), preferred_element_type=F32) * _MLA_SCALE
            p = jnp.exp(s - lse_q)
            if diagonal:
                p = jnp.where(_lower_tri(tb), p, 0.0)
            dp = lax.dot_general(dob, v_ref[rows, :], (NT, ((), ())), preferred_element_type=F32)
            ds = (p * (dp - dsum) * _MLA_SCALE).astype(BF16)
            return acc + lax.dot_general(ds, k, (NN, ((), ())), preferred_element_type=F32)

        acc = lax.fori_loop(0, qi, lambda kb, a: block(kb, a, False), jnp.zeros((tb, MLA_SLOT), F32))
        dq_ref[...] = block(qi, acc, True)

    qs = lambda w: _bs((tb, w), lambda h, qi: (qi, h))
    col = _bs((None, tb, 1), lambda h, qi: (h, qi, 0))
    return _pcall(
        body, name=name, grid=(MLA_HEADS, nb),
        in_specs=[qs(MLA_SLOT), _bs((T, MLA_SLOT), lambda h, qi: (0, h)), _bs((T, MLA_V), lambda h, qi: (0, h)),
                  qs(MLA_V), qs(MLA_V), col],
        out_specs=[qs(MLA_SLOT), col],
        out_shape=[jax.ShapeDtypeStruct((T, MLA_HEADS * MLA_SLOT), F32), jax.ShapeDtypeStruct((MLA_HEADS, T, 1), F32)],
        compiler_params=pltpu.CompilerParams(dimension_semantics=("parallel", "arbitrary")),
    )(qcat, kcat, v, o, do, lse)


def _mla_bwd_kv(name, qcat, kcat, v, lse, dsum, do):
    T = qcat.shape[0]
    tb = _tile(T, MLA_BLOCK)
    nb = T // tb

    def body(q_ref, k_ref, v_ref, do_ref, l_ref, ds_ref, dk_ref, dv_ref):
        ki = pl.program_id(1)
        k = k_ref[...]
        vv = v_ref[...]

        def block(qb, carry, diagonal):
            dk, dv = carry
            rows = pl.ds(pl.multiple_of(qb * tb, tb), tb)
            q = q_ref[rows, :]
            dob = do_ref[rows, :].astype(BF16)
            s = lax.dot_general(q, k, (NT, ((), ())), preferred_element_type=F32) * _MLA_SCALE
            p = jnp.exp(s - l_ref[rows, :])
            if diagonal:
                p = jnp.where(_lower_tri(tb), p, 0.0)
            dp = lax.dot_general(dob, vv, (NT, ((), ())), preferred_element_type=F32)
            ds = (p * (dp - ds_ref[rows, :]) * _MLA_SCALE).astype(BF16)
            dv = dv + lax.dot_general(p.astype(BF16), dob, (TN, ((), ())), preferred_element_type=F32)
            dk = dk + lax.dot_general(ds, q, (TN, ((), ())), preferred_element_type=F32)
            return dk, dv

        carry = block(ki, (jnp.zeros((tb, MLA_SLOT), F32), jnp.zeros((tb, MLA_V), F32)), True)
        dk, dv = lax.fori_loop(ki + 1, nb, lambda qb, c: block(qb, c, False), carry)
        dk_ref[...] = dk
        dv_ref[...] = dv

    ks = lambda w: _bs((tb, w), lambda h, ki: (ki, h))
    col = _bs((None, T, 1), lambda h, ki: (h, 0, 0))
    return _pcall(
        body, name=name, grid=(MLA_HEADS, nb),
        in_specs=[_bs((T, MLA_SLOT), lambda h, ki: (0, h)), ks(MLA_SLOT), ks(MLA_V),
                  _bs((T, MLA_V), lambda h, ki: (0, h)), col, col],
        out_specs=[ks(MLA_SLOT), ks(MLA_V)],
        out_shape=[jax.ShapeDtypeStruct((T, MLA_HEADS * MLA_SLOT), F32), jax.ShapeDtypeStruct((T, MLA_WIDTH), F32)],
        compiler_params=pltpu.CompilerParams(dimension_semantics=("parallel", "arbitrary")),
    )(qcat, kcat, v, do, lse, dsum)


def _kcat_fwd(name, kpre, krs, c256, s256):
    def fn(t, b):
        kr = t[1] * t[2] + _half_swap(t[1]) * t[3]
        return (t[0] + jnp.tile(kr, (1, MLA_HEADS)),), ()
    return _rowwise(name, fn, [kpre, krs, c256, s256], [], [(kpre.shape[1], BF16)])[0]


def _kcat_bwd(name, dkcat, c256, s256):
    def fn(t, b):
        d = t[0][:, 0:MLA_SLOT]
        for h in range(1, MLA_HEADS):
            d = d + t[0][:, h * MLA_SLOT:(h + 1) * MLA_SLOT]
        return (d * t[1] + _half_swap(d * t[2]),), ()
    return _rowwise(name, fn, [dkcat, c256, s256], [], [(MLA_SLOT, F32)])[0]


def _in_widths(ws):
    return [ws, SWA_WIDTH, SWA_KV_WIDTH, SWA_KV_WIDTH, MLA_Q_RANK, MLA_KV_RANK, MLA_SLOT]


def _kr_offset(ws):
    return ws + SWA_WIDTH + 2 * SWA_KV_WIDTH + MLA_Q_RANK + MLA_KV_RANK


def _pad_w_in(w, ws):
    z = lambda n: jnp.zeros((w.shape[0], n), w.dtype)
    o = _kr_offset(ws)
    return jnp.concatenate([w[:, :o], z(MLA_NOPE), w[:, o:], z(MLA_SLOT - MLA_NOPE - MLA_ROPE)], axis=1)


def _unpad_w_in(dw, ws):
    o = _kr_offset(ws)
    return jnp.concatenate([dw[:, :o], dw[:, o + MLA_NOPE:o + MLA_NOPE + MLA_ROPE]], axis=1)


def _pad_w_uq(w):
    r = w.shape[0]
    w3 = w.reshape(r, MLA_HEADS, MLA_NOPE + MLA_ROPE)
    return jnp.pad(w3, ((0, 0), (0, 0), (0, MLA_SLOT - MLA_NOPE - MLA_ROPE))).reshape(r, MLA_HEADS * MLA_SLOT)


def _unpad_w_uq(dw):
    r = dw.shape[0]
    return dw.reshape(r, MLA_HEADS, MLA_SLOT)[..., :MLA_NOPE + MLA_ROPE].reshape(r, -1)


def _pad_w_ukv(w):
    r = w.shape[0]
    w3 = w.reshape(r, MLA_HEADS, MLA_NOPE + MLA_V)
    wk = jnp.pad(w3[..., :MLA_NOPE], ((0, 0), (0, 0), (0, MLA_SLOT - MLA_NOPE))).reshape(r, MLA_HEADS * MLA_SLOT)
    wv = w3[..., MLA_NOPE:].reshape(r, MLA_WIDTH)
    return wk, wv


def _unpad_w_ukv(dwk, dwv):
    r = dwk.shape[0]
    return jnp.concatenate([dwk.reshape(r, MLA_HEADS, MLA_SLOT)[..., :MLA_NOPE], dwv.reshape(r, MLA_HEADS, MLA_V)],
                           axis=-1).reshape(r, -1)


def _layer_prep(lw):
    ws = lw['ssm_d'].shape[0]
    p = dict(lw)
    w_in_pad = _pad_w_in(lw['w_in'], ws)
    p['w_in_pad'] = w_in_pad
    offs = np.cumsum([0] + _in_widths(ws))
    p['w_in_parts'] = [w_in_pad[:, offs[i]:offs[i + 1]] for i in range(7)]
    p['s5_prep'] = _s5_prep(lw['ssm_log_dt'], lw['ssm_a_re'], lw['ssm_a_im'], lw['ssm_b_re'], lw['ssm_b_im'],
                            lw['ssm_c_re'], lw['ssm_c_im'])
    p['sinks_b'] = jnp.broadcast_to(lw['swa_sinks'][:, None], (SWA_HEADS, 128))
    p['w_uq_pad'] = _pad_w_uq(lw['mla_w_uq'])
    p['w_k_pad'], p['w_v'] = _pad_w_ukv(lw['mla_w_ukv'])
    p['w_ukv_pad'] = jnp.concatenate([p['w_k_pad'], p['w_v']], axis=1)
    b = [0, ws, ws + SWA_WIDTH, ws + SWA_WIDTH + MLA_WIDTH]
    p['w_out_g'] = [lw['w_out'][b[g]:b[g + 1]] for g in range(3)]
    p['out_norm_g'] = [lw['out_norm'][b[g]:b[g + 1]] for g in range(3)]
    return p


def _mixer_fwd(tag, x, p, tabs):
    T, D = x.shape
    c64, s64, c256, s256 = tabs
    ws = p['ssm_d'].shape[0]
    widths = _in_widths(ws)
    n_in = sum(widths)
    offs = [int(o) for o in np.cumsum([0] + widths[:-1])]
    h, rstd = _rms_fwd(tag + "_rms", x, p['mix_norm'])
    tm = _tile(T, 256)
    row_i = lambda w: _bs((tm, w), lambda i, j, k: (i, 0))
    parts = _mm(tag + "_in", (T // tm, 1, 1),
                [(h, row_i(D), p['w_in_pad'], _bs((D, n_in), lambda i, j, k: (0, 0)), NN, 0)],
                [(tm, n_in)], [((T, w), F32, row_i(w)) for w in widths],
                lambda accs, ex: tuple(accs[0][:, o:o + w] for o, w in zip(offs, widths)))
    u, q, k, v, cq, ckv, krs = parts
    y_ssm, s5_saved = _s5_fwd(tag + "_s5", u, p['s5_prep'], p['ssm_d'], p['ssm_w_glu'], p['ssm_b_glu'])
    q_r = _rope(tag + "_ropeq", q, c64, s64)
    k_r = _rope(tag + "_ropek", k, c64, s64)
    y_swa, lse_swa = _swa_fwd(tag + "_swa", q_r, k_r, v, p['sinks_b'])
    cqn, r_q = _rms_fwd(tag + "_rmsq", cq, p['mla_q_norm'])
    ckvn, r_kv = _rms_fwd(tag + "_rmskv", ckv, p['mla_kv_norm'])
    qpre = _mm_simple(tag + "_uq", cqn, p['w_uq_pad'], NN, tm=512, tn=1024, tk=MLA_Q_RANK)
    nk_, nv_ = MLA_HEADS * MLA_SLOT, MLA_WIDTH
    kpre, vm = _mm(tag + "_ukv", (T // tm, 1, 1),
                   [(ckvn, row_i(MLA_KV_RANK), p['w_ukv_pad'], _bs((MLA_KV_RANK, nk_ + nv_), lambda i, j, k: (0, 0)),
                     NN, 0)],
                   [(tm, nk_ + nv_)], [((T, nk_), F32, row_i(nk_)), ((T, nv_), BF16, row_i(nv_))],
                   lambda accs, ex: (accs[0][:, :nk_], accs[0][:, nk_:]))
    qcat = _rope(tag + "_ropemq", qpre, c256, s256, out_dtype=BF16)
    kcat = _kcat_fwd(tag + "_kcat", kpre, krs, c256, s256)
    y_mla, lse_mla = _mla_fwd(tag + "_mla", qcat, kcat, vm)
    ys = [y_ssm, y_swa, y_mla]
    yn, rs = [], []
    for g in range(3):
        n_, r_ = _rms_fwd(f"{tag}_rmso{g}", ys[g], p['out_norm_g'][g])
        yn.append(n_)
        rs.append(r_)
    tm3, tn3 = _tile(T, 512), _tile(D, 1024)
    pairs = []
    for g in range(3):
        wg = ys[g].shape[1]
        pairs.append((yn[g], _bs((tm3, wg), lambda i, j, k: (i, 0)), p['w_out_g'][g],
                      _bs((wg, tn3), lambda i, j, k: (0, j)), NN, 0))
    o_spec = _bs((tm3, tn3), lambda i, j, k: (i, j))
    x2, = _mm(tag + "_out", (T // tm3, D // tn3, 1), pairs, [(tm3, tn3)], [((T, D), F32, o_spec)],
              lambda accs, ex: (ex[0] + accs[0],), [(x, o_spec)])
    saved = (x, h, rstd, q_r, k_r, v, cq, ckv, s5_saved, y_swa, lse_swa, cqn, r_q, ckvn, r_kv, qcat, kcat, vm,
             y_mla, lse_mla, ys, yn, rs)
    return x2, saved


def _mixer_bwd(tag, dx2, dx2_16, saved, p, tabs):
    (x, h, rstd, q_r, k_r, v, cq, ckv, s5_saved, y_swa, lse_swa, cqn, r_q, ckvn, r_kv, qcat, kcat, vm,
     y_mla, lse_mla, ys, yn, rs) = saved
    T, D = x.shape
    c64, s64, c256, s256 = tabs
    ws = p['ssm_d'].shape[0]
    g_ = {}
    dys, dwo, don = [], [], []
    for g in range(3):
        wg = ys[g].shape[1]
        dyn = _mm_simple(f"{tag}_dyn{g}", dx2_16, p['w_out_g'][g], NT, tm=512, tn=wg, tk=1024)
        dwo.append(_mm_simple(f"{tag}_dwo{g}", yn[g], dx2_16, TN, tm=512, tn=1024, tk=512, out_dtype=BF16))
        dy_g, don_g = _rms_bwd(f"{tag}_rmsob{g}", dyn, ys[g], rs[g], p['out_norm_g'][g])
        dys.append(dy_g)
        don.append(don_g)
    g_['w_out'] = jnp.concatenate(dwo, axis=0)
    g_['out_norm'] = jnp.concatenate(don, axis=1)[0]
    dqcat, dsum = _mla_bwd_q(tag + "_mlabq", qcat, kcat, vm, y_mla, lse_mla, dys[2])
    dkcat, dvm = _mla_bwd_kv(tag + "_mlabkv", qcat, kcat, vm, lse_mla, dsum, dys[2])
    dqpre = _rope(tag + "_ropemqb", dqcat, c256, s256, backward=True)
    dkrs = _kcat_bwd(tag + "_kcatb", dkcat, c256, s256)
    g_['mla_w_uq'] = _unpad_w_uq(_mm_simple(tag + "_dwuq", cqn, dqpre, TN, tm=MLA_Q_RANK, tn=1024, tk=512,
                                            out_dtype=BF16))
    dcqn = _mm_simple(tag + "_dcqn", dqpre, p['w_uq_pad'], NT, tm=512, tn=MLA_Q_RANK, tk=1024)
    dcq, dqn = _rms_bwd(tag + "_rmsqb", dcqn, cq, r_q, p['mla_q_norm'])
    g_['mla_q_norm'] = dqn[0]
    dwk = _mm_simple(tag + "_dwk", ckvn, dkcat, TN, tm=MLA_KV_RANK, tn=1024, tk=512, out_dtype=BF16)
    dwv = _mm_simple(tag + "_dwv", ckvn, dvm, TN, tm=MLA_KV_RANK, tn=1024, tk=512, out_dtype=BF16)
    g_['mla_w_ukv'] = _unpad_w_ukv(dwk, dwv)
    tm = _tile(T, 512)
    nk_, nv_ = MLA_HEADS * MLA_SLOT, MLA_WIDTH
    tkk = _tile(nk_, 1024)
    dckvn_k = _mm_simple(tag + "_dckvk", dkcat, p['w_k_pad'], NT, tm=512, tn=MLA_KV_RANK, tk=tkk)
    dckvn = _mm_simple(tag + "_dckvv", dvm, p['w_v'], NT, tm=512, tn=MLA_KV_RANK, tk=nv_,
                       epilogue=lambda accs, ex: (accs[0] + ex[0],), extras=[dckvn_k])
    dckv, dkvn = _rms_bwd(tag + "_rmskvb", dckvn, ckv, r_kv, p['mla_kv_norm'])
    g_['mla_kv_norm'] = dkvn[0]
    dq_r, dkc, dkp, dvc, dvp, dsinks = _swa_bwd(tag + "_swab", q_r, k_r, v, p['sinks_b'], y_swa, lse_swa, dys[1])
    g_['swa_sinks'] = dsinks[0, :SWA_HEADS]
    dk_r = _shift_add(tag + "_dksum", dkc, dkp)
    dv = _shift_add(tag + "_dvsum", dvc, dvp)
    dq = _rope(tag + "_ropeqb", dq_r, c64, s64, backward=True)
    dk = _rope(tag + "_ropekb", dk_r, c64, s64, backward=True)
    du, s5g, dd, dw_glu, db_glu = _s5_bwd(tag + "_s5b", dys[0], s5_saved, p['s5_prep'], p['ssm_d'], p['ssm_w_glu'])
    g_['ssm_d'], g_['ssm_w_glu'], g_['ssm_b_glu'] = dd[0], dw_glu, db_glu[0]
    _, pull = jax.vjp(_s5_prep, p['ssm_log_dt'], p['ssm_a_re'], p['ssm_a_im'], p['ssm_b_re'], p['ssm_b_im'],
                      p['ssm_c_re'], p['ssm_c_im'])
    for name, val in zip(['ssm_log_dt', 'ssm_a_re', 'ssm_a_im', 'ssm_b_re', 'ssm_b_im', 'ssm_c_re', 'ssm_c_im'],
                         pull(tuple(s5g))):
        g_[name] = val
    dparts = [du, dq, dk, dv, dcq, dckv, dkrs]
    widths = _in_widths(ws)
    tn = _tile(D, 1024)
    pairs = []
    for dp_, wp_, w in zip(dparts, p['w_in_parts'], widths):
        pairs.append((dp_, _bs((tm, w), lambda i, j, k: (i, 0)), wp_, _bs((tn, w), lambda i, j, k: (j, 0)), NT, 0))
    o_spec = _bs((tm, tn), lambda i, j, k: (i, j))
    dh, = _mm(tag + "_dh", (T // tm, D // tn, 1), pairs, [(tm, tn)], [((T, D), F32, o_spec)],
              lambda accs, ex: (accs[0],))
    tmw, tk = _tile(D, 512), _tile(T, 512)
    pairs = []
    for i_, (dp_, w) in enumerate(zip(dparts, widths)):
        pairs.append((h, _bs((tk, tmw), lambda i, j, k: (k, i)), dp_, _bs((tk, w), lambda i, j, k: (k, 0)), TN, i_))
    dws = _mm(tag + "_dwin", (D // tmw, 1, T // tk), pairs, [(tmw, w) for w in widths],
              [((D, w), BF16, _bs((tmw, w), lambda i, j, k: (i, 0))) for w in widths], lambda accs, ex: tuple(accs))
    g_['w_in'] = _unpad_w_in(jnp.concatenate(dws, axis=1), ws)
    dx, dx16, dmix = _rms_bwd(tag + "_rmsb", dh, x, rstd, p['mix_norm'], dres=dx2, with_bf16=True)
    g_['mix_norm'] = dmix[0]
    return dx, dx16, g_


def _final_loss(name, x, target, gain):
    D = x.shape[1]

    def fn(t, b):
        xv = t[0]
        r = lax.rsqrt(jnp.mean(xv * xv, axis=-1, keepdims=True) + EPS)
        xh = xv * r
        err = xh * b[0] - t[1]
        part = 0.5 * jnp.sum(jnp.sum(err * err, axis=-1, keepdims=True), axis=0, keepdims=True) / D
        dy = err / D
        dxh = dy * b[0]
        dx = r * (dxh - xh * jnp.mean(dxh * xh, axis=-1, keepdims=True))
        return (dx, dx), (jnp.broadcast_to(part, (1, 128)), _colsum(dy * xh))
    dx, dx16, part, dg = _rowwise(name, fn, [x, target], [gain.reshape(1, D)], [(D, F32), (D, BF16)], [128, D])
    return part[0, 0], dx, dx16, dg[0]


def _device_step(x, positions, target, n_layers, get_block, final_norm, on_grads):
    tabs = _rope_tables(positions)
    saved = []
    for l in range(n_layers):
        w1 = get_block(l, 'ffn1a', x)
        x, s1, w1['ffn1_wd'] = _ffn_fwd("ffn1", x, w1['ffn1_norm'], w1['ffn1_wg2'], w1['ffn1_wu2'],
                                        lambda t, l=l: get_block(l, 'ffn1b', t)['ffn1_wd'])
        p = _layer_prep(get_block(l, 'mix', x))
        x, s2 = _mixer_fwd("mix", x, p, tabs)
        w3 = get_block(l, 'ffn2', x)
        x, s3, _ = _ffn_fwd("ffn2", x, w3['ffn2_norm'], w3['ffn2_wg2'], w3['ffn2_wu2'], w3['ffn2_wd'])
        saved.append((w1, s1, p, s2, w3, s3))
    loss_part, dx, dx16, dfinal = _final_loss("loss", x, target, final_norm)
    for l in range(n_layers - 1, -1, -1):
        w1, s1, p, s2, w3, s3 = saved[l]
        early = lambda g, carry, l=l: on_grads(l, g, (carry,))[0]
        dx, dx16, gn, _, _, _ = _ffn_bwd("ffn2b", dx, dx16, s3, w3['ffn2_norm'], w3['ffn2_wg2'], w3['ffn2_wu2'],
                                         w3['ffn2_wd'], early, ('ffn2_wg2', 'ffn2_wu2', 'ffn2_wd'))
        dx, dx16 = on_grads(l, {'ffn2_norm': gn[0]}, (dx, dx16))
        dx, dx16, gm = _mixer_bwd("mixb", dx, dx16, s2, p, tabs)
        dx, dx16 = on_grads(l, gm, (dx, dx16))
        dx, dx16, gn, _, _, _ = _ffn_bwd("ffn1b", dx, dx16, s1, w1['ffn1_norm'], w1['ffn1_wg2'], w1['ffn1_wu2'],
                                         w1['ffn1_wd'], early, ('ffn1_wg2', 'ffn1_wu2', 'ffn1_wd'))
        dx, dx16 = on_grads(l, {'ffn1_norm': gn[0]}, (dx, dx16))
    return loss_part, dx, dfinal


_ANY = pl.BlockSpec(memory_space=pl.ANY)
_CHIP_MASKS = (2, 1, 3)


def _place():
    x, y, c = lax.axis_index("x"), lax.axis_index("y"), lax.axis_index("c")
    chips = [(1 - x, y), (x, 1 - y), (1 - x, 1 - y)]
    return x, y, c, 2 * x + y, chips


_HBM = pl.BlockSpec(memory_space=pltpu.HBM)
_SEMS = pl.BlockSpec(memory_space=pltpu.SEMAPHORE)
_EFFECT = pltpu.SideEffectType.DATAFLOW_SIDE_EFFECTING


def _split_start(name, bufs, sem_sizes, copies):
    n, ns = len(bufs), len(sem_sizes)

    def body(*refs):
        for cp in copies(refs[:n], refs[n:n + ns])[0]:
            cp.start()

    outs = _pcall(
        body, name=name, in_specs=[_HBM] * n,
        out_shape=(*[pltpu.SemaphoreType.DMA((k,)) for k in sem_sizes], *[pltpu.HBM(b.shape, b.dtype) for b in bufs]),
        out_specs=(*[_SEMS] * ns, *[_HBM] * n), input_output_aliases={i: i + ns for i in range(n)},
        compiler_params=pltpu.CompilerParams(has_side_effects=_EFFECT),
    )(*[pltpu.with_memory_space_constraint(b, pltpu.HBM) for b in bufs])
    return list(outs[:ns]), list(outs[ns:])


def _split_wait(name, bufs, sems, after, copies):
    n, ns = len(bufs), len(sems)

    def body(*refs):
        _, sent, landed = copies(refs[:n], refs[n:n + ns])
        for cp in sent:
            cp.wait_send()
        for cp in landed:
            cp.wait_recv()

    return list(_pcall(
        body, name=name, in_specs=[_HBM] * n + [_SEMS] * ns + [_ANY],
        out_shape=tuple(pltpu.HBM(b.shape, b.dtype) for b in bufs), out_specs=tuple([_HBM] * n),
        input_output_aliases={i: i for i in range(n)},
        compiler_params=pltpu.CompilerParams(has_side_effects=_EFFECT),
    )(*bufs, *sems, after))


def _gather_group(bufs, send, recv):
    x, y, c, s, chips = _place()
    start, landed = [], []
    for a, buf in enumerate(bufs):
        for j in range(3):
            k = a * 3 + j
            to = (*chips[j], c)
            start.append(pltpu.make_async_remote_copy(
                src_ref=buf.at[s], dst_ref=buf.at[s], send_sem=send.at[k], recv_sem=recv.at[k],
                device_id=to, device_id_type=MESH))
            theirs = buf.at[s ^ _CHIP_MASKS[j]]
            landed.append(pltpu.make_async_remote_copy(
                src_ref=theirs, dst_ref=theirs, send_sem=send.at[k], recv_sem=recv.at[k],
                device_id=to, device_id_type=MESH))
    return start, start, landed


def _gather_start_copies(n_pass, sizes):
    def copies(refs, sems):
        start, o = [], n_pass
        for g, k in enumerate(sizes):
            start += _gather_group(refs[o:o + k], sems[2 * g], sems[2 * g + 1])[0]
            o += k
        return start, start, []
    return copies


def _gather_wait_copies(refs, sems):
    return _gather_group(refs, sems[0], sems[1])


_N_SLOTS = 7


def _reduce_copies(n_pass, n):
    def copies(refs, sems):
        send, recv = sems
        x, y, c, s, chips = _place()
        contrib, lands = refs[n_pass:n_pass + n], refs[n_pass + n:]
        start, landed = [], []
        for a in range(n):
            half = contrib[a].shape[1] // 2
            base = a * _N_SLOTS
            for j in range(3):
                for h in range(2):
                    start.append(pltpu.make_async_remote_copy(
                        src_ref=contrib[a].at[s ^ _CHIP_MASKS[j], pl.ds(h * half, half)], dst_ref=lands[a].at[2 * j + c],
                        send_sem=send.at[base + 2 * j + h], recv_sem=recv.at[base + 2 * j + c],
                        device_id=(*chips[j], h), device_id_type=MESH))
            start.append(pltpu.make_async_remote_copy(
                src_ref=contrib[a].at[s, pl.ds((1 - c) * half, half)], dst_ref=lands[a].at[6],
                send_sem=send.at[base + 6], recv_sem=recv.at[base + 6],
                device_id=(x, y, 1 - c), device_id_type=MESH))
            for slot in range(_N_SLOTS):
                landed.append(pltpu.make_async_remote_copy(
                    src_ref=lands[a].at[slot], dst_ref=lands[a].at[slot], send_sem=send.at[base + slot],
                    recv_sem=recv.at[base + slot], device_id=(x, y, c), device_id_type=MESH))
        return start, start, landed
    return copies


def _sum_partials(name, contrib, lands):
    ns, half, w = lands.shape
    tm = _tile(half, 256)
    nb = half // tm
    mine = _half_rows(nb)
    shard = lambda: 2 * lax.axis_index("x") + lax.axis_index("y")

    def body(g_ref, *refs):
        o_ref = refs[-1]
        acc = g_ref[...].astype(F32) + refs[6][...].astype(F32)
        for slot in range(6):
            acc = acc + refs[slot][...].astype(F32)
        o_ref[...] = acc

    blk = lambda j: _bs((None, tm, w), lambda i: (j, i, 0))
    return _pcall(
        body, name=name, grid=(nb,),
        in_specs=[_bs((None, tm, w), lambda i: (shard(), mine(i), 0))] + [blk(j) for j in range(_N_SLOTS)],
        out_specs=_bs((tm, w), lambda i: (mine(i), 0)), out_shape=jax.ShapeDtypeStruct((2 * half, w), F32),
        compiler_params=pltpu.CompilerParams(dimension_semantics=("parallel",)),
    )(contrib, *[lands] * _N_SLOTS)


def _stage_shard(name, wl):
    r, c = wl.shape
    tm = _tile(r, 256)
    shard = lambda: 2 * lax.axis_index("x") + lax.axis_index("y")

    def body(w_ref, o_ref):
        o_ref[...] = w_ref[...].astype(o_ref.dtype)

    return _pcall(
        body, name=name, grid=(r // tm,), in_specs=[_bs((tm, c), lambda i: (i, 0))],
        out_specs=_bs((None, tm, c), lambda i: (shard(), i, 0)),
        out_shape=jax.ShapeDtypeStruct((N_SHARD, r, c), BF16),
        compiler_params=pltpu.CompilerParams(dimension_semantics=("parallel",)),
    )(wl)


def _half_rows(nb):
    return lambda i: lax.axis_index("c") * nb + i


def _pair_join_call(name, halves):
    n = len(halves)

    def body(*refs):
        outs = refs[n:2 * n]
        send, recv = refs[2 * n:]
        x, y, c, s, chips = _place()
        cps = []
        for a in range(n):
            half = outs[a].shape[0] // 2
            mine = outs[a].at[pl.ds(c * half, half)]
            cp = pltpu.make_async_remote_copy(
                src_ref=mine, dst_ref=mine, send_sem=send.at[a], recv_sem=recv.at[a],
                device_id=(x, y, 1 - c), device_id_type=MESH)
            cp.start()
            cps.append(cp)
        for a in range(n):
            half = outs[a].shape[0] // 2
            theirs = outs[a].at[pl.ds((1 - c) * half, half)]
            pltpu.make_async_remote_copy(
                src_ref=theirs, dst_ref=theirs, send_sem=send.at[a], recv_sem=recv.at[a],
                device_id=(x, y, 1 - c), device_id_type=MESH).wait_recv()
        for cp in cps:
            cp.wait_send()

    return _pcall(
        body, name=name, in_specs=[_ANY] * n, out_specs=[_ANY] * n,
        out_shape=[jax.ShapeDtypeStruct(h.shape, h.dtype) for h in halves],
        input_output_aliases={a: a for a in range(n)},
        scratch_shapes=[pltpu.SemaphoreType.DMA((n,))] * 2,
    )(*halves)


_N_DEV = 8


def _small_copies(n_pass):
    def copies(refs, sems):
        send, recv = sems
        buf, land = refs[n_pass], refs[n_pass + 1]
        x, y, c, s, chips = _place()
        me = 4 * x + 2 * y + c
        start, landed = [], []
        for k in range(1, _N_DEV):
            to = (x ^ (k >> 2), y ^ ((k >> 1) & 1), c ^ (k & 1))
            start.append(pltpu.make_async_remote_copy(
                src_ref=buf, dst_ref=land.at[me], send_sem=send.at[k - 1], recv_sem=recv.at[k - 1],
                device_id=to, device_id_type=MESH))
            theirs = land.at[me ^ k]
            landed.append(pltpu.make_async_remote_copy(
                src_ref=theirs, dst_ref=theirs, send_sem=send.at[k - 1], recv_sem=recv.at[k - 1],
                device_id=to, device_id_type=MESH))
        return start, start, landed
    return copies


def _sum_devices(name, land):
    rows = land.shape[1]
    tm = _tile(rows, 512)

    def sum_body(a_ref, o_ref):
        acc = a_ref[0]
        for k in range(1, _N_DEV):
            acc = acc + a_ref[k]
        o_ref[...] = acc

    return _pcall(
        sum_body, name=name, grid=(rows // tm,),
        in_specs=[_bs((_N_DEV, tm, 128), lambda i: (0, i, 0))], out_specs=_bs((tm, 128), lambda i: (i, 0)),
        out_shape=jax.ShapeDtypeStruct(land.shape[1:], F32),
        compiler_params=pltpu.CompilerParams(dimension_semantics=("parallel",)),
    )(land)


def _adamw(name, w, g, m, v, token=None):
    def fn(t, b):
        wv, gv, mv, vv = t
        m2 = ADAM_B1 * mv + (1.0 - ADAM_B1) * gv
        v2 = ADAM_B2 * vv + (1.0 - ADAM_B2) * (gv * gv)
        m_hat = m2 / (1.0 - ADAM_B1 ** ADAM_STEP)
        v_hat = v2 / (1.0 - ADAM_B2 ** ADAM_STEP)
        delta = -ADAM_LR * (m_hat / (jnp.sqrt(v_hat) + ADAM_EPS) + ADAM_WD * wv)
        return (delta, m2, v2), ()
    wd = w.shape[1]
    return _rowwise(name, fn, [w, g, m, v], [] if token is None else [token], [(wd, F32)] * 3,
                    tm=256 if wd > 1024 else 512)


_WEIGHTS = ['ffn1_norm', 'ffn1_w_gate', 'ffn1_w_up', 'ffn1_w_down', 'mix_norm', 'w_in', 'ssm_log_dt', 'ssm_a_re',
            'ssm_a_im', 'ssm_b_re', 'ssm_b_im', 'ssm_c_re', 'ssm_c_im', 'ssm_d', 'ssm_w_glu', 'ssm_b_glu',
            'swa_sinks', 'mla_q_norm', 'mla_w_uq', 'mla_kv_norm', 'mla_w_ukv', 'out_norm', 'w_out', 'ffn2_norm',
            'ffn2_w_gate', 'ffn2_w_up', 'ffn2_w_down', 'final_norm']
_COL_SHARDED = ['ffn1_w_gate', 'ffn1_w_up', 'w_in', 'mla_w_uq', 'mla_w_ukv', 'ffn2_w_gate', 'ffn2_w_up']
_ROW_SHARDED = ['ffn1_w_down', 'ssm_w_glu', 'w_out', 'ffn2_w_down']
_STACKED = {'ffn1_w_gate': 'ffn1_wg2', 'ffn1_w_up': 'ffn1_wu2', 'ffn2_w_gate': 'ffn2_wg2', 'ffn2_w_up': 'ffn2_wu2'}
_RENAMED = {'ffn1_w_down': 'ffn1_wd', 'ffn2_w_down': 'ffn2_wd'}
_BLOCKS = ('ffn1', 'mix', 'ffn2')
_BLOCK_BIG = {'ffn1': ['ffn1_w_gate', 'ffn1_w_up', 'ffn1_w_down'],
              'mix': ['w_in', 'ssm_w_glu', 'mla_w_uq', 'mla_w_ukv', 'w_out'],
              'ffn2': ['ffn2_w_gate', 'ffn2_w_up', 'ffn2_w_down']}
_BLOCK_SMALL = {'ffn1': ['ffn1_norm'],
                'mix': ['mix_norm', 'ssm_log_dt', 'ssm_a_re', 'ssm_a_im', 'ssm_b_re', 'ssm_b_im', 'ssm_c_re', 'ssm_c_im',
                        'ssm_d', 'ssm_b_glu', 'swa_sinks', 'mla_q_norm', 'mla_kv_norm', 'out_norm'],
                'ffn2': ['ffn2_norm']}
_BIG = [n for b in _BLOCKS for n in _BLOCK_BIG[b]]
_SMALL = [n for n in _WEIGHTS if n not in _BIG]


def _pack(vals):
    flat = jnp.concatenate([v.reshape(-1) for v in vals])
    pad = (-flat.shape[0]) % 1024
    return jnp.pad(flat, (0, pad)).reshape(-1, 128)


def _unpack(buf, like):
    flat = buf.reshape(-1)
    out, o = [], 0
    for v in like:
        out.append(flat[o:o + v.size].reshape(v.shape))
        o += v.size
    return out


def kernel(x, positions, ffn1_norm, ffn1_w_gate, ffn1_w_up, ffn1_w_down, mix_norm, w_in, ssm_log_dt, ssm_a_re, ssm_a_im, ssm_b_re, ssm_b_im, ssm_c_re, ssm_c_im, ssm_d, ssm_w_glu, ssm_b_glu, swa_sinks, mla_q_norm, mla_w_uq, mla_kv_norm, mla_w_ukv, out_norm, w_out, ffn2_norm, ffn2_w_gate, ffn2_w_up, ffn2_w_down, final_norm, loss_target, m_ffn1_norm, m_ffn1_w_gate, m_ffn1_w_up, m_ffn1_w_down, m_mix_norm, m_w_in, m_ssm_log_dt, m_ssm_a_re, m_ssm_a_im, m_ssm_b_re, m_ssm_b_im, m_ssm_c_re, m_ssm_c_im, m_ssm_d, m_ssm_w_glu, m_ssm_b_glu, m_swa_sinks, m_mla_q_norm, m_mla_w_uq, m_mla_kv_norm, m_mla_w_ukv, m_out_norm, m_w_out, m_ffn2_norm, m_ffn2_w_gate, m_ffn2_w_up, m_ffn2_w_down, m_final_norm, v_ffn1_norm, v_ffn1_w_gate, v_ffn1_w_up, v_ffn1_w_down, v_mix_norm, v_w_in, v_ssm_log_dt, v_ssm_a_re, v_ssm_a_im, v_ssm_b_re, v_ssm_b_im, v_ssm_c_re, v_ssm_c_im, v_ssm_d, v_ssm_w_glu, v_ssm_b_glu, v_swa_sinks, v_mla_q_norm, v_mla_w_uq, v_mla_kv_norm, v_mla_w_ukv, v_out_norm, v_w_out, v_ffn2_norm, v_ffn2_w_gate, v_ffn2_w_up, v_ffn2_w_down, v_final_norm):
    w = dict(zip(_WEIGHTS, (ffn1_norm, ffn1_w_gate, ffn1_w_up, ffn1_w_down, mix_norm, w_in, ssm_log_dt, ssm_a_re, ssm_a_im, ssm_b_re, ssm_b_im, ssm_c_re, ssm_c_im, ssm_d, ssm_w_glu, ssm_b_glu, swa_sinks, mla_q_norm, mla_w_uq, mla_kv_norm, mla_w_ukv, out_norm, w_out, ffn2_norm, ffn2_w_gate, ffn2_w_up, ffn2_w_down, final_norm)))
    m = dict(zip(_WEIGHTS, (m_ffn1_norm, m_ffn1_w_gate, m_ffn1_w_up, m_ffn1_w_down, m_mix_norm, m_w_in, m_ssm_log_dt, m_ssm_a_re, m_ssm_a_im, m_ssm_b_re, m_ssm_b_im, m_ssm_c_re, m_ssm_c_im, m_ssm_d, m_ssm_w_glu, m_ssm_b_glu, m_swa_sinks, m_mla_q_norm, m_mla_w_uq, m_mla_kv_norm, m_mla_w_ukv, m_out_norm, m_w_out, m_ffn2_norm, m_ffn2_w_gate, m_ffn2_w_up, m_ffn2_w_down, m_final_norm)))
    v = dict(zip(_WEIGHTS, (v_ffn1_norm, v_ffn1_w_gate, v_ffn1_w_up, v_ffn1_w_down, v_mix_norm, v_w_in, v_ssm_log_dt, v_ssm_a_re, v_ssm_a_im, v_ssm_b_re, v_ssm_b_im, v_ssm_c_re, v_ssm_c_im, v_ssm_d, v_ssm_w_glu, v_ssm_b_glu, v_swa_sinks, v_mla_q_norm, v_mla_w_uq, v_mla_kv_norm, v_mla_w_ukv, v_out_norm, v_w_out, v_ffn2_norm, v_ffn2_w_gate, v_ffn2_w_up, v_ffn2_w_down, v_final_norm)))
    n_layers = ffn1_norm.shape[0]

    groups = (('ffn1a', ['ffn1_w_gate', 'ffn1_w_up'], ['ffn1_norm']), ('ffn1b', ['ffn1_w_down'], []),
              ('mix', _BLOCK_BIG['mix'], _BLOCK_SMALL['mix']), ('ffn2', _BLOCK_BIG['ffn2'], ['ffn2_norm']))
    assert [n for _, names, _ in groups for n in names] == _BIG
    sizes = [len(names) for _, names, _ in groups]

    x0 = x[0]
    travelling = []
    for l in range(n_layers):
        staged = [_stage_shard("stage_" + n, w[n][l]) for n in _BIG]
        sems, thru = _split_start(f"gstart{l}", [x0] + staged, [3 * k for k in sizes for _ in range(2)],
                                  _gather_start_copies(1, sizes))
        x0 = thru[0]
        travelling.append((sems, thru[1:]))

    def get_block(l, blk, x_now):
        g = [name for name, _, _ in groups].index(blk)
        sems, staged = travelling[l]
        o = sum(sizes[:g])
        full = _split_wait(f"gwait{l}{blk}", staged[o:o + sizes[g]], sems[2 * g:2 * g + 2], x_now,
                           _gather_wait_copies)
        lw = {n: w[n][l] for n in groups[g][2]}
        for n, gth in zip(groups[g][1], full):
            ns, r, c = gth.shape
            if n in _STACKED:
                lw[_STACKED[n]] = gth.reshape(ns * r, c)
            elif n in _ROW_SHARDED:
                lw[_RENAMED.get(n, n)] = gth.reshape(ns * r, c)
            else:
                lw[n] = jnp.moveaxis(gth, 0, 1).reshape(r, ns * c)
        return lw

    reduced = {n: [None] * n_layers for n in _BIG}
    grads = [dict() for _ in range(n_layers)]
    pending = []
    counter = [0, 0]

    def finish(after):
        _, l, names, call, sems, contrib, lands = pending.pop(0)
        k = len(contrib)
        done = _split_wait(f"cwait{call}", contrib + lands, sems, after, _reduce_copies(0, k))
        halves = [_sum_partials("rs_sum", g, q) for g, q in zip(done[:k], done[k:])]
        for n, g in zip(names, _pair_join_call("rs_join", halves)):
            reduced[n][l] = g

    def on_grads(l, g_, carry):
        small = [n for n in _SMALL if n in g_]
        for n in small:
            grads[l][n] = g_[n]
        if small:
            while pending and pending[0][0] < counter[0]:
                finish(g_[small[0]])
        names = [n for n in _BIG if _STACKED.get(n, _RENAMED.get(n, n)) in g_]
        if names:
            contrib = []
            for n in names:
                r, c = w[n].shape[1:]
                if n in _STACKED:
                    g = g_[_STACKED[n]].reshape(N_SHARD, r, c)
                elif n in _ROW_SHARDED:
                    g = g_[_RENAMED.get(n, n)].reshape(N_SHARD, r, c)
                else:
                    g = jnp.moveaxis(g_[n].reshape(r, N_SHARD, c), 1, 0)
                contrib.append(g)
            k, p = len(contrib), len(carry) + 1
            lands = [lax.empty((_N_SLOTS, g.shape[1] // 2, g.shape[2]), g.dtype) for g in contrib]
            sems, thru = _split_start(f"cstart{counter[1]}", list(carry) + [token[0]] + contrib + lands,
                                      [_N_SLOTS * k] * 2, _reduce_copies(p, k))
            token[0] = thru[p - 1]
            pending.append((counter[0], l, names, counter[1], sems, thru[p:p + k], thru[p + k:]))
            counter[1] += 1
            carry = tuple(thru[:p - 1])
        if small:
            counter[0] += 1
        return carry

    token = [jnp.zeros((8, 128), F32)]
    loss_part, dx, dfinal = _device_step(x0, positions[0], loss_target[0], n_layers, get_block, final_norm, on_grads)
    loss = lax.psum(loss_part, ("x", "y", "c"))

    small_like = [w[n] for n in _SMALL]
    small_g = [jnp.stack([grads[l][n] for l in range(n_layers)]) for n in _SMALL if n != 'final_norm'] + [dfinal]
    packed = _pack(small_g)
    me = 4 * lax.axis_index("x") + 2 * lax.axis_index("y") + lax.axis_index("c")
    land = lax.dynamic_update_slice(lax.empty((_N_DEV,) + packed.shape, F32), packed[None], (me, 0, 0))
    s_sems, s_thru = _split_start("sstart", [token[0], packed, land], [_N_DEV - 1] * 2, _small_copies(1))
    token[0] = s_thru[0]

    grad, delta, new_m, new_v = {}, {}, {}, {}

    def update(n):
        shp = w[n].shape
        two = lambda t: t.reshape(shp[0] * shp[1], shp[2])
        grad[n] = jnp.stack(reduced[n])
        d2, m2, v2 = _adamw("adam_" + n, two(w[n]), two(grad[n]), two(m[n]), two(v[n]), token=token[0])
        delta[n], new_m[n], new_v[n] = d2.reshape(shp), m2.reshape(shp), v2.reshape(shp)
        return d2

    late = [n for entry in pending for n in entry[2]]
    for n in _BIG:
        if n not in late:
            meanwhile = update(n)
    while pending:
        finish(meanwhile)
    g_small = _sum_devices("small_sum", _split_wait("swait", s_thru[1:], s_sems, meanwhile, _small_copies(0))[1])
    d_small, m_small, v_small = _adamw("adam_small", _pack(small_like), g_small, _pack([m[n] for n in _SMALL]),
                                       _pack([v[n] for n in _SMALL]))
    for n, gv, dv_, mv, vv in zip(_SMALL, _unpack(g_small, small_like), _unpack(d_small, small_like),
                                  _unpack(m_small, small_like), _unpack(v_small, small_like)):
        grad[n], delta[n], new_m[n], new_v[n] = gv, dv_, mv, vv
    for n in late:
        update(n)
    return (loss, dx[None], *[grad[n] for n in _WEIGHTS], *[delta[n] for n in _WEIGHTS],
            *[new_m[n] for n in _WEIGHTS], *[new_v[n] for n in _WEIGHTS])
```

```python
import functools
import math

import jax
import jax.numpy as jnp
import numpy as np
from jax import lax
from jax.experimental import pallas as pl
from jax.experimental.pallas import tpu as pltpu

F32 = jnp.float32
BF16 = jnp.bfloat16

EPS = 1e-6
ROPE_THETA = 10000.0
SSM_GROUP = 16
SSM_STATE = 64
S5_TILE = 8
SWA_HEADS = 8
SWA_KV_HEADS = 2
SWA_HEAD_DIM = 64
SWA_BLOCK = 128
SWA_WIDTH = SWA_HEADS * SWA_HEAD_DIM
SWA_KV_WIDTH = SWA_KV_HEADS * SWA_HEAD_DIM
MLA_HEADS = 8
MLA_Q_RANK = 512
MLA_KV_RANK = 256
MLA_NOPE = 128
MLA_ROPE = 64
MLA_V = 128
MLA_SLOT = 256
MLA_WIDTH = MLA_HEADS * MLA_V
ROPE_HALF = 32

ADAM_LR = 0.001
ADAM_B1 = 0.9
ADAM_B2 = 0.999
ADAM_EPS = 1e-08
ADAM_WD = 0.01
ADAM_STEP = 10

N_SHARD = 4
MESH = pl.DeviceIdType.MESH

NN = ((1,), (0,))
NT = ((1,), (1,))
TN = ((0,), (0,))


def _pcall(body, **kw):
    return pl.pallas_call(body, **kw)


def _tile(n, want, align=16):
    if n <= want:
        return n
    t = want - want % align
    while t >= align:
        if n % t == 0:
            return t
        t -= align
    return n


def _mm(name, grid, pairs, acc_shapes, outs, epilogue, extras=()):
    nk = grid[2]
    n_p, n_e, n_o, n_a = len(pairs), len(extras), len(outs), len(acc_shapes)
    dims_idx = [(p[4], p[5]) for p in pairs]

    def body(*refs):
        ab = refs[:2 * n_p]
        ex = refs[2 * n_p:2 * n_p + n_e]
        o = refs[2 * n_p + n_e:2 * n_p + n_e + n_o]
        accs = refs[2 * n_p + n_e + n_o:]

        def partial_sums():
            sums = [None] * n_a
            for p, (dims, ai) in enumerate(dims_idx):
                a = ab[2 * p][...].astype(BF16)
                b = ab[2 * p + 1][...].astype(BF16)
                d = lax.dot_general(a, b, (dims, ((), ())), preferred_element_type=F32)
                sums[ai] = d if sums[ai] is None else sums[ai] + d
            return sums

        def finish(vals):
            res = epilogue(vals, [e[...] for e in ex])
            for r, oref in zip(res, o):
                oref[...] = r.astype(oref.dtype)

        if nk == 1:
            finish(partial_sums())
        else:
            k = pl.program_id(2)

            @pl.when(k == 0)
            def _():
                for acc in accs:
                    acc[...] = jnp.zeros_like(acc)

            for acc, s in zip(accs, partial_sums()):
                acc[...] += s

            @pl.when(k == nk - 1)
            def _():
                finish([acc[...] for acc in accs])

    in_arrays, in_specs = [], []
    for a, a_spec, b, b_spec, _, _ in pairs:
        in_arrays += [a, b]
        in_specs += [a_spec, b_spec]
    for e, e_spec in extras:
        in_arrays.append(e)
        in_specs.append(e_spec)
    res = _pcall(
        body, name=name, grid=grid, in_specs=in_specs,
        out_specs=[o[2] for o in outs],
        out_shape=[jax.ShapeDtypeStruct(o[0], o[1]) for o in outs],
        scratch_shapes=[] if nk == 1 else [pltpu.VMEM(s, F32) for s in acc_shapes],
        compiler_params=pltpu.CompilerParams(dimension_semantics=("parallel", "parallel", "arbitrary")),
    )(*in_arrays)
    return res


def _bs(shape, fn):
    return pl.BlockSpec(shape, fn)


def _mm_simple(name, a, b, dims, *, tm=512, tn=512, tk=512, out_dtype=F32, epilogue=None, extras=(), scale=None):
    if dims == NN:
        (M, K), N = a.shape, b.shape[1]
    elif dims == NT:
        (M, K), N = a.shape, b.shape[0]
    else:
        (K, M), N = a.shape, b.shape[1]
    tm, tn, tk = _tile(M, tm, 128), _tile(N, tn, 128), _tile(K, tk, 128)
    if dims == NN:
        a_spec, b_spec = _bs((tm, tk), lambda i, j, k: (i, k)), _bs((tk, tn), lambda i, j, k: (k, j))
    elif dims == NT:
        a_spec, b_spec = _bs((tm, tk), lambda i, j, k: (i, k)), _bs((tn, tk), lambda i, j, k: (j, k))
    else:
        a_spec, b_spec = _bs((tk, tm), lambda i, j, k: (k, i)), _bs((tk, tn), lambda i, j, k: (k, j))
    ex = []
    for e in extras:
        if e.shape[0] == 1:
            ex.append((e, _bs((1, tn), lambda i, j, k: (0, j))))
        else:
            ex.append((e, _bs((tm, tn), lambda i, j, k: (i, j))))
    if epilogue is None:
        if scale is None:
            epilogue = lambda accs, ex_: (accs[0],)
        else:
            epilogue = lambda accs, ex_: (accs[0] * scale,)
        out_dtypes = (out_dtype,)
    else:
        out_dtypes = out_dtype if isinstance(out_dtype, tuple) else (out_dtype,)
    outs = [((M, N), dt, _bs((tm, tn), lambda i, j, k: (i, j))) for dt in out_dtypes]
    res = _mm(name, (M // tm, N // tn, K // tk), [(a, a_spec, b, b_spec, dims, 0)], [(tm, tn)], outs, epilogue, ex)
    return res[0] if len(res) == 1 else res


def _rowwise(name, fn, tiles, bcasts, outs, accs=(), *, tm=256):
    rows = tiles[0].shape[0]
    tm = _tile(rows, tm)
    n_t, n_b, n_o, n_a = len(tiles), len(bcasts), len(outs), len(accs)

    def body(*refs):
        t = [r[...] for r in refs[:n_t]]
        b = [r[...] for r in refs[n_t:n_t + n_b]]
        o = refs[n_t + n_b:n_t + n_b + n_o]
        a = refs[n_t + n_b + n_o:]
        ov, av = fn(t, b)
        for r, val in zip(o, ov):
            r[...] = val.astype(r.dtype)
        if n_a:
            @pl.when(pl.program_id(0) == 0)
            def _():
                for r in a:
                    r[...] = jnp.zeros_like(r)
            for r, val in zip(a, av):
                r[...] += val

    in_specs = [_bs((tm, x.shape[1]), lambda i: (i, 0)) for x in tiles]
    in_specs += [_bs(x.shape, lambda i, nd=x.ndim: (0,) * nd) for x in bcasts]
    out_specs = [_bs((tm, w), lambda i: (i, 0)) for w, _ in outs]
    out_specs += [_bs((1, w), lambda i: (0, 0)) for w in accs]
    out_shape = [jax.ShapeDtypeStruct((rows, w), dt) for w, dt in outs]
    out_shape += [jax.ShapeDtypeStruct((1, w), F32) for w in accs]
    return _pcall(
        body, name=name, grid=(rows // tm,), in_specs=in_specs, out_specs=out_specs, out_shape=out_shape,
        compiler_params=pltpu.CompilerParams(dimension_semantics=("arbitrary",)),
    )(*tiles, *bcasts)


def _colsum(v):
    return jnp.sum(v, axis=0, keepdims=True)


def _rms_fwd(name, x, gain, out_dtype=BF16):
    def fn(t, b):
        xv = t[0]
        r = lax.rsqrt(jnp.mean(xv * xv, axis=-1, keepdims=True) + EPS)
        return (xv * r * b[0], r), ()
    w = x.shape[1]
    return _rowwise(name, fn, [x], [gain.reshape(1, w)], [(w, out_dtype), (1, F32)])


def _rms_bwd(name, dh, x, rstd, gain, dres=None, with_bf16=False):
    def fn(t, b):
        dhv, xv, r = t[0], t[1], t[2]
        xh = xv * r
        dxh = dhv * b[0]
        dx = r * (dxh - xh * jnp.mean(dxh * xh, axis=-1, keepdims=True))
        if dres is not None:
            dx = dx + t[3]
        return ((dx, dx) if with_bf16 else (dx,)), (_colsum(dhv * xh),)
    w = x.shape[1]
    tiles = [dh, x, rstd] + ([dres] if dres is not None else [])
    outs = [(w, F32), (w, BF16)] if with_bf16 else [(w, F32)]
    return _rowwise(name, fn, tiles, [gain.reshape(1, w)], outs, [w])


def _sigmoid(v):
    return 1.0 / (1.0 + jnp.exp(-v))


def _ffn_fwd(tag, x, gain, wg2, wu2, wd):
    T, D = x.shape
    fs = wg2.shape[1]
    h, rstd = _rms_fwd(tag + "_rms", x, gain)
    tm = _tile(T, 256)
    a_spec = _bs((tm, D), lambda i, j, k: (j, 0))
    w_spec = _bs((D, fs), lambda i, j, k: (i, 0))
    o_spec = _bs((tm, fs), lambda i, j, k: (j, i))

    def epi(accs, ex):
        a, b = accs
        return a, b, a * _sigmoid(a) * b

    a, b, t = _mm(
        tag + "_up", (N_SHARD, T // tm, 1),
        [(h, a_spec, wg2, w_spec, NN, 0), (h, a_spec, wu2, w_spec, NN, 1)],
        [(tm, fs), (tm, fs)],
        [((T, N_SHARD * fs), BF16, o_spec)] * 3,
        epi)
    if callable(wd):
        wd = wd(t)
    y = _mm_simple(tag + "_down", t, wd, NN, tm=512, tn=D, tk=fs // 2 if fs % 256 == 0 else fs,
                   epilogue=lambda accs, ex: (ex[0] + 0.5 * accs[0],), extras=[x])
    return y, (x, h, rstd, a, b, t), wd


def _ffn_bwd(tag, dy, dy16, saved, gain, wg2, wu2, wd, early=None, keys=None):
    x, h, rstd, a, b, t = saved
    T, D = x.shape
    fs = wg2.shape[1]
    F = N_SHARD * fs
    dwd = _mm_simple(tag + "_dwd", t, dy16, TN, tm=fs, tn=D, tk=512, scale=0.5, out_dtype=BF16)
    if early is not None:
        dy16 = early({keys[2]: dwd}, dy16)

    def epi(accs, ex):
        dt = 0.5 * accs[0]
        av, bv = ex[0].astype(F32), ex[1].astype(F32)
        sig = _sigmoid(av)
        da = dt * bv * (sig * (1.0 + av * (1.0 - sig)))
        db = dt * (av * sig)
        return da, db

    tn = 512 if F % 512 == 0 else (256 if F % 256 == 0 else 128)
    da, db = _mm_simple(tag + "_dt", dy16, wd, NT, tm=1024, tn=tn, tk=D, out_dtype=(BF16, BF16), epilogue=epi,
                        extras=[a, b])
    tm = _tile(D, 1024)
    per = D // tm
    tk = _tile(T, 512)
    h_spec = _bs((tk, tm), lambda i, j, k: (k, i % per))
    d_spec = _bs((tk, fs), lambda i, j, k: (k, i // per))
    o_spec = _bs((tm, fs), lambda i, j, k: (i, 0))
    dwg2, dwu2 = _mm(
        tag + "_dwgu", (N_SHARD * per, 1, T // tk),
        [(h, h_spec, da, d_spec, TN, 0), (h, h_spec, db, d_spec, TN, 1)],
        [(tm, fs), (tm, fs)],
        [((N_SHARD * D, fs), BF16, o_spec), ((N_SHARD * D, fs), BF16, o_spec)],
        lambda accs, ex: tuple(accs))
    tm2 = _tile(T, 512)
    tn2 = _tile(D, 1024)
    per2 = D // tn2
    if early is not None:
        da = early({keys[0]: dwg2, keys[1]: dwu2}, da)
    g_spec = _bs((tm2, fs), lambda i, j, k: (i, k))
    w_spec = _bs((tn2, fs), lambda i, j, k: (k * per2 + j, 0))
    o2 = _bs((tm2, tn2), lambda i, j, k: (i, j))
    dh, = _mm(
        tag + "_dh", (T // tm2, per2, N_SHARD),
        [(da, g_spec, wg2, w_spec, NT, 0), (db, g_spec, wu2, w_spec, NT, 0)],
        [(tm2, tn2)], [((T, D), F32, o2)], lambda accs, ex: (accs[0],))
    dx, dx16, dgain = _rms_bwd(tag + "_rmsb", dh, x, rstd, gain, dres=dy, with_bf16=True)
    return dx, dx16, dgain, dwg2, dwu2, dwd


def _rope_tables(positions):
    inv_freq = ROPE_THETA ** (-jnp.arange(0, 2 * ROPE_HALF, 2, dtype=F32) / (2 * ROPE_HALF))
    ang = positions.astype(F32)[:, None] * inv_freq
    c, s = jnp.cos(ang), jnp.sin(ang)
    c64 = jnp.concatenate([c, c], axis=1)
    s64 = jnp.concatenate([-s, s], axis=1)
    T = positions.shape[0]
    one, zero = jnp.ones((T, MLA_NOPE), F32), jnp.zeros((T, MLA_NOPE), F32)
    pad1, pad0 = jnp.ones((T, MLA_SLOT - MLA_NOPE - MLA_ROPE), F32), jnp.zeros((T, MLA_SLOT - MLA_NOPE - MLA_ROPE), F32)
    c256 = jnp.concatenate([one, c64, pad1], axis=1)
    s256 = jnp.concatenate([zero, s64, pad0], axis=1)
    return c64, s64, c256, s256


def _half_swap(v):
    w = v.shape[1]
    lane = lax.broadcasted_iota(jnp.int32, v.shape, 1)
    first = (lane % (2 * ROPE_HALF)) < ROPE_HALF
    return jnp.where(first, pltpu.roll(v, w - ROPE_HALF, 1), pltpu.roll(v, ROPE_HALF, 1))


def _rope(name, x, ctab, stab, backward=False, out_dtype=F32):
    reps = x.shape[1] // ctab.shape[1]

    def fn(t, b):
        xv = t[0]
        c = jnp.tile(t[1], (1, reps)) if reps > 1 else t[1]
        s = jnp.tile(t[2], (1, reps)) if reps > 1 else t[2]
        if backward:
            return (xv * c + _half_swap(xv * s),), ()
        return (xv * c + _half_swap(xv) * s,), ()
    return _rowwise(name, fn, [x, ctab, stab], [], [(x.shape[1], out_dtype)])[0]


def _s5_prep(log_dt, a_re, a_im, b_re, b_im, c_re, c_im):
    G, P = a_re.shape
    C = b_re.shape[-1]
    dt = jnp.exp(log_dt)[:, None]
    mag = jnp.exp(a_re * dt)
    abar_r = mag * jnp.cos(a_im * dt)
    abar_i = mag * jnp.sin(a_im * dt)
    den = a_re * a_re + a_im * a_im
    nr = abar_r - 1.0
    qr = (nr * a_re + abar_i * a_im) / den
    qi = (abar_i * a_re - nr * a_im) / den
    bbar_r = qr[..., None] * b_re - qi[..., None] * b_im
    bbar_i = qr[..., None] * b_im + qi[..., None] * b_re
    nt = G // S5_TILE
    eye = jnp.eye(S5_TILE, dtype=F32)
    tiles_b = lambda t: jnp.einsum('tgpc,gh->tgchp', t.reshape(nt, S5_TILE, P, C), eye).reshape(
        nt, S5_TILE * C, S5_TILE * P)
    tiles_c = lambda t: jnp.einsum('tgcp,gh->tgphc', t.reshape(nt, S5_TILE, C, P), eye).reshape(
        nt, S5_TILE * P, S5_TILE * C)
    return (abar_r.reshape(1, G * P), abar_i.reshape(1, G * P), tiles_b(bbar_r), tiles_b(bbar_i), tiles_c(c_re),
            -tiles_c(c_im))


def _scan_lanes(gp):
    for tl in (1024, 512, 256):
        if gp % tl == 0:
            return tl
    return 128


def _scan_fwd(name, bu_r, bu_i, a_r, a_i):
    T, gp = bu_r.shape
    tl = _scan_lanes(gp)
    tc = _tile(T, 256)

    def body(br_ref, bi_ref, ar_ref, ai_ref, xr_ref, xi_ref, pr_ref, pi_ref, sr, si):
        @pl.when(pl.program_id(1) == 0)
        def _():
            sr[...] = jnp.zeros_like(sr)
            si[...] = jnp.zeros_like(si)
        ar, ai = ar_ref[...], ai_ref[...]

        def step(g, carry):
            xr, xi = carry
            base = pl.multiple_of(g * 8, 8)
            b_r = br_ref[pl.ds(base, 8), :]
            b_i = bi_ref[pl.ds(base, 8), :]
            rows_r, rows_i, prev_r, prev_i = [], [], [], []
            for j in range(8):
                prev_r.append(xr)
                prev_i.append(xi)
                nr = ar * xr - ai * xi + b_r[j:j + 1, :]
                ni = ar * xi + ai * xr + b_i[j:j + 1, :]
                xr, xi = nr, ni
                rows_r.append(xr)
                rows_i.append(xi)
            xr_ref[pl.ds(base, 8), :] = jnp.concatenate(rows_r, axis=0)
            xi_ref[pl.ds(base, 8), :] = jnp.concatenate(rows_i, axis=0)
            pr_ref[pl.ds(base, 8), :] = jnp.concatenate(prev_r, axis=0)
            pi_ref[pl.ds(base, 8), :] = jnp.concatenate(prev_i, axis=0)
            return xr, xi

        xr, xi = lax.fori_loop(0, tc // 8, step, (sr[...], si[...]))
        sr[...] = xr
        si[...] = xi

    spec = _bs((tc, tl), lambda l, t: (t, l))
    a_spec = _bs((1, tl), lambda l, t: (0, l))
    return _pcall(
        body, name=name, grid=(gp // tl, T // tc),
        in_specs=[spec, spec, a_spec, a_spec], out_specs=[spec] * 4,
        out_shape=[jax.ShapeDtypeStruct((T, gp), F32)] * 4,
        scratch_shapes=[pltpu.VMEM((1, tl), F32), pltpu.VMEM((1, tl), F32)],
        compiler_params=pltpu.CompilerParams(dimension_semantics=("parallel", "arbitrary")),
    )(bu_r, bu_i, a_r, a_i)


def _scan_bwd(name, g_r, g_i, p_r, p_i, a_r, a_i):
    T, gp = g_r.shape
    tl = _scan_lanes(gp)
    tc = _tile(T, 256)
    nt = T // tc
    ng = tc // 8

    def body(gr_ref, gi_ref, pr_ref, pi_ref, ar_ref, ai_ref, lr_ref, li_ref, dar_ref, dai_ref, sr, si, accr, acci):
        t = pl.program_id(1)

        @pl.when(t == 0)
        def _():
            sr[...] = jnp.zeros_like(sr)
            si[...] = jnp.zeros_like(si)
            accr[...] = jnp.zeros_like(accr)
            acci[...] = jnp.zeros_like(acci)
        ar, ai = ar_ref[...], ai_ref[...]

        def step(kk, carry):
            lr, li = carry
            base = pl.multiple_of((ng - 1 - kk) * 8, 8)
            gr8 = gr_ref[pl.ds(base, 8), :]
            gi8 = gi_ref[pl.ds(base, 8), :]
            rows_r, rows_i = [None] * 8, [None] * 8
            for j in range(7, -1, -1):
                nr = gr8[j:j + 1, :] + ar * lr + ai * li
                ni = gi8[j:j + 1, :] - ai * lr + ar * li
                lr, li = nr, ni
                rows_r[j] = lr
                rows_i[j] = li
            lam_r = jnp.concatenate(rows_r, axis=0)
            lam_i = jnp.concatenate(rows_i, axis=0)
            lr_ref[pl.ds(base, 8), :] = lam_r
            li_ref[pl.ds(base, 8), :] = lam_i
            pr8 = pr_ref[pl.ds(base, 8), :]
            pi8 = pi_ref[pl.ds(base, 8), :]
            accr[...] += lam_r * pr8 + lam_i * pi8
            acci[...] += lam_i * pr8 - lam_r * pi8
            return lr, li

        lr, li = lax.fori_loop(0, ng, step, (sr[...], si[...]))
        sr[...] = lr
        si[...] = li

        @pl.when(t == nt - 1)
        def _():
            dar_ref[...] = jnp.sum(accr[...], axis=0, keepdims=True)
            dai_ref[...] = jnp.sum(acci[...], axis=0, keepdims=True)

    spec = _bs((tc, tl), lambda l, t: (nt - 1 - t, l))
    a_spec = _bs((1, tl), lambda l, t: (0, l))
    return _pcall(
        body, name=name, grid=(gp // tl, nt),
        in_specs=[spec] * 4 + [a_spec, a_spec], out_specs=[spec, spec, a_spec, a_spec],
        out_shape=[jax.ShapeDtypeStruct((T, gp), F32)] * 2 + [jax.ShapeDtypeStruct((1, gp), F32)] * 2,
        scratch_shapes=[pltpu.VMEM((1, tl), F32), pltpu.VMEM((1, tl), F32),
                        pltpu.VMEM((8, tl), F32), pltpu.VMEM((8, tl), F32)],
        compiler_params=pltpu.CompilerParams(dimension_semantics=("parallel", "arbitrary")),
    )(g_r, g_i, p_r, p_i, a_r, a_i)


_GELU_C = math.sqrt(2.0 / math.pi)


def _gelu(v):
    return 0.5 * v * (1.0 + jnp.tanh(_GELU_C * (v + 0.044715 * v * v * v)))


def _gelu_grad(v):
    th = jnp.tanh(_GELU_C * (v + 0.044715 * v * v * v))
    return 0.5 * (1.0 + th) + 0.5 * v * (1.0 - th * th) * _GELU_C * (1.0 + 3.0 * 0.044715 * v * v)


def _s5_specs(T, nt, cu, cs):
    tm = _tile(T, 512)
    return dict(
        tm=tm,
        chan=_bs((tm, cu), lambda i, j, k: (i, j)), state=_bs((tm, cs), lambda i, j, k: (i, j)),
        b=_bs((None, cu, cs), lambda i, j, k: (j, 0, 0)), c=_bs((None, cs, cu), lambda i, j, k: (j, 0, 0)),
        row=_bs((1, cu), lambda i, j, k: (0, j)))


def _s5_fwd(tag, u, prep, d_skip, w_glu, b_glu):
    a_r, a_i, b_r, b_i, c_r, c_in = prep
    T, W = u.shape
    gp = a_r.shape[1]
    nt, cu, cs = b_r.shape
    sp = _s5_specs(T, nt, cu, cs)
    tm = sp['tm']
    bu_r, bu_i = _mm(tag + "_bu", (T // tm, nt, 1),
                     [(u, sp['chan'], b_r, sp['b'], NN, 0), (u, sp['chan'], b_i, sp['b'], NN, 1)],
                     [(tm, cs)] * 2, [((T, gp), F32, sp['state'])] * 2, lambda accs, ex: tuple(accs))
    x_r, x_i, p_r, p_i = _scan_fwd(tag + "_scan", bu_r, bu_i, a_r, a_i)

    def epi(accs, ex):
        y = accs[0] + ex[1] * ex[0]
        return y, _gelu(y)

    ypre, yg = _mm(tag + "_y", (T // tm, nt, 1),
                   [(x_r, sp['state'], c_r, sp['c'], NN, 0), (x_i, sp['state'], c_in, sp['c'], NN, 0)],
                   [(tm, cu)], [((T, W), F32, sp['chan'])] * 2, epi,
                   [(u, sp['chan']), (d_skip.reshape(1, W), sp['row'])])

    def epi2(accs, ex):
        pre = accs[0] + ex[1]
        return ex[0] * _sigmoid(pre), pre

    out, pre = _mm_simple(tag + "_glu", yg, w_glu, NN, tm=512, tn=W, tk=W, out_dtype=(F32, F32), epilogue=epi2,
                          extras=[yg, b_glu.reshape(1, W)])
    return out, (u, x_r, x_i, p_r, p_i, ypre, yg, pre)


def _s5_bwd(tag, d_out, saved, prep, d_skip, w_glu):
    u, x_r, x_i, p_r, p_i, ypre, yg, pre = saved
    a_r, a_i, b_r, b_i, c_r, c_in = prep
    T, W = u.shape

    def gate_fn(t, b):
        gate = _sigmoid(t[2])
        dpre = t[0] * t[1] * gate * (1.0 - gate)
        return (dpre, t[0] * gate), (_colsum(dpre),)

    dpre, tmp, db_glu = _rowwise(tag + "_gateb", gate_fn, [d_out, yg, pre], [], [(W, F32), (W, F32)], [W])
    dw_glu = _mm_simple(tag + "_dwglu", yg, dpre, TN, tm=W, tn=W, tk=512, out_dtype=BF16)
    dy = _mm_simple(tag + "_dyg", dpre, w_glu, NT, tm=512, tn=W, tk=W,
                    epilogue=lambda accs, ex: ((accs[0] + ex[0]) * _gelu_grad(ex[1]),), extras=[tmp, ypre])
    dd, = _rowwise(tag + "_dd", lambda t, b: ((), (_colsum(t[0] * t[1]),)), [dy, u], [], [], [W])
    gp = a_r.shape[1]
    nt, cu, cs = b_r.shape
    sp = _s5_specs(T, nt, cu, cs)
    tm = sp['tm']
    dx_r, dx_i = _mm(tag + "_dx", (T // tm, nt, 1),
                     [(dy, sp['chan'], c_r, sp['c'], NT, 0), (dy, sp['chan'], c_in, sp['c'], NT, 1)],
                     [(tm, cs)] * 2, [((T, gp), F32, sp['state'])] * 2, lambda accs, ex: tuple(accs))
    tk = _tile(T, 512)
    t_chan = _bs((tk, cu), lambda i, j, k: (k, i))
    t_state = _bs((tk, cs), lambda i, j, k: (k, i))
    o_b = _bs((None, cu, cs), lambda i, j, k: (i, 0, 0))
    o_c = _bs((None, cs, cu), lambda i, j, k: (i, 0, 0))
    dc_r, dc_in = _mm(tag + "_dc", (nt, 1, T // tk),
                      [(x_r, t_state, dy, t_chan, TN, 0), (x_i, t_state, dy, t_chan, TN, 1)],
                      [(cs, cu)] * 2, [((nt, cs, cu), F32, o_c)] * 2, lambda accs, ex: tuple(accs))
    lam_r, lam_i, da_r, da_i = _scan_bwd(tag + "_scanb", dx_r, dx_i, p_r, p_i, a_r, a_i)
    du, = _mm(tag + "_du", (T // tm, nt, 1),
              [(lam_r, sp['state'], b_r, sp['b'], NT, 0), (lam_i, sp['state'], b_i, sp['b'], NT, 0)],
              [(tm, cu)], [((T, W), F32, sp['chan'])], lambda accs, ex: (accs[0] + ex[1] * ex[0],),
              [(dy, sp['chan']), (d_skip.reshape(1, W), sp['row'])])
    db_r, db_i = _mm(tag + "_db", (nt, 1, T // tk),
                     [(u, t_chan, lam_r, t_state, TN, 0), (u, t_chan, lam_i, t_state, TN, 1)],
                     [(cu, cs)] * 2, [((nt, cu, cs), F32, o_b)] * 2, lambda accs, ex: tuple(accs))
    return du, (da_r, da_i, db_r, db_i, dc_r, dc_in), dd, dw_glu, db_glu


def _swa_mask(n):
    B = SWA_BLOCK
    r = lax.broadcasted_iota(jnp.int32, (B, 2 * B), 0)
    c = lax.broadcasted_iota(jnp.int32, (B, 2 * B), 1)
    d = r + B - c
    return (d >= 0) & (d < B) & ((n > 0) | (c >= B))


def _swa_specs(T):
    B = SWA_BLOCK
    cur = lambda w: _bs((B, w), lambda n: (n, 0))
    prev = lambda w: _bs((B, w), lambda n: (jnp.maximum(n - 1, 0), 0))
    return cur, prev


def _swa_fwd(name, q, k, v, sinks_b):
    T = q.shape[0]
    B = SWA_BLOCK
    scale = SWA_HEAD_DIM ** -0.5
    per_kv = SWA_HEADS // SWA_KV_HEADS

    def body(q_ref, kc_ref, kp_ref, vc_ref, vp_ref, s_ref, o_ref, l_ref):
        n = pl.program_id(0)
        kcat = jnp.concatenate([kp_ref[...], kc_ref[...]], axis=0).astype(BF16)
        vcat = jnp.concatenate([vp_ref[...], vc_ref[...]], axis=0).astype(BF16)
        mask = _swa_mask(n)
        lane = lax.broadcasted_iota(jnp.int32, (B, 128), 1)
        lo = lane < SWA_HEAD_DIM
        lse_out = jnp.zeros((B, 128), F32)
        for jb in range(SWA_HEADS // 2):
            qblk = q_ref[:, jb * 128:(jb + 1) * 128]
            h = (2 * jb) // per_kv
            half_h = lo if h == 0 else jnp.logical_not(lo)
            outs = []
            for e in range(2):
                j = 2 * jb + e
                qa = qblk if e == h else pltpu.roll(qblk, SWA_HEAD_DIM, 1)
                qm = jnp.where(half_h, qa, 0.0).astype(BF16)
                s = lax.dot_general(qm, kcat, (NT, ((), ())), preferred_element_type=F32) * scale
                s = jnp.where(mask, s, -jnp.inf)
                sk = s_ref[j:j + 1, 0:1]
                m = jnp.maximum(jnp.max(s, axis=1, keepdims=True), sk)
                ex = jnp.exp(s - m)
                den = jnp.sum(ex, axis=1, keepdims=True) + jnp.exp(sk - m)
                p = ex / den
                r = lax.dot_general(p.astype(BF16), vcat, (NN, ((), ())), preferred_element_type=F32)
                outs.append(r if e == h else pltpu.roll(r, SWA_HEAD_DIM, 1))
                lse_out = jnp.where(lane == j, m + jnp.log(den), lse_out)
            o_ref[:, jb * 128:(jb + 1) * 128] = jnp.where(lo, outs[0], outs[1])
        l_ref[...] = lse_out

    cur, prev = _swa_specs(T)
    return _pcall(
        body, name=name, grid=(T // B,),
        in_specs=[cur(SWA_WIDTH), cur(SWA_KV_WIDTH), prev(SWA_KV_WIDTH), cur(SWA_KV_WIDTH), prev(SWA_KV_WIDTH),
                  _bs(sinks_b.shape, lambda n: (0, 0))],
        out_specs=[cur(SWA_WIDTH), cur(128)],
        out_shape=[jax.ShapeDtypeStruct((T, SWA_WIDTH), F32), jax.ShapeDtypeStruct((T, 128), F32)],
        compiler_params=pltpu.CompilerParams(dimension_semantics=("parallel",)),
    )(q, k, k, v, v, sinks_b)


def _swa_bwd(name, q, k, v, sinks_b, o, lse, do):
    T = q.shape[0]
    B = SWA_BLOCK
    scale = SWA_HEAD_DIM ** -0.5
    per_kv = SWA_HEADS // SWA_KV_HEADS

    def body(q_ref, kc_ref, kp_ref, vc_ref, vp_ref, s_ref, o_ref, l_ref, do_ref,
             dq_ref, dkc_ref, dkp_ref, dvc_ref, dvp_ref, ds_ref):
        n = pl.program_id(0)

        @pl.when(n == 0)
        def _():
            ds_ref[...] = jnp.zeros_like(ds_ref)
        kcat = jnp.concatenate([kp_ref[...], kc_ref[...]], axis=0).astype(BF16)
        vcat = jnp.concatenate([vp_ref[...], vc_ref[...]], axis=0).astype(BF16)
        mask = _swa_mask(n)
        lane = lax.broadcasted_iota(jnp.int32, (B, 128), 1)
        lane1 = lax.broadcasted_iota(jnp.int32, (1, 128), 1)
        lo = lane < SWA_HEAD_DIM
        lblk = l_ref[...]
        dk = jnp.zeros((2 * B, 128), F32)
        dv = jnp.zeros((2 * B, 128), F32)
        dsink = jnp.zeros((1, 128), F32)
        for jb in range(SWA_HEADS // 2):
            sl = slice(jb * 128, (jb + 1) * 128)
            qblk, doblk = q_ref[:, sl], do_ref[:, sl]
            prod = doblk * o_ref[:, sl]
            h = (2 * jb) // per_kv
            half_h = lo if h == 0 else jnp.logical_not(lo)
            parts = []
            for e in range(2):
                j = 2 * jb + e
                half_e = lo if e == 0 else jnp.logical_not(lo)
                dsum = jnp.sum(jnp.where(half_e, prod, 0.0), axis=1, keepdims=True)
                lj = jnp.sum(jnp.where(lane == j, lblk, 0.0), axis=1, keepdims=True)
                qa = qblk if e == h else pltpu.roll(qblk, SWA_HEAD_DIM, 1)
                da = doblk if e == h else pltpu.roll(doblk, SWA_HEAD_DIM, 1)
                qm = jnp.where(half_h, qa, 0.0).astype(BF16)
                dm = jnp.where(half_h, da, 0.0).astype(BF16)
                s = lax.dot_general(qm, kcat, (NT, ((), ())), preferred_element_type=F32) * scale
                p = jnp.where(mask, jnp.exp(s - lj), 0.0)
                dp = lax.dot_general(dm, vcat, (NT, ((), ())), preferred_element_type=F32)
                dsb = (p * (dp - dsum) * scale).astype(BF16)
                dqa = lax.dot_general(dsb, kcat, (NN, ((), ())), preferred_element_type=F32)
                parts.append(dqa if e == h else pltpu.roll(dqa, SWA_HEAD_DIM, 1))
                dk = dk + lax.dot_general(dsb, qm, (TN, ((), ())), preferred_element_type=F32)
                dv = dv + lax.dot_general(p.astype(BF16), dm, (TN, ((), ())), preferred_element_type=F32)
                sk = s_ref[j:j + 1, 0:1]
                contrib = jnp.sum(jnp.exp(sk - lj) * dsum, axis=0, keepdims=True)
                dsink = jnp.where(lane1 == j, dsink - contrib, dsink)
            dq_ref[:, sl] = jnp.where(lo, parts[0], parts[1])
        dkp_ref[...] = dk[:B]
        dkc_ref[...] = dk[B:]
        dvp_ref[...] = dv[:B]
        dvc_ref[...] = dv[B:]
        ds_ref[...] += dsink

    cur, prev = _swa_specs(T)
    kv = jax.ShapeDtypeStruct((T, SWA_KV_WIDTH), F32)
    return _pcall(
        body, name=name, grid=(T // B,),
        in_specs=[cur(SWA_WIDTH), cur(SWA_KV_WIDTH), prev(SWA_KV_WIDTH), cur(SWA_KV_WIDTH), prev(SWA_KV_WIDTH),
                  _bs(sinks_b.shape, lambda n: (0, 0)), cur(SWA_WIDTH), cur(128), cur(SWA_WIDTH)],
        out_specs=[cur(SWA_WIDTH)] + [cur(SWA_KV_WIDTH)] * 4 + [_bs((1, 128), lambda n: (0, 0))],
        out_shape=[jax.ShapeDtypeStruct((T, SWA_WIDTH), F32), kv, kv, kv, kv, jax.ShapeDtypeStruct((1, 128), F32)],
        compiler_params=pltpu.CompilerParams(dimension_semantics=("arbitrary",)),
    )(q, k, k, v, v, sinks_b, o, lse, do)


def _shift_add(name, cur, prv):
    T, W = cur.shape
    B = SWA_BLOCK
    nb = T // B

    def body(c_ref, p_ref, o_ref):
        n = pl.program_id(0)
        o_ref[...] = c_ref[...] + jnp.where(n < nb - 1, p_ref[...], 0.0)

    return _pcall(
        body, name=name, grid=(nb,),
        in_specs=[_bs((B, W), lambda n: (n, 0)), _bs((B, W), lambda n: (jnp.minimum(n + 1, nb - 1), 0))],
        out_specs=_bs((B, W), lambda n: (n, 0)), out_shape=jax.ShapeDtypeStruct((T, W), F32),
        compiler_params=pltpu.CompilerParams(dimension_semantics=("parallel",)),
    )(cur, prv)


MLA_BLOCK = 512
_MLA_SCALE = (MLA_NOPE + MLA_ROPE) ** -0.5


def _lower_tri(tb):
    r = lax.broadcasted_iota(jnp.int32, (tb, tb), 0)
    c = lax.broadcasted_iota(jnp.int32, (tb, tb), 1)
    return c <= r


def _mla_fwd(name, qcat, kcat, v):
    T = qcat.shape[0]
    tb = _tile(T, MLA_BLOCK)
    nb = T // tb

    def body(q_ref, k_ref, v_ref, o_ref, l_ref):
        qi = pl.program_id(1)
        q = q_ref[...]

        def block(kb, carry, diagonal):
            m_prev, l_prev, acc = carry
            rows = pl.ds(pl.multiple_of(kb * tb, tb), tb)
            s = lax.dot_general(q, k_ref[rows, :], (NT, ((), ())), preferred_element_type=F32) * _MLA_SCALE
            if diagonal:
                s = jnp.where(_lower_tri(tb), s, -jnp.inf)
            m_new = jnp.maximum(m_prev, jnp.max(s, axis=1, keepdims=True))
            alpha = jnp.exp(m_prev - m_new)
            p = jnp.exp(s - m_new)
            l_new = alpha * l_prev + jnp.sum(p, axis=1, keepdims=True)
            acc = alpha * acc + lax.dot_general(p.astype(BF16), v_ref[rows, :], (NN, ((), ())),
                                                preferred_element_type=F32)
            return m_new, l_new, acc

        init = (jnp.full((tb, 1), -jnp.inf, F32), jnp.zeros((tb, 1), F32), jnp.zeros((tb, MLA_V), F32))
        carry = lax.fori_loop(0, qi, lambda kb, c: block(kb, c, False), init)
        m_fin, l_fin, acc = block(qi, carry, True)
        o_ref[...] = acc / l_fin
        l_ref[...] = m_fin + jnp.log(l_fin)

    return _pcall(
        body, name=name, grid=(MLA_HEADS, nb),
        in_specs=[_bs((tb, MLA_SLOT), lambda h, qi: (qi, h)), _bs((T, MLA_SLOT), lambda h, qi: (0, h)),
                  _bs((T, MLA_V), lambda h, qi: (0, h))],
        out_specs=[_bs((tb, MLA_V), lambda h, qi: (qi, h)), _bs((None, tb, 1), lambda h, qi: (h, qi, 0))],
        out_shape=[jax.ShapeDtypeStruct((T, MLA_WIDTH), F32), jax.ShapeDtypeStruct((MLA_HEADS, T, 1), F32)],
        compiler_params=pltpu.CompilerParams(dimension_semantics=("parallel", "arbitrary")),
    )(qcat, kcat, v)


def _mla_bwd_q(name, qcat, kcat, v, o, lse, do):
    T = qcat.shape[0]
    tb = _tile(T, MLA_BLOCK)
    nb = T // tb

    def body(q_ref, k_ref, v_ref, o_ref, do_ref, l_ref, dq_ref, ds_ref):
        qi = pl.program_id(1)
        q = q_ref[...]
        do = do_ref[...]
        dsum = jnp.sum(do * o_ref[...], axis=1, keepdims=True)
        ds_ref[...] = dsum
        dob = do.astype(BF16)
        lse_q = l_ref[...]

        def block(kb, acc, diagonal):
            rows = pl.ds(pl.multiple_of(kb * tb, tb), tb)
            k = k_ref[rows, :]
            s = lax.dot_general(q, k, (NT, ((), ())), preferred_element_type=F32) * _MLA_SCALE
            p = jnp.exp(s - lse_q)
            if diagonal:
                p = jnp.where(_lower_tri(tb), p, 0.0)
            dp = lax.dot_general(dob, v_ref[rows, :], (NT, ((), ())), preferred_element_type=F32)
            ds = (p * (dp - dsum) * _MLA_SCALE).astype(BF16)
            return acc + lax.dot_general(ds, k, (NN, ((), ())), preferred_element_type=F32)

        acc = lax.fori_loop(0, qi, lambda kb, a: block(kb, a, False), jnp.zeros((tb, MLA_SLOT), F32))
        dq_ref[...] = block(qi, acc, True)

    qs = lambda w: _bs((tb, w), lambda h, qi: (qi, h))
    col = _bs((None, tb, 1), lambda h, qi: (h, qi, 0))
    return _pcall(
        body, name=name, grid=(MLA_HEADS, nb),
        in_specs=[qs(MLA_SLOT), _bs((T, MLA_SLOT), lambda h, qi: (0, h)), _bs((T, MLA_V), lambda h, qi: (0, h)),
                  qs(MLA_V), qs(MLA_V), col],
        out_specs=[qs(MLA_SLOT), col],
        out_shape=[jax.ShapeDtypeStruct((T, MLA_HEADS * MLA_SLOT), F32), jax.ShapeDtypeStruct((MLA_HEADS, T, 1), F32)],
        compiler_params=pltpu.CompilerParams(dimension_semantics=("parallel", "arbitrary")),
    )(qcat, kcat, v, o, do, lse)


def _mla_bwd_kv(name, qcat, kcat, v, lse, dsum, do):
    T = qcat.shape[0]
    tb = _tile(T, MLA_BLOCK)
    nb = T // tb

    def body(q_ref, k_ref, v_ref, do_ref, l_ref, ds_ref, dk_ref, dv_ref):
        ki = pl.program_id(1)
        k = k_ref[...]
        vv = v_ref[...]

        def block(qb, carry, diagonal):
            dk, dv = carry
            rows = pl.ds(pl.multiple_of(qb * tb, tb), tb)
            q = q_ref[rows, :]
            dob = do_ref[rows, :].astype(BF16)
            s = lax.dot_general(q, k, (NT, ((), ())), preferred_element_type=F32) * _MLA_SCALE
            p = jnp.exp(s - l_ref[rows, :])
            if diagonal:
                p = jnp.where(_lower_tri(tb), p, 0.0)
            dp = lax.dot_general(dob, vv, (NT, ((), ())), preferred_element_type=F32)
            ds = (p * (dp - ds_ref[rows, :]) * _MLA_SCALE).astype(BF16)
            dv = dv + lax.dot_general(p.astype(BF16), dob, (TN, ((), ())), preferred_element_type=F32)
            dk = dk + lax.dot_general(ds, q, (TN, ((), ())), preferred_element_type=F32)
            return dk, dv

        carry = block(ki, (jnp.zeros((tb, MLA_SLOT), F32), jnp.zeros((tb, MLA_V), F32)), True)
        dk, dv = lax.fori_loop(ki + 1, nb, lambda qb, c: block(qb, c, False), carry)
        dk_ref[...] = dk
        dv_ref[...] = dv

    ks = lambda w: _bs((tb, w), lambda h, ki: (ki, h))
    col = _bs((None, T, 1), lambda h, ki: (h, 0, 0))
    return _pcall(
        body, name=name, grid=(MLA_HEADS, nb),
        in_specs=[_bs((T, MLA_SLOT), lambda h, ki: (0, h)), ks(MLA_SLOT), ks(MLA_V),
                  _bs((T, MLA_V), lambda h, ki: (0, h)), col, col],
        out_specs=[ks(MLA_SLOT), ks(MLA_V)],
        out_shape=[jax.ShapeDtypeStruct((T, MLA_HEADS * MLA_SLOT), F32), jax.ShapeDtypeStruct((T, MLA_WIDTH), F32)],
        compiler_params=pltpu.CompilerParams(dimension_semantics=("parallel", "arbitrary")),
    )(qcat, kcat, v, do, lse, dsum)


def _kcat_fwd(name, kpre, krs, c256, s256):
    def fn(t, b):
        kr = t[1] * t[2] + _half_swap(t[1]) * t[3]
        return (t[0] + jnp.tile(kr, (1, MLA_HEADS)),), ()
    return _rowwise(name, fn, [kpre, krs, c256, s256], [], [(kpre.shape[1], BF16)])[0]


def _kcat_bwd(name, dkcat, c256, s256):
    def fn(t, b):
        d = t[0][:, 0:MLA_SLOT]
        for h in range(1, MLA_HEADS):
            d = d + t[0][:, h * MLA_SLOT:(h + 1) * MLA_SLOT]
        return (d * t[1] + _half_swap(d * t[2]),), ()
    return _rowwise(name, fn, [dkcat, c256, s256], [], [(MLA_SLOT, F32)])[0]


def _in_widths(ws):
    return [ws, SWA_WIDTH, SWA_KV_WIDTH, SWA_KV_WIDTH, MLA_Q_RANK, MLA_KV_RANK, MLA_SLOT]


def _kr_offset(ws):
    return ws + SWA_WIDTH + 2 * SWA_KV_WIDTH + MLA_Q_RANK + MLA_KV_RANK


def _pad_w_in(w, ws):
    z = lambda n: jnp.zeros((w.shape[0], n), w.dtype)
    o = _kr_offset(ws)
    return jnp.concatenate([w[:, :o], z(MLA_NOPE), w[:, o:], z(MLA_SLOT - MLA_NOPE - MLA_ROPE)], axis=1)


def _unpad_w_in(dw, ws):
    o = _kr_offset(ws)
    return jnp.concatenate([dw[:, :o], dw[:, o + MLA_NOPE:o + MLA_NOPE + MLA_ROPE]], axis=1)


def _pad_w_uq(w):
    r = w.shape[0]
    w3 = w.reshape(r, MLA_HEADS, MLA_NOPE + MLA_ROPE)
    return jnp.pad(w3, ((0, 0), (0, 0), (0, MLA_SLOT - MLA_NOPE - MLA_ROPE))).reshape(r, MLA_HEADS * MLA_SLOT)


def _unpad_w_uq(dw):
    r = dw.shape[0]
    return dw.reshape(r, MLA_HEADS, MLA_SLOT)[..., :MLA_NOPE + MLA_ROPE].reshape(r, -1)


def _pad_w_ukv(w):
    r = w.shape[0]
    w3 = w.reshape(r, MLA_HEADS, MLA_NOPE + MLA_V)
    wk = jnp.pad(w3[..., :MLA_NOPE], ((0, 0), (0, 0), (0, MLA_SLOT - MLA_NOPE))).reshape(r, MLA_HEADS * MLA_SLOT)
    wv = w3[..., MLA_NOPE:].reshape(r, MLA_WIDTH)
    return wk, wv


def _unpad_w_ukv(dwk, dwv):
    r = dwk.shape[0]
    return jnp.concatenate([dwk.reshape(r, MLA_HEADS, MLA_SLOT)[..., :MLA_NOPE], dwv.reshape(r, MLA_HEADS, MLA_V)],
                           axis=-1).reshape(r, -1)


def _layer_prep(lw):
    ws = lw['ssm_d'].shape[0]
    p = dict(lw)
    w_in_pad = _pad_w_in(lw['w_in'], ws)
    p['w_in_pad'] = w_in_pad
    offs = np.cumsum([0] + _in_widths(ws))
    p['w_in_parts'] = [w_in_pad[:, offs[i]:offs[i + 1]] for i in range(7)]
    p['s5_prep'] = _s5_prep(lw['ssm_log_dt'], lw['ssm_a_re'], lw['ssm_a_im'], lw['ssm_b_re'], lw['ssm_b_im'],
                            lw['ssm_c_re'], lw['ssm_c_im'])
    p['sinks_b'] = jnp.broadcast_to(lw['swa_sinks'][:, None], (SWA_HEADS, 128))
    p['w_uq_pad'] = _pad_w_uq(lw['mla_w_uq'])
    p['w_k_pad'], p['w_v'] = _pad_w_ukv(lw['mla_w_ukv'])
    p['w_ukv_pad'] = jnp.concatenate([p['w_k_pad'], p['w_v']], axis=1)
    b = [0, ws, ws + SWA_WIDTH, ws + SWA_WIDTH + MLA_WIDTH]
    p['w_out_g'] = [lw['w_out'][b[g]:b[g + 1]] for g in range(3)]
    p['out_norm_g'] = [lw['out_norm'][b[g]:b[g + 1]] for g in range(3)]
    return p


def _mixer_fwd(tag, x, p, tabs):
    T, D = x.shape
    c64, s64, c256, s256 = tabs
    ws = p['ssm_d'].shape[0]
    widths = _in_widths(ws)
    n_in = sum(widths)
    offs = [int(o) for o in np.cumsum([0] + widths[:-1])]
    h, rstd = _rms_fwd(tag + "_rms", x, p['mix_norm'])
    tm = _tile(T, 256)
    row_i = lambda w: _bs((tm, w), lambda i, j, k: (i, 0))
    parts = _mm(tag + "_in", (T // tm, 1, 1),
                [(h, row_i(D), p['w_in_pad'], _bs((D, n_in), lambda i, j, k: (0, 0)), NN, 0)],
                [(tm, n_in)], [((T, w), F32, row_i(w)) for w in widths],
                lambda accs, ex: tuple(accs[0][:, o:o + w] for o, w in zip(offs, widths)))
    u, q, k, v, cq, ckv, krs = parts
    y_ssm, s5_saved = _s5_fwd(tag + "_s5", u, p['s5_prep'], p['ssm_d'], p['ssm_w_glu'], p['ssm_b_glu'])
    q_r = _rope(tag + "_ropeq", q, c64, s64)
    k_r = _rope(tag + "_ropek", k, c64, s64)
    y_swa, lse_swa = _swa_fwd(tag + "_swa", q_r, k_r, v, p['sinks_b'])
    cqn, r_q = _rms_fwd(tag + "_rmsq", cq, p['mla_q_norm'])
    ckvn, r_kv = _rms_fwd(tag + "_rmskv", ckv, p['mla_kv_norm'])
    qpre = _mm_simple(tag + "_uq", cqn, p['w_uq_pad'], NN, tm=512, tn=1024, tk=MLA_Q_RANK)
    nk_, nv_ = MLA_HEADS * MLA_SLOT, MLA_WIDTH
    kpre, vm = _mm(tag + "_ukv", (T // tm, 1, 1),
                   [(ckvn, row_i(MLA_KV_RANK), p['w_ukv_pad'], _bs((MLA_KV_RANK, nk_ + nv_), lambda i, j, k: (0, 0)),
                     NN, 0)],
                   [(tm, nk_ + nv_)], [((T, nk_), F32, row_i(nk_)), ((T, nv_), BF16, row_i(nv_))],
                   lambda accs, ex: (accs[0][:, :nk_], accs[0][:, nk_:]))
    qcat = _rope(tag + "_ropemq", qpre, c256, s256, out_dtype=BF16)
    kcat = _kcat_fwd(tag + "_kcat", kpre, krs, c256, s256)
    y_mla, lse_mla = _mla_fwd(tag + "_mla", qcat, kcat, vm)
    ys = [y_ssm, y_swa, y_mla]
    yn, rs = [], []
    for g in range(3):
        n_, r_ = _rms_fwd(f"{tag}_rmso{g}", ys[g], p['out_norm_g'][g])
        yn.append(n_)
        rs.append(r_)
    tm3, tn3 = _tile(T, 512), _tile(D, 1024)
    pairs = []
    for g in range(3):
        wg = ys[g].shape[1]
        pairs.append((yn[g], _bs((tm3, wg), lambda i, j, k: (i, 0)), p['w_out_g'][g],
                      _bs((wg, tn3), lambda i, j, k: (0, j)), NN, 0))
    o_spec = _bs((tm3, tn3), lambda i, j, k: (i, j))
    x2, = _mm(tag + "_out", (T // tm3, D // tn3, 1), pairs, [(tm3, tn3)], [((T, D), F32, o_spec)],
              lambda accs, ex: (ex[0] + accs[0],), [(x, o_spec)])
    saved = (x, h, rstd, q_r, k_r, v, cq, ckv, s5_saved, y_swa, lse_swa, cqn, r_q, ckvn, r_kv, qcat, kcat, vm,
             y_mla, lse_mla, ys, yn, rs)
    return x2, saved


def _mixer_bwd(tag, dx2, dx2_16, saved, p, tabs):
    (x, h, rstd, q_r, k_r, v, cq, ckv, s5_saved, y_swa, lse_swa, cqn, r_q, ckvn, r_kv, qcat, kcat, vm,
     y_mla, lse_mla, ys, yn, rs) = saved
    T, D = x.shape
    c64, s64, c256, s256 = tabs
    ws = p['ssm_d'].shape[0]
    g_ = {}
    dys, dwo, don = [], [], []
    for g in range(3):
        wg = ys[g].shape[1]
        dyn = _mm_simple(f"{tag}_dyn{g}", dx2_16, p['w_out_g'][g], NT, tm=512, tn=wg, tk=1024)
        dwo.append(_mm_simple(f"{tag}_dwo{g}", yn[g], dx2_16, TN, tm=512, tn=1024, tk=512, out_dtype=BF16))
        dy_g, don_g = _rms_bwd(f"{tag}_rmsob{g}", dyn, ys[g], rs[g], p['out_norm_g'][g])
        dys.append(dy_g)
        don.append(don_g)
    g_['w_out'] = jnp.concatenate(dwo, axis=0)
    g_['out_norm'] = jnp.concatenate(don, axis=1)[0]
    dqcat, dsum = _mla_bwd_q(tag + "_mlabq", qcat, kcat, vm, y_mla, lse_mla, dys[2])
    dkcat, dvm = _mla_bwd_kv(tag + "_mlabkv", qcat, kcat, vm, lse_mla, dsum, dys[2])
    dqpre = _rope(tag + "_ropemqb", dqcat, c256, s256, backward=True)
    dkrs = _kcat_bwd(tag + "_kcatb", dkcat, c256, s256)
    g_['mla_w_uq'] = _unpad_w_uq(_mm_simple(tag + "_dwuq", cqn, dqpre, TN, tm=MLA_Q_RANK, tn=1024, tk=512,
                                            out_dtype=BF16))
    dcqn = _mm_simple(tag + "_dcqn", dqpre, p['w_uq_pad'], NT, tm=512, tn=MLA_Q_RANK, tk=1024)
    dcq, dqn = _rms_bwd(tag + "_rmsqb", dcqn, cq, r_q, p['mla_q_norm'])
    g_['mla_q_norm'] = dqn[0]
    dwk = _mm_simple(tag + "_dwk", ckvn, dkcat, TN, tm=MLA_KV_RANK, tn=1024, tk=512, out_dtype=BF16)
    dwv = _mm_simple(tag + "_dwv", ckvn, dvm, TN, tm=MLA_KV_RANK, tn=1024, tk=512, out_dtype=BF16)
    g_['mla_w_ukv'] = _unpad_w_ukv(dwk, dwv)
    tm = _tile(T, 512)
    nk_, nv_ = MLA_HEADS * MLA_SLOT, MLA_WIDTH
    tkk = _tile(nk_, 1024)
    dckvn_k = _mm_simple(tag + "_dckvk", dkcat, p['w_k_pad'], NT, tm=512, tn=MLA_KV_RANK, tk=tkk)
    dckvn = _mm_simple(tag + "_dckvv", dvm, p['w_v'], NT, tm=512, tn=MLA_KV_RANK, tk=nv_,
                       epilogue=lambda accs, ex: (accs[0] + ex[0],), extras=[dckvn_k])
    dckv, dkvn = _rms_bwd(tag + "_rmskvb", dckvn, ckv, r_kv, p['mla_kv_norm'])
    g_['mla_kv_norm'] = dkvn[0]
    dq_r, dkc, dkp, dvc, dvp, dsinks = _swa_bwd(tag + "_swab", q_r, k_r, v, p['sinks_b'], y_swa, lse_swa, dys[1])
    g_['swa_sinks'] = dsinks[0, :SWA_HEADS]
    dk_r = _shift_add(tag + "_dksum", dkc, dkp)
    dv = _shift_add(tag + "_dvsum", dvc, dvp)
    dq = _rope(tag + "_ropeqb", dq_r, c64, s64, backward=True)
    dk = _rope(tag + "_ropekb", dk_r, c64, s64, backward=True)
    du, s5g, dd, dw_glu, db_glu = _s5_bwd(tag + "_s5b", dys[0], s5_saved, p['s5_prep'], p['ssm_d'], p['ssm_w_glu'])
    g_['ssm_d'], g_['ssm_w_glu'], g_['ssm_b_glu'] = dd[0], dw_glu, db_glu[0]
    _, pull = jax.vjp(_s5_prep, p['ssm_log_dt'], p['ssm_a_re'], p['ssm_a_im'], p['ssm_b_re'], p['ssm_b_im'],
                      p['ssm_c_re'], p['ssm_c_im'])
    for name, val in zip(['ssm_log_dt', 'ssm_a_re', 'ssm_a_im', 'ssm_b_re', 'ssm_b_im', 'ssm_c_re', 'ssm_c_im'],
                         pull(tuple(s5g))):
        g_[name] = val
    dparts = [du, dq, dk, dv, dcq, dckv, dkrs]
    widths = _in_widths(ws)
    tn = _tile(D, 1024)
    pairs = []
    for dp_, wp_, w in zip(dparts, p['w_in_parts'], widths):
        pairs.append((dp_, _bs((tm, w), lambda i, j, k: (i, 0)), wp_, _bs((tn, w), lambda i, j, k: (j, 0)), NT, 0))
    o_spec = _bs((tm, tn), lambda i, j, k: (i, j))
    dh, = _mm(tag + "_dh", (T // tm, D // tn, 1), pairs, [(tm, tn)], [((T, D), F32, o_spec)],
              lambda accs, ex: (accs[0],))
    tmw, tk = _tile(D, 512), _tile(T, 512)
    pairs = []
    for i_, (dp_, w) in enumerate(zip(dparts, widths)):
        pairs.append((h, _bs((tk, tmw), lambda i, j, k: (k, i)), dp_, _bs((tk, w), lambda i, j, k: (k, 0)), TN, i_))
    dws = _mm(tag + "_dwin", (D // tmw, 1, T // tk), pairs, [(tmw, w) for w in widths],
              [((D, w), BF16, _bs((tmw, w), lambda i, j, k: (i, 0))) for w in widths], lambda accs, ex: tuple(accs))
    g_['w_in'] = _unpad_w_in(jnp.concatenate(dws, axis=1), ws)
    dx, dx16, dmix = _rms_bwd(tag + "_rmsb", dh, x, rstd, p['mix_norm'], dres=dx2, with_bf16=True)
    g_['mix_norm'] = dmix[0]
    return dx, dx16, g_


def _final_loss(name, x, target, gain):
    D = x.shape[1]

    def fn(t, b):
        xv = t[0]
        r = lax.rsqrt(jnp.mean(xv * xv, axis=-1, keepdims=True) + EPS)
        xh = xv * r
        err = xh * b[0] - t[1]
        part = 0.5 * jnp.sum(jnp.sum(err * err, axis=-1, keepdims=True), axis=0, keepdims=True) / D
        dy = err / D
        dxh = dy * b[0]
        dx = r * (dxh - xh * jnp.mean(dxh * xh, axis=-1, keepdims=True))
        return (dx, dx), (jnp.broadcast_to(part, (1, 128)), _colsum(dy * xh))
    dx, dx16, part, dg = _rowwise(name, fn, [x, target], [gain.reshape(1, D)], [(D, F32), (D, BF16)], [128, D])
    return part[0, 0], dx, dx16, dg[0]


def _device_step(x, positions, target, n_layers, get_block, final_norm, on_grads):
    tabs = _rope_tables(positions)
    saved = []
    for l in range(n_layers):
        w1 = get_block(l, 'ffn1a', x)
        x, s1, w1['ffn1_wd'] = _ffn_fwd("ffn1", x, w1['ffn1_norm'], w1['ffn1_wg2'], w1['ffn1_wu2'],
                                        lambda t, l=l: get_block(l, 'ffn1b', t)['ffn1_wd'])
        p = _layer_prep(get_block(l, 'mix', x))
        x, s2 = _mixer_fwd("mix", x, p, tabs)
        w3 = get_block(l, 'ffn2', x)
        x, s3, _ = _ffn_fwd("ffn2", x, w3['ffn2_norm'], w3['ffn2_wg2'], w3['ffn2_wu2'], w3['ffn2_wd'])
        saved.append((w1, s1, p, s2, w3, s3))
    loss_part, dx, dx16, dfinal = _final_loss("loss", x, target, final_norm)
    for l in range(n_layers - 1, -1, -1):
        w1, s1, p, s2, w3, s3 = saved[l]
        early = lambda g, carry, l=l: on_grads(l, g, (carry,))[0]
        dx, dx16, gn, _, _, _ = _ffn_bwd("ffn2b", dx, dx16, s3, w3['ffn2_norm'], w3['ffn2_wg2'], w3['ffn2_wu2'],
                                         w3['ffn2_wd'], early, ('ffn2_wg2', 'ffn2_wu2', 'ffn2_wd'))
        dx, dx16 = on_grads(l, {'ffn2_norm': gn[0]}, (dx, dx16))
        dx, dx16, gm = _mixer_bwd("mixb", dx, dx16, s2, p, tabs)
        dx, dx16 = on_grads(l, gm, (dx, dx16))
        dx, dx16, gn, _, _, _ = _ffn_bwd("ffn1b", dx, dx16, s1, w1['ffn1_norm'], w1['ffn1_wg2'], w1['ffn1_wu2'],
                                         w1['ffn1_wd'], early, ('ffn1_wg2', 'ffn1_wu2', 'ffn1_wd'))
        dx, dx16 = on_grads(l, {'ffn1_norm': gn[0]}, (dx, dx16))
    return loss_part, dx, dfinal


_ANY = pl.BlockSpec(memory_space=pl.ANY)
_CHIP_MASKS = (2, 1, 3)


def _place():
    x, y, c = lax.axis_index("x"), lax.axis_index("y"), lax.axis_index("c")
    chips = [(1 - x, y), (x, 1 - y), (1 - x, 1 - y)]
    return x, y, c, 2 * x + y, chips


_HBM = pl.BlockSpec(memory_space=pltpu.HBM)
_SEMS = pl.BlockSpec(memory_space=pltpu.SEMAPHORE)
_EFFECT = pltpu.SideEffectType.DATAFLOW_SIDE_EFFECTING


def _split_start(name, bufs, sem_sizes, copies):
    n, ns = len(bufs), len(sem_sizes)

    def body(*refs):
        for cp in copies(refs[:n], refs[n:n + ns])[0]:
            cp.start()

    outs = _pcall(
        body, name=name, in_specs=[_HBM] * n,
        out_shape=(*[pltpu.SemaphoreType.DMA((k,)) for k in sem_sizes], *[pltpu.HBM(b.shape, b.dtype) for b in bufs]),
        out_specs=(*[_SEMS] * ns, *[_HBM] * n), input_output_aliases={i: i + ns for i in range(n)},
        compiler_params=pltpu.CompilerParams(has_side_effects=_EFFECT),
    )(*[pltpu.with_memory_space_constraint(b, pltpu.HBM) for b in bufs])
    return list(outs[:ns]), list(outs[ns:])


def _split_wait(name, bufs, sems, after, copies):
    n, ns = len(bufs), len(sems)

    def body(*refs):
        _, sent, landed = copies(refs[:n], refs[n:n + ns])
        for cp in sent:
            cp.wait_send()
        for cp in landed:
            cp.wait_recv()

    return list(_pcall(
        body, name=name, in_specs=[_HBM] * n + [_SEMS] * ns + [_ANY],
        out_shape=tuple(pltpu.HBM(b.shape, b.dtype) for b in bufs), out_specs=tuple([_HBM] * n),
        input_output_aliases={i: i for i in range(n)},
        compiler_params=pltpu.CompilerParams(has_side_effects=_EFFECT),
    )(*bufs, *sems, after))


def _gather_group(bufs, send, recv):
    x, y, c, s, chips = _place()
    start, landed = [], []
    for a, buf in enumerate(bufs):
        for j in range(3):
            k = a * 3 + j
            to = (*chips[j], c)
            start.append(pltpu.make_async_remote_copy(
                src_ref=buf.at[s], dst_ref=buf.at[s], send_sem=send.at[k], recv_sem=recv.at[k],
                device_id=to, device_id_type=MESH))
            theirs = buf.at[s ^ _CHIP_MASKS[j]]
            landed.append(pltpu.make_async_remote_copy(
                src_ref=theirs, dst_ref=theirs, send_sem=send.at[k], recv_sem=recv.at[k],
                device_id=to, device_id_type=MESH))
    return start, start, landed


def _gather_start_copies(n_pass, sizes):
    def copies(refs, sems):
        start, o = [], n_pass
        for g, k in enumerate(sizes):
            start += _gather_group(refs[o:o + k], sems[2 * g], sems[2 * g + 1])[0]
            o += k
        return start, start, []
    return copies


def _gather_wait_copies(refs, sems):
    return _gather_group(refs, sems[0], sems[1])


_N_SLOTS = 7


def _reduce_copies(n_pass, n):
    def copies(refs, sems):
        send, recv = sems
        x, y, c, s, chips = _place()
        contrib, lands = refs[n_pass:n_pass + n], refs[n_pass + n:]
        start, landed = [], []
        for a in range(n):
            half = contrib[a].shape[1] // 2
            base = a * _N_SLOTS
            for j in range(3):
                for h in range(2):
                    start.append(pltpu.make_async_remote_copy(
                        src_ref=contrib[a].at[s ^ _CHIP_MASKS[j], pl.ds(h * half, half)], dst_ref=lands[a].at[2 * j + c],
                        send_sem=send.at[base + 2 * j + h], recv_sem=recv.at[base + 2 * j + c],
                        device_id=(*chips[j], h), device_id_type=MESH))
            start.append(pltpu.make_async_remote_copy(
                src_ref=contrib[a].at[s, pl.ds((1 - c) * half, half)], dst_ref=lands[a].at[6],
                send_sem=send.at[base + 6], recv_sem=recv.at[base + 6],
                device_id=(x, y, 1 - c), device_id_type=MESH))
            for slot in range(_N_SLOTS):
                landed.append(pltpu.make_async_remote_copy(
                    src_ref=lands[a].at[slot], dst_ref=lands[a].at[slot], send_sem=send.at[base + slot],
                    recv_sem=recv.at[base + slot], device_id=(x, y, c), device_id_type=MESH))
        return start, start, landed
    return copies


def _sum_partials(name, contrib, lands):
    ns, half, w = lands.shape
    tm = _tile(half, 256)
    nb = half // tm
    mine = _half_rows(nb)
    shard = lambda: 2 * lax.axis_index("x") + lax.axis_index("y")

    def body(g_ref, *refs):
        o_ref = refs[-1]
        acc = g_ref[...].astype(F32) + refs[6][...].astype(F32)
        for slot in range(6):
            acc = acc + refs[slot][...].astype(F32)
        o_ref[...] = acc

    blk = lambda j: _bs((None, tm, w), lambda i: (j, i, 0))
    return _pcall(
        body, name=name, grid=(nb,),
        in_specs=[_bs((None, tm, w), lambda i: (shard(), mine(i), 0))] + [blk(j) for j in range(_N_SLOTS)],
        out_specs=_bs((tm, w), lambda i: (mine(i), 0)), out_shape=jax.ShapeDtypeStruct((2 * half, w), F32),
        compiler_params=pltpu.CompilerParams(dimension_semantics=("parallel",)),
    )(contrib, *[lands] * _N_SLOTS)


def _stage_shard(name, wl):
    r, c = wl.shape
    tm = _tile(r, 256)
    shard = lambda: 2 * lax.axis_index("x") + lax.axis_index("y")

    def body(w_ref, o_ref):
        o_ref[...] = w_ref[...].astype(o_ref.dtype)

    return _pcall(
        body, name=name, grid=(r // tm,), in_specs=[_bs((tm, c), lambda i: (i, 0))],
        out_specs=_bs((None, tm, c), lambda i: (shard(), i, 0)),
        out_shape=jax.ShapeDtypeStruct((N_SHARD, r, c), BF16),
        compiler_params=pltpu.CompilerParams(dimension_semantics=("parallel",)),
    )(wl)


def _join_copies(n_pass, n):
    def copies(refs, sems):
        send, recv = sems
        x, y, c, s, chips = _place()
        start, landed = [], []
        for a, buf in enumerate(refs[n_pass:n_pass + n]):
            half = buf.shape[0] // 2
            mine = buf.at[pl.ds(c * half, half)]
            theirs = buf.at[pl.ds((1 - c) * half, half)]
            start.append(pltpu.make_async_remote_copy(
                src_ref=mine, dst_ref=mine, send_sem=send.at[a], recv_sem=recv.at[a],
                device_id=(x, y, 1 - c), device_id_type=MESH))
            landed.append(pltpu.make_async_remote_copy(
                src_ref=theirs, dst_ref=theirs, send_sem=send.at[a], recv_sem=recv.at[a],
                device_id=(x, y, 1 - c), device_id_type=MESH))
        return start, start, landed
    return copies


def _half_rows(nb):
    return lambda i: lax.axis_index("c") * nb + i


def _pair_join_call(name, halves):
    n = len(halves)

    def body(*refs):
        outs = refs[n:2 * n]
        send, recv = refs[2 * n:]
        x, y, c, s, chips = _place()
        cps = []
        for a in range(n):
            half = outs[a].shape[0] // 2
            mine = outs[a].at[pl.ds(c * half, half)]
            cp = pltpu.make_async_remote_copy(
                src_ref=mine, dst_ref=mine, send_sem=send.at[a], recv_sem=recv.at[a],
                device_id=(x, y, 1 - c), device_id_type=MESH)
            cp.start()
            cps.append(cp)
        for a in range(n):
            half = outs[a].shape[0] // 2
            theirs = outs[a].at[pl.ds((1 - c) * half, half)]
            pltpu.make_async_remote_copy(
                src_ref=theirs, dst_ref=theirs, send_sem=send.at[a], recv_sem=recv.at[a],
                device_id=(x, y, 1 - c), device_id_type=MESH).wait_recv()
        for cp in cps:
            cp.wait_send()

    return _pcall(
        body, name=name, in_specs=[_ANY] * n, out_specs=[_ANY] * n,
        out_shape=[jax.ShapeDtypeStruct(h.shape, h.dtype) for h in halves],
        input_output_aliases={a: a for a in range(n)},
        scratch_shapes=[pltpu.SemaphoreType.DMA((n,))] * 2,
    )(*halves)


_N_DEV = 8


def _small_copies(n_pass):
    def copies(refs, sems):
        send, recv = sems
        buf, land = refs[n_pass], refs[n_pass + 1]
        x, y, c, s, chips = _place()
        me = 4 * x + 2 * y + c
        start, landed = [], []
        for k in range(1, _N_DEV):
            to = (x ^ (k >> 2), y ^ ((k >> 1) & 1), c ^ (k & 1))
            start.append(pltpu.make_async_remote_copy(
                src_ref=buf, dst_ref=land.at[me], send_sem=send.at[k - 1], recv_sem=recv.at[k - 1],
                device_id=to, device_id_type=MESH))
            theirs = land.at[me ^ k]
            landed.append(pltpu.make_async_remote_copy(
                src_ref=theirs, dst_ref=theirs, send_sem=send.at[k - 1], recv_sem=recv.at[k - 1],
                device_id=to, device_id_type=MESH))
        return start, start, landed
    return copies


def _sum_devices(name, land):
    rows = land.shape[1]
    tm = _tile(rows, 512)

    def sum_body(a_ref, o_ref):
        acc = a_ref[0]
        for k in range(1, _N_DEV):
            acc = acc + a_ref[k]
        o_ref[...] = acc

    return _pcall(
        sum_body, name=name, grid=(rows // tm,),
        in_specs=[_bs((_N_DEV, tm, 128), lambda i: (0, i, 0))], out_specs=_bs((tm, 128), lambda i: (i, 0)),
        out_shape=jax.ShapeDtypeStruct(land.shape[1:], F32),
        compiler_params=pltpu.CompilerParams(dimension_semantics=("parallel",)),
    )(land)


def _adamw(name, w, g, m, v, token=None):
    def fn(t, b):
        wv, gv, mv, vv = t
        m2 = ADAM_B1 * mv + (1.0 - ADAM_B1) * gv
        v2 = ADAM_B2 * vv + (1.0 - ADAM_B2) * (gv * gv)
        m_hat = m2 / (1.0 - ADAM_B1 ** ADAM_STEP)
        v_hat = v2 / (1.0 - ADAM_B2 ** ADAM_STEP)
        delta = -ADAM_LR * (m_hat / (jnp.sqrt(v_hat) + ADAM_EPS) + ADAM_WD * wv)
        return (delta, m2, v2), ()
    wd = w.shape[1]
    return _rowwise(name, fn, [w, g, m, v], [] if token is None else [token], [(wd, F32)] * 3,
                    tm=256 if wd > 1024 else 512)


_WEIGHTS = ['ffn1_norm', 'ffn1_w_gate', 'ffn1_w_up', 'ffn1_w_down', 'mix_norm', 'w_in', 'ssm_log_dt', 'ssm_a_re',
            'ssm_a_im', 'ssm_b_re', 'ssm_b_im', 'ssm_c_re', 'ssm_c_im', 'ssm_d', 'ssm_w_glu', 'ssm_b_glu',
            'swa_sinks', 'mla_q_norm', 'mla_w_uq', 'mla_kv_norm', 'mla_w_ukv', 'out_norm', 'w_out', 'ffn2_norm',
            'ffn2_w_gate', 'ffn2_w_up', 'ffn2_w_down', 'final_norm']
_COL_SHARDED = ['ffn1_w_gate', 'ffn1_w_up', 'w_in', 'mla_w_uq', 'mla_w_ukv', 'ffn2_w_gate', 'ffn2_w_up']
_ROW_SHARDED = ['ffn1_w_down', 'ssm_w_glu', 'w_out', 'ffn2_w_down']
_STACKED = {'ffn1_w_gate': 'ffn1_wg2', 'ffn1_w_up': 'ffn1_wu2', 'ffn2_w_gate': 'ffn2_wg2', 'ffn2_w_up': 'ffn2_wu2'}
_RENAMED = {'ffn1_w_down': 'ffn1_wd', 'ffn2_w_down': 'ffn2_wd'}
_BLOCKS = ('ffn1', 'mix', 'ffn2')
_BLOCK_BIG = {'ffn1': ['ffn1_w_gate', 'ffn1_w_up', 'ffn1_w_down'],
              'mix': ['w_in', 'ssm_w_glu', 'mla_w_uq', 'mla_w_ukv', 'w_out'],
              'ffn2': ['ffn2_w_gate', 'ffn2_w_up', 'ffn2_w_down']}
_BLOCK_SMALL = {'ffn1': ['ffn1_norm'],
                'mix': ['mix_norm', 'ssm_log_dt', 'ssm_a_re', 'ssm_a_im', 'ssm_b_re', 'ssm_b_im', 'ssm_c_re', 'ssm_c_im',
                        'ssm_d', 'ssm_b_glu', 'swa_sinks', 'mla_q_norm', 'mla_kv_norm', 'out_norm'],
                'ffn2': ['ffn2_norm']}
_BIG = [n for b in _BLOCKS for n in _BLOCK_BIG[b]]
_SMALL = [n for n in _WEIGHTS if n not in _BIG]


def _pack(vals):
    flat = jnp.concatenate([v.reshape(-1) for v in vals])
    pad = (-flat.shape[0]) % 1024
    return jnp.pad(flat, (0, pad)).reshape(-1, 128)


def _unpack(buf, like):
    flat = buf.reshape(-1)
    out, o = [], 0
    for v in like:
        out.append(flat[o:o + v.size].reshape(v.shape))
        o += v.size
    return out


def kernel(x, positions, ffn1_norm, ffn1_w_gate, ffn1_w_up, ffn1_w_down, mix_norm, w_in, ssm_log_dt, ssm_a_re, ssm_a_im, ssm_b_re, ssm_b_im, ssm_c_re, ssm_c_im, ssm_d, ssm_w_glu, ssm_b_glu, swa_sinks, mla_q_norm, mla_w_uq, mla_kv_norm, mla_w_ukv, out_norm, w_out, ffn2_norm, ffn2_w_gate, ffn2_w_up, ffn2_w_down, final_norm, loss_target, m_ffn1_norm, m_ffn1_w_gate, m_ffn1_w_up, m_ffn1_w_down, m_mix_norm, m_w_in, m_ssm_log_dt, m_ssm_a_re, m_ssm_a_im, m_ssm_b_re, m_ssm_b_im, m_ssm_c_re, m_ssm_c_im, m_ssm_d, m_ssm_w_glu, m_ssm_b_glu, m_swa_sinks, m_mla_q_norm, m_mla_w_uq, m_mla_kv_norm, m_mla_w_ukv, m_out_norm, m_w_out, m_ffn2_norm, m_ffn2_w_gate, m_ffn2_w_up, m_ffn2_w_down, m_final_norm, v_ffn1_norm, v_ffn1_w_gate, v_ffn1_w_up, v_ffn1_w_down, v_mix_norm, v_w_in, v_ssm_log_dt, v_ssm_a_re, v_ssm_a_im, v_ssm_b_re, v_ssm_b_im, v_ssm_c_re, v_ssm_c_im, v_ssm_d, v_ssm_w_glu, v_ssm_b_glu, v_swa_sinks, v_mla_q_norm, v_mla_w_uq, v_mla_kv_norm, v_mla_w_ukv, v_out_norm, v_w_out, v_ffn2_norm, v_ffn2_w_gate, v_ffn2_w_up, v_ffn2_w_down, v_final_norm):
    w = dict(zip(_WEIGHTS, (ffn1_norm, ffn1_w_gate, ffn1_w_up, ffn1_w_down, mix_norm, w_in, ssm_log_dt, ssm_a_re, ssm_a_im, ssm_b_re, ssm_b_im, ssm_c_re, ssm_c_im, ssm_d, ssm_w_glu, ssm_b_glu, swa_sinks, mla_q_norm, mla_w_uq, mla_kv_norm, mla_w_ukv, out_norm, w_out, ffn2_norm, ffn2_w_gate, ffn2_w_up, ffn2_w_down, final_norm)))
    m = dict(zip(_WEIGHTS, (m_ffn1_norm, m_ffn1_w_gate, m_ffn1_w_up, m_ffn1_w_down, m_mix_norm, m_w_in, m_ssm_log_dt, m_ssm_a_re, m_ssm_a_im, m_ssm_b_re, m_ssm_b_im, m_ssm_c_re, m_ssm_c_im, m_ssm_d, m_ssm_w_glu, m_ssm_b_glu, m_swa_sinks, m_mla_q_norm, m_mla_w_uq, m_mla_kv_norm, m_mla_w_ukv, m_out_norm, m_w_out, m_ffn2_norm, m_ffn2_w_gate, m_ffn2_w_up, m_ffn2_w_down, m_final_norm)))
    v = dict(zip(_WEIGHTS, (v_ffn1_norm, v_ffn1_w_gate, v_ffn1_w_up, v_ffn1_w_down, v_mix_norm, v_w_in, v_ssm_log_dt, v_ssm_a_re, v_ssm_a_im, v_ssm_b_re, v_ssm_b_im, v_ssm_c_re, v_ssm_c_im, v_ssm_d, v_ssm_w_glu, v_ssm_b_glu, v_swa_sinks, v_mla_q_norm, v_mla_w_uq, v_mla_kv_norm, v_mla_w_ukv, v_out_norm, v_w_out, v_ffn2_norm, v_ffn2_w_gate, v_ffn2_w_up, v_ffn2_w_down, v_final_norm)))
    n_layers = ffn1_norm.shape[0]

    groups = (('ffn1a', ['ffn1_w_gate', 'ffn1_w_up'], ['ffn1_norm']), ('ffn1b', ['ffn1_w_down'], []),
              ('mix', _BLOCK_BIG['mix'], _BLOCK_SMALL['mix']), ('ffn2', _BLOCK_BIG['ffn2'], ['ffn2_norm']))
    assert [n for _, names, _ in groups for n in names] == _BIG
    sizes = [len(names) for _, names, _ in groups]

    x0 = x[0]
    travelling = []
    for l in range(n_layers):
        staged = [_stage_shard("stage_" + n, w[n][l]) for n in _BIG]
        sems, thru = _split_start(f"gstart{l}", [x0] + staged, [3 * k for k in sizes for _ in range(2)],
                                  _gather_start_copies(1, sizes))
        x0 = thru[0]
        travelling.append((sems, thru[1:]))

    def get_block(l, blk, x_now):
        g = [name for name, _, _ in groups].index(blk)
        sems, staged = travelling[l]
        o = sum(sizes[:g])
        full = _split_wait(f"gwait{l}{blk}", staged[o:o + sizes[g]], sems[2 * g:2 * g + 2], x_now,
                           _gather_wait_copies)
        lw = {n: w[n][l] for n in groups[g][2]}
        for n, gth in zip(groups[g][1], full):
            ns, r, c = gth.shape
            if n in _STACKED:
                lw[_STACKED[n]] = gth.reshape(ns * r, c)
            elif n in _ROW_SHARDED:
                lw[_RENAMED.get(n, n)] = gth.reshape(ns * r, c)
            else:
                lw[n] = jnp.moveaxis(gth, 0, 1).reshape(r, ns * c)
        return lw

    reduced = {n: [None] * n_layers for n in _BIG}
    grads = [dict() for _ in range(n_layers)]
    joining = []
    pending = []
    counter = [0, 0]

    def finish(after):
        _, l, names, call, sems, contrib, lands = pending.pop(0)
        k = len(contrib)
        done = _split_wait(f"cwait{call}", contrib + lands, sems, after, _reduce_copies(0, k))
        halves = [_sum_partials("rs_sum", g, q) for g, q in zip(done[:k], done[k:])]
        jsems, jthru = _split_start(f"jstart{call}", [token[0]] + halves, [k] * 2, _join_copies(1, k))
        token[0] = jthru[0]
        joining.append((l, names, call, jsems, jthru[1:]))

    def joined(entries, after):
        for l, names, call, jsems, bufs in entries:
            for n, g in zip(names, _split_wait(f"jwait{call}", bufs, jsems, after, _join_copies(0, len(bufs)))):
                reduced[n][l] = g

    def on_grads(l, g_, carry):
        small = [n for n in _SMALL if n in g_]
        for n in small:
            grads[l][n] = g_[n]
        if small:
            while pending and pending[0][0] < counter[0]:
                finish(g_[small[0]])
        names = [n for n in _BIG if _STACKED.get(n, _RENAMED.get(n, n)) in g_]
        if names:
            contrib = []
            for n in names:
                r, c = w[n].shape[1:]
                if n in _STACKED:
                    g = g_[_STACKED[n]].reshape(N_SHARD, r, c)
                elif n in _ROW_SHARDED:
                    g = g_[_RENAMED.get(n, n)].reshape(N_SHARD, r, c)
                else:
                    g = jnp.moveaxis(g_[n].reshape(r, N_SHARD, c), 1, 0)
                contrib.append(g)
            k, p = len(contrib), len(carry) + 1
            lands = [lax.empty((_N_SLOTS, g.shape[1] // 2, g.shape[2]), g.dtype) for g in contrib]
            sems, thru = _split_start(f"cstart{counter[1]}", list(carry) + [token[0]] + contrib + lands,
                                      [_N_SLOTS * k] * 2, _reduce_copies(p, k))
            token[0] = thru[p - 1]
            pending.append((counter[0], l, names, counter[1], sems, thru[p:p + k], thru[p + k:]))
            counter[1] += 1
            carry = tuple(thru[:p - 1])
        if small:
            counter[0] += 1
        return carry

    token = [jnp.zeros((8, 128), F32)]
    loss_part, dx, dfinal = _device_step(x0, positions[0], loss_target[0], n_layers, get_block, final_norm, on_grads)
    loss = lax.psum(loss_part, ("x", "y", "c"))

    small_like = [w[n] for n in _SMALL]
    small_g = [jnp.stack([grads[l][n] for l in range(n_layers)]) for n in _SMALL if n != 'final_norm'] + [dfinal]
    packed = _pack(small_g)
    me = 4 * lax.axis_index("x") + 2 * lax.axis_index("y") + lax.axis_index("c")
    land = lax.dynamic_update_slice(lax.empty((_N_DEV,) + packed.shape, F32), packed[None], (me, 0, 0))
    s_sems, s_thru = _split_start("sstart", [token[0], packed, land], [_N_DEV - 1] * 2, _small_copies(1))
    token[0] = s_thru[0]

    grad, delta, new_m, new_v = {}, {}, {}, {}

    def update(n):
        shp = w[n].shape
        two = lambda t: t.reshape(shp[0] * shp[1], shp[2])
        grad[n] = jnp.stack(reduced[n])
        d2, m2, v2 = _adamw("adam_" + n, two(w[n]), two(grad[n]), two(m[n]), two(v[n]), token=token[0])
        delta[n], new_m[n], new_v[n] = d2.reshape(shp), m2.reshape(shp), v2.reshape(shp)
        return d2

    late = [n for entry in pending for n in entry[2]]
    arrived = list(joining)
    del joining[:]
    joined(arrived, dx)
    for n in _BIG:
        if n not in late:
            meanwhile = update(n)
    while pending:
        finish(meanwhile)
    joined(joining, meanwhile)
    g_small = _sum_devices("small_sum", _split_wait("swait", s_thru[1:], s_sems, meanwhile, _small_copies(0))[1])
    d_small, m_small, v_small = _adamw("adam_small", _pack(small_like), g_small, _pack([m[n] for n in _SMALL]),
                                       _pack([v[n] for n in _SMALL]))
    for n, gv, dv_, mv, vv in zip(_SMALL, _unpack(g_small, small_like), _unpack(d_small, small_like),
                                  _unpack(m_small, small_like), _unpack(v_small, small_like)):
        grad[n], delta[n], new_m[n], new_v[n] = gv, dv_, mv, vv
    for n in late:
        update(n)
    return (loss, dx[None], *[grad[n] for n in _WEIGHTS], *[delta[n] for n in _WEIGHTS],
            *[new_m[n] for n in _WEIGHTS], *[new_v[n] for n in _WEIGHTS])
```
